```python
import jax, jax.numpy as jnp
from jax import lax
import numpy as np

D_MODEL = 1024
BATCH = 8
SEQ = 2048
DEPTH = 1
DEC_BATCH = 128
DEC_SEQ = 1
PAST_LEN = 2048
PAGE_SIZE = 128

MIX_W = D_MODEL
LRU_W = MIX_W // 2
LRU_BLOCKS = 8
LRU_BW = LRU_W // LRU_BLOCKS
CONV_W = 4
LRU_C = 8.0
N_HEADS = 8
HEAD_DIM = (MIX_W - LRU_W) // N_HEADS
ATT_W = N_HEADS * HEAD_DIM
N_KV_HEADS = 2
KV_GROUP = N_HEADS // N_KV_HEADS
ROT_DIM = HEAD_DIM // 4
ROPE_THETA = 500000.0
IDX_HEADS = 4
IDX_DIM = 64
TOPK_MAX = 256
Q_BLOCK = 128
D_FF = -(-8 * D_MODEL // (3 * 256)) * 256
EPS = 1e-6
IN_SIZES = (LRU_W, LRU_W, ATT_W, N_KV_HEADS * HEAD_DIM, N_KV_HEADS * HEAD_DIM,
            IDX_HEADS * IDX_DIM, IDX_DIM, IDX_HEADS)
IN_W = sum(IN_SIZES)

kernel_name = 'hymba_rglru_dsa_decode_step'


def _split_points():
    pts, acc = [], 0
    for s in IN_SIZES[:-1]:
        acc += s
        pts.append(acc)
    return pts


def rms_norm(x, g):
    xf = x.astype(jnp.float32)
    y = xf * lax.rsqrt(jnp.mean(xf * xf, axis=-1, keepdims=True) + EPS)
    return (y * g.astype(jnp.float32)).astype(x.dtype)


def modulate(h, shift, scale):
    return h * (1.0 + scale[:, None, :]) + shift[:, None, :]


def rope_partial(x, pos):
    half = ROT_DIM // 2
    freq = ROPE_THETA ** (-jnp.arange(half, dtype=jnp.float32) / half)
    ang = pos.astype(jnp.float32)[:, None] * freq
    ang = ang.reshape(ang.shape[:1] + (1,) * (x.ndim - 3) + (half,))
    cos, sin = jnp.cos(ang), jnp.sin(ang)
    xr = x[..., :ROT_DIM].astype(jnp.float32)
    x1, x2 = xr[..., :half], xr[..., half:]
    rot = jnp.concatenate([x1 * cos - x2 * sin, x2 * cos + x1 * sin], axis=-1).astype(x.dtype)
    return jnp.concatenate([rot, x[..., ROT_DIM:]], axis=-1)


def causal_conv(x, buf, w, b):
    t = x.shape[1]
    xp = jnp.concatenate([buf.astype(x.dtype), x], axis=1)
    y = b
    for i in range(CONV_W):
        y = y + xp[:, i:i + t] * w[i]
    return y, xp[:, xp.shape[1] - (CONV_W - 1):]


def rg_lru(x, h0, w_ra, b_ra, w_ix, b_ix, lam):
    bsz, t, c = x.shape
    xb = x.reshape(bsz, t, LRU_BLOCKS, LRU_BW)
    r = jax.nn.sigmoid(jnp.einsum('btni,nij->btnj', xb, w_ra).reshape(bsz, t, c) + b_ra)
    gi = jax.nn.sigmoid(jnp.einsum('btni,nij->btnj', xb, w_ix).reshape(bsz, t, c) + b_ix)
    log_a = -LRU_C * jax.nn.softplus(-lam.astype(jnp.float32)) * r.astype(jnp.float32)
    a = jnp.exp(log_a)
    inp = jnp.sqrt(-jnp.expm1(2.0 * log_a)) * gi.astype(jnp.float32) * x.astype(jnp.float32)

    def combine(left, right):
        a_l, b_l = left
        a_r, b_r = right
        return a_l * a_r, a_r * b_l + b_r

    a_cum, b_cum = lax.associative_scan(combine, (a, inp), axis=1)
    h = a_cum * h0.astype(jnp.float32)[:, None, :] + b_cum
    return h.astype(x.dtype), h[:, -1].astype(h0.dtype)


def sparse_attend(q, iq, iw, q_pos, k_all, v_all, ik_all, k_pos, topk):
    bsz, tq = q.shape[:2]
    s = jnp.einsum('bthd,bsd->bths', iq.astype(jnp.float32), ik_all.astype(jnp.float32)) * IDX_DIM ** -0.5
    score = jnp.einsum('bth,bths->bts', iw.astype(jnp.float32), jax.nn.relu(s)) * IDX_HEADS ** -0.5
    causal = k_pos[None, :] <= q_pos[:, None]
    score = jnp.where(causal[None], score, -jnp.inf)
    _, idx = lax.top_k(score, topk)
    valid = jnp.take(k_pos, idx) <= q_pos[None, :, None]
    bi = jnp.arange(bsz)[:, None, None]
    kg = k_all[bi, idx]
    vg = v_all[bi, idx]
    qg = q.reshape(bsz, tq, N_KV_HEADS, KV_GROUP, HEAD_DIM)
    logits = jnp.einsum('btngd,btsnd->btngs', qg.astype(jnp.float32), kg.astype(jnp.float32)) * HEAD_DIM ** -0.5
    logits = jnp.where(valid[:, :, None, None, :], logits, -jnp.inf)
    p = jax.nn.softmax(logits, axis=-1)
    o = jnp.einsum('btngs,btsnd->btngd', p.astype(vg.dtype), vg)
    return o.reshape(bsz, tq, ATT_W)


def hybrid_layer(x, c, pos, conv_buf, h0, past, lw):
    bsz, t = x.shape[:2]
    mod = jax.nn.silu(c) @ lw['ada_w'] + lw['ada_b']
    sh1, sc1, g1, sh2, sc2, g2 = jnp.split(mod, 6, axis=-1)
    h = modulate(rms_norm(x, lw['ln1_g']), sh1, sc1)
    xr, gr, q, k, v, iq, ik, iw = jnp.split(h @ lw['w_in'], _split_points(), axis=-1)
    xc, conv_new = causal_conv(xr, conv_buf, lw['conv_w'], lw['conv_b'])
    hr, h_last = rg_lru(xc, h0, lw['lru_wa'], lw['lru_ba'], lw['lru_wx'], lw['lru_bx'], lw['lru_lambda'])
    out_a = jax.nn.gelu(gr) * hr
    q = rope_partial(rms_norm(q.reshape(bsz, t, N_HEADS, HEAD_DIM), lw['q_norm_g']), pos)
    k = rope_partial(rms_norm(k.reshape(bsz, t, N_KV_HEADS, HEAD_DIM), lw['k_norm_g']), pos)
    v = v.reshape(bsz, t, N_KV_HEADS, HEAD_DIM)
    iq = rope_partial(iq.reshape(bsz, t, IDX_HEADS, IDX_DIM), pos)
    ik = rope_partial(ik, pos)
    if past is None:
        topk = min(TOPK_MAX, t // 4)
        nblk = t // Q_BLOCK

        def to_blocks(a):
            return jnp.moveaxis(a.reshape((bsz, nblk, Q_BLOCK) + a.shape[2:]), 1, 0)

        def one_block(args):
            qb, iqb, iwb, pb = args
            return sparse_attend(qb, iqb, iwb, pb, k, v, ik, pos, topk)

        ob = lax.map(one_block, (to_blocks(q), to_blocks(iq), to_blocks(iw), pos.reshape(nblk, Q_BLOCK)))
        out_b = jnp.moveaxis(ob, 0, 1).reshape(bsz, t, ATT_W)
    else:
        k_past, v_past, ik_past = past
        n_keys = k_past.shape[1] + t
        topk = min(TOPK_MAX, n_keys // 4)
        k_all = jnp.concatenate([k_past.astype(k.dtype), k], axis=1)
        v_all = jnp.concatenate([v_past.astype(v.dtype), v], axis=1)
        ik_all = jnp.concatenate([ik_past.astype(ik.dtype), ik], axis=1)
        k_pos = jnp.arange(n_keys, dtype=jnp.int32)
        out_b = sparse_attend(q, iq, iw, pos, k_all, v_all, ik_all, k_pos, topk)
    mix = jnp.concatenate([rms_norm(out_a, lw['gn_lru_g']), rms_norm(out_b, lw['gn_att_g'])], axis=-1) @ lw['w_out']
    x = x + g1[:, None, :] * mix
    h2 = modulate(rms_norm(x, lw['ln2_g']), sh2, sc2)
    ff = (jax.nn.silu(h2 @ lw['w_gate']) * (h2 @ lw['w_up'])) @ lw['w_down']
    x = x + g2[:, None, :] * ff
    return x, (k, v, ik, h_last, conv_new)


def setup_inputs(seed: int = 0) -> dict:
    key = jax.random.key(seed)
    ks = jax.random.split(key, 40)
    f32 = jnp.float32

    def nrm(k, shape, scale):
        return jax.random.normal(k, shape, f32) * scale

    def gain(k, shape):
        return 1.0 + 0.01 * jax.random.normal(k, shape, f32)

    n_pages = PAST_LEN // PAGE_SIZE
    n_used = DEC_BATCH * n_pages
    n_pool = n_used + (n_used + 3) // 4
    page_table = jax.random.permutation(ks[0], n_pool)[:n_used].reshape(DEC_BATCH, n_pages).astype(jnp.int32)
    u = jax.random.uniform(ks[1], (DEPTH, LRU_W), f32, 0.9, 0.999)
    base = u ** (1.0 / LRU_C)
    lru_lambda = jnp.log(base) - jnp.log1p(-base)
    return {
        'x_prompt': nrm(ks[2], (BATCH, SEQ, D_MODEL), 1.0),
        'x_sample': nrm(ks[3], (DEC_BATCH, DEC_SEQ, D_MODEL), 1.0),
        'cache_k': nrm(ks[4], (DEPTH, n_pool, PAGE_SIZE, N_KV_HEADS, HEAD_DIM), 1.0),
        'cache_v': nrm(ks[5], (DEPTH, n_pool, PAGE_SIZE, N_KV_HEADS, HEAD_DIM), 1.0),
        'cache_ik': nrm(ks[6], (DEPTH, n_pool, PAGE_SIZE, IDX_DIM), 1.0),
        'state_h': nrm(ks[7], (DEPTH, DEC_BATCH, LRU_W), 0.5),
        'state_conv': nrm(ks[8], (DEPTH, DEC_BATCH, CONV_W - 1, LRU_W), 1.0),
        'page_table': page_table,
        'c_prompt': nrm(ks[9], (BATCH, D_MODEL), 1.0),
        'c_sample': nrm(ks[10], (DEC_BATCH, D_MODEL), 1.0),
        'ada_w': nrm(ks[11], (DEPTH, D_MODEL, 6 * D_MODEL), 0.5 * D_MODEL ** -0.5),
        'ada_b': nrm(ks[12], (DEPTH, 6 * D_MODEL), 0.01),
        'ln1_g': gain(ks[13], (DEPTH, D_MODEL)),
        'w_in': nrm(ks[14], (DEPTH, D_MODEL, IN_W), D_MODEL ** -0.5),
        'conv_w': nrm(ks[15], (DEPTH, CONV_W, LRU_W), CONV_W ** -0.5),
        'conv_b': nrm(ks[16], (DEPTH, LRU_W), 0.01),
        'lru_wa': nrm(ks[17], (DEPTH, LRU_BLOCKS, LRU_BW, LRU_BW), LRU_BW ** -0.5),
        'lru_ba': nrm(ks[18], (DEPTH, LRU_W), 0.01),
        'lru_wx': nrm(ks[19], (DEPTH, LRU_BLOCKS, LRU_BW, LRU_BW), LRU_BW ** -0.5),
        'lru_bx': nrm(ks[20], (DEPTH, LRU_W), 0.01),
        'lru_lambda': lru_lambda,
        'q_norm_g': gain(ks[21], (DEPTH, HEAD_DIM)),
        'k_norm_g': gain(ks[22], (DEPTH, HEAD_DIM)),
        'gn_lru_g': gain(ks[23], (DEPTH, LRU_W)),
        'gn_att_g': gain(ks[24], (DEPTH, ATT_W)),
        'w_out': nrm(ks[25], (DEPTH, MIX_W, D_MODEL), MIX_W ** -0.5),
        'ln2_g': gain(ks[26], (DEPTH, D_MODEL)),
        'w_gate': nrm(ks[27], (DEPTH, D_MODEL, D_FF), D_MODEL ** -0.5),
        'w_up': nrm(ks[28], (DEPTH, D_MODEL, D_FF), D_MODEL ** -0.5),
        'w_down': nrm(ks[29], (DEPTH, D_FF, D_MODEL), D_FF ** -0.5),
    }


def reference(x_prompt, x_sample, cache_k, cache_v, cache_ik, state_h, state_conv, page_table,
              c_prompt, c_sample, ada_w, ada_b, ln1_g, w_in, conv_w, conv_b, lru_wa, lru_ba,
              lru_wx, lru_bx, lru_lambda, q_norm_g, k_norm_g, gn_lru_g, gn_att_g, w_out, ln2_g,
              w_gate, w_up, w_down):
    bsz, t = x_prompt.shape[:2]
    dbsz, dt = x_sample.shape[:2]
    n_pages = page_table.shape[1]
    past_len = n_pages * PAGE_SIZE
    pos_p = jnp.arange(t, dtype=jnp.int32)
    pos_s = past_len + jnp.arange(dt, dtype=jnp.int32)
    yp, ys = x_prompt, x_sample
    st_p, st_s = [], []
    for l in range(DEPTH):
        lw = {'ada_w': ada_w[l], 'ada_b': ada_b[l], 'ln1_g': ln1_g[l], 'w_in': w_in[l],
              'conv_w': conv_w[l], 'conv_b': conv_b[l], 'lru_wa': lru_wa[l], 'lru_ba': lru_ba[l],
              'lru_wx': lru_wx[l], 'lru_bx': lru_bx[l], 'lru_lambda': lru_lambda[l],
              'q_norm_g': q_norm_g[l], 'k_norm_g': k_norm_g[l], 'gn_lru_g': gn_lru_g[l],
              'gn_att_g': gn_att_g[l], 'w_out': w_out[l], 'ln2_g': ln2_g[l],
              'w_gate': w_gate[l], 'w_up': w_up[l], 'w_down': w_down[l]}
        zero_buf = jnp.zeros((bsz, CONV_W - 1, LRU_W), x_prompt.dtype)
        zero_h = jnp.zeros((bsz, LRU_W), state_h.dtype)
        yp, sp = hybrid_layer(yp, c_prompt, pos_p, zero_buf, zero_h, None, lw)
        past = (cache_k[l][page_table].reshape(dbsz, past_len, N_KV_HEADS, HEAD_DIM),
                cache_v[l][page_table].reshape(dbsz, past_len, N_KV_HEADS, HEAD_DIM),
                cache_ik[l][page_table].reshape(dbsz, past_len, IDX_DIM))
        ys, ss = hybrid_layer(ys, c_sample, pos_s, state_conv[l], state_h[l], past, lw)
        st_p.append(sp)
        st_s.append(ss)
    k_prompt = jnp.stack([s[0] for s in st_p])
    v_prompt = jnp.stack([s[1] for s in st_p])
    ik_prompt = jnp.stack([s[2] for s in st_p])
    h_prompt = jnp.stack([s[3] for s in st_p])
    conv_prompt = jnp.stack([s[4] for s in st_p])
    k_sample = jnp.stack([s[0] for s in st_s])
    v_sample = jnp.stack([s[1] for s in st_s])
    ik_sample = jnp.stack([s[2] for s in st_s])
    h_sample = jnp.stack([s[3] for s in st_s])
    conv_sample = jnp.stack([s[4] for s in st_s])
    return (yp, ys, k_prompt, v_prompt, ik_prompt, h_prompt, conv_prompt,
            k_sample, v_sample, ik_sample, h_sample, conv_sample)
```

```python
import functools

import numpy as np
import jax
import jax.numpy as jnp
from jax import lax
from jax.experimental import pallas as pl
from jax.experimental.pallas import tpu as pltpu

F32 = jnp.float32
BF16 = jnp.bfloat16
I32 = jnp.int32

D_MODEL = 1024
LRU_W = 512
LRU_BLOCKS = 8
LRU_BW = LRU_W // LRU_BLOCKS
CONV_W = 4
LRU_C = 8.0
N_HEADS = 8
HEAD_DIM = 64
ATT_W = N_HEADS * HEAD_DIM
N_KV_HEADS = 2
KV_GROUP = N_HEADS // N_KV_HEADS
KV_W = N_KV_HEADS * HEAD_DIM
ROT_DIM = HEAD_DIM // 4
ROPE_THETA = 500000.0
IDX_HEADS = 4
IDX_DIM = 64
IDX_W = IDX_HEADS * IDX_DIM
TOPK_MAX = 256
PAGE_SIZE = 128
D_FF = 2816
EPS = 1e-6

COL_XR = 0
COL_GR = COL_XR + LRU_W
COL_Q = COL_GR + LRU_W
COL_K = COL_Q + ATT_W
COL_V = COL_K + KV_W
COL_IQ = COL_V + KV_W
COL_IK = COL_IQ + IDX_W
MAIN_W = COL_IK
TAIL_W = 128
IW_LANE = IDX_DIM

LANES = 128
SUBLANES = 8
TRI_W = 256
INT_MIN = -2147483648
NEG_BIG = -1e30
VMEM_LIMIT = 48 * 1024 * 1024

PROMPT_ROW_TILE = 256
Q_BLOCK = 128
POST_ROW_TILE = 512
FF_CHUNK = D_FF // 2
ADA_COL_TILE = 512


def _dot(a, b):
    return jnp.dot(a, b, preferred_element_type=F32)


def _dot_nt(a, b):
    return lax.dot_general(a, b, (((1,), (1,)), ((), ())), preferred_element_type=F32)


def _split_bf16(x):
    hi = x.astype(BF16)
    lo = (x - hi.astype(F32)).astype(BF16)
    return hi, lo


def _rms(x, g):
    return x * lax.rsqrt(jnp.mean(x * x, axis=-1, keepdims=True) + EPS) * g


def _head_rms(x, ones_bd, g):
    hi, lo = _split_bf16(x * x)
    ss = _dot(hi, ones_bd) + _dot(lo, ones_bd)
    return x * lax.rsqrt(ss * (1.0 / HEAD_DIM) + EPS) * g


def _rope(x, cos, sin_next, sin_prev):
    w = x.shape[-1]
    half = ROT_DIM // 2
    return x * cos + pltpu.roll(x, w - half, 1) * sin_next + pltpu.roll(x, half, 1) * sin_prev


def _rope_tables(tab, reps):
    cos = tab[:, 0:LANES]
    sa = tab[:, LANES:2 * LANES]
    sb = tab[:, 2 * LANES:3 * LANES]
    if reps > 1:
        cos = jnp.concatenate([cos] * reps, axis=1)
        sa = jnp.concatenate([sa] * reps, axis=1)
        sb = jnp.concatenate([sb] * reps, axis=1)
    return cos, sa, sb


def _rope_tail(zt, tab):
    cos, sa, sb = _rope_tables(tab, 1)
    lane = lax.broadcasted_iota(I32, zt.shape, 1)
    first = lane < IDX_DIM
    return _rope(zt, jnp.where(first, cos, 1.0), jnp.where(first, sa, 0.0), jnp.where(first, sb, 0.0))


def _gelu_tanh(x):
    c = float(np.sqrt(2.0 / np.pi))
    return x * (0.5 * (1.0 + jnp.tanh(c * (x + 0.044715 * (x * x * x)))))


def _softplus(x):
    return jnp.maximum(x, 0.0) + jnp.log1p(jnp.exp(-jnp.abs(x)))


def _project_in(x, sh1, sc1, ln1, w_main, w_tail):
    h = _rms(x, ln1) * (1.0 + sc1) + sh1
    hb = h.astype(BF16)
    return _dot(hb, w_main), _dot(hb, w_tail)


def _lru_gates(xc, wa, ba, wx, bx, lam):
    xcb = xc.astype(BF16)
    r = jax.nn.sigmoid(_dot(xcb, wa) + ba)
    gi = jax.nn.sigmoid(_dot(xcb, wx) + bx)
    log_a = (-LRU_C * _softplus(-lam)) * r
    a = jnp.exp(log_a)
    inp = jnp.sqrt(jnp.tanh(-log_a) * (1.0 + a * a)) * gi * xc
    return a, inp


def _attention_inputs(z, zt, tab, qg, kg, ones_q, ones_k):
    cq, saq, sbq = _rope_tables(tab, ATT_W // LANES)
    ck, sak, sbk = _rope_tables(tab, KV_W // LANES)
    ci, sai, sbi = _rope_tables(tab, IDX_W // LANES)
    q = _rope(_head_rms(z[:, COL_Q:COL_K], ones_q, qg), cq, saq, sbq) * (HEAD_DIM ** -0.5)
    k = _rope(_head_rms(z[:, COL_K:COL_V], ones_k, kg), ck, sak, sbk)
    v = z[:, COL_V:COL_IQ]
    iq = _rope(z[:, COL_IQ:COL_IK], ci, sai, sbi)
    ik = _rope_tail(zt, tab)[:, 0:IDX_DIM]
    return q, k, v, iq, ik


def _ada_kernel(c_ref, w_ref, b_ref, o_ref):
    c = c_ref[...]
    s = c * jax.nn.sigmoid(c)
    sh, sl = _split_bf16(s)
    wh, wl = _split_bf16(w_ref[...])
    o_ref[...] = _dot(sh, wh) + _dot(sh, wl) + _dot(sl, wh) + b_ref[...]


def _ada(c_all, ada_w, ada_b):
    rows = c_all.shape[0]
    n = ada_w.shape[1]
    return pl.pallas_call(
        _ada_kernel,
        grid=(n // ADA_COL_TILE,),
        in_specs=[
            pl.BlockSpec((rows, D_MODEL), lambda j: (0, 0)),
            pl.BlockSpec((D_MODEL, ADA_COL_TILE), lambda j: (0, j)),
            pl.BlockSpec((1, ADA_COL_TILE), lambda j: (0, j)),
        ],
        out_specs=pl.BlockSpec((rows, ADA_COL_TILE), lambda j: (0, j)),
        out_shape=jax.ShapeDtypeStruct((rows, n), F32),
        compiler_params=pltpu.CompilerParams(dimension_semantics=("arbitrary",), vmem_limit_bytes=VMEM_LIMIT),
        name="ada_modulation",
    )(c_all, ada_w, ada_b)


def _prompt_stage_kernel(x_ref, mod_ref, ln1_ref, wmain_ref, wtail_ref, convw_ref, convb_ref,
                         wa_ref, ba_ref, wx_ref, bx_ref, lam_ref, qg_ref, kg_ref, gnl_ref,
                         onesq_ref, onesk_ref, rope_ref,
                         na_ref, q_ref, k_ref, v_ref, ik_ref, kb_ref, vb_ref,
                         iqh_ref, iql_ref, ikh_ref, ikl_ref, iw_ref, h_ref, conv_ref,
                         xr_buf, hcarry):
    tt = x_ref.shape[1]
    t = pl.program_id(1)

    @pl.when(t == 0)
    def _():
        xr_buf[0:SUBLANES, :] = jnp.zeros((SUBLANES, LRU_W), F32)
        hcarry[...] = jnp.zeros_like(hcarry)

    mod = mod_ref[0]
    z, zt = _project_in(x_ref[0], mod[:, 0:D_MODEL], mod[:, D_MODEL:2 * D_MODEL], ln1_ref[...],
                        wmain_ref[...], wtail_ref[...])

    xr = z[:, COL_XR:COL_GR]
    xr_buf[SUBLANES:SUBLANES + tt, :] = xr
    convw = convw_ref[...]
    xc = convb_ref[...]
    for i in range(CONV_W):
        xc = xc + xr_buf[pl.ds(SUBLANES - (CONV_W - 1) + i, tt), :] * convw[i:i + 1, :]
    conv_ref[0] = xr_buf[pl.ds(SUBLANES + tt - (CONV_W - 1), CONV_W - 1), :]
    xr_buf[0:SUBLANES, :] = xr_buf[tt:tt + SUBLANES, :]

    a, b = _lru_gates(xc, wa_ref[...], ba_ref[...], wx_ref[...], bx_ref[...], lam_ref[...])
    row = lax.broadcasted_iota(I32, a.shape, 0)
    d = 1
    while d < tt:
        keep = row >= d
        a_s = jnp.where(keep, pltpu.roll(a, d, 0), 1.0)
        b_s = jnp.where(keep, pltpu.roll(b, d, 0), 0.0)
        b = a * b_s + b
        a = a * a_s
        d *= 2
    hr = a * hcarry[...] + b
    hlast = hr[tt - 1:tt, :]
    hcarry[...] = hlast
    h_ref[0] = hlast

    na_ref[0] = _rms(_gelu_tanh(z[:, COL_GR:COL_Q]) * hr, gnl_ref[...]).astype(BF16)

    q, k, v, iq, ik = _attention_inputs(z, zt, rope_ref[...], qg_ref[...], kg_ref[...],
                                        onesq_ref[...], onesk_ref[...])
    q_ref[0] = q.astype(BF16)
    k_ref[0] = k
    v_ref[0] = v
    ik_ref[0] = ik
    kb_ref[0] = k.astype(BF16)
    vb_ref[0] = v.astype(BF16)
    iqh, iql = _split_bf16(iq)
    iqh_ref[0] = iqh
    iql_ref[0] = iql
    ikh, ikl = _split_bf16(ik)
    ikh_ref[0] = ikh
    ikl_ref[0] = ikl
    iw_ref[0] = zt


def _const_spec(shape):
    nd = len(shape)
    return pl.BlockSpec(shape, lambda *_: (0,) * nd)


def _prompt_stage(x, mod_p, p, rope_tab):
    bsz, t, _ = x.shape
    tt = PROMPT_ROW_TILE
    grid = (bsz, t // tt)

    def rows(w):
        return pl.BlockSpec((1, tt, w), lambda b, i: (b, i, 0))

    in_specs = [
        rows(D_MODEL),
        pl.BlockSpec((1, 1, 6 * D_MODEL), lambda b, i: (b, 0, 0)),
        _const_spec((1, D_MODEL)),
        _const_spec((D_MODEL, MAIN_W)),
        _const_spec((D_MODEL, TAIL_W)),
        _const_spec((CONV_W, LRU_W)),
        _const_spec((1, LRU_W)),
        _const_spec((LRU_W, LRU_W)),
        _const_spec((1, LRU_W)),
        _const_spec((LRU_W, LRU_W)),
        _const_spec((1, LRU_W)),
        _const_spec((1, LRU_W)),
        _const_spec((1, ATT_W)),
        _const_spec((1, KV_W)),
        _const_spec((1, LRU_W)),
        _const_spec((ATT_W, ATT_W)),
        _const_spec((KV_W, KV_W)),
        pl.BlockSpec((tt, 3 * LANES), lambda b, i: (i, 0)),
    ]
    out_specs = [
        rows(LRU_W), rows(ATT_W), rows(KV_W), rows(KV_W), rows(IDX_DIM), rows(KV_W), rows(KV_W),
        rows(IDX_W), rows(IDX_W), rows(IDX_DIM), rows(IDX_DIM), rows(TAIL_W),
        pl.BlockSpec((1, 1, LRU_W), lambda b, i: (b, 0, 0)),
        pl.BlockSpec((1, CONV_W - 1, LRU_W), lambda b, i: (b, 0, 0)),
    ]

    def sds(w, dt):
        return jax.ShapeDtypeStruct((bsz, t, w), dt)

    out_shape = [
        sds(LRU_W, BF16), sds(ATT_W, BF16), sds(KV_W, F32), sds(KV_W, F32), sds(IDX_DIM, F32),
        sds(KV_W, BF16), sds(KV_W, BF16), sds(IDX_W, BF16), sds(IDX_W, BF16), sds(IDX_DIM, BF16),
        sds(IDX_DIM, BF16), sds(TAIL_W, F32),
        jax.ShapeDtypeStruct((bsz, 1, LRU_W), F32),
        jax.ShapeDtypeStruct((bsz, CONV_W - 1, LRU_W), F32),
    ]
    return pl.pallas_call(
        _prompt_stage_kernel,
        grid=grid,
        in_specs=in_specs,
        out_specs=out_specs,
        out_shape=out_shape,
        scratch_shapes=[pltpu.VMEM((tt + SUBLANES, LRU_W), F32), pltpu.VMEM((1, LRU_W), F32)],
        compiler_params=pltpu.CompilerParams(dimension_semantics=("arbitrary", "arbitrary"),
                                             vmem_limit_bytes=VMEM_LIMIT),
        name="prompt_stage",
    )(x, mod_p, p["ln1"], p["w_main"], p["w_tail"], p["conv_w"], p["conv_b"], p["wa"], p["ba"],
      p["wx"], p["bx"], p["lam"], p["qg"], p["kg"], p["gnl"], p["ones_q"], p["ones_k"], rope_tab)


def _sortable_key(score):
    score = jnp.where(score == 0.0, 0.0, score)
    bits = lax.bitcast_convert_type(score, I32)
    return jnp.where(bits < 0, bits ^ jnp.int32(0x7FFFFFFF), bits)


def _kth_largest(key_ref, k, extra=None):
    def count_ge(c):
        cnt = jnp.sum(jnp.where(key_ref[...] >= c, 1.0, 0.0), axis=1, keepdims=True)
        if extra is not None:
            cnt = cnt + jnp.where(extra >= c, 1.0, 0.0)
        return cnt

    rows = key_ref.shape[0]
    zero = jnp.zeros((rows, 1), I32)
    t0 = jnp.where(count_ge(zero) >= k, zero, jnp.full((rows, 1), INT_MIN, I32))

    def body(i, t):
        c = t | lax.shift_left(jnp.int32(1), jnp.int32(30) - i)
        return jnp.where(count_ge(c) >= k, c, t)

    return lax.fori_loop(0, 31, body, t0)


def _selection_bias(key_ref, bias_ref, tri, k, extra=None):
    thr = _kth_largest(key_ref, k, extra)
    key = key_ref[...]
    n_gt = jnp.sum(jnp.where(key > thr, 1.0, 0.0), axis=1, keepdims=True)
    if extra is not None:
        n_gt = n_gt + jnp.where(extra > thr, 1.0, 0.0)
    need = k - n_gt
    offset = jnp.zeros_like(need)
    for c in range(key.shape[1] // TRI_W):
        kc = key[:, c * TRI_W:(c + 1) * TRI_W]
        eq = kc == thr
        e = jnp.where(eq, 1.0, 0.0)
        incl = _dot(e.astype(BF16), tri)
        rank = incl - e + offset
        tie = jnp.where(rank < need, 0.0, NEG_BIG)
        bias_ref[:, c * TRI_W:(c + 1) * TRI_W] = jnp.where(kc > thr, 0.0, jnp.where(eq, tie, NEG_BIG))
        offset = offset + incl[:, TRI_W - 1:TRI_W]
    if extra is None:
        return None
    tie = jnp.where(offset < need, 0.0, NEG_BIG)
    return jnp.where(extra > thr, 0.0, jnp.where(extra == thr, tie, NEG_BIG))


def _prompt_attn_kernel(q_ref, iqh_ref, iql_ref, iw_ref, kb_ref, vb_ref, ikh_ref, ikl_ref, gn_ref,
                        tri_ref, o_ref, key_ref, bias_ref):
    qb, n_keys = key_ref.shape
    qi = pl.program_id(1)
    ikh = ikh_ref[0]
    ikl = ikl_ref[0]
    iqh = iqh_ref[0]
    iql = iql_ref[0]
    iw = iw_ref[0]
    score = jnp.zeros((qb, n_keys), F32)
    for h in range(IDX_HEADS):
        ah = iqh[:, h * IDX_DIM:(h + 1) * IDX_DIM]
        al = iql[:, h * IDX_DIM:(h + 1) * IDX_DIM]
        s = _dot_nt(ah, ikh) + _dot_nt(ah, ikl) + _dot_nt(al, ikh)
        score = score + iw[:, IW_LANE + h:IW_LANE + h + 1] * jnp.maximum(s, 0.0)
    qpos = qi * qb + lax.broadcasted_iota(I32, (qb, n_keys), 0)
    kpos = lax.broadcasted_iota(I32, (qb, n_keys), 1)
    causal = kpos <= qpos
    key_ref[...] = jnp.where(causal, _sortable_key(score), INT_MIN)
    _selection_bias(key_ref, bias_ref, tri_ref[...], float(TOPK_MAX))
    bias_ref[...] = jnp.where(causal, bias_ref[...], NEG_BIG)

    q = q_ref[0]
    kb = kb_ref[0]
    vb = vb_ref[0]
    outs = []
    for h in range(N_HEADS):
        n = h // KV_GROUP
        logits = _dot_nt(q[:, h * HEAD_DIM:(h + 1) * HEAD_DIM], kb[:, n * HEAD_DIM:(n + 1) * HEAD_DIM]) + bias_ref[...]
        m = jnp.max(logits, axis=1, keepdims=True)
        p = jnp.exp(logits - m)
        den = jnp.sum(p, axis=1, keepdims=True)
        outs.append(_dot(p.astype(BF16), vb[:, n * HEAD_DIM:(n + 1) * HEAD_DIM]) / den)
    o_ref[0] = _rms(jnp.concatenate(outs, axis=1), gn_ref[...]).astype(BF16)


def _prompt_attention(q, iqh, iql, iw, kb, vb, ikh, ikl, gn_att, tri):
    bsz, t, _ = q.shape
    qb = Q_BLOCK

    def qrows(w):
        return pl.BlockSpec((1, qb, w), lambda b, i: (b, i, 0))

    def keys(w):
        return pl.BlockSpec((1, t, w), lambda b, i: (b, 0, 0))

    return pl.pallas_call(
        _prompt_attn_kernel,
        grid=(bsz, t // qb),
        in_specs=[qrows(ATT_W), qrows(IDX_W), qrows(IDX_W), qrows(TAIL_W), keys(KV_W), keys(KV_W),
                  keys(IDX_DIM), keys(IDX_DIM), _const_spec((1, ATT_W)), _const_spec((TRI_W, TRI_W))],
        out_specs=qrows(ATT_W),
        out_shape=jax.ShapeDtypeStruct((bsz, t, ATT_W), BF16),
        scratch_shapes=[pltpu.VMEM((qb, t), I32), pltpu.VMEM((qb, t), F32)],
        compiler_params=pltpu.CompilerParams(dimension_semantics=("arbitrary", "arbitrary"),
                                             vmem_limit_bytes=VMEM_LIMIT),
        name="prompt_attention",
    )(q, iqh, iql, iw, kb, vb, ikh, ikl, gn_att, tri)


def _post_kernel(x_ref, na_ref, nb_ref, mod_ref, woa_ref, wob_ref, ln2_ref, wg_ref, wu_ref, wd_ref,
                 y_ref, *, shared_mod):
    mod = mod_ref[0] if shared_mod else mod_ref[...]
    g1 = mod[:, 2 * D_MODEL:3 * D_MODEL]
    sh2 = mod[:, 3 * D_MODEL:4 * D_MODEL]
    sc2 = mod[:, 4 * D_MODEL:5 * D_MODEL]
    g2 = mod[:, 5 * D_MODEL:6 * D_MODEL]
    x = x_ref[...]
    mix = _dot(na_ref[...], woa_ref[...]) + _dot(nb_ref[...], wob_ref[...])
    x1 = x + g1 * mix
    h2 = (_rms(x1, ln2_ref[...]) * (1.0 + sc2) + sh2).astype(BF16)
    ff = jnp.zeros_like(x1)
    for c in range(D_FF // FF_CHUNK):
        lo, hi = c * FF_CHUNK, (c + 1) * FF_CHUNK
        g = _dot(h2, wg_ref[:, lo:hi])
        u = _dot(h2, wu_ref[:, lo:hi])
        act = (g * jax.nn.sigmoid(g) * u).astype(BF16)
        ff = ff + _dot(act, wd_ref[lo:hi, :])
    y_ref[...] = x1 + g2 * ff


def _post(x, na, nb, mod, p, rows_per_mod, tile):
    n = x.shape[0]
    shared = rows_per_mod > 1
    if shared:
        mod_spec = pl.BlockSpec((1, 1, 6 * D_MODEL), lambda i: (i * tile // rows_per_mod, 0, 0))
    else:
        mod_spec = pl.BlockSpec((tile, 6 * D_MODEL), lambda i: (i, 0))

    def rows(w):
        return pl.BlockSpec((tile, w), lambda i: (i, 0))

    def weight(shape):
        return pl.BlockSpec(shape, lambda i: (0, 0), pipeline_mode=pl.Buffered(1))

    return pl.pallas_call(
        functools.partial(_post_kernel, shared_mod=shared),
        grid=(n // tile,),
        in_specs=[rows(D_MODEL), rows(LRU_W), rows(ATT_W), mod_spec,
                  weight((LRU_W, D_MODEL)), weight((ATT_W, D_MODEL)), weight((1, D_MODEL)),
                  weight((D_MODEL, D_FF)), weight((D_MODEL, D_FF)), weight((D_FF, D_MODEL))],
        out_specs=rows(D_MODEL),
        out_shape=jax.ShapeDtypeStruct((n, D_MODEL), F32),
        compiler_params=pltpu.CompilerParams(dimension_semantics=("arbitrary",), vmem_limit_bytes=VMEM_LIMIT),
        name="post_ffn",
    )(x, na, nb, mod, p["wo_a"], p["wo_b"], p["ln2"], p["wg"], p["wu"], p["wd"])


def _sample_stage_kernel(x_ref, mod_ref, ln1_ref, wmain_ref, wtail_ref, convw_ref, convb_ref,
                         wa_ref, ba_ref, wx_ref, bx_ref, lam_ref, qg_ref, kg_ref, gnl_ref,
                         onesq_ref, onesk_ref, rope_ref, sconv_ref, sh_ref,
                         na_ref, q_ref, k_ref, v_ref, ik_ref, iq_ref, iw_ref, h_ref, conv_ref):
    mod = mod_ref[...]
    z, zt = _project_in(x_ref[...], mod[:, 0:D_MODEL], mod[:, D_MODEL:2 * D_MODEL], ln1_ref[...],
                        wmain_ref[...], wtail_ref[...])
    xr = z[:, COL_XR:COL_GR]
    convw = convw_ref[...]
    sconv = sconv_ref[...]
    xc = convb_ref[...]
    for i in range(CONV_W - 1):
        xc = xc + sconv[:, i * LRU_W:(i + 1) * LRU_W] * convw[i:i + 1, :]
    xc = xc + xr * convw[CONV_W - 1:CONV_W, :]
    conv_ref[:, 0:(CONV_W - 2) * LRU_W] = sconv[:, LRU_W:(CONV_W - 1) * LRU_W]
    conv_ref[:, (CONV_W - 2) * LRU_W:(CONV_W - 1) * LRU_W] = xr

    a, b = _lru_gates(xc, wa_ref[...], ba_ref[...], wx_ref[...], bx_ref[...], lam_ref[...])
    hr = a * sh_ref[...] + b
    h_ref[...] = hr
    na_ref[...] = _rms(_gelu_tanh(z[:, COL_GR:COL_Q]) * hr, gnl_ref[...]).astype(BF16)

    q, k, v, iq, ik = _attention_inputs(z, zt, rope_ref[...], qg_ref[...], kg_ref[...],
                                        onesq_ref[...], onesk_ref[...])
    q_ref[...] = q
    k_ref[...] = k
    v_ref[...] = v
    ik_ref[...] = ik
    iq_ref[...] = iq
    iw_ref[...] = zt


def _sample_stage(x, mod_s, p, rope_row, sconv, sh):
    n = x.shape[0]

    def out(w, dt=F32):
        return jax.ShapeDtypeStruct((n, w), dt)

    return pl.pallas_call(
        _sample_stage_kernel,
        out_shape=[out(LRU_W, BF16), out(ATT_W), out(KV_W), out(KV_W), out(IDX_DIM), out(IDX_W),
                   out(TAIL_W), out(LRU_W), out((CONV_W - 1) * LRU_W)],
        compiler_params=pltpu.CompilerParams(vmem_limit_bytes=VMEM_LIMIT),
        name="sample_stage",
    )(x, mod_s, p["ln1"], p["w_main"], p["w_tail"], p["conv_w"], p["conv_b"], p["wa"], p["ba"],
      p["wx"], p["bx"], p["lam"], p["qg"], p["kg"], p["gnl"], p["ones_q"], p["ones_k"], rope_row, sconv, sh)


def _sample_score_kernel(pt_ref, iq_ref, iw_ref, *rest):
    n_pages = len(rest) - 1
    page_refs, o_ref = rest[:n_pages], rest[n_pages]
    iqh, iql = _split_bf16(iq_ref[0])
    iw = iw_ref[0]
    for pg in range(n_pages):
        ikh, ikl = _split_bf16(page_refs[pg][...])
        s = _dot_nt(iqh, ikh) + _dot_nt(iqh, ikl) + _dot_nt(iql, ikh)
        o_ref[0, :, pg * PAGE_SIZE:(pg + 1) * PAGE_SIZE] = jnp.sum(iw * jnp.maximum(s, 0.0), axis=0, keepdims=True)


def _page_spec(width, pg):
    return pl.BlockSpec((None, None, PAGE_SIZE, width), lambda s, pt: (0, pt[s, pg], 0, 0))


def _sample_scores(page_table, iq3, iw3, cache_ik):
    n, n_pages = page_table.shape
    grid_spec = pltpu.PrefetchScalarGridSpec(
        num_scalar_prefetch=1,
        grid=(n,),
        in_specs=[pl.BlockSpec((1, IDX_HEADS, IDX_DIM), lambda s, pt: (s, 0, 0)),
                  pl.BlockSpec((1, IDX_HEADS, 1), lambda s, pt: (s, 0, 0))]
                 + [_page_spec(IDX_DIM, pg) for pg in range(n_pages)],
        out_specs=pl.BlockSpec((1, 1, n_pages * PAGE_SIZE), lambda s, pt: (s, 0, 0)),
    )
    return pl.pallas_call(
        _sample_score_kernel,
        grid_spec=grid_spec,
        out_shape=jax.ShapeDtypeStruct((n, 1, n_pages * PAGE_SIZE), F32),
        compiler_params=pltpu.CompilerParams(dimension_semantics=("arbitrary",), vmem_limit_bytes=VMEM_LIMIT),
        name="sample_scores",
    )(page_table, iq3, iw3, *([cache_ik] * n_pages))


def _sample_select_kernel(score_ref, iq_ref, ik_ref, iw_ref, tri_ref, bias_ref, bias_new_ref, key_ref):
    iq = iq_ref[...]
    ik = ik_ref[...]
    iw = iw_ref[...]
    new = jnp.zeros((iq.shape[0], 1), F32)
    for h in range(IDX_HEADS):
        s = jnp.sum(iq[:, h * IDX_DIM:(h + 1) * IDX_DIM] * ik, axis=1, keepdims=True)
        new = new + iw[:, IW_LANE + h:IW_LANE + h + 1] * jnp.maximum(s, 0.0)
    key_ref[...] = _sortable_key(score_ref[...])
    b_new = _selection_bias(key_ref, bias_ref, tri_ref[...], float(TOPK_MAX), extra=_sortable_key(new))
    bias_new_ref[...] = jnp.broadcast_to(b_new, bias_new_ref.shape)


def _sample_select(score, iq, ik, iw, tri):
    n, n_keys = score.shape
    return pl.pallas_call(
        _sample_select_kernel,
        out_shape=[jax.ShapeDtypeStruct((n, n_keys), F32), jax.ShapeDtypeStruct((n, LANES), F32)],
        scratch_shapes=[pltpu.VMEM((n, n_keys), I32)],
        compiler_params=pltpu.CompilerParams(vmem_limit_bytes=VMEM_LIMIT),
        name="sample_select",
    )(score, iq, ik, iw, tri)


def _sample_attn_kernel(pt_ref, q_ref, kn_ref, vn_ref, bias_ref, bnew_ref, gn_ref, *rest):
    n_pages = (len(rest) - 1) // 2
    k_refs, v_refs, o_ref = rest[:n_pages], rest[n_pages:2 * n_pages], rest[2 * n_pages]
    q = q_ref[0]
    qb = q.astype(BF16)
    row = lax.broadcasted_iota(I32, (N_HEADS, HEAD_DIM), 0)
    first = row < KV_GROUP
    bias = bias_ref[0]
    logits = []
    for pg in range(n_pages):
        kp = k_refs[pg][...].astype(BF16)
        l0 = _dot_nt(qb, kp[:, 0:HEAD_DIM])
        l1 = _dot_nt(qb, kp[:, HEAD_DIM:KV_W])
        first_l = lax.broadcasted_iota(I32, l0.shape, 0) < KV_GROUP
        logits.append(jnp.where(first_l, l0, l1) + bias[:, pg * PAGE_SIZE:(pg + 1) * PAGE_SIZE])
    kn = kn_ref[0]
    vn = vn_ref[0]
    k_sel = jnp.where(first, kn[:, 0:HEAD_DIM], kn[:, HEAD_DIM:KV_W])
    v_sel = jnp.where(first, vn[:, 0:HEAD_DIM], vn[:, HEAD_DIM:KV_W])
    l_new = jnp.sum(q * k_sel, axis=1, keepdims=True) + bnew_ref[0][:, 0:1]
    m = l_new
    for l in logits:
        m = jnp.maximum(m, jnp.max(l, axis=1, keepdims=True))
    p_new = jnp.exp(l_new - m)
    den = p_new
    acc = p_new * v_sel
    for pg in range(n_pages):
        p = jnp.exp(logits[pg] - m)
        den = den + jnp.sum(p, axis=1, keepdims=True)
        o = _dot(p.astype(BF16), v_refs[pg][...].astype(BF16))
        acc = acc + jnp.where(first, o[:, 0:HEAD_DIM], o[:, HEAD_DIM:KV_W])
    out = acc / den
    ms = jnp.sum(jnp.sum(out * out, axis=1, keepdims=True), axis=0, keepdims=True) * (1.0 / ATT_W)
    o_ref[0] = (out * lax.rsqrt(ms + EPS) * gn_ref[...]).astype(BF16)


def _sample_attention(page_table, q3, k_new, v_new, bias, bias_new, gn8, cache_k, cache_v):
    n, n_pages = page_table.shape
    n_keys = n_pages * PAGE_SIZE

    def per_seq(shape):
        return pl.BlockSpec((1,) + shape, lambda s, pt: (s, 0, 0))

    grid_spec = pltpu.PrefetchScalarGridSpec(
        num_scalar_prefetch=1,
        grid=(n,),
        in_specs=[per_seq((N_HEADS, HEAD_DIM)), per_seq((1, KV_W)), per_seq((1, KV_W)),
                  per_seq((1, n_keys)), per_seq((1, LANES)),
                  pl.BlockSpec((N_HEADS, HEAD_DIM), lambda s, pt: (0, 0))]
                 + [_page_spec(KV_W, pg) for pg in range(n_pages)] * 2,
        out_specs=per_seq((N_HEADS, HEAD_DIM)),
    )
    return pl.pallas_call(
        _sample_attn_kernel,
        grid_spec=grid_spec,
        out_shape=jax.ShapeDtypeStruct((n, N_HEADS, HEAD_DIM), BF16),
        compiler_params=pltpu.CompilerParams(dimension_semantics=("arbitrary",), vmem_limit_bytes=VMEM_LIMIT),
        name="sample_attention",
    )(page_table, q3, k_new, v_new, bias, bias_new, gn8, *([cache_k] * n_pages), *([cache_v] * n_pages))


def _rope_table_np(positions):
    half = ROT_DIM // 2
    freq = ROPE_THETA ** (-(np.arange(half, dtype=np.float64) / half))
    ang = np.asarray(positions, np.float64)[:, None] * freq[None, :]
    cos, sin = np.cos(ang), np.sin(ang)
    n = len(positions)
    tab = np.zeros((n, 3, LANES), np.float64)
    tab[:, 0, :] = 1.0
    for base in range(0, LANES, HEAD_DIM):
        tab[:, 0, base:base + half] = cos
        tab[:, 0, base + half:base + ROT_DIM] = cos
        tab[:, 1, base:base + half] = -sin
        tab[:, 2, base + half:base + ROT_DIM] = sin
    return tab.reshape(n, 3 * LANES).astype(np.float32)


def _block_diag(w):
    n, a, b = w.shape
    return jnp.einsum("nij,nm->nimj", w, jnp.eye(n, dtype=w.dtype)).reshape(n * a, n * b)


def _layer_params(l, ln1_g, w_in, conv_w, conv_b, lru_wa, lru_ba, lru_wx, lru_bx, lru_lambda, q_norm_g,
                  k_norm_g, gn_lru_g, gn_att_g, w_out, ln2_g, w_gate, w_up, w_down):
    w_tail = jnp.pad(w_in[l][:, COL_IK:], ((0, 0), (0, TAIL_W - (w_in.shape[2] - COL_IK))))
    head_id = np.arange(ATT_W) // HEAD_DIM
    ones_q = (head_id[:, None] == head_id[None, :]).astype(np.float32)
    return {
        "ln1": ln1_g[l][None, :],
        "w_main": w_in[l][:, :COL_IK].astype(BF16),
        "w_tail": w_tail.astype(BF16),
        "conv_w": conv_w[l],
        "conv_b": conv_b[l][None, :],
        "wa": _block_diag(lru_wa[l]).astype(BF16),
        "ba": lru_ba[l][None, :],
        "wx": _block_diag(lru_wx[l]).astype(BF16),
        "bx": lru_bx[l][None, :],
        "lam": lru_lambda[l][None, :],
        "qg": jnp.tile(q_norm_g[l], N_HEADS)[None, :],
        "kg": jnp.tile(k_norm_g[l], N_KV_HEADS)[None, :],
        "gnl": gn_lru_g[l][None, :],
        "gna": gn_att_g[l][None, :],
        "gna8": gn_att_g[l].reshape(N_HEADS, HEAD_DIM),
        "ones_q": jnp.asarray(ones_q, BF16),
        "ones_k": jnp.asarray(ones_q[:KV_W, :KV_W], BF16),
        "wo_a": w_out[l][:LRU_W].astype(BF16),
        "wo_b": w_out[l][LRU_W:].astype(BF16),
        "ln2": ln2_g[l][None, :],
        "wg": w_gate[l].astype(BF16),
        "wu": w_up[l].astype(BF16),
        "wd": w_down[l].astype(BF16),
    }


def kernel(x_prompt, x_sample, cache_k, cache_v, cache_ik, state_h, state_conv, page_table, c_prompt, c_sample, ada_w, ada_b, ln1_g, w_in, conv_w, conv_b, lru_wa, lru_ba, lru_wx, lru_bx, lru_lambda, q_norm_g, k_norm_g, gn_lru_g, gn_att_g, w_out, ln2_g, w_gate, w_up, w_down):
    bsz, t, _ = x_prompt.shape
    dbsz, dt, _ = x_sample.shape
    depth = ada_w.shape[0]
    n_pages = page_table.shape[1]
    past_len = n_pages * PAGE_SIZE
    n_pool = cache_k.shape[1]
    assert dt == 1 and t % PROMPT_ROW_TILE == 0 and t % POST_ROW_TILE == 0 and t // 4 >= TOPK_MAX
    assert (past_len + dt) // 4 >= TOPK_MAX

    rope_p = jnp.asarray(_rope_table_np(np.arange(t)))
    rope_s = jnp.asarray(_rope_table_np(past_len + np.arange(dt)))
    tri = jnp.asarray(np.triu(np.ones((TRI_W, TRI_W), np.float32)), BF16)

    yp = x_prompt.reshape(bsz * t, D_MODEL)
    ys = x_sample.reshape(dbsz, D_MODEL)
    c_all = jnp.concatenate([c_prompt, c_sample], axis=0)
    outs_p, outs_s = [], []
    for l in range(depth):
        p = _layer_params(l, ln1_g, w_in, conv_w, conv_b, lru_wa, lru_ba, lru_wx, lru_bx, lru_lambda,
                          q_norm_g, k_norm_g, gn_lru_g, gn_att_g, w_out, ln2_g, w_gate, w_up, w_down)
        mod = _ada(c_all, ada_w[l], ada_b[l][None, :])
        mod_p = mod[:bsz].reshape(bsz, 1, 6 * D_MODEL)
        mod_s = mod[bsz:]

        (na, q, k, v, ik, kb, vb, iqh, iql, ikh, ikl, iw, h_last, conv_new) = _prompt_stage(
            yp.reshape(bsz, t, D_MODEL), mod_p, p, rope_p)
        nb = _prompt_attention(q, iqh, iql, iw, kb, vb, ikh, ikl, p["gna"], tri)
        yp = _post(yp, na.reshape(bsz * t, LRU_W), nb.reshape(bsz * t, ATT_W), mod_p, p, t, POST_ROW_TILE)
        outs_p.append((k.reshape(bsz, t, N_KV_HEADS, HEAD_DIM), v.reshape(bsz, t, N_KV_HEADS, HEAD_DIM), ik,
                       h_last.reshape(bsz, LRU_W), conv_new))

        (na_s, q_s, k_s, v_s, ik_s, iq_s, iw_s, h_s, conv_s) = _sample_stage(
            ys, mod_s, p, rope_s, state_conv[l].reshape(dbsz, (CONV_W - 1) * LRU_W), state_h[l])
        score = _sample_scores(page_table, iq_s.reshape(dbsz, IDX_HEADS, IDX_DIM),
                               iw_s[:, IW_LANE:IW_LANE + IDX_HEADS].reshape(dbsz, IDX_HEADS, 1),
                               cache_ik[l].reshape(1, n_pool, PAGE_SIZE, IDX_DIM))
        bias, bias_new = _sample_select(score.reshape(dbsz, past_len), iq_s, ik_s, iw_s, tri)
        nb_s = _sample_attention(page_table, q_s.reshape(dbsz, N_HEADS, HEAD_DIM),
                                 k_s.reshape(dbsz, 1, KV_W), v_s.reshape(dbsz, 1, KV_W),
                                 bias.reshape(dbsz, 1, past_len), bias_new.reshape(dbsz, 1, LANES), p["gna8"],
                                 cache_k[l].reshape(1, n_pool, PAGE_SIZE, KV_W),
                                 cache_v[l].reshape(1, n_pool, PAGE_SIZE, KV_W))
        ys = _post(ys, na_s, nb_s.reshape(dbsz, ATT_W), mod_s, p, 1, dbsz)
        outs_s.append((k_s.reshape(dbsz, dt, N_KV_HEADS, HEAD_DIM), v_s.reshape(dbsz, dt, N_KV_HEADS, HEAD_DIM),
                       ik_s.reshape(dbsz, dt, IDX_DIM), h_s,
                       conv_s.reshape(dbsz, CONV_W - 1, LRU_W)))

    def stack(outs, i):
        return jnp.stack([o[i] for o in outs])

    return (yp.reshape(bsz, t, D_MODEL), ys.reshape(dbsz, dt, D_MODEL),
            stack(outs_p, 0), stack(outs_p, 1), stack(outs_p, 2), stack(outs_p, 3), stack(outs_p, 4),
            stack(outs_s, 0), stack(outs_s, 1), stack(outs_s, 2), stack(outs_s, 3), stack(outs_s, 4))
```

```python
import functools

import numpy as np
import jax
import jax.numpy as jnp
from jax import lax
from jax.experimental import pallas as pl
from jax.experimental.pallas import tpu as pltpu

F32 = jnp.float32
BF16 = jnp.bfloat16
I32 = jnp.int32

D_MODEL = 1024
LRU_W = 512
LRU_BLOCKS = 8
LRU_BW = LRU_W // LRU_BLOCKS
CONV_W = 4
LRU_C = 8.0
N_HEADS = 8
HEAD_DIM = 64
ATT_W = N_HEADS * HEAD_DIM
N_KV_HEADS = 2
KV_GROUP = N_HEADS // N_KV_HEADS
KV_W = N_KV_HEADS * HEAD_DIM
ROT_DIM = HEAD_DIM // 4
ROPE_THETA = 500000.0
IDX_HEADS = 4
IDX_DIM = 64
IDX_W = IDX_HEADS * IDX_DIM
TOPK_MAX = 256
PAGE_SIZE = 128
D_FF = 2816
EPS = 1e-6

COL_XR = 0
COL_GR = COL_XR + LRU_W
COL_Q = COL_GR + LRU_W
COL_K = COL_Q + ATT_W
COL_V = COL_K + KV_W
COL_IQ = COL_V + KV_W
COL_IK = COL_IQ + IDX_W
MAIN_W = COL_IK
TAIL_W = 128
IW_LANE = IDX_DIM
IDX_CAT = 3 * IDX_DIM

LANES = 128
SUBLANES = 8
TRI_W = 256
INT_MIN = -2147483648
NEG_BIG = -1e30
VMEM_LIMIT = 48 * 1024 * 1024

PROMPT_ROW_TILE = 256
Q_BLOCK = 256
POST_ROW_TILE = 512
FF_CHUNK = D_FF // 2
ADA_COL_TILE = 512


def _dot(a, b):
    return jnp.dot(a, b, preferred_element_type=F32)


def _dot_nt(a, b):
    return lax.dot_general(a, b, (((1,), (1,)), ((), ())), preferred_element_type=F32)


def _split_bf16(x):
    hi = x.astype(BF16)
    lo = (x - hi.astype(F32)).astype(BF16)
    return hi, lo


def _rms(x, g):
    return x * lax.rsqrt(jnp.mean(x * x, axis=-1, keepdims=True) + EPS) * g


def _head_rms(x, ones_bd, g):
    hi, lo = _split_bf16(x * x)
    ss = _dot(hi, ones_bd) + _dot(lo, ones_bd)
    return x * lax.rsqrt(ss * (1.0 / HEAD_DIM) + EPS) * g


def _rope(x, cos, sin_next, sin_prev):
    w = x.shape[-1]
    half = ROT_DIM // 2
    return x * cos + pltpu.roll(x, w - half, 1) * sin_next + pltpu.roll(x, half, 1) * sin_prev


def _rope_tables(tab, reps):
    cos = tab[:, 0:LANES]
    sa = tab[:, LANES:2 * LANES]
    sb = tab[:, 2 * LANES:3 * LANES]
    if reps > 1:
        cos = jnp.concatenate([cos] * reps, axis=1)
        sa = jnp.concatenate([sa] * reps, axis=1)
        sb = jnp.concatenate([sb] * reps, axis=1)
    return cos, sa, sb


def _rope_tail(zt, tab):
    cos, sa, sb = _rope_tables(tab, 1)
    lane = lax.broadcasted_iota(I32, zt.shape, 1)
    first = lane < IDX_DIM
    return _rope(zt, jnp.where(first, cos, 1.0), jnp.where(first, sa, 0.0), jnp.where(first, sb, 0.0))


def _gelu_tanh(x):
    c = float(np.sqrt(2.0 / np.pi))
    return x * (0.5 * (1.0 + jnp.tanh(c * (x + 0.044715 * (x * x * x)))))


def _softplus(x):
    return jnp.maximum(x, 0.0) + jnp.log1p(jnp.exp(-jnp.abs(x)))


def _project_in(x, sh1, sc1, ln1, w_main, w_tail):
    h = _rms(x, ln1) * (1.0 + sc1) + sh1
    hb = h.astype(BF16)
    return _dot(hb, w_main), _dot(hb, w_tail)


def _lru_gates(xc, wa, ba, wx, bx, lam):
    xcb = xc.astype(BF16)
    r = jax.nn.sigmoid(_dot(xcb, wa) + ba)
    gi = jax.nn.sigmoid(_dot(xcb, wx) + bx)
    log_a = (-LRU_C * _softplus(-lam)) * r
    a = jnp.exp(log_a)
    inp = jnp.sqrt(jnp.tanh(-log_a) * (1.0 + a * a)) * gi * xc
    return a, inp


def _attention_inputs(z, zt, tab, qg, kg, ones_q, ones_k):
    cq, saq, sbq = _rope_tables(tab, ATT_W // LANES)
    ck, sak, sbk = _rope_tables(tab, KV_W // LANES)
    ci, sai, sbi = _rope_tables(tab, IDX_W // LANES)
    q = _rope(_head_rms(z[:, COL_Q:COL_K], ones_q, qg), cq, saq, sbq) * (HEAD_DIM ** -0.5)
    k = _rope(_head_rms(z[:, COL_K:COL_V], ones_k, kg), ck, sak, sbk)
    v = z[:, COL_V:COL_IQ]
    iq = _rope(z[:, COL_IQ:COL_IK], ci, sai, sbi)
    tail = _rope_tail(zt, tab)
    return q, k, v, iq, tail


def _ada_kernel(c_ref, w_ref, b_ref, o_ref):
    c = c_ref[...]
    s = c * jax.nn.sigmoid(c)
    sh, sl = _split_bf16(s)
    wh, wl = _split_bf16(w_ref[...])
    o_ref[...] = _dot(sh, wh) + _dot(sh, wl) + _dot(sl, wh) + b_ref[...]


def _ada(c_all, ada_w, ada_b):
    rows = c_all.shape[0]
    n = ada_w.shape[1]
    return pl.pallas_call(
        _ada_kernel,
        grid=(n // ADA_COL_TILE,),
        in_specs=[
            pl.BlockSpec((rows, D_MODEL), lambda j: (0, 0)),
            pl.BlockSpec((D_MODEL, ADA_COL_TILE), lambda j: (0, j)),
            pl.BlockSpec((1, ADA_COL_TILE), lambda j: (0, j)),
        ],
        out_specs=pl.BlockSpec((rows, ADA_COL_TILE), lambda j: (0, j)),
        out_shape=jax.ShapeDtypeStruct((rows, n), F32),
        compiler_params=pltpu.CompilerParams(dimension_semantics=("arbitrary",), vmem_limit_bytes=VMEM_LIMIT),
        name="ada_modulation",
    )(c_all, ada_w, ada_b)


def _prompt_stage_kernel(x_ref, mod_ref, ln1_ref, wmain_ref, wtail_ref, convw_ref, convb_ref,
                         wa_ref, ba_ref, wx_ref, bx_ref, lam_ref, qg_ref, kg_ref, gnl_ref,
                         onesq_ref, onesk_ref, rope_ref,
                         na_ref, q_ref, kt_ref, vt_ref, ikt_ref, ktb_ref, vb_ref,
                         ikcat_ref, iqcat_ref, iw_ref, h_ref, conv_ref,
                         xr_buf, hcarry):
    tt = x_ref.shape[1]
    t = pl.program_id(1)

    @pl.when(t == 0)
    def _():
        xr_buf[0:SUBLANES, :] = jnp.zeros((SUBLANES, LRU_W), F32)
        hcarry[...] = jnp.zeros_like(hcarry)

    mod = mod_ref[0]
    z, zt = _project_in(x_ref[0], mod[:, 0:D_MODEL], mod[:, D_MODEL:2 * D_MODEL], ln1_ref[...],
                        wmain_ref[...], wtail_ref[...])

    xr = z[:, COL_XR:COL_GR]
    xr_buf[SUBLANES:SUBLANES + tt, :] = xr
    convw = convw_ref[...]
    xc = convb_ref[...]
    for i in range(CONV_W):
        xc = xc + xr_buf[pl.ds(SUBLANES - (CONV_W - 1) + i, tt), :] * convw[i:i + 1, :]
    conv_ref[0] = xr_buf[pl.ds(SUBLANES + tt - (CONV_W - 1), CONV_W - 1), :]
    xr_buf[0:SUBLANES, :] = xr_buf[tt:tt + SUBLANES, :]

    a, b = _lru_gates(xc, wa_ref[...], ba_ref[...], wx_ref[...], bx_ref[...], lam_ref[...])
    row = lax.broadcasted_iota(I32, a.shape, 0)
    d = 1
    while d < tt:
        keep = row >= d
        a_s = jnp.where(keep, pltpu.roll(a, d, 0), 1.0)
        b_s = jnp.where(keep, pltpu.roll(b, d, 0), 0.0)
        b = a * b_s + b
        a = a * a_s
        d *= 2
    hr = a * hcarry[...] + b
    hlast = hr[tt - 1:tt, :]
    hcarry[...] = hlast
    h_ref[0] = hlast

    na_ref[0] = _rms(_gelu_tanh(z[:, COL_GR:COL_Q]) * hr, gnl_ref[...]).astype(BF16)

    q, k, v, iq, tail = _attention_inputs(z, zt, rope_ref[...], qg_ref[...], kg_ref[...],
                                          onesq_ref[...], onesk_ref[...])
    for h in range(N_HEADS):
        q_ref[0, h] = q[:, h * HEAD_DIM:(h + 1) * HEAD_DIM].astype(BF16)
    kt = jnp.transpose(k)
    vt = jnp.transpose(v)
    ikt = jnp.transpose(tail)[0:IDX_DIM, :]
    kt_ref[0] = kt.reshape(N_KV_HEADS, HEAD_DIM, tt)
    vt_ref[0] = vt.reshape(N_KV_HEADS, HEAD_DIM, tt)
    ikt_ref[0] = ikt
    ktb_ref[0] = kt.astype(BF16).reshape(N_KV_HEADS, HEAD_DIM, tt)
    for n in range(N_KV_HEADS):
        vb_ref[0, n] = v[:, n * HEAD_DIM:(n + 1) * HEAD_DIM].astype(BF16)
    ikh, ikl = _split_bf16(ikt)
    ikcat_ref[0] = jnp.concatenate([ikh, ikl, ikh], axis=0)
    iql = iq - iq.astype(BF16).astype(F32)
    for h in range(IDX_HEADS):
        qs = iq[:, h * IDX_DIM:(h + 1) * IDX_DIM]
        iqcat_ref[0, h] = jnp.concatenate([qs, qs, iql[:, h * IDX_DIM:(h + 1) * IDX_DIM]], axis=1).astype(BF16)
    iw_ref[0] = zt


def _const_spec(shape):
    nd = len(shape)
    return pl.BlockSpec(shape, lambda *_: (0,) * nd)


def _prompt_stage(x, mod_p, p, rope_tab):
    bsz, t, _ = x.shape
    tt = PROMPT_ROW_TILE
    grid = (bsz, t // tt)

    def rows(w):
        return pl.BlockSpec((1, tt, w), lambda b, i: (b, i, 0))

    in_specs = [
        rows(D_MODEL),
        pl.BlockSpec((1, 1, 6 * D_MODEL), lambda b, i: (b, 0, 0)),
        _const_spec((1, D_MODEL)),
        _const_spec((D_MODEL, MAIN_W)),
        _const_spec((D_MODEL, TAIL_W)),
        _const_spec((CONV_W, LRU_W)),
        _const_spec((1, LRU_W)),
        _const_spec((LRU_W, LRU_W)),
        _const_spec((1, LRU_W)),
        _const_spec((LRU_W, LRU_W)),
        _const_spec((1, LRU_W)),
        _const_spec((1, LRU_W)),
        _const_spec((1, ATT_W)),
        _const_spec((1, KV_W)),
        _const_spec((1, LRU_W)),
        _const_spec((ATT_W, ATT_W)),
        _const_spec((KV_W, KV_W)),
        pl.BlockSpec((tt, 3 * LANES), lambda b, i: (i, 0)),
    ]
    def heads(n, w):
        return pl.BlockSpec((1, n, tt, w), lambda b, i: (b, 0, i, 0))

    def heads_t(n):
        return pl.BlockSpec((1, n, HEAD_DIM, tt), lambda b, i: (b, 0, 0, i))

    def feat_t(w):
        return pl.BlockSpec((1, w, tt), lambda b, i: (b, 0, i))

    out_specs = [
        rows(LRU_W), heads(N_HEADS, HEAD_DIM), heads_t(N_KV_HEADS), heads_t(N_KV_HEADS), feat_t(IDX_DIM),
        heads_t(N_KV_HEADS), heads(N_KV_HEADS, HEAD_DIM), feat_t(IDX_CAT), heads(IDX_HEADS, IDX_CAT), rows(TAIL_W),
        pl.BlockSpec((1, 1, LRU_W), lambda b, i: (b, 0, 0)),
        pl.BlockSpec((1, CONV_W - 1, LRU_W), lambda b, i: (b, 0, 0)),
    ]
    out_shape = [
        jax.ShapeDtypeStruct((bsz, t, LRU_W), BF16),
        jax.ShapeDtypeStruct((bsz, N_HEADS, t, HEAD_DIM), BF16),
        jax.ShapeDtypeStruct((bsz, N_KV_HEADS, HEAD_DIM, t), F32),
        jax.ShapeDtypeStruct((bsz, N_KV_HEADS, HEAD_DIM, t), F32),
        jax.ShapeDtypeStruct((bsz, IDX_DIM, t), F32),
        jax.ShapeDtypeStruct((bsz, N_KV_HEADS, HEAD_DIM, t), BF16),
        jax.ShapeDtypeStruct((bsz, N_KV_HEADS, t, HEAD_DIM), BF16),
        jax.ShapeDtypeStruct((bsz, IDX_CAT, t), BF16),
        jax.ShapeDtypeStruct((bsz, IDX_HEADS, t, IDX_CAT), BF16),
        jax.ShapeDtypeStruct((bsz, t, TAIL_W), F32),
        jax.ShapeDtypeStruct((bsz, 1, LRU_W), F32),
        jax.ShapeDtypeStruct((bsz, CONV_W - 1, LRU_W), F32),
    ]
    return pl.pallas_call(
        _prompt_stage_kernel,
        grid=grid,
        in_specs=in_specs,
        out_specs=out_specs,
        out_shape=out_shape,
        scratch_shapes=[pltpu.VMEM((tt + SUBLANES, LRU_W), F32), pltpu.VMEM((1, LRU_W), F32)],
        compiler_params=pltpu.CompilerParams(dimension_semantics=("arbitrary", "arbitrary"),
                                             vmem_limit_bytes=VMEM_LIMIT),
        name="prompt_stage",
    )(x, mod_p, p["ln1"], p["w_main"], p["w_tail"], p["conv_w"], p["conv_b"], p["wa"], p["ba"],
      p["wx"], p["bx"], p["lam"], p["qg"], p["kg"], p["gnl"], p["ones_q"], p["ones_k"], rope_tab)


def _sortable_key(score):
    score = jnp.where(score == 0.0, 0.0, score)
    bits = lax.bitcast_convert_type(score, I32)
    return jnp.where(bits < 0, bits ^ jnp.int32(0x7FFFFFFF), bits)


def _kth_largest(key_ref, k, extra=None):
    def count_ge(c):
        cnt = jnp.sum(jnp.where(key_ref[...] >= c, 1.0, 0.0), axis=1, keepdims=True)
        if extra is not None:
            cnt = cnt + jnp.where(extra >= c, 1.0, 0.0)
        return cnt

    rows = key_ref.shape[0]
    zero = jnp.zeros((rows, 1), I32)
    t0 = jnp.where(count_ge(zero) >= k, zero, jnp.full((rows, 1), INT_MIN, I32))

    def body(i, t):
        c = t | lax.shift_left(jnp.int32(1), jnp.int32(30) - i)
        return jnp.where(count_ge(c) >= k, c, t)

    return lax.fori_loop(0, 31, body, t0)


def _selection_bias(key_ref, bias_ref, tri, k, extra=None):
    thr = _kth_largest(key_ref, k, extra)
    n_gt = jnp.sum(jnp.where(key_ref[...] > thr, 1.0, 0.0), axis=1, keepdims=True)
    if extra is not None:
        n_gt = n_gt + jnp.where(extra > thr, 1.0, 0.0)
    need = k - n_gt
    offset = jnp.zeros_like(need)
    for c in range(key_ref.shape[1] // TRI_W):
        kc = key_ref[:, c * TRI_W:(c + 1) * TRI_W]
        eq = kc == thr
        e = jnp.where(eq, 1.0, 0.0)
        incl = _dot(e.astype(BF16), tri)
        rank = incl - e + offset
        tie = jnp.where(rank < need, 0.0, NEG_BIG)
        bias_ref[:, c * TRI_W:(c + 1) * TRI_W] = jnp.where(kc > thr, 0.0, jnp.where(eq, tie, NEG_BIG))
        offset = offset + incl[:, TRI_W - 1:TRI_W]
    if extra is None:
        return None
    tie = jnp.where(offset < need, 0.0, NEG_BIG)
    return jnp.where(extra > thr, 0.0, jnp.where(extra == thr, tie, NEG_BIG))


def _prompt_attn_kernel(q_ref, iq_ref, iw_ref, kt_ref, v_ref, ikt_ref, gn_ref, tri_ref, o_ref,
                        key_ref, bias_ref, *, q_start):
    qb, n_keys = bias_ref.shape
    qpos = q_start + lax.broadcasted_iota(I32, (qb, n_keys), 0)
    kpos = lax.broadcasted_iota(I32, (qb, n_keys), 1)
    causal = kpos <= qpos
    if n_keys > TOPK_MAX:
        iw = iw_ref[0]
        ikt = ikt_ref[0]
        score = jnp.zeros((qb, n_keys), F32)
        for h in range(IDX_HEADS):
            score = score + iw[:, IW_LANE + h:IW_LANE + h + 1] * jnp.maximum(_dot(iq_ref[0, h], ikt), 0.0)
        key_ref[...] = jnp.where(causal, _sortable_key(score), INT_MIN)
        _selection_bias(key_ref, bias_ref, tri_ref[...], float(TOPK_MAX))
        bias_ref[...] = jnp.where(causal, bias_ref[...], NEG_BIG)
    else:
        bias_ref[...] = jnp.where(causal, 0.0, NEG_BIG)

    outs = []
    for h in range(N_HEADS):
        n = h // KV_GROUP
        logits = _dot(q_ref[0, h], kt_ref[0, n]) + bias_ref[...]
        m = jnp.max(logits, axis=1, keepdims=True)
        p = jnp.exp(logits - m)
        den = jnp.sum(p, axis=1, keepdims=True)
        outs.append(_dot(p.astype(BF16), v_ref[0, n]) / den)
    o_ref[0] = _rms(jnp.concatenate(outs, axis=1), gn_ref[...]).astype(BF16)


def _prompt_attention_block(qh, iqcat, iw, ktb, vb, ikcat, gn_att, tri, q_block):
    bsz, _, t, _ = qh.shape
    qb = Q_BLOCK
    n_keys = (q_block + 1) * qb

    def qheads(n, w):
        return pl.BlockSpec((1, n, qb, w), lambda b: (b, 0, q_block, 0))

    return pl.pallas_call(
        functools.partial(_prompt_attn_kernel, q_start=q_block * qb),
        grid=(bsz,),
        in_specs=[qheads(N_HEADS, HEAD_DIM), qheads(IDX_HEADS, IDX_CAT),
                  pl.BlockSpec((1, qb, TAIL_W), lambda b: (b, q_block, 0)),
                  pl.BlockSpec((1, N_KV_HEADS, HEAD_DIM, n_keys), lambda b: (b, 0, 0, 0)),
                  pl.BlockSpec((1, N_KV_HEADS, n_keys, HEAD_DIM), lambda b: (b, 0, 0, 0)),
                  pl.BlockSpec((1, IDX_CAT, n_keys), lambda b: (b, 0, 0)),
                  _const_spec((1, ATT_W)), _const_spec((TRI_W, TRI_W))],
        out_specs=pl.BlockSpec((1, qb, ATT_W), lambda b: (b, 0, 0)),
        out_shape=jax.ShapeDtypeStruct((bsz, qb, ATT_W), BF16),
        scratch_shapes=[pltpu.VMEM((qb, n_keys), I32), pltpu.VMEM((qb, n_keys), F32)],
        compiler_params=pltpu.CompilerParams(dimension_semantics=("arbitrary",), vmem_limit_bytes=VMEM_LIMIT),
        name=f"prompt_attention_{q_block}",
    )(qh, iqcat, iw, ktb, vb, ikcat, gn_att, tri)


def _prompt_attention(qh, iqcat, iw, ktb, vb, ikcat, gn_att, tri):
    t = qh.shape[2]
    blocks = [_prompt_attention_block(qh, iqcat, iw, ktb, vb, ikcat, gn_att, tri, i) for i in range(t // Q_BLOCK)]
    return jnp.concatenate(blocks, axis=1)


def _post_kernel(x_ref, na_ref, nb_ref, mod_ref, woa_ref, wob_ref, ln2_ref, wg_ref, wu_ref, wd_ref,
                 y_ref, *, shared_mod):
    mod = mod_ref[0] if shared_mod else mod_ref[...]
    g1 = mod[:, 2 * D_MODEL:3 * D_MODEL]
    sh2 = mod[:, 3 * D_MODEL:4 * D_MODEL]
    sc2 = mod[:, 4 * D_MODEL:5 * D_MODEL]
    g2 = mod[:, 5 * D_MODEL:6 * D_MODEL]
    x = x_ref[...]
    mix = _dot(na_ref[...], woa_ref[...]) + _dot(nb_ref[...], wob_ref[...])
    x1 = x + g1 * mix
    h2 = (_rms(x1, ln2_ref[...]) * (1.0 + sc2) + sh2).astype(BF16)
    ff = jnp.zeros_like(x1)
    for c in range(D_FF // FF_CHUNK):
        lo, hi = c * FF_CHUNK, (c + 1) * FF_CHUNK
        g = _dot(h2, wg_ref[:, lo:hi])
        u = _dot(h2, wu_ref[:, lo:hi])
        act = (g * jax.nn.sigmoid(g) * u).astype(BF16)
        ff = ff + _dot(act, wd_ref[lo:hi, :])
    y_ref[...] = x1 + g2 * ff


def _post(x, na, nb, mod, p, rows_per_mod, tile):
    n = x.shape[0]
    shared = rows_per_mod > 1
    if shared:
        mod_spec = pl.BlockSpec((1, 1, 6 * D_MODEL), lambda i: (i * tile // rows_per_mod, 0, 0))
    else:
        mod_spec = pl.BlockSpec((tile, 6 * D_MODEL), lambda i: (i, 0))

    def rows(w):
        return pl.BlockSpec((tile, w), lambda i: (i, 0))

    def weight(shape):
        return pl.BlockSpec(shape, lambda i: (0, 0), pipeline_mode=pl.Buffered(1))

    return pl.pallas_call(
        functools.partial(_post_kernel, shared_mod=shared),
        grid=(n // tile,),
        in_specs=[rows(D_MODEL), rows(LRU_W), rows(ATT_W), mod_spec,
                  weight((LRU_W, D_MODEL)), weight((ATT_W, D_MODEL)), weight((1, D_MODEL)),
                  weight((D_MODEL, D_FF)), weight((D_MODEL, D_FF)), weight((D_FF, D_MODEL))],
        out_specs=rows(D_MODEL),
        out_shape=jax.ShapeDtypeStruct((n, D_MODEL), F32),
        compiler_params=pltpu.CompilerParams(dimension_semantics=("arbitrary",), vmem_limit_bytes=VMEM_LIMIT),
        name="post_ffn",
    )(x, na, nb, mod, p["wo_a"], p["wo_b"], p["ln2"], p["wg"], p["wu"], p["wd"])


def _sample_stage_kernel(x_ref, mod_ref, ln1_ref, wmain_ref, wtail_ref, convw_ref, convb_ref,
                         wa_ref, ba_ref, wx_ref, bx_ref, lam_ref, qg_ref, kg_ref, gnl_ref,
                         onesq_ref, onesk_ref, rope_ref, sconv_ref, sh_ref,
                         na_ref, q_ref, k_ref, v_ref, ik_ref, iq_ref, iw_ref, h_ref, conv_ref):
    mod = mod_ref[...]
    z, zt = _project_in(x_ref[...], mod[:, 0:D_MODEL], mod[:, D_MODEL:2 * D_MODEL], ln1_ref[...],
                        wmain_ref[...], wtail_ref[...])
    xr = z[:, COL_XR:COL_GR]
    convw = convw_ref[...]
    sconv = sconv_ref[...]
    xc = convb_ref[...]
    for i in range(CONV_W - 1):
        xc = xc + sconv[:, i * LRU_W:(i + 1) * LRU_W] * convw[i:i + 1, :]
    xc = xc + xr * convw[CONV_W - 1:CONV_W, :]
    conv_ref[:, 0:(CONV_W - 2) * LRU_W] = sconv[:, LRU_W:(CONV_W - 1) * LRU_W]
    conv_ref[:, (CONV_W - 2) * LRU_W:(CONV_W - 1) * LRU_W] = xr

    a, b = _lru_gates(xc, wa_ref[...], ba_ref[...], wx_ref[...], bx_ref[...], lam_ref[...])
    hr = a * sh_ref[...] + b
    h_ref[...] = hr
    na_ref[...] = _rms(_gelu_tanh(z[:, COL_GR:COL_Q]) * hr, gnl_ref[...]).astype(BF16)

    q, k, v, iq, tail = _attention_inputs(z, zt, rope_ref[...], qg_ref[...], kg_ref[...],
                                          onesq_ref[...], onesk_ref[...])
    q_ref[...] = q
    k_ref[...] = k
    v_ref[...] = v
    ik_ref[...] = tail[:, 0:IDX_DIM]
    iq_ref[...] = iq
    iw_ref[...] = zt


def _sample_stage(x, mod_s, p, rope_row, sconv, sh):
    n = x.shape[0]

    def out(w, dt=F32):
        return jax.ShapeDtypeStruct((n, w), dt)

    return pl.pallas_call(
        _sample_stage_kernel,
        out_shape=[out(LRU_W, BF16), out(ATT_W), out(KV_W), out(KV_W), out(IDX_DIM), out(IDX_W),
                   out(TAIL_W), out(LRU_W), out((CONV_W - 1) * LRU_W)],
        compiler_params=pltpu.CompilerParams(vmem_limit_bytes=VMEM_LIMIT),
        name="sample_stage",
    )(x, mod_s, p["ln1"], p["w_main"], p["w_tail"], p["conv_w"], p["conv_b"], p["wa"], p["ba"],
      p["wx"], p["bx"], p["lam"], p["qg"], p["kg"], p["gnl"], p["ones_q"], p["ones_k"], rope_row, sconv, sh)


def _sample_score_kernel(pt_ref, iq_ref, iw_ref, *rest):
    n_pages = len(rest) - 1
    page_refs, o_ref = rest[:n_pages], rest[n_pages]
    iqh, iql = _split_bf16(iq_ref[0])
    iw = iw_ref[0]
    for pg in range(n_pages):
        ikh, ikl = _split_bf16(page_refs[pg][...])
        s = _dot(iqh, ikh) + _dot(iqh, ikl) + _dot(iql, ikh)
        o_ref[0, :, pg * PAGE_SIZE:(pg + 1) * PAGE_SIZE] = jnp.sum(iw * jnp.maximum(s, 0.0), axis=0, keepdims=True)


def _page_spec(lead, pg):
    zeros = (0,) * (len(lead) + 1)
    return pl.BlockSpec((None,) + lead + (PAGE_SIZE,), lambda s, pt: (pt[s, pg],) + zeros)


def _sample_scores(page_table, iq3, iw3, cache_ik):
    n, n_pages = page_table.shape
    grid_spec = pltpu.PrefetchScalarGridSpec(
        num_scalar_prefetch=1,
        grid=(n,),
        in_specs=[pl.BlockSpec((1, IDX_HEADS, IDX_DIM), lambda s, pt: (s, 0, 0)),
                  pl.BlockSpec((1, IDX_HEADS, 1), lambda s, pt: (s, 0, 0))]
                 + [_page_spec((IDX_DIM,), pg) for pg in range(n_pages)],
        out_specs=pl.BlockSpec((1, 1, n_pages * PAGE_SIZE), lambda s, pt: (s, 0, 0)),
    )
    return pl.pallas_call(
        _sample_score_kernel,
        grid_spec=grid_spec,
        out_shape=jax.ShapeDtypeStruct((n, 1, n_pages * PAGE_SIZE), F32),
        compiler_params=pltpu.CompilerParams(dimension_semantics=("arbitrary",), vmem_limit_bytes=VMEM_LIMIT),
        name="sample_scores",
    )(page_table, iq3, iw3, *([cache_ik] * n_pages))


def _sample_select_kernel(score_ref, iq_ref, ik_ref, iw_ref, tri_ref, bias_ref, bias_new_ref, key_ref):
    iq = iq_ref[...]
    ik = ik_ref[...]
    iw = iw_ref[...]
    new = jnp.zeros((iq.shape[0], 1), F32)
    for h in range(IDX_HEADS):
        s = jnp.sum(iq[:, h * IDX_DIM:(h + 1) * IDX_DIM] * ik, axis=1, keepdims=True)
        new = new + iw[:, IW_LANE + h:IW_LANE + h + 1] * jnp.maximum(s, 0.0)
    key_ref[...] = _sortable_key(score_ref[...])
    b_new = _selection_bias(key_ref, bias_ref, tri_ref[...], float(TOPK_MAX), extra=_sortable_key(new))
    bias_new_ref[...] = jnp.broadcast_to(b_new, bias_new_ref.shape)


def _sample_select(score, iq, ik, iw, tri):
    n, n_keys = score.shape
    return pl.pallas_call(
        _sample_select_kernel,
        out_shape=[jax.ShapeDtypeStruct((n, n_keys), F32), jax.ShapeDtypeStruct((n, LANES), F32)],
        scratch_shapes=[pltpu.VMEM((n, n_keys), I32)],
        compiler_params=pltpu.CompilerParams(vmem_limit_bytes=VMEM_LIMIT),
        name="sample_select",
    )(score, iq, ik, iw, tri)


def _sample_attn_kernel(pt_ref, q_ref, kn_ref, vn_ref, bias_ref, bnew_ref, gn_ref, *rest):
    n_pages = (len(rest) - 1) // 2
    k_refs, v_refs, o_ref = rest[:n_pages], rest[n_pages:2 * n_pages], rest[2 * n_pages]
    q = q_ref[0]
    qb = q.astype(BF16)
    row = lax.broadcasted_iota(I32, (N_HEADS, HEAD_DIM), 0)
    first = row < KV_GROUP
    bias = bias_ref[0]
    logits = []
    for pg in range(n_pages):
        kp = k_refs[pg][...].astype(BF16)
        l0 = _dot(qb, kp[0])
        l1 = _dot(qb, kp[1])
        first_l = lax.broadcasted_iota(I32, l0.shape, 0) < KV_GROUP
        logits.append(jnp.where(first_l, l0, l1) + bias[:, pg * PAGE_SIZE:(pg + 1) * PAGE_SIZE])
    kn = kn_ref[0]
    vn = vn_ref[0]
    k_sel = jnp.where(first, kn[:, 0:HEAD_DIM], kn[:, HEAD_DIM:KV_W])
    v_sel = jnp.where(first, vn[:, 0:HEAD_DIM], vn[:, HEAD_DIM:KV_W])
    l_new = jnp.sum(q * k_sel, axis=1, keepdims=True) + bnew_ref[0][:, 0:1]
    m = l_new
    for l in logits:
        m = jnp.maximum(m, jnp.max(l, axis=1, keepdims=True))
    p_new = jnp.exp(l_new - m)
    den = p_new
    acc = p_new * v_sel
    for pg in range(n_pages):
        p = jnp.exp(logits[pg] - m)
        den = den + jnp.sum(p, axis=1, keepdims=True)
        pb = p.astype(BF16)
        vp = v_refs[pg][...].astype(BF16)
        acc = acc + jnp.where(first, _dot_nt(pb, vp[0]), _dot_nt(pb, vp[1]))
    out = acc / den
    ms = jnp.sum(jnp.sum(out * out, axis=1, keepdims=True), axis=0, keepdims=True) * (1.0 / ATT_W)
    o_ref[0] = (out * lax.rsqrt(ms + EPS) * gn_ref[...]).astype(BF16)


def _sample_attention(page_table, q3, k_new, v_new, bias, bias_new, gn8, cache_k, cache_v):
    n, n_pages = page_table.shape
    n_keys = n_pages * PAGE_SIZE

    def per_seq(shape):
        return pl.BlockSpec((1,) + shape, lambda s, pt: (s, 0, 0))

    grid_spec = pltpu.PrefetchScalarGridSpec(
        num_scalar_prefetch=1,
        grid=(n,),
        in_specs=[per_seq((N_HEADS, HEAD_DIM)), per_seq((1, KV_W)), per_seq((1, KV_W)),
                  per_seq((1, n_keys)), per_seq((1, LANES)),
                  pl.BlockSpec((N_HEADS, HEAD_DIM), lambda s, pt: (0, 0))]
                 + [_page_spec((N_KV_HEADS, HEAD_DIM), pg) for pg in range(n_pages)]
                 + [_page_spec((N_KV_HEADS, HEAD_DIM), pg) for pg in range(n_pages)],
        out_specs=per_seq((N_HEADS, HEAD_DIM)),
    )
    return pl.pallas_call(
        _sample_attn_kernel,
        grid_spec=grid_spec,
        out_shape=jax.ShapeDtypeStruct((n, N_HEADS, HEAD_DIM), BF16),
        compiler_params=pltpu.CompilerParams(dimension_semantics=("arbitrary",), vmem_limit_bytes=VMEM_LIMIT),
        name="sample_attention",
    )(page_table, q3, k_new, v_new, bias, bias_new, gn8, *([cache_k] * n_pages), *([cache_v] * n_pages))


def _rope_table_np(positions):
    half = ROT_DIM // 2
    freq = ROPE_THETA ** (-(np.arange(half, dtype=np.float64) / half))
    ang = np.asarray(positions, np.float64)[:, None] * freq[None, :]
    cos, sin = np.cos(ang), np.sin(ang)
    n = len(positions)
    tab = np.zeros((n, 3, LANES), np.float64)
    tab[:, 0, :] = 1.0
    for base in range(0, LANES, HEAD_DIM):
        tab[:, 0, base:base + half] = cos
        tab[:, 0, base + half:base + ROT_DIM] = cos
        tab[:, 1, base:base + half] = -sin
        tab[:, 2, base + half:base + ROT_DIM] = sin
    return tab.reshape(n, 3 * LANES).astype(np.float32)


def _block_diag(w):
    n, a, b = w.shape
    return jnp.einsum("nij,nm->nimj", w, jnp.eye(n, dtype=w.dtype)).reshape(n * a, n * b)


def _layer_params(l, ln1_g, w_in, conv_w, conv_b, lru_wa, lru_ba, lru_wx, lru_bx, lru_lambda, q_norm_g,
                  k_norm_g, gn_lru_g, gn_att_g, w_out, ln2_g, w_gate, w_up, w_down):
    w_tail = jnp.pad(w_in[l][:, COL_IK:], ((0, 0), (0, TAIL_W - (w_in.shape[2] - COL_IK))))
    head_id = np.arange(ATT_W) // HEAD_DIM
    ones_q = (head_id[:, None] == head_id[None, :]).astype(np.float32)
    return {
        "ln1": ln1_g[l][None, :],
        "w_main": w_in[l][:, :COL_IK].astype(BF16),
        "w_tail": w_tail.astype(BF16),
        "conv_w": conv_w[l],
        "conv_b": conv_b[l][None, :],
        "wa": _block_diag(lru_wa[l]).astype(BF16),
        "ba": lru_ba[l][None, :],
        "wx": _block_diag(lru_wx[l]).astype(BF16),
        "bx": lru_bx[l][None, :],
        "lam": lru_lambda[l][None, :],
        "qg": jnp.tile(q_norm_g[l], N_HEADS)[None, :],
        "kg": jnp.tile(k_norm_g[l], N_KV_HEADS)[None, :],
        "gnl": gn_lru_g[l][None, :],
        "gna": gn_att_g[l][None, :],
        "gna8": gn_att_g[l].reshape(N_HEADS, HEAD_DIM),
        "ones_q": jnp.asarray(ones_q, BF16),
        "ones_k": jnp.asarray(ones_q[:KV_W, :KV_W], BF16),
        "wo_a": w_out[l][:LRU_W].astype(BF16),
        "wo_b": w_out[l][LRU_W:].astype(BF16),
        "ln2": ln2_g[l][None, :],
        "wg": w_gate[l].astype(BF16),
        "wu": w_up[l].astype(BF16),
        "wd": w_down[l].astype(BF16),
    }


def kernel(x_prompt, x_sample, cache_k, cache_v, cache_ik, state_h, state_conv, page_table, c_prompt, c_sample, ada_w, ada_b, ln1_g, w_in, conv_w, conv_b, lru_wa, lru_ba, lru_wx, lru_bx, lru_lambda, q_norm_g, k_norm_g, gn_lru_g, gn_att_g, w_out, ln2_g, w_gate, w_up, w_down):
    bsz, t, _ = x_prompt.shape
    dbsz, dt, _ = x_sample.shape
    depth = ada_w.shape[0]
    n_pages = page_table.shape[1]
    past_len = n_pages * PAGE_SIZE
    n_pool = cache_k.shape[1]
    assert dt == 1 and t % PROMPT_ROW_TILE == 0 and t % POST_ROW_TILE == 0 and t // 4 >= TOPK_MAX
    assert (past_len + dt) // 4 >= TOPK_MAX

    rope_p = jnp.asarray(_rope_table_np(np.arange(t)))
    rope_s = jnp.asarray(_rope_table_np(past_len + np.arange(dt)))
    tri = jnp.asarray(np.triu(np.ones((TRI_W, TRI_W), np.float32)), BF16)

    yp = x_prompt.reshape(bsz * t, D_MODEL)
    ys = x_sample.reshape(dbsz, D_MODEL)
    c_all = jnp.concatenate([c_prompt, c_sample], axis=0)
    outs_p, outs_s = [], []
    for l in range(depth):
        p = _layer_params(l, ln1_g, w_in, conv_w, conv_b, lru_wa, lru_ba, lru_wx, lru_bx, lru_lambda,
                          q_norm_g, k_norm_g, gn_lru_g, gn_att_g, w_out, ln2_g, w_gate, w_up, w_down)
        mod = _ada(c_all, ada_w[l], ada_b[l][None, :])
        mod_p = mod[:bsz].reshape(bsz, 1, 6 * D_MODEL)
        mod_s = mod[bsz:]

        (na, qh, kt, vt, ikt, ktb, vb, ikcat, iqcat, iw, h_last, conv_new) = _prompt_stage(
            yp.reshape(bsz, t, D_MODEL), mod_p, p, rope_p)
        nb = _prompt_attention(qh, iqcat, iw, ktb, vb, ikcat, p["gna"], tri)
        yp = _post(yp, na.reshape(bsz * t, LRU_W), nb.reshape(bsz * t, ATT_W), mod_p, p, t, POST_ROW_TILE)
        outs_p.append((jnp.transpose(kt, (0, 3, 1, 2)), jnp.transpose(vt, (0, 3, 1, 2)),
                       jnp.transpose(ikt, (0, 2, 1)), h_last.reshape(bsz, LRU_W), conv_new))

        (na_s, q_s, k_s, v_s, ik_s, iq_s, iw_s, h_s, conv_s) = _sample_stage(
            ys, mod_s, p, rope_s, state_conv[l].reshape(dbsz, (CONV_W - 1) * LRU_W), state_h[l])
        score = _sample_scores(page_table, iq_s.reshape(dbsz, IDX_HEADS, IDX_DIM),
                               iw_s[:, IW_LANE:IW_LANE + IDX_HEADS].reshape(dbsz, IDX_HEADS, 1),
                               jnp.transpose(cache_ik[l], (0, 2, 1)))
        bias, bias_new = _sample_select(score.reshape(dbsz, past_len), iq_s, ik_s, iw_s, tri)
        nb_s = _sample_attention(page_table, q_s.reshape(dbsz, N_HEADS, HEAD_DIM),
                                 k_s.reshape(dbsz, 1, KV_W), v_s.reshape(dbsz, 1, KV_W),
                                 bias.reshape(dbsz, 1, past_len), bias_new.reshape(dbsz, 1, LANES), p["gna8"],
                                 jnp.transpose(cache_k[l], (0, 2, 3, 1)), jnp.transpose(cache_v[l], (0, 2, 3, 1)))
        ys = _post(ys, na_s, nb_s.reshape(dbsz, ATT_W), mod_s, p, 1, dbsz)
        outs_s.append((k_s.reshape(dbsz, dt, N_KV_HEADS, HEAD_DIM), v_s.reshape(dbsz, dt, N_KV_HEADS, HEAD_DIM),
                       ik_s.reshape(dbsz, dt, IDX_DIM), h_s,
                       conv_s.reshape(dbsz, CONV_W - 1, LRU_W)))

    def stack(outs, i):
        return jnp.stack([o[i] for o in outs])

    return (yp.reshape(bsz, t, D_MODEL), ys.reshape(dbsz, dt, D_MODEL),
            stack(outs_p, 0), stack(outs_p, 1), stack(outs_p, 2), stack(outs_p, 3), stack(outs_p, 4),
            stack(outs_s, 0), stack(outs_s, 1), stack(outs_s, 2), stack(outs_s, 3), stack(outs_s, 4))
```

```python
import functools

import numpy as np
import jax
import jax.numpy as jnp
from jax import lax
from jax.experimental import pallas as pl
from jax.experimental.pallas import tpu as pltpu

F32 = jnp.float32
BF16 = jnp.bfloat16
I32 = jnp.int32
I16 = jnp.int16

D_MODEL = 1024
LRU_W = 512
LRU_BLOCKS = 8
LRU_BW = LRU_W // LRU_BLOCKS
CONV_W = 4
LRU_C = 8.0
N_HEADS = 8
HEAD_DIM = 64
ATT_W = N_HEADS * HEAD_DIM
N_KV_HEADS = 2
KV_GROUP = N_HEADS // N_KV_HEADS
KV_W = N_KV_HEADS * HEAD_DIM
ROT_DIM = HEAD_DIM // 4
ROPE_THETA = 500000.0
IDX_HEADS = 4
IDX_DIM = 64
IDX_W = IDX_HEADS * IDX_DIM
TOPK_MAX = 256
PAGE_SIZE = 128
D_FF = 2816
EPS = 1e-6

COL_XR = 0
COL_GR = COL_XR + LRU_W
COL_Q = COL_GR + LRU_W
COL_K = COL_Q + ATT_W
COL_V = COL_K + KV_W
COL_IQ = COL_V + KV_W
MAIN_W = COL_IQ
TAIL_W = 128
IDXP_W = IDX_W + TAIL_W
IW_LANE = IDX_DIM
IDX_CAT = 6 * IDX_DIM

LANES = 128
SUBLANES = 8
TRI_W = 256
INT_MIN = -2147483648
HALF_MIN = -32768
NEG_BIG = -1e30
VMEM_LIMIT = 48 * 1024 * 1024

PROMPT_ROW_TILE = 256
Q_BLOCK = 256
POST_ROW_TILE = 512
FF_CHUNK = D_FF // 2
ADA_COL_TILE = 512
SCORE_SEQS = 4
ATTN_SEQS = 4


def _dot(a, b):
    return jnp.dot(a, b, preferred_element_type=F32)


def _dot_nt(a, b):
    return lax.dot_general(a, b, (((1,), (1,)), ((), ())), preferred_element_type=F32)


def _split_bf16(x):
    hi = x.astype(BF16)
    lo = (x - hi.astype(F32)).astype(BF16)
    return hi, lo


def _split3(x):
    p1 = x.astype(BF16)
    r1 = x - p1.astype(F32)
    p2 = r1.astype(BF16)
    p3 = (r1 - p2.astype(F32)).astype(BF16)
    return p1, p2, p3


def _dot_f32(a, b):
    a1, a2, a3 = a
    b1, b2, b3 = b
    return ((_dot(a1, b3) + _dot(a2, b2) + _dot(a3, b1)) + (_dot(a1, b2) + _dot(a2, b1))) + _dot(a1, b1)


def _cat6_lhs(x, axis):
    r1 = x - x.astype(BF16).astype(F32)
    r2 = r1 - r1.astype(BF16).astype(F32)
    return jnp.concatenate([x, x, r1, x, r1, r2], axis=axis).astype(BF16)


def _cat6_rhs(x, axis):
    p1, p2, p3 = _split3(x)
    return jnp.concatenate([p1, p2, p1, p3, p2, p1], axis=axis)


def _rms(x, g):
    return x * lax.rsqrt(jnp.mean(x * x, axis=-1, keepdims=True) + EPS) * g


def _head_rms(x, ones_bd, g):
    hi, lo = _split_bf16(x * x)
    ss = _dot(hi, ones_bd) + _dot(lo, ones_bd)
    return x * lax.rsqrt(ss * (1.0 / HEAD_DIM) + EPS) * g


def _rope(x, cos, sin_next, sin_prev):
    w = x.shape[-1]
    half = ROT_DIM // 2
    return x * cos + pltpu.roll(x, w - half, 1) * sin_next + pltpu.roll(x, half, 1) * sin_prev


def _rope_tables(tab, reps):
    cos = tab[:, 0:LANES]
    sa = tab[:, LANES:2 * LANES]
    sb = tab[:, 2 * LANES:3 * LANES]
    if reps > 1:
        cos = jnp.concatenate([cos] * reps, axis=1)
        sa = jnp.concatenate([sa] * reps, axis=1)
        sb = jnp.concatenate([sb] * reps, axis=1)
    return cos, sa, sb


def _rope_tail(zt, tab):
    cos, sa, sb = _rope_tables(tab, 1)
    lane = lax.broadcasted_iota(I32, zt.shape, 1)
    first = lane < IDX_DIM
    return _rope(zt, jnp.where(first, cos, 1.0), jnp.where(first, sa, 0.0), jnp.where(first, sb, 0.0))


def _gelu_tanh(x):
    c = float(np.sqrt(2.0 / np.pi))
    return x * (0.5 * (1.0 + jnp.tanh(c * (x + 0.044715 * (x * x * x)))))


def _softplus(x):
    return jnp.maximum(x, 0.0) + jnp.log1p(jnp.exp(-jnp.abs(x)))


def _project_in(x, sh1, sc1, ln1, w_main, w_idx_parts):
    h = _rms(x, ln1) * (1.0 + sc1) + sh1
    hp = _split3(h)
    return _dot(hp[0], w_main), _dot_f32(hp, w_idx_parts)


def _lru_gates(xc, wa, ba, wx, bx, lam):
    xcb = xc.astype(BF16)
    r = jax.nn.sigmoid(_dot(xcb, wa) + ba)
    gi = jax.nn.sigmoid(_dot(xcb, wx) + bx)
    log_a = (-LRU_C * _softplus(-lam)) * r
    a = jnp.exp(log_a)
    y = jnp.tanh(-log_a) * (1.0 + a * a)
    inp = jnp.where(y > 0.0, y * lax.rsqrt(y), 0.0) * gi * xc
    return a, inp


def _attention_inputs(z, zi, tab, qg, kg, ones_q, ones_k):
    cq, saq, sbq = _rope_tables(tab, ATT_W // LANES)
    ck, sak, sbk = _rope_tables(tab, KV_W // LANES)
    ci, sai, sbi = _rope_tables(tab, IDX_W // LANES)
    q = _rope(_head_rms(z[:, COL_Q:COL_K], ones_q, qg), cq, saq, sbq) * (HEAD_DIM ** -0.5)
    k = _rope(_head_rms(z[:, COL_K:COL_V], ones_k, kg), ck, sak, sbk)
    v = z[:, COL_V:COL_IQ]
    iq = _rope(zi[:, 0:IDX_W], ci, sai, sbi)
    tail = _rope_tail(zi[:, IDX_W:IDXP_W], tab)
    return q, k, v, iq, tail


def _ada_kernel(c_ref, w_ref, b_ref, o_ref):
    c = c_ref[...]
    s = c * jax.nn.sigmoid(c)
    o_ref[...] = _dot_f32(_split3(s), _split3(w_ref[...])) + b_ref[...]


def _ada(c_all, ada_w, ada_b):
    rows = c_all.shape[0]
    n = ada_w.shape[1]
    return pl.pallas_call(
        _ada_kernel,
        grid=(n // ADA_COL_TILE,),
        in_specs=[
            pl.BlockSpec((rows, D_MODEL), lambda j: (0, 0)),
            pl.BlockSpec((D_MODEL, ADA_COL_TILE), lambda j: (0, j)),
            pl.BlockSpec((1, ADA_COL_TILE), lambda j: (0, j)),
        ],
        out_specs=pl.BlockSpec((rows, ADA_COL_TILE), lambda j: (0, j)),
        out_shape=jax.ShapeDtypeStruct((rows, n), F32),
        compiler_params=pltpu.CompilerParams(dimension_semantics=("arbitrary",), vmem_limit_bytes=VMEM_LIMIT),
        name="ada_modulation",
    )(c_all, ada_w, ada_b)


def _prompt_stage_kernel(x_ref, mod_ref, ln1_ref, wmain_ref, widx_ref, convw_ref, convb_ref,
                         wa_ref, ba_ref, wx_ref, bx_ref, lam_ref, qg_ref, kg_ref, gnl_ref,
                         onesq_ref, onesk_ref, rope_ref,
                         na_ref, q_ref, kt_ref, vt_ref, ikt_ref, ktb_ref, vb_ref,
                         ikcat_ref, iqcat_ref, iw_ref, h_ref, conv_ref,
                         xr_buf, hcarry, widx_parts):
    tt = x_ref.shape[1]
    t = pl.program_id(1)

    @pl.when((t == 0) & (pl.program_id(0) == 0))
    def _():
        for j, part in enumerate(_split3(widx_ref[...])):
            widx_parts[j] = part

    @pl.when(t == 0)
    def _():
        xr_buf[0:SUBLANES, :] = jnp.zeros((SUBLANES, LRU_W), F32)
        hcarry[...] = jnp.zeros_like(hcarry)

    mod = mod_ref[0]
    z, zi = _project_in(x_ref[0], mod[:, 0:D_MODEL], mod[:, D_MODEL:2 * D_MODEL], ln1_ref[...],
                        wmain_ref[...], (widx_parts[0], widx_parts[1], widx_parts[2]))

    xr = z[:, COL_XR:COL_GR]
    xr_buf[SUBLANES:SUBLANES + tt, :] = xr
    convw = convw_ref[...]
    xc = convb_ref[...]
    for i in range(CONV_W):
        xc = xc + xr_buf[pl.ds(SUBLANES - (CONV_W - 1) + i, tt), :] * convw[i:i + 1, :]
    conv_ref[0] = xr_buf[pl.ds(SUBLANES + tt - (CONV_W - 1), CONV_W - 1), :]
    xr_buf[0:SUBLANES, :] = xr_buf[tt:tt + SUBLANES, :]

    a, b = _lru_gates(xc, wa_ref[...], ba_ref[...], wx_ref[...], bx_ref[...], lam_ref[...])
    row = lax.broadcasted_iota(I32, a.shape, 0)
    d = 1
    while d < tt:
        keep = row >= d
        a_s = jnp.where(keep, pltpu.roll(a, d, 0), 1.0)
        b_s = jnp.where(keep, pltpu.roll(b, d, 0), 0.0)
        b = a * b_s + b
        a = a * a_s
        d *= 2
    hr = a * hcarry[...] + b
    hlast = hr[tt - 1:tt, :]
    hcarry[...] = hlast
    h_ref[0] = hlast

    na_ref[0] = _rms(_gelu_tanh(z[:, COL_GR:COL_Q]) * hr, gnl_ref[...]).astype(BF16)

    q, k, v, iq, tail = _attention_inputs(z, zi, rope_ref[...], qg_ref[...], kg_ref[...],
                                          onesq_ref[...], onesk_ref[...])
    for h in range(N_HEADS):
        q_ref[0, h] = q[:, h * HEAD_DIM:(h + 1) * HEAD_DIM].astype(BF16)
    kt = jnp.transpose(k)
    vt = jnp.transpose(v)
    ikt = jnp.transpose(tail)[0:IDX_DIM, :]
    kt_ref[0] = kt.reshape(N_KV_HEADS, HEAD_DIM, tt)
    vt_ref[0] = vt.reshape(N_KV_HEADS, HEAD_DIM, tt)
    ikt_ref[0] = ikt
    ktb_ref[0] = kt.astype(BF16).reshape(N_KV_HEADS, HEAD_DIM, tt)
    for n in range(N_KV_HEADS):
        vb_ref[0, n] = v[:, n * HEAD_DIM:(n + 1) * HEAD_DIM].astype(BF16)
    ikcat_ref[0] = _cat6_rhs(ikt, 0)
    for h in range(IDX_HEADS):
        iqcat_ref[0, h] = _cat6_lhs(iq[:, h * IDX_DIM:(h + 1) * IDX_DIM], 1)
    iw_ref[0] = zi[:, IDX_W:IDXP_W]


def _const_spec(shape):
    nd = len(shape)
    return pl.BlockSpec(shape, lambda *_: (0,) * nd)


def _prompt_stage(x, mod_p, p, rope_tab):
    bsz, t, _ = x.shape
    tt = PROMPT_ROW_TILE
    grid = (bsz, t // tt)

    def rows(w):
        return pl.BlockSpec((1, tt, w), lambda b, i: (b, i, 0))

    in_specs = [
        rows(D_MODEL),
        pl.BlockSpec((1, 1, 6 * D_MODEL), lambda b, i: (b, 0, 0)),
        _const_spec((1, D_MODEL)),
        _const_spec((D_MODEL, MAIN_W)),
        _const_spec((D_MODEL, IDXP_W)),
        _const_spec((CONV_W, LRU_W)),
        _const_spec((1, LRU_W)),
        _const_spec((LRU_W, LRU_W)),
        _const_spec((1, LRU_W)),
        _const_spec((LRU_W, LRU_W)),
        _const_spec((1, LRU_W)),
        _const_spec((1, LRU_W)),
        _const_spec((1, ATT_W)),
        _const_spec((1, KV_W)),
        _const_spec((1, LRU_W)),
        _const_spec((ATT_W, ATT_W)),
        _const_spec((KV_W, KV_W)),
        pl.BlockSpec((tt, 3 * LANES), lambda b, i: (i, 0)),
    ]
    def heads(n, w):
        return pl.BlockSpec((1, n, tt, w), lambda b, i: (b, 0, i, 0))

    def heads_t(n):
        return pl.BlockSpec((1, n, HEAD_DIM, tt), lambda b, i: (b, 0, 0, i))

    def feat_t(w):
        return pl.BlockSpec((1, w, tt), lambda b, i: (b, 0, i))

    out_specs = [
        rows(LRU_W), heads(N_HEADS, HEAD_DIM), heads_t(N_KV_HEADS), heads_t(N_KV_HEADS), feat_t(IDX_DIM),
        heads_t(N_KV_HEADS), heads(N_KV_HEADS, HEAD_DIM), feat_t(IDX_CAT), heads(IDX_HEADS, IDX_CAT), rows(TAIL_W),
        pl.BlockSpec((1, 1, LRU_W), lambda b, i: (b, 0, 0)),
        pl.BlockSpec((1, CONV_W - 1, LRU_W), lambda b, i: (b, 0, 0)),
    ]
    out_shape = [
        jax.ShapeDtypeStruct((bsz, t, LRU_W), BF16),
        jax.ShapeDtypeStruct((bsz, N_HEADS, t, HEAD_DIM), BF16),
        jax.ShapeDtypeStruct((bsz, N_KV_HEADS, HEAD_DIM, t), F32),
        jax.ShapeDtypeStruct((bsz, N_KV_HEADS, HEAD_DIM, t), F32),
        jax.ShapeDtypeStruct((bsz, IDX_DIM, t), F32),
        jax.ShapeDtypeStruct((bsz, N_KV_HEADS, HEAD_DIM, t), BF16),
        jax.ShapeDtypeStruct((bsz, N_KV_HEADS, t, HEAD_DIM), BF16),
        jax.ShapeDtypeStruct((bsz, IDX_CAT, t), BF16),
        jax.ShapeDtypeStruct((bsz, IDX_HEADS, t, IDX_CAT), BF16),
        jax.ShapeDtypeStruct((bsz, t, TAIL_W), F32),
        jax.ShapeDtypeStruct((bsz, 1, LRU_W), F32),
        jax.ShapeDtypeStruct((bsz, CONV_W - 1, LRU_W), F32),
    ]
    return pl.pallas_call(
        _prompt_stage_kernel,
        grid=grid,
        in_specs=in_specs,
        out_specs=out_specs,
        out_shape=out_shape,
        scratch_shapes=[pltpu.VMEM((tt + SUBLANES, LRU_W), F32), pltpu.VMEM((1, LRU_W), F32),
                        pltpu.VMEM((3, D_MODEL, IDXP_W), BF16)],
        compiler_params=pltpu.CompilerParams(dimension_semantics=("arbitrary", "arbitrary"),
                                             vmem_limit_bytes=VMEM_LIMIT),
        name="prompt_stage",
    )(x, mod_p, p["ln1"], p["w_main"], p["w_idx"], p["conv_w"], p["conv_b"], p["wa"], p["ba"],
      p["wx"], p["bx"], p["lam"], p["qg"], p["kg"], p["gnl"], p["ones_q"], p["ones_k"], rope_tab)


def _sortable_key(score):
    score = jnp.where(score == 0.0, 0.0, score)
    bits = lax.bitcast_convert_type(score, I32)
    return jnp.where(bits < 0, bits ^ jnp.int32(0x7FFFFFFF), bits)


def _count_ge16(ref, c, strict=False):
    c16 = c.astype(I16)
    hit = ref[...] > c16 if strict else ref[...] >= c16
    ones = jnp.where(hit, jnp.int16(1), jnp.int16(0))
    acc = ones[:, 0:LANES]
    for j in range(1, ref.shape[1] // LANES):
        acc = acc + ones[:, j * LANES:(j + 1) * LANES]
    return jnp.sum(acc.astype(F32), axis=1, keepdims=True)


def _bitwise_max16(count_ge, k):
    def body(i, t):
        c = t + lax.shift_left(jnp.int32(1), jnp.int32(15) - i)
        return jnp.where(count_ge(c) >= k, c, t)

    return lax.fori_loop(0, 16, body, jnp.full(k.shape, HALF_MIN, I32), unroll=True)


def _kth_largest(key_ref, hi_ref, lo_ref, k, extra=None):
    key = key_ref[...]
    hi_ref[...] = lax.shift_right_arithmetic(key, 16).astype(I16)
    lo_ref[...] = ((key & 0xFFFF) + HALF_MIN).astype(I16)
    kf = jnp.full((key_ref.shape[0], 1), k, F32)
    if extra is not None:
        hi_x = lax.shift_right_arithmetic(extra, 16)
        lo_x = (extra & 0xFFFF) + HALF_MIN

    def count_hi(c):
        cnt = _count_ge16(hi_ref, c)
        return cnt if extra is None else cnt + jnp.where(hi_x >= c, 1.0, 0.0)

    t_hi = _bitwise_max16(count_hi, kf)
    above = _count_ge16(hi_ref, t_hi, strict=True)
    if extra is not None:
        above = above + jnp.where(hi_x > t_hi, 1.0, 0.0)
    k_lo = kf - above
    lo_ref[...] = jnp.where(hi_ref[...] == t_hi.astype(I16), lo_ref[...], jnp.int16(HALF_MIN))
    if extra is not None:
        lo_x = jnp.where(hi_x == t_hi, lo_x, HALF_MIN)

    def count_lo(c):
        cnt = _count_ge16(lo_ref, c)
        return cnt if extra is None else cnt + jnp.where(lo_x >= c, 1.0, 0.0)

    t_lo = _bitwise_max16(count_lo, k_lo)
    return t_hi * 65536 + (t_lo - HALF_MIN)


def _selection_bias(key_ref, hi_ref, lo_ref, bias_ref, tri, k, extra=None):
    thr = _kth_largest(key_ref, hi_ref, lo_ref, k, extra)
    n_gt = jnp.sum(jnp.where(key_ref[...] > thr, 1.0, 0.0), axis=1, keepdims=True)
    if extra is not None:
        n_gt = n_gt + jnp.where(extra > thr, 1.0, 0.0)
    need = k - n_gt
    offset = jnp.zeros_like(need)
    for c in range(key_ref.shape[1] // TRI_W):
        kc = key_ref[:, c * TRI_W:(c + 1) * TRI_W]
        eq = kc == thr
        e = jnp.where(eq, 1.0, 0.0)
        incl = _dot(e.astype(BF16), tri)
        rank = incl - e + offset
        tie = jnp.where(rank < need, 0.0, NEG_BIG)
        bias_ref[:, c * TRI_W:(c + 1) * TRI_W] = jnp.where(kc > thr, 0.0, jnp.where(eq, tie, NEG_BIG))
        offset = offset + incl[:, TRI_W - 1:TRI_W]
    if extra is None:
        return None
    tie = jnp.where(offset < need, 0.0, NEG_BIG)
    return jnp.where(extra > thr, 0.0, jnp.where(extra == thr, tie, NEG_BIG))


def _prompt_attn_kernel(q_ref, iq_ref, iw_ref, kt_ref, v_ref, ikt_ref, gn_ref, tri_ref, o_ref,
                        key_ref, hi_ref, lo_ref, bias_ref, *, q_start):
    qb, n_keys = bias_ref.shape
    qpos = q_start + lax.broadcasted_iota(I32, (qb, n_keys), 0)
    kpos = lax.broadcasted_iota(I32, (qb, n_keys), 1)
    causal = kpos <= qpos
    if n_keys > TOPK_MAX:
        iw = iw_ref[0]
        ikt = ikt_ref[0]
        score = jnp.zeros((qb, n_keys), F32)
        for h in range(IDX_HEADS):
            score = score + iw[:, IW_LANE + h:IW_LANE + h + 1] * jnp.maximum(_dot(iq_ref[0, h], ikt), 0.0)
        key_ref[...] = jnp.where(causal, _sortable_key(score), INT_MIN)
        _selection_bias(key_ref, hi_ref, lo_ref, bias_ref, tri_ref[...], float(TOPK_MAX))
        bias_ref[...] = jnp.where(causal, bias_ref[...], NEG_BIG)
    else:
        bias_ref[...] = jnp.where(causal, 0.0, NEG_BIG)

    outs = []
    for h in range(N_HEADS):
        n = h // KV_GROUP
        logits = _dot(q_ref[0, h], kt_ref[0, n]) + bias_ref[...]
        m = jnp.max(logits, axis=1, keepdims=True)
        p = jnp.exp(logits - m)
        den = jnp.sum(p, axis=1, keepdims=True)
        outs.append(_dot(p.astype(BF16), v_ref[0, n]) / den)
    o_ref[0] = _rms(jnp.concatenate(outs, axis=1), gn_ref[...]).astype(BF16)


def _prompt_attention_block(qh, iqcat, iw, ktb, vb, ikcat, gn_att, tri, q_block):
    bsz, _, t, _ = qh.shape
    qb = Q_BLOCK
    n_keys = (q_block + 1) * qb

    def qheads(n, w):
        return pl.BlockSpec((1, n, qb, w), lambda b: (b, 0, q_block, 0))

    return pl.pallas_call(
        functools.partial(_prompt_attn_kernel, q_start=q_block * qb),
        grid=(bsz,),
        in_specs=[qheads(N_HEADS, HEAD_DIM), qheads(IDX_HEADS, IDX_CAT),
                  pl.BlockSpec((1, qb, TAIL_W), lambda b: (b, q_block, 0)),
                  pl.BlockSpec((1, N_KV_HEADS, HEAD_DIM, n_keys), lambda b: (b, 0, 0, 0)),
                  pl.BlockSpec((1, N_KV_HEADS, n_keys, HEAD_DIM), lambda b: (b, 0, 0, 0)),
                  pl.BlockSpec((1, IDX_CAT, n_keys), lambda b: (b, 0, 0)),
                  _const_spec((1, ATT_W)), _const_spec((TRI_W, TRI_W))],
        out_specs=pl.BlockSpec((1, qb, ATT_W), lambda b: (b, 0, 0)),
        out_shape=jax.ShapeDtypeStruct((bsz, qb, ATT_W), BF16),
        scratch_shapes=[pltpu.VMEM((qb, n_keys), I32), pltpu.VMEM((qb, n_keys), I16), pltpu.VMEM((qb, n_keys), I16),
                        pltpu.VMEM((qb, n_keys), F32)],
        compiler_params=pltpu.CompilerParams(dimension_semantics=("arbitrary",), vmem_limit_bytes=VMEM_LIMIT),
        name=f"prompt_attention_{q_block}",
    )(qh, iqcat, iw, ktb, vb, ikcat, gn_att, tri)


def _prompt_attention(qh, iqcat, iw, ktb, vb, ikcat, gn_att, tri):
    t = qh.shape[2]
    blocks = [_prompt_attention_block(qh, iqcat, iw, ktb, vb, ikcat, gn_att, tri, i) for i in range(t // Q_BLOCK)]
    return jnp.concatenate(blocks, axis=1)


def _post_kernel(x_ref, na_ref, nb_ref, mod_ref, woa_ref, wob_ref, ln2_ref, wg_ref, wu_ref, wd_ref,
                 y_ref, *, shared_mod):
    mod = mod_ref[0] if shared_mod else mod_ref[...]
    g1 = mod[:, 2 * D_MODEL:3 * D_MODEL]
    sh2 = mod[:, 3 * D_MODEL:4 * D_MODEL]
    sc2 = mod[:, 4 * D_MODEL:5 * D_MODEL]
    g2 = mod[:, 5 * D_MODEL:6 * D_MODEL]
    x = x_ref[...]
    mix = _dot(na_ref[...], woa_ref[...]) + _dot(nb_ref[...], wob_ref[...])
    x1 = x + g1 * mix
    h2 = (_rms(x1, ln2_ref[...]) * (1.0 + sc2) + sh2).astype(BF16)
    ff = jnp.zeros_like(x1)
    for c in range(D_FF // FF_CHUNK):
        lo, hi = c * FF_CHUNK, (c + 1) * FF_CHUNK
        g = _dot(h2, wg_ref[:, lo:hi])
        u = _dot(h2, wu_ref[:, lo:hi])
        act = (g * jax.nn.sigmoid(g) * u).astype(BF16)
        ff = ff + _dot(act, wd_ref[lo:hi, :])
    y_ref[...] = x1 + g2 * ff


def _post(x, na, nb, mod, p, rows_per_mod, tile):
    n = x.shape[0]
    shared = rows_per_mod > 1
    if shared:
        mod_spec = pl.BlockSpec((1, 1, 6 * D_MODEL), lambda i: (i * tile // rows_per_mod, 0, 0))
    else:
        mod_spec = pl.BlockSpec((tile, 6 * D_MODEL), lambda i: (i, 0))

    def rows(w):
        return pl.BlockSpec((tile, w), lambda i: (i, 0))

    def weight(shape):
        return pl.BlockSpec(shape, lambda i: (0, 0), pipeline_mode=pl.Buffered(1))

    return pl.pallas_call(
        functools.partial(_post_kernel, shared_mod=shared),
        grid=(n // tile,),
        in_specs=[rows(D_MODEL), rows(LRU_W), rows(ATT_W), mod_spec,
                  weight((LRU_W, D_MODEL)), weight((ATT_W, D_MODEL)), weight((1, D_MODEL)),
                  weight((D_MODEL, D_FF)), weight((D_MODEL, D_FF)), weight((D_FF, D_MODEL))],
        out_specs=rows(D_MODEL),
        out_shape=jax.ShapeDtypeStruct((n, D_MODEL), F32),
        compiler_params=pltpu.CompilerParams(dimension_semantics=("arbitrary",), vmem_limit_bytes=VMEM_LIMIT),
        name="post_ffn",
    )(x, na, nb, mod, p["wo_a"], p["wo_b"], p["ln2"], p["wg"], p["wu"], p["wd"])


def _sample_stage_kernel(x_ref, mod_ref, ln1_ref, wmain_ref, widx_ref, convw_ref, convb_ref,
                         wa_ref, ba_ref, wx_ref, bx_ref, lam_ref, qg_ref, kg_ref, gnl_ref,
                         onesq_ref, onesk_ref, rope_ref, sconv_ref, sh_ref,
                         na_ref, q_ref, k_ref, v_ref, ik_ref, iq_ref, iw_ref, h_ref, conv_ref):
    mod = mod_ref[...]
    z, zi = _project_in(x_ref[...], mod[:, 0:D_MODEL], mod[:, D_MODEL:2 * D_MODEL], ln1_ref[...],
                        wmain_ref[...], _split3(widx_ref[...]))
    xr = z[:, COL_XR:COL_GR]
    convw = convw_ref[...]
    sconv = sconv_ref[...]
    xc = convb_ref[...]
    for i in range(CONV_W - 1):
        xc = xc + sconv[:, i * LRU_W:(i + 1) * LRU_W] * convw[i:i + 1, :]
    xc = xc + xr * convw[CONV_W - 1:CONV_W, :]
    conv_ref[:, 0:(CONV_W - 2) * LRU_W] = sconv[:, LRU_W:(CONV_W - 1) * LRU_W]
    conv_ref[:, (CONV_W - 2) * LRU_W:(CONV_W - 1) * LRU_W] = xr

    a, b = _lru_gates(xc, wa_ref[...], ba_ref[...], wx_ref[...], bx_ref[...], lam_ref[...])
    hr = a * sh_ref[...] + b
    h_ref[...] = hr
    na_ref[...] = _rms(_gelu_tanh(z[:, COL_GR:COL_Q]) * hr, gnl_ref[...]).astype(BF16)

    q, k, v, iq, tail = _attention_inputs(z, zi, rope_ref[...], qg_ref[...], kg_ref[...],
                                          onesq_ref[...], onesk_ref[...])
    q_ref[...] = q
    k_ref[...] = k
    v_ref[...] = v
    ik_ref[...] = tail[:, 0:IDX_DIM]
    iq_ref[...] = iq
    iw_ref[...] = zi[:, IDX_W:IDXP_W]


def _sample_stage(x, mod_s, p, rope_row, sconv, sh):
    n = x.shape[0]

    def out(w, dt=F32):
        return jax.ShapeDtypeStruct((n, w), dt)

    return pl.pallas_call(
        _sample_stage_kernel,
        out_shape=[out(LRU_W, BF16), out(ATT_W), out(KV_W), out(KV_W), out(IDX_DIM), out(IDX_W),
                   out(TAIL_W), out(LRU_W), out((CONV_W - 1) * LRU_W)],
        compiler_params=pltpu.CompilerParams(vmem_limit_bytes=VMEM_LIMIT),
        name="sample_stage",
    )(x, mod_s, p["ln1"], p["w_main"], p["w_idx"], p["conv_w"], p["conv_b"], p["wa"], p["ba"],
      p["wx"], p["bx"], p["lam"], p["qg"], p["kg"], p["gnl"], p["ones_q"], p["ones_k"], rope_row, sconv, sh)


def _sample_score_kernel(pt_ref, iq_ref, iw_ref, *rest, n_pages):
    page_refs, o_ref = rest[:-1], rest[-1]
    for j in range(iq_ref.shape[0]):
        iqcat = _cat6_lhs(iq_ref[j], 1)
        iw = iw_ref[j]
        for pg in range(n_pages):
            page = page_refs[j * n_pages + pg][...]
            s = _dot(iqcat, _cat6_rhs(page, 0))
            o_ref[j, :, pg * PAGE_SIZE:(pg + 1) * PAGE_SIZE] = jnp.sum(iw * jnp.maximum(s, 0.0), axis=0,
                                                                        keepdims=True)


def _page_specs(lead, group, n_pages):
    zeros = (0,) * (len(lead) + 1)

    def one(j, pg):
        return pl.BlockSpec((None,) + lead + (PAGE_SIZE,), lambda s, pt: (pt[s * group + j, pg],) + zeros)

    return [one(j, pg) for j in range(group) for pg in range(n_pages)]


def _sample_scores(page_table, iq3, iw3, cache_ik):
    n, n_pages = page_table.shape
    g = SCORE_SEQS
    grid_spec = pltpu.PrefetchScalarGridSpec(
        num_scalar_prefetch=1,
        grid=(n // g,),
        in_specs=[pl.BlockSpec((g, IDX_HEADS, IDX_DIM), lambda s, pt: (s, 0, 0)),
                  pl.BlockSpec((g, IDX_HEADS, 1), lambda s, pt: (s, 0, 0))]
                 + _page_specs((IDX_DIM,), g, n_pages),
        out_specs=pl.BlockSpec((g, 1, n_pages * PAGE_SIZE), lambda s, pt: (s, 0, 0)),
    )
    return pl.pallas_call(
        functools.partial(_sample_score_kernel, n_pages=n_pages),
        grid_spec=grid_spec,
        out_shape=jax.ShapeDtypeStruct((n, 1, n_pages * PAGE_SIZE), F32),
        compiler_params=pltpu.CompilerParams(dimension_semantics=("arbitrary",), vmem_limit_bytes=VMEM_LIMIT),
        name="sample_scores",
    )(page_table, iq3, iw3, *([cache_ik] * (g * n_pages)))


def _sample_select_kernel(score_ref, iq_ref, ik_ref, iw_ref, tri_ref, bias_ref, bias_new_ref,
                          key_ref, hi_ref, lo_ref):
    iq = iq_ref[...]
    ik = ik_ref[...]
    iw = iw_ref[...]
    new = jnp.zeros((iq.shape[0], 1), F32)
    for h in range(IDX_HEADS):
        s = jnp.sum(iq[:, h * IDX_DIM:(h + 1) * IDX_DIM] * ik, axis=1, keepdims=True)
        new = new + iw[:, IW_LANE + h:IW_LANE + h + 1] * jnp.maximum(s, 0.0)
    key_ref[...] = _sortable_key(score_ref[...])
    b_new = _selection_bias(key_ref, hi_ref, lo_ref, bias_ref, tri_ref[...], float(TOPK_MAX),
                            extra=_sortable_key(new))
    bias_new_ref[...] = jnp.broadcast_to(b_new, bias_new_ref.shape)


def _sample_select(score, iq, ik, iw, tri):
    n, n_keys = score.shape
    return pl.pallas_call(
        _sample_select_kernel,
        out_shape=[jax.ShapeDtypeStruct((n, n_keys), F32), jax.ShapeDtypeStruct((n, LANES), F32)],
        scratch_shapes=[pltpu.VMEM((n, n_keys), I32), pltpu.VMEM((n, n_keys), I16), pltpu.VMEM((n, n_keys), I16)],
        compiler_params=pltpu.CompilerParams(vmem_limit_bytes=VMEM_LIMIT),
        name="sample_select",
    )(score, iq, ik, iw, tri)


def _sample_attn_one(q, kn, vn, bias, bias_new, gn, k_refs, v_refs):
    row = lax.broadcasted_iota(I32, (N_HEADS, HEAD_DIM), 0)
    first = row < KV_GROUP
    q2 = jnp.concatenate([jnp.where(first, q, 0.0), jnp.where(first, 0.0, q)], axis=1).astype(BF16)
    logits = []
    for pg, k_ref in enumerate(k_refs):
        kp = k_ref[...].astype(BF16).reshape(KV_W, PAGE_SIZE)
        logits.append(_dot(q2, kp) + bias[:, pg * PAGE_SIZE:(pg + 1) * PAGE_SIZE])
    k_sel = jnp.where(first, kn[:, 0:HEAD_DIM], kn[:, HEAD_DIM:KV_W])
    v_sel = jnp.where(first, vn[:, 0:HEAD_DIM], vn[:, HEAD_DIM:KV_W])
    l_new = jnp.sum(q * k_sel, axis=1, keepdims=True) + bias_new
    m = l_new
    for l in logits:
        m = jnp.maximum(m, jnp.max(l, axis=1, keepdims=True))
    p_new = jnp.exp(l_new - m)
    den = p_new
    acc = jnp.zeros((N_HEADS, KV_W), F32)
    for pg, v_ref in enumerate(v_refs):
        p = jnp.exp(logits[pg] - m)
        den = den + jnp.sum(p, axis=1, keepdims=True)
        vp = v_ref[...].astype(BF16).reshape(KV_W, PAGE_SIZE)
        acc = acc + _dot_nt(p.astype(BF16), vp)
    out = (p_new * v_sel + jnp.where(first, acc[:, 0:HEAD_DIM], acc[:, HEAD_DIM:KV_W])) / den
    ms = jnp.sum(jnp.sum(out * out, axis=1, keepdims=True), axis=0, keepdims=True) * (1.0 / ATT_W)
    return (out * lax.rsqrt(ms + EPS) * gn).astype(BF16)


def _sample_attn_kernel(pt_ref, q_ref, kn_ref, vn_ref, bias_ref, bnew_ref, gn_ref, *rest, n_pages):
    group = q_ref.shape[0]
    k_refs, v_refs, o_ref = rest[:group * n_pages], rest[group * n_pages:2 * group * n_pages], rest[-1]
    for j in range(group):
        o_ref[j] = _sample_attn_one(q_ref[j], kn_ref[j], vn_ref[j], bias_ref[j], bnew_ref[j][:, 0:1], gn_ref[...],
                                    k_refs[j * n_pages:(j + 1) * n_pages], v_refs[j * n_pages:(j + 1) * n_pages])


def _sample_attention(page_table, q3, k_new, v_new, bias, bias_new, gn8, cache_k, cache_v):
    n, n_pages = page_table.shape
    n_keys = n_pages * PAGE_SIZE
    g = ATTN_SEQS

    def per_seq(shape):
        return pl.BlockSpec((g,) + shape, lambda s, pt: (s, 0, 0))

    grid_spec = pltpu.PrefetchScalarGridSpec(
        num_scalar_prefetch=1,
        grid=(n // g,),
        in_specs=[per_seq((N_HEADS, HEAD_DIM)), per_seq((1, KV_W)), per_seq((1, KV_W)),
                  per_seq((1, n_keys)), per_seq((1, LANES)),
                  pl.BlockSpec((N_HEADS, HEAD_DIM), lambda s, pt: (0, 0))]
                 + _page_specs((N_KV_HEADS, HEAD_DIM), g, n_pages)
                 + _page_specs((N_KV_HEADS, HEAD_DIM), g, n_pages),
        out_specs=per_seq((N_HEADS, HEAD_DIM)),
    )
    return pl.pallas_call(
        functools.partial(_sample_attn_kernel, n_pages=n_pages),
        grid_spec=grid_spec,
        out_shape=jax.ShapeDtypeStruct((n, N_HEADS, HEAD_DIM), BF16),
        compiler_params=pltpu.CompilerParams(dimension_semantics=("arbitrary",), vmem_limit_bytes=VMEM_LIMIT),
        name="sample_attention",
    )(page_table, q3, k_new, v_new, bias, bias_new, gn8,
      *([cache_k] * (g * n_pages)), *([cache_v] * (g * n_pages)))


def _rope_table_np(positions):
    half = ROT_DIM // 2
    freq = ROPE_THETA ** (-(np.arange(half, dtype=np.float64) / half))
    ang = np.asarray(positions, np.float64)[:, None] * freq[None, :]
    cos, sin = np.cos(ang), np.sin(ang)
    n = len(positions)
    tab = np.zeros((n, 3, LANES), np.float64)
    tab[:, 0, :] = 1.0
    for base in range(0, LANES, HEAD_DIM):
        tab[:, 0, base:base + half] = cos
        tab[:, 0, base + half:base + ROT_DIM] = cos
        tab[:, 1, base:base + half] = -sin
        tab[:, 2, base + half:base + ROT_DIM] = sin
    return tab.reshape(n, 3 * LANES).astype(np.float32)


def _block_diag(w):
    n, a, b = w.shape
    return jnp.einsum("nij,nm->nimj", w, jnp.eye(n, dtype=w.dtype)).reshape(n * a, n * b)


def _layer_params(l, ln1_g, w_in, conv_w, conv_b, lru_wa, lru_ba, lru_wx, lru_bx, lru_lambda, q_norm_g,
                  k_norm_g, gn_lru_g, gn_att_g, w_out, ln2_g, w_gate, w_up, w_down):
    w_idx = jnp.pad(w_in[l][:, COL_IQ:], ((0, 0), (0, IDXP_W - (w_in.shape[2] - COL_IQ))))
    head_id = np.arange(ATT_W) // HEAD_DIM
    ones_q = (head_id[:, None] == head_id[None, :]).astype(np.float32)
    return {
        "ln1": ln1_g[l][None, :],
        "w_main": w_in[l][:, :COL_IQ].astype(BF16),
        "w_idx": w_idx,
        "conv_w": conv_w[l],
        "conv_b": conv_b[l][None, :],
        "wa": _block_diag(lru_wa[l]).astype(BF16),
        "ba": lru_ba[l][None, :],
        "wx": _block_diag(lru_wx[l]).astype(BF16),
        "bx": lru_bx[l][None, :],
        "lam": lru_lambda[l][None, :],
        "qg": jnp.tile(q_norm_g[l], N_HEADS)[None, :],
        "kg": jnp.tile(k_norm_g[l], N_KV_HEADS)[None, :],
        "gnl": gn_lru_g[l][None, :],
        "gna": gn_att_g[l][None, :],
        "gna8": gn_att_g[l].reshape(N_HEADS, HEAD_DIM),
        "ones_q": jnp.asarray(ones_q, BF16),
        "ones_k": jnp.asarray(ones_q[:KV_W, :KV_W], BF16),
        "wo_a": w_out[l][:LRU_W].astype(BF16),
        "wo_b": w_out[l][LRU_W:].astype(BF16),
        "ln2": ln2_g[l][None, :],
        "wg": w_gate[l].astype(BF16),
        "wu": w_up[l].astype(BF16),
        "wd": w_down[l].astype(BF16),
    }


def kernel(x_prompt, x_sample, cache_k, cache_v, cache_ik, state_h, state_conv, page_table, c_prompt, c_sample, ada_w, ada_b, ln1_g, w_in, conv_w, conv_b, lru_wa, lru_ba, lru_wx, lru_bx, lru_lambda, q_norm_g, k_norm_g, gn_lru_g, gn_att_g, w_out, ln2_g, w_gate, w_up, w_down):
    bsz, t, _ = x_prompt.shape
    dbsz, dt, _ = x_sample.shape
    depth = ada_w.shape[0]
    n_pages = page_table.shape[1]
    past_len = n_pages * PAGE_SIZE
    n_pool = cache_k.shape[1]
    assert dt == 1 and t % PROMPT_ROW_TILE == 0 and t % POST_ROW_TILE == 0 and t // 4 >= TOPK_MAX
    assert (past_len + dt) // 4 >= TOPK_MAX

    rope_p = jnp.asarray(_rope_table_np(np.arange(t)))
    rope_s = jnp.asarray(_rope_table_np(past_len + np.arange(dt)))
    tri = jnp.asarray(np.triu(np.ones((TRI_W, TRI_W), np.float32)), BF16)

    yp = x_prompt.reshape(bsz * t, D_MODEL)
    ys = x_sample.reshape(dbsz, D_MODEL)
    c_all = jnp.concatenate([c_prompt, c_sample], axis=0)
    outs_p, outs_s = [], []
    for l in range(depth):
        p = _layer_params(l, ln1_g, w_in, conv_w, conv_b, lru_wa, lru_ba, lru_wx, lru_bx, lru_lambda,
                          q_norm_g, k_norm_g, gn_lru_g, gn_att_g, w_out, ln2_g, w_gate, w_up, w_down)
        mod = _ada(c_all, ada_w[l], ada_b[l][None, :])
        mod_p = mod[:bsz].reshape(bsz, 1, 6 * D_MODEL)
        mod_s = mod[bsz:]

        (na, qh, kt, vt, ikt, ktb, vb, ikcat, iqcat, iw, h_last, conv_new) = _prompt_stage(
            yp.reshape(bsz, t, D_MODEL), mod_p, p, rope_p)
        nb = _prompt_attention(qh, iqcat, iw, ktb, vb, ikcat, p["gna"], tri)
        yp = _post(yp, na.reshape(bsz * t, LRU_W), nb.reshape(bsz * t, ATT_W), mod_p, p, t, POST_ROW_TILE)
        outs_p.append((jnp.transpose(kt, (0, 3, 1, 2)), jnp.transpose(vt, (0, 3, 1, 2)),
                       jnp.transpose(ikt, (0, 2, 1)), h_last.reshape(bsz, LRU_W), conv_new))

        (na_s, q_s, k_s, v_s, ik_s, iq_s, iw_s, h_s, conv_s) = _sample_stage(
            ys, mod_s, p, rope_s, state_conv[l].reshape(dbsz, (CONV_W - 1) * LRU_W), state_h[l])
        score = _sample_scores(page_table, iq_s.reshape(dbsz, IDX_HEADS, IDX_DIM),
                               iw_s[:, IW_LANE:IW_LANE + IDX_HEADS].reshape(dbsz, IDX_HEADS, 1),
                               jnp.transpose(cache_ik[l], (0, 2, 1)))
        bias, bias_new = _sample_select(score.reshape(dbsz, past_len), iq_s, ik_s, iw_s, tri)
        nb_s = _sample_attention(page_table, q_s.reshape(dbsz, N_HEADS, HEAD_DIM),
                                 k_s.reshape(dbsz, 1, KV_W), v_s.reshape(dbsz, 1, KV_W),
                                 bias.reshape(dbsz, 1, past_len), bias_new.reshape(dbsz, 1, LANES), p["gna8"],
                                 jnp.transpose(cache_k[l], (0, 2, 3, 1)), jnp.transpose(cache_v[l], (0, 2, 3, 1)))
        ys = _post(ys, na_s, nb_s.reshape(dbsz, ATT_W), mod_s, p, 1, dbsz)
        outs_s.append((k_s.reshape(dbsz, dt, N_KV_HEADS, HEAD_DIM), v_s.reshape(dbsz, dt, N_KV_HEADS, HEAD_DIM),
                       ik_s.reshape(dbsz, dt, IDX_DIM), h_s,
                       conv_s.reshape(dbsz, CONV_W - 1, LRU_W)))

    def stack(outs, i):
        return jnp.stack([o[i] for o in outs])

    return (yp.reshape(bsz, t, D_MODEL), ys.reshape(dbsz, dt, D_MODEL),
            stack(outs_p, 0), stack(outs_p, 1), stack(outs_p, 2), stack(outs_p, 3), stack(outs_p, 4),
            stack(outs_s, 0), stack(outs_s, 1), stack(outs_s, 2), stack(outs_s, 3), stack(outs_s, 4))
```

```python
import functools

import numpy as np
import jax
import jax.numpy as jnp
from jax import lax
from jax.experimental import pallas as pl
from jax.experimental.pallas import tpu as pltpu

F32 = jnp.float32
BF16 = jnp.bfloat16
I32 = jnp.int32
I16 = jnp.int16

D_MODEL = 1024
LRU_W = 512
LRU_BLOCKS = 8
LRU_BW = LRU_W // LRU_BLOCKS
CONV_W = 4
LRU_C = 8.0
N_HEADS = 8
HEAD_DIM = 64
ATT_W = N_HEADS * HEAD_DIM
N_KV_HEADS = 2
KV_GROUP = N_HEADS // N_KV_HEADS
KV_W = N_KV_HEADS * HEAD_DIM
ROT_DIM = HEAD_DIM // 4
ROPE_THETA = 500000.0
IDX_HEADS = 4
IDX_DIM = 64
IDX_W = IDX_HEADS * IDX_DIM
TOPK_MAX = 256
PAGE_SIZE = 128
D_FF = 2816
EPS = 1e-6

COL_XR = 0
COL_GR = COL_XR + LRU_W
COL_Q = COL_GR + LRU_W
COL_K = COL_Q + ATT_W
COL_V = COL_K + KV_W
COL_IQ = COL_V + KV_W
MAIN_W = COL_IQ
TAIL_W = 128
IDXP_W = IDX_W + TAIL_W
IW_LANE = IDX_DIM
IDX_CAT = 6 * IDX_DIM

LANES = 128
SUBLANES = 8
TRI_W = 256
INT_MIN = -2147483648
HALF_MIN = -32768
NEG_BIG = -1e30
VMEM_LIMIT = 48 * 1024 * 1024

PROMPT_ROW_TILE = 256
Q_BLOCK = 256
POST_ROW_TILE = 512
FF_CHUNK = D_FF // 2
ADA_COL_TILE = 512
DECODE_GROUP = 4


def _dot(a, b):
    return jnp.dot(a, b, preferred_element_type=F32)


def _dot_nt(a, b):
    return lax.dot_general(a, b, (((1,), (1,)), ((), ())), preferred_element_type=F32)


def _split_bf16(x):
    hi = x.astype(BF16)
    lo = (x - hi.astype(F32)).astype(BF16)
    return hi, lo


def _split3(x):
    p1 = x.astype(BF16)
    r1 = x - p1.astype(F32)
    p2 = r1.astype(BF16)
    p3 = (r1 - p2.astype(F32)).astype(BF16)
    return p1, p2, p3


def _dot_f32(a, b):
    a1, a2, a3 = a
    b1, b2, b3 = b
    return ((_dot(a1, b3) + _dot(a2, b2) + _dot(a3, b1)) + (_dot(a1, b2) + _dot(a2, b1))) + _dot(a1, b1)


def _cat6_lhs(x, axis):
    r1 = x - x.astype(BF16).astype(F32)
    r2 = r1 - r1.astype(BF16).astype(F32)
    return jnp.concatenate([x, x, r1, x, r1, r2], axis=axis).astype(BF16)


def _cat6_rhs(x, axis):
    p1, p2, p3 = _split3(x)
    return jnp.concatenate([p1, p2, p1, p3, p2, p1], axis=axis)


def _rms(x, g):
    return x * lax.rsqrt(jnp.mean(x * x, axis=-1, keepdims=True) + EPS) * g


def _head_rms(x, ones_bd, g):
    hi, lo = _split_bf16(x * x)
    ss = _dot(hi, ones_bd) + _dot(lo, ones_bd)
    return x * lax.rsqrt(ss * (1.0 / HEAD_DIM) + EPS) * g


def _rope(x, cos, sin_next, sin_prev):
    w = x.shape[-1]
    half = ROT_DIM // 2
    return x * cos + pltpu.roll(x, w - half, 1) * sin_next + pltpu.roll(x, half, 1) * sin_prev


def _rope_tables(tab, reps):
    cos = tab[:, 0:LANES]
    sa = tab[:, LANES:2 * LANES]
    sb = tab[:, 2 * LANES:3 * LANES]
    if reps > 1:
        cos = jnp.concatenate([cos] * reps, axis=1)
        sa = jnp.concatenate([sa] * reps, axis=1)
        sb = jnp.concatenate([sb] * reps, axis=1)
    return cos, sa, sb


def _rope_tail(zt, tab):
    cos, sa, sb = _rope_tables(tab, 1)
    lane = lax.broadcasted_iota(I32, zt.shape, 1)
    first = lane < IDX_DIM
    return _rope(zt, jnp.where(first, cos, 1.0), jnp.where(first, sa, 0.0), jnp.where(first, sb, 0.0))


def _gelu_tanh(x):
    c = float(np.sqrt(2.0 / np.pi))
    return x * (0.5 * (1.0 + jnp.tanh(c * (x + 0.044715 * (x * x * x)))))


def _softplus(x):
    return jnp.maximum(x, 0.0) + jnp.log1p(jnp.exp(-jnp.abs(x)))


def _project_in(x, sh1, sc1, ln1, w_main, w_idx_parts):
    h = _rms(x, ln1) * (1.0 + sc1) + sh1
    hp = _split3(h)
    return _dot(hp[0], w_main), _dot_f32(hp, w_idx_parts)


def _lru_gates(xc, wa, ba, wx, bx, lam):
    xcb = xc.astype(BF16)
    r = jax.nn.sigmoid(_dot(xcb, wa) + ba)
    gi = jax.nn.sigmoid(_dot(xcb, wx) + bx)
    log_a = (-LRU_C * _softplus(-lam)) * r
    a = jnp.exp(log_a)
    y = jnp.tanh(-log_a) * (1.0 + a * a)
    inp = jnp.where(y > 0.0, y * lax.rsqrt(y), 0.0) * gi * xc
    return a, inp


def _attention_inputs(z, zi, tab, qg, kg, ones_q, ones_k):
    cq, saq, sbq = _rope_tables(tab, ATT_W // LANES)
    ck, sak, sbk = _rope_tables(tab, KV_W // LANES)
    ci, sai, sbi = _rope_tables(tab, IDX_W // LANES)
    q = _rope(_head_rms(z[:, COL_Q:COL_K], ones_q, qg), cq, saq, sbq) * (HEAD_DIM ** -0.5)
    k = _rope(_head_rms(z[:, COL_K:COL_V], ones_k, kg), ck, sak, sbk)
    v = z[:, COL_V:COL_IQ]
    iq = _rope(zi[:, 0:IDX_W], ci, sai, sbi)
    tail = _rope_tail(zi[:, IDX_W:IDXP_W], tab)
    return q, k, v, iq, tail


def _ada_kernel(c_ref, w_ref, b_ref, o_ref):
    c = c_ref[...]
    s = c * jax.nn.sigmoid(c)
    o_ref[...] = _dot_f32(_split3(s), _split3(w_ref[...])) + b_ref[...]


def _ada(c_all, ada_w, ada_b):
    rows = c_all.shape[0]
    n = ada_w.shape[1]
    return pl.pallas_call(
        _ada_kernel,
        grid=(n // ADA_COL_TILE,),
        in_specs=[
            pl.BlockSpec((rows, D_MODEL), lambda j: (0, 0)),
            pl.BlockSpec((D_MODEL, ADA_COL_TILE), lambda j: (0, j)),
            pl.BlockSpec((1, ADA_COL_TILE), lambda j: (0, j)),
        ],
        out_specs=pl.BlockSpec((rows, ADA_COL_TILE), lambda j: (0, j)),
        out_shape=jax.ShapeDtypeStruct((rows, n), F32),
        compiler_params=pltpu.CompilerParams(dimension_semantics=("arbitrary",), vmem_limit_bytes=VMEM_LIMIT),
        name="ada_modulation",
    )(c_all, ada_w, ada_b)


def _prompt_stage_kernel(x_ref, mod_ref, ln1_ref, wmain_ref, widx_ref, convw_ref, convb_ref,
                         wa_ref, ba_ref, wx_ref, bx_ref, lam_ref, qg_ref, kg_ref, gnl_ref,
                         onesq_ref, onesk_ref, rope_ref,
                         na_ref, q_ref, kt_ref, vt_ref, ikt_ref, ktb_ref, vb_ref,
                         ikcat_ref, iqcat_ref, iw_ref, h_ref, conv_ref,
                         xr_buf, hcarry, widx_parts):
    tt = x_ref.shape[1]
    t = pl.program_id(1)

    @pl.when((t == 0) & (pl.program_id(0) == 0))
    def _():
        for j, part in enumerate(_split3(widx_ref[...])):
            widx_parts[j] = part

    @pl.when(t == 0)
    def _():
        xr_buf[0:SUBLANES, :] = jnp.zeros((SUBLANES, LRU_W), F32)
        hcarry[...] = jnp.zeros_like(hcarry)

    mod = mod_ref[0]
    z, zi = _project_in(x_ref[0], mod[:, 0:D_MODEL], mod[:, D_MODEL:2 * D_MODEL], ln1_ref[...],
                        wmain_ref[...], (widx_parts[0], widx_parts[1], widx_parts[2]))

    xr = z[:, COL_XR:COL_GR]
    xr_buf[SUBLANES:SUBLANES + tt, :] = xr
    convw = convw_ref[...]
    xc = convb_ref[...]
    for i in range(CONV_W):
        xc = xc + xr_buf[pl.ds(SUBLANES - (CONV_W - 1) + i, tt), :] * convw[i:i + 1, :]
    conv_ref[0] = xr_buf[pl.ds(SUBLANES + tt - (CONV_W - 1), CONV_W - 1), :]
    xr_buf[0:SUBLANES, :] = xr_buf[tt:tt + SUBLANES, :]

    a, b = _lru_gates(xc, wa_ref[...], ba_ref[...], wx_ref[...], bx_ref[...], lam_ref[...])
    row = lax.broadcasted_iota(I32, a.shape, 0) % SUBLANES
    d = 1
    while d < SUBLANES:
        keep = row >= d
        a_s = jnp.where(keep, pltpu.roll(a, d, 0), 1.0)
        b_s = jnp.where(keep, pltpu.roll(b, d, 0), 0.0)
        b = a * b_s + b
        a = a * a_s
        d *= 2
    h_in = hcarry[...]
    groups = []
    for g in range(tt // SUBLANES):
        rows = slice(g * SUBLANES, (g + 1) * SUBLANES)
        hg = a[rows] * h_in + b[rows]
        h_in = hg[SUBLANES - 1:SUBLANES, :]
        groups.append(hg)
    hr = jnp.concatenate(groups, axis=0)
    hcarry[...] = h_in
    h_ref[0] = h_in

    na_ref[0] = _rms(_gelu_tanh(z[:, COL_GR:COL_Q]) * hr, gnl_ref[...]).astype(BF16)

    q, k, v, iq, tail = _attention_inputs(z, zi, rope_ref[...], qg_ref[...], kg_ref[...],
                                          onesq_ref[...], onesk_ref[...])
    for h in range(N_HEADS):
        q_ref[0, h] = q[:, h * HEAD_DIM:(h + 1) * HEAD_DIM].astype(BF16)
    kt = jnp.transpose(k)
    vt = jnp.transpose(v)
    ikt = jnp.transpose(tail)[0:IDX_DIM, :]
    kt_ref[0] = kt.reshape(N_KV_HEADS, HEAD_DIM, tt)
    vt_ref[0] = vt.reshape(N_KV_HEADS, HEAD_DIM, tt)
    ikt_ref[0] = ikt
    ktb_ref[0] = kt.astype(BF16).reshape(N_KV_HEADS, HEAD_DIM, tt)
    for n in range(N_KV_HEADS):
        vb_ref[0, n] = v[:, n * HEAD_DIM:(n + 1) * HEAD_DIM].astype(BF16)
    ikcat_ref[0] = _cat6_rhs(ikt, 0)
    for h in range(IDX_HEADS):
        iqcat_ref[0, h] = _cat6_lhs(iq[:, h * IDX_DIM:(h + 1) * IDX_DIM], 1)
    iw_ref[0] = zi[:, IDX_W:IDXP_W]


def _const_spec(shape):
    nd = len(shape)
    return pl.BlockSpec(shape, lambda *_: (0,) * nd)


def _prompt_stage(x, mod_p, p, rope_tab):
    bsz, t, _ = x.shape
    tt = PROMPT_ROW_TILE

    def rows(w):
        return pl.BlockSpec((1, tt, w), lambda b, i: (b, i, 0))

    in_specs = [
        rows(D_MODEL),
        pl.BlockSpec((1, 1, 6 * D_MODEL), lambda b, i: (b, 0, 0)),
        _const_spec((1, D_MODEL)),
        _const_spec((D_MODEL, MAIN_W)),
        _const_spec((D_MODEL, IDXP_W)),
        _const_spec((CONV_W, LRU_W)),
        _const_spec((1, LRU_W)),
        _const_spec((LRU_W, LRU_W)),
        _const_spec((1, LRU_W)),
        _const_spec((LRU_W, LRU_W)),
        _const_spec((1, LRU_W)),
        _const_spec((1, LRU_W)),
        _const_spec((1, ATT_W)),
        _const_spec((1, KV_W)),
        _const_spec((1, LRU_W)),
        _const_spec((ATT_W, ATT_W)),
        _const_spec((KV_W, KV_W)),
        pl.BlockSpec((tt, 3 * LANES), lambda b, i: (i, 0)),
    ]

    def heads(n, w):
        return pl.BlockSpec((1, n, tt, w), lambda b, i: (b, 0, i, 0))

    def heads_t(n):
        return pl.BlockSpec((1, n, HEAD_DIM, tt), lambda b, i: (b, 0, 0, i))

    def feat_t(w):
        return pl.BlockSpec((1, w, tt), lambda b, i: (b, 0, i))

    out_specs = [
        rows(LRU_W), heads(N_HEADS, HEAD_DIM), heads_t(N_KV_HEADS), heads_t(N_KV_HEADS), feat_t(IDX_DIM),
        heads_t(N_KV_HEADS), heads(N_KV_HEADS, HEAD_DIM), feat_t(IDX_CAT), heads(IDX_HEADS, IDX_CAT), rows(TAIL_W),
        pl.BlockSpec((1, 1, LRU_W), lambda b, i: (b, 0, 0)),
        pl.BlockSpec((1, CONV_W - 1, LRU_W), lambda b, i: (b, 0, 0)),
    ]
    out_shape = [
        jax.ShapeDtypeStruct((bsz, t, LRU_W), BF16),
        jax.ShapeDtypeStruct((bsz, N_HEADS, t, HEAD_DIM), BF16),
        jax.ShapeDtypeStruct((bsz, N_KV_HEADS, HEAD_DIM, t), F32),
        jax.ShapeDtypeStruct((bsz, N_KV_HEADS, HEAD_DIM, t), F32),
        jax.ShapeDtypeStruct((bsz, IDX_DIM, t), F32),
        jax.ShapeDtypeStruct((bsz, N_KV_HEADS, HEAD_DIM, t), BF16),
        jax.ShapeDtypeStruct((bsz, N_KV_HEADS, t, HEAD_DIM), BF16),
        jax.ShapeDtypeStruct((bsz, IDX_CAT, t), BF16),
        jax.ShapeDtypeStruct((bsz, IDX_HEADS, t, IDX_CAT), BF16),
        jax.ShapeDtypeStruct((bsz, t, TAIL_W), F32),
        jax.ShapeDtypeStruct((bsz, 1, LRU_W), F32),
        jax.ShapeDtypeStruct((bsz, CONV_W - 1, LRU_W), F32),
    ]
    return pl.pallas_call(
        _prompt_stage_kernel,
        grid=(bsz, t // tt),
        in_specs=in_specs,
        out_specs=out_specs,
        out_shape=out_shape,
        scratch_shapes=[pltpu.VMEM((tt + SUBLANES, LRU_W), F32), pltpu.VMEM((1, LRU_W), F32),
                        pltpu.VMEM((3, D_MODEL, IDXP_W), BF16)],
        compiler_params=pltpu.CompilerParams(dimension_semantics=("arbitrary", "arbitrary"),
                                             vmem_limit_bytes=VMEM_LIMIT),
        name="prompt_stage",
    )(x, mod_p, p["ln1"], p["w_main"], p["w_idx"], p["conv_w"], p["conv_b"], p["wa"], p["ba"],
      p["wx"], p["bx"], p["lam"], p["qg"], p["kg"], p["gnl"], p["ones_q"], p["ones_k"], rope_tab)


def _sortable_key(score):
    score = jnp.where(score == 0.0, 0.0, score)
    bits = lax.bitcast_convert_type(score, I32)
    return jnp.where(bits < 0, bits ^ jnp.int32(0x7FFFFFFF), bits)


def _count_ge16(ref, c, strict=False):
    c16 = c.astype(I16)
    hit = ref[...] > c16 if strict else ref[...] >= c16
    ones = jnp.where(hit, jnp.int16(1), jnp.int16(0))
    acc = ones[:, 0:LANES]
    for j in range(1, ref.shape[1] // LANES):
        acc = acc + ones[:, j * LANES:(j + 1) * LANES]
    return jnp.sum(acc.astype(F32), axis=1, keepdims=True)


def _bitwise_max16(count_ge, k):
    def body(i, t):
        c = t + lax.shift_left(jnp.int32(1), jnp.int32(15) - i)
        return jnp.where(count_ge(c) >= k, c, t)

    return lax.fori_loop(0, 16, body, jnp.full(k.shape, HALF_MIN, I32), unroll=True)


def _kth_largest(key_ref, hi_ref, lo_ref, k, extra=None):
    key = key_ref[...]
    hi_ref[...] = lax.shift_right_arithmetic(key, 16).astype(I16)
    lo_ref[...] = ((key & 0xFFFF) + HALF_MIN).astype(I16)
    kf = jnp.full((key_ref.shape[0], 1), k, F32)
    if extra is not None:
        hi_x = lax.shift_right_arithmetic(extra, 16)
        lo_x = (extra & 0xFFFF) + HALF_MIN

    def count_hi(c):
        cnt = _count_ge16(hi_ref, c)
        return cnt if extra is None else cnt + jnp.where(hi_x >= c, 1.0, 0.0)

    t_hi = _bitwise_max16(count_hi, kf)
    above = _count_ge16(hi_ref, t_hi, strict=True)
    if extra is not None:
        above = above + jnp.where(hi_x > t_hi, 1.0, 0.0)
    k_lo = kf - above
    lo_ref[...] = jnp.where(hi_ref[...] == t_hi.astype(I16), lo_ref[...], jnp.int16(HALF_MIN))
    if extra is not None:
        lo_x = jnp.where(hi_x == t_hi, lo_x, HALF_MIN)

    def count_lo(c):
        cnt = _count_ge16(lo_ref, c)
        return cnt if extra is None else cnt + jnp.where(lo_x >= c, 1.0, 0.0)

    t_lo = _bitwise_max16(count_lo, k_lo)
    return t_hi * 65536 + (t_lo - HALF_MIN)


def _selection_bias(key_ref, hi_ref, lo_ref, bias_ref, tri, k, extra=None):
    thr = _kth_largest(key_ref, hi_ref, lo_ref, k, extra)
    n_gt = jnp.sum(jnp.where(key_ref[...] > thr, 1.0, 0.0), axis=1, keepdims=True)
    if extra is not None:
        n_gt = n_gt + jnp.where(extra > thr, 1.0, 0.0)
    need = k - n_gt
    offset = jnp.zeros_like(need)
    for c in range(key_ref.shape[1] // TRI_W):
        kc = key_ref[:, c * TRI_W:(c + 1) * TRI_W]
        eq = kc == thr
        e = jnp.where(eq, 1.0, 0.0)
        incl = _dot(e.astype(BF16), tri)
        rank = incl - e + offset
        tie = jnp.where(rank < need, 0.0, NEG_BIG)
        bias_ref[:, c * TRI_W:(c + 1) * TRI_W] = jnp.where(kc > thr, 0.0, jnp.where(eq, tie, NEG_BIG))
        offset = offset + incl[:, TRI_W - 1:TRI_W]
    if extra is None:
        return None
    tie = jnp.where(offset < need, 0.0, NEG_BIG)
    return jnp.where(extra > thr, 0.0, jnp.where(extra == thr, tie, NEG_BIG))


def _prompt_attn_kernel(q_ref, iq_ref, iw_ref, kt_ref, v_ref, ikt_ref, gn_ref, tri_ref, o_ref,
                        key_ref, hi_ref, lo_ref, bias_ref, *, q_start):
    qb, n_keys = bias_ref.shape
    qpos = q_start + lax.broadcasted_iota(I32, (qb, n_keys), 0)
    kpos = lax.broadcasted_iota(I32, (qb, n_keys), 1)
    causal = kpos <= qpos
    if n_keys > TOPK_MAX:
        iw = iw_ref[0]
        ikt = ikt_ref[0]
        score = jnp.zeros((qb, n_keys), F32)
        for h in range(IDX_HEADS):
            score = score + iw[:, IW_LANE + h:IW_LANE + h + 1] * jnp.maximum(_dot(iq_ref[0, h], ikt), 0.0)
        key_ref[...] = jnp.where(causal, _sortable_key(score), INT_MIN)
        _selection_bias(key_ref, hi_ref, lo_ref, bias_ref, tri_ref[...], float(TOPK_MAX))
        bias_ref[...] = jnp.where(causal, bias_ref[...], NEG_BIG)
    else:
        bias_ref[...] = jnp.where(causal, 0.0, NEG_BIG)

    outs = []
    for h in range(N_HEADS):
        n = h // KV_GROUP
        logits = _dot(q_ref[0, h], kt_ref[0, n]) + bias_ref[...]
        m = jnp.max(logits, axis=1, keepdims=True)
        p = jnp.exp(logits - m)
        den = jnp.sum(p, axis=1, keepdims=True)
        outs.append(_dot(p.astype(BF16), v_ref[0, n]) / den)
    o_ref[0] = _rms(jnp.concatenate(outs, axis=1), gn_ref[...]).astype(BF16)


def _prompt_attention_block(qh, iqcat, iw, ktb, vb, ikcat, gn_att, tri, q_block):
    bsz, _, t, _ = qh.shape
    qb = Q_BLOCK
    n_keys = (q_block + 1) * qb

    def qheads(n, w):
        return pl.BlockSpec((1, n, qb, w), lambda b: (b, 0, q_block, 0))

    return pl.pallas_call(
        functools.partial(_prompt_attn_kernel, q_start=q_block * qb),
        grid=(bsz,),
        in_specs=[qheads(N_HEADS, HEAD_DIM), qheads(IDX_HEADS, IDX_CAT),
                  pl.BlockSpec((1, qb, TAIL_W), lambda b: (b, q_block, 0)),
                  pl.BlockSpec((1, N_KV_HEADS, HEAD_DIM, n_keys), lambda b: (b, 0, 0, 0)),
                  pl.BlockSpec((1, N_KV_HEADS, n_keys, HEAD_DIM), lambda b: (b, 0, 0, 0)),
                  pl.BlockSpec((1, IDX_CAT, n_keys), lambda b: (b, 0, 0)),
                  _const_spec((1, ATT_W)), _const_spec((TRI_W, TRI_W))],
        out_specs=pl.BlockSpec((1, qb, ATT_W), lambda b: (b, 0, 0)),
        out_shape=jax.ShapeDtypeStruct((bsz, qb, ATT_W), BF16),
        scratch_shapes=[pltpu.VMEM((qb, n_keys), I32), pltpu.VMEM((qb, n_keys), I16), pltpu.VMEM((qb, n_keys), I16),
                        pltpu.VMEM((qb, n_keys), F32)],
        compiler_params=pltpu.CompilerParams(dimension_semantics=("arbitrary",), vmem_limit_bytes=VMEM_LIMIT),
        name=f"prompt_attention_{q_block}",
    )(qh, iqcat, iw, ktb, vb, ikcat, gn_att, tri)


def _prompt_attention(qh, iqcat, iw, ktb, vb, ikcat, gn_att, tri):
    t = qh.shape[2]
    blocks = [_prompt_attention_block(qh, iqcat, iw, ktb, vb, ikcat, gn_att, tri, i) for i in range(t // Q_BLOCK)]
    return jnp.concatenate(blocks, axis=1)


def _post_kernel(x_ref, na_ref, nb_ref, mod_ref, woa_ref, wob_ref, ln2_ref, wg_ref, wu_ref, wd_ref,
                 y_ref, *, shared_mod):
    mod = mod_ref[0] if shared_mod else mod_ref[...]
    g1 = mod[:, 2 * D_MODEL:3 * D_MODEL]
    sh2 = mod[:, 3 * D_MODEL:4 * D_MODEL]
    sc2 = mod[:, 4 * D_MODEL:5 * D_MODEL]
    g2 = mod[:, 5 * D_MODEL:6 * D_MODEL]
    x = x_ref[...]
    mix = _dot(na_ref[...], woa_ref[...]) + _dot(nb_ref[...], wob_ref[...])
    x1 = x + g1 * mix
    h2 = (_rms(x1, ln2_ref[...]) * (1.0 + sc2) + sh2).astype(BF16)
    ff = jnp.zeros_like(x1)
    for c in range(D_FF // FF_CHUNK):
        lo, hi = c * FF_CHUNK, (c + 1) * FF_CHUNK
        g = _dot(h2, wg_ref[:, lo:hi])
        u = _dot(h2, wu_ref[:, lo:hi])
        act = (g * jax.nn.sigmoid(g) * u).astype(BF16)
        ff = ff + _dot(act, wd_ref[lo:hi, :])
    y_ref[...] = x1 + g2 * ff


def _post(x, na, nb, mod, p, rows_per_mod, tile):
    n = x.shape[0]
    shared = rows_per_mod > 1
    if shared:
        mod_spec = pl.BlockSpec((1, 1, 6 * D_MODEL), lambda i: (i * tile // rows_per_mod, 0, 0))
    else:
        mod_spec = pl.BlockSpec((tile, 6 * D_MODEL), lambda i: (i, 0))

    def rows(w):
        return pl.BlockSpec((tile, w), lambda i: (i, 0))

    def weight(shape):
        return pl.BlockSpec(shape, lambda i: (0, 0), pipeline_mode=pl.Buffered(1))

    return pl.pallas_call(
        functools.partial(_post_kernel, shared_mod=shared),
        grid=(n // tile,),
        in_specs=[rows(D_MODEL), rows(LRU_W), rows(ATT_W), mod_spec,
                  weight((LRU_W, D_MODEL)), weight((ATT_W, D_MODEL)), weight((1, D_MODEL)),
                  weight((D_MODEL, D_FF)), weight((D_MODEL, D_FF)), weight((D_FF, D_MODEL))],
        out_specs=rows(D_MODEL),
        out_shape=jax.ShapeDtypeStruct((n, D_MODEL), F32),
        compiler_params=pltpu.CompilerParams(dimension_semantics=("arbitrary",), vmem_limit_bytes=VMEM_LIMIT),
        name="post_ffn",
    )(x, na, nb, mod, p["wo_a"], p["wo_b"], p["ln2"], p["wg"], p["wu"], p["wd"])


def _sample_stage_kernel(x_ref, mod_ref, ln1_ref, wmain_ref, widx_ref, convw_ref, convb_ref,
                         wa_ref, ba_ref, wx_ref, bx_ref, lam_ref, qg_ref, kg_ref, gnl_ref,
                         onesq_ref, onesk_ref, rope_ref, sconv_ref, sh_ref,
                         na_ref, q_ref, k_ref, v_ref, ik_ref, iq_ref, iw_ref, h_ref, conv_ref):
    mod = mod_ref[...]
    z, zi = _project_in(x_ref[...], mod[:, 0:D_MODEL], mod[:, D_MODEL:2 * D_MODEL], ln1_ref[...],
                        wmain_ref[...], _split3(widx_ref[...]))
    xr = z[:, COL_XR:COL_GR]
    convw = convw_ref[...]
    sconv = sconv_ref[...]
    xc = convb_ref[...]
    for i in range(CONV_W - 1):
        xc = xc + sconv[:, i * LRU_W:(i + 1) * LRU_W] * convw[i:i + 1, :]
    xc = xc + xr * convw[CONV_W - 1:CONV_W, :]
    conv_ref[:, 0:(CONV_W - 2) * LRU_W] = sconv[:, LRU_W:(CONV_W - 1) * LRU_W]
    conv_ref[:, (CONV_W - 2) * LRU_W:(CONV_W - 1) * LRU_W] = xr

    a, b = _lru_gates(xc, wa_ref[...], ba_ref[...], wx_ref[...], bx_ref[...], lam_ref[...])
    hr = a * sh_ref[...] + b
    h_ref[...] = hr
    na_ref[...] = _rms(_gelu_tanh(z[:, COL_GR:COL_Q]) * hr, gnl_ref[...]).astype(BF16)

    q, k, v, iq, tail = _attention_inputs(z, zi, rope_ref[...], qg_ref[...], kg_ref[...],
                                          onesq_ref[...], onesk_ref[...])
    q_ref[...] = q
    k_ref[...] = k
    v_ref[...] = v
    ik_ref[...] = tail[:, 0:IDX_DIM]
    iq_ref[...] = iq
    iw_ref[...] = zi[:, IDX_W:IDXP_W]


def _sample_stage(x, mod_s, p, rope_row, sconv, sh):
    n = x.shape[0]

    def out(w, dt=F32):
        return jax.ShapeDtypeStruct((n, w), dt)

    return pl.pallas_call(
        _sample_stage_kernel,
        out_shape=[out(LRU_W, BF16), out(ATT_W), out(KV_W), out(KV_W), out(IDX_DIM), out(IDX_W),
                   out(TAIL_W), out(LRU_W), out((CONV_W - 1) * LRU_W)],
        compiler_params=pltpu.CompilerParams(vmem_limit_bytes=VMEM_LIMIT),
        name="sample_stage",
    )(x, mod_s, p["ln1"], p["w_main"], p["w_idx"], p["conv_w"], p["conv_b"], p["wa"], p["ba"],
      p["wx"], p["bx"], p["lam"], p["qg"], p["kg"], p["gnl"], p["ones_q"], p["ones_k"], rope_row, sconv, sh)


def _page_copy(cache_ref, buf_ref, sem, page, slot, j, pg):
    return pltpu.make_async_copy(cache_ref.at[page], buf_ref.at[slot, j, pg], sem)


def _paged_prefetch(pt_ref, caches, bufs, sems, group, n_pages):
    s = pl.program_id(0)
    slot = s % 2

    def start(grp, half):
        for cache_ref, buf_ref, sem in zip(caches, bufs, sems):
            for j in range(group):
                for pg in range(n_pages):
                    page = pt_ref[grp * group + j, pg]
                    _page_copy(cache_ref, buf_ref, sem.at[half], page, half, j, pg).start()

    @pl.when(s == 0)
    def _():
        start(0, 0)

    @pl.when(s + 1 < pl.num_programs(0))
    def _():
        start(s + 1, 1 - slot)

    for cache_ref, buf_ref, sem in zip(caches, bufs, sems):
        for j in range(group):
            for pg in range(n_pages):
                _page_copy(cache_ref, buf_ref, sem.at[slot], 0, slot, j, pg).wait()
    return slot


def _sample_score_kernel(pt_ref, iq_ref, iw_ref, ik_hbm, o_ref, ik_buf, sem, *, n_pages):
    group = iq_ref.shape[0]
    slot = _paged_prefetch(pt_ref, (ik_hbm,), (ik_buf,), (sem,), group, n_pages)
    iqcat = [_cat6_lhs(iq_ref[j], 1) for j in range(group)]
    for pg in range(n_pages):
        for j in range(group):
            s = _dot(iqcat[j], _cat6_rhs(ik_buf[slot, j, pg], 0))
            o_ref[j, :, pg * PAGE_SIZE:(pg + 1) * PAGE_SIZE] = jnp.sum(iw_ref[j] * jnp.maximum(s, 0.0), axis=0,
                                                                        keepdims=True)


def _sample_scores(page_table, iq3, iw3, cache_ik):
    n, n_pages = page_table.shape
    g = DECODE_GROUP
    grid_spec = pltpu.PrefetchScalarGridSpec(
        num_scalar_prefetch=1,
        grid=(n // g,),
        in_specs=[pl.BlockSpec((g, IDX_HEADS, IDX_DIM), lambda s, pt: (s, 0, 0)),
                  pl.BlockSpec((g, IDX_HEADS, 1), lambda s, pt: (s, 0, 0)),
                  pl.BlockSpec(memory_space=pl.ANY)],
        out_specs=pl.BlockSpec((g, 1, n_pages * PAGE_SIZE), lambda s, pt: (s, 0, 0)),
        scratch_shapes=[pltpu.VMEM((2, g, n_pages, IDX_DIM, PAGE_SIZE), F32), pltpu.SemaphoreType.DMA((2,))],
    )
    return pl.pallas_call(
        functools.partial(_sample_score_kernel, n_pages=n_pages),
        grid_spec=grid_spec,
        out_shape=jax.ShapeDtypeStruct((n, 1, n_pages * PAGE_SIZE), F32),
        compiler_params=pltpu.CompilerParams(dimension_semantics=("arbitrary",), vmem_limit_bytes=VMEM_LIMIT),
        name="sample_scores",
    )(page_table, iq3, iw3, cache_ik)


def _sample_select_kernel(score_ref, iq_ref, ik_ref, iw_ref, tri_ref, bias_ref, bias_new_ref,
                          key_ref, hi_ref, lo_ref):
    iq = iq_ref[...]
    ik = ik_ref[...]
    iw = iw_ref[...]
    new = jnp.zeros((iq.shape[0], 1), F32)
    for h in range(IDX_HEADS):
        s = jnp.sum(iq[:, h * IDX_DIM:(h + 1) * IDX_DIM] * ik, axis=1, keepdims=True)
        new = new + iw[:, IW_LANE + h:IW_LANE + h + 1] * jnp.maximum(s, 0.0)
    key_ref[...] = _sortable_key(score_ref[...])
    b_new = _selection_bias(key_ref, hi_ref, lo_ref, bias_ref, tri_ref[...], float(TOPK_MAX),
                            extra=_sortable_key(new))
    bias_new_ref[...] = jnp.broadcast_to(b_new, bias_new_ref.shape)


def _sample_select(score, iq, ik, iw, tri):
    n, n_keys = score.shape
    return pl.pallas_call(
        _sample_select_kernel,
        out_shape=[jax.ShapeDtypeStruct((n, n_keys), F32), jax.ShapeDtypeStruct((n, LANES), F32)],
        scratch_shapes=[pltpu.VMEM((n, n_keys), I32), pltpu.VMEM((n, n_keys), I16), pltpu.VMEM((n, n_keys), I16)],
        compiler_params=pltpu.CompilerParams(vmem_limit_bytes=VMEM_LIMIT),
        name="sample_select",
    )(score, iq, ik, iw, tri)


def _sample_attn_kernel(pt_ref, q_ref, kn_ref, vn_ref, bias_ref, bnew_ref, gn_ref, k_hbm, v_hbm, o_ref,
                        k_buf, v_buf, k_sem, v_sem, *, n_pages):
    group = q_ref.shape[0]
    seqs = range(group)
    slot = _paged_prefetch(pt_ref, (k_hbm, v_hbm), (k_buf, v_buf), (k_sem, v_sem), group, n_pages)
    row = lax.broadcasted_iota(I32, (N_HEADS, HEAD_DIM), 0)
    first = row < KV_GROUP
    q = [q_ref[j] for j in seqs]
    q2 = [jnp.concatenate([jnp.where(first, q[j], 0.0), jnp.where(first, 0.0, q[j])], axis=1).astype(BF16)
          for j in seqs]
    logits = [[None] * n_pages for _ in seqs]
    for pg in range(n_pages):
        for j in seqs:
            kp = k_buf[slot, j, pg].astype(BF16).reshape(KV_W, PAGE_SIZE)
            logits[j][pg] = _dot(q2[j], kp) + bias_ref[j][:, pg * PAGE_SIZE:(pg + 1) * PAGE_SIZE]
    m, p_new, den, acc, v_sel = [], [], [], [], []
    for j in seqs:
        kn = kn_ref[j]
        vn = vn_ref[j]
        k_sel = jnp.where(first, kn[:, 0:HEAD_DIM], kn[:, HEAD_DIM:KV_W])
        v_sel.append(jnp.where(first, vn[:, 0:HEAD_DIM], vn[:, HEAD_DIM:KV_W]))
        l_new = jnp.sum(q[j] * k_sel, axis=1, keepdims=True) + bnew_ref[j][:, 0:1]
        mj = l_new
        for l in logits[j]:
            mj = jnp.maximum(mj, jnp.max(l, axis=1, keepdims=True))
        m.append(mj)
        p_new.append(jnp.exp(l_new - mj))
        den.append(p_new[j])
        acc.append(jnp.zeros((N_HEADS, KV_W), F32))
    for pg in range(n_pages):
        for j in seqs:
            p = jnp.exp(logits[j][pg] - m[j])
            den[j] = den[j] + jnp.sum(p, axis=1, keepdims=True)
            vp = v_buf[slot, j, pg].astype(BF16).reshape(KV_W, PAGE_SIZE)
            acc[j] = acc[j] + _dot_nt(p.astype(BF16), vp)
    for j in seqs:
        out = (p_new[j] * v_sel[j] + jnp.where(first, acc[j][:, 0:HEAD_DIM], acc[j][:, HEAD_DIM:KV_W])) / den[j]
        ms = jnp.sum(jnp.sum(out * out, axis=1, keepdims=True), axis=0, keepdims=True) * (1.0 / ATT_W)
        o_ref[j] = (out * lax.rsqrt(ms + EPS) * gn_ref[...]).astype(BF16)


def _sample_attention(page_table, q3, k_new, v_new, bias, bias_new, gn8, cache_k, cache_v):
    n, n_pages = page_table.shape
    n_keys = n_pages * PAGE_SIZE

    g = DECODE_GROUP

    def per_seq(shape):
        return pl.BlockSpec((g,) + shape, lambda s, pt: (s, 0, 0))

    page_buf = pltpu.VMEM((2, g, n_pages, N_KV_HEADS, HEAD_DIM, PAGE_SIZE), F32)
    grid_spec = pltpu.PrefetchScalarGridSpec(
        num_scalar_prefetch=1,
        grid=(n // g,),
        in_specs=[per_seq((N_HEADS, HEAD_DIM)), per_seq((1, KV_W)), per_seq((1, KV_W)),
                  per_seq((1, n_keys)), per_seq((1, LANES)),
                  pl.BlockSpec((N_HEADS, HEAD_DIM), lambda s, pt: (0, 0)),
                  pl.BlockSpec(memory_space=pl.ANY), pl.BlockSpec(memory_space=pl.ANY)],
        out_specs=per_seq((N_HEADS, HEAD_DIM)),
        scratch_shapes=[page_buf, page_buf, pltpu.SemaphoreType.DMA((2,)), pltpu.SemaphoreType.DMA((2,))],
    )
    return pl.pallas_call(
        functools.partial(_sample_attn_kernel, n_pages=n_pages),
        grid_spec=grid_spec,
        out_shape=jax.ShapeDtypeStruct((n, N_HEADS, HEAD_DIM), BF16),
        compiler_params=pltpu.CompilerParams(dimension_semantics=("arbitrary",), vmem_limit_bytes=VMEM_LIMIT),
        name="sample_attention",
    )(page_table, q3, k_new, v_new, bias, bias_new, gn8, cache_k, cache_v)


def _rope_table_np(positions):
    half = ROT_DIM // 2
    freq = ROPE_THETA ** (-(np.arange(half, dtype=np.float64) / half))
    ang = np.asarray(positions, np.float64)[:, None] * freq[None, :]
    cos, sin = np.cos(ang), np.sin(ang)
    n = len(positions)
    tab = np.zeros((n, 3, LANES), np.float64)
    tab[:, 0, :] = 1.0
    for base in range(0, LANES, HEAD_DIM):
        tab[:, 0, base:base + half] = cos
        tab[:, 0, base + half:base + ROT_DIM] = cos
        tab[:, 1, base:base + half] = -sin
        tab[:, 2, base + half:base + ROT_DIM] = sin
    return tab.reshape(n, 3 * LANES).astype(np.float32)


def _block_diag(w):
    n, a, b = w.shape
    return jnp.einsum("nij,nm->nimj", w, jnp.eye(n, dtype=w.dtype)).reshape(n * a, n * b)


def _layer_params(l, ln1_g, w_in, conv_w, conv_b, lru_wa, lru_ba, lru_wx, lru_bx, lru_lambda, q_norm_g,
                  k_norm_g, gn_lru_g, gn_att_g, w_out, ln2_g, w_gate, w_up, w_down):
    w_idx = jnp.pad(w_in[l][:, COL_IQ:], ((0, 0), (0, IDXP_W - (w_in.shape[2] - COL_IQ))))
    head_id = np.arange(ATT_W) // HEAD_DIM
    ones_q = (head_id[:, None] == head_id[None, :]).astype(np.float32)
    return {
        "ln1": ln1_g[l][None, :],
        "w_main": w_in[l][:, :COL_IQ].astype(BF16),
        "w_idx": w_idx,
        "conv_w": conv_w[l],
        "conv_b": conv_b[l][None, :],
        "wa": _block_diag(lru_wa[l]).astype(BF16),
        "ba": lru_ba[l][None, :],
        "wx": _block_diag(lru_wx[l]).astype(BF16),
        "bx": lru_bx[l][None, :],
        "lam": lru_lambda[l][None, :],
        "qg": jnp.tile(q_norm_g[l], N_HEADS)[None, :],
        "kg": jnp.tile(k_norm_g[l], N_KV_HEADS)[None, :],
        "gnl": gn_lru_g[l][None, :],
        "gna": gn_att_g[l][None, :],
        "gna8": gn_att_g[l].reshape(N_HEADS, HEAD_DIM),
        "ones_q": jnp.asarray(ones_q, BF16),
        "ones_k": jnp.asarray(ones_q[:KV_W, :KV_W], BF16),
        "wo_a": w_out[l][:LRU_W].astype(BF16),
        "wo_b": w_out[l][LRU_W:].astype(BF16),
        "ln2": ln2_g[l][None, :],
        "wg": w_gate[l].astype(BF16),
        "wu": w_up[l].astype(BF16),
        "wd": w_down[l].astype(BF16),
    }


def kernel(x_prompt, x_sample, cache_k, cache_v, cache_ik, state_h, state_conv, page_table, c_prompt, c_sample, ada_w, ada_b, ln1_g, w_in, conv_w, conv_b, lru_wa, lru_ba, lru_wx, lru_bx, lru_lambda, q_norm_g, k_norm_g, gn_lru_g, gn_att_g, w_out, ln2_g, w_gate, w_up, w_down):
    bsz, t, _ = x_prompt.shape
    dbsz, dt, _ = x_sample.shape
    depth = ada_w.shape[0]
    n_pages = page_table.shape[1]
    past_len = n_pages * PAGE_SIZE
    n_pool = cache_k.shape[1]
    assert dt == 1 and t % PROMPT_ROW_TILE == 0 and t % POST_ROW_TILE == 0 and t // 4 >= TOPK_MAX
    assert (past_len + dt) // 4 >= TOPK_MAX

    rope_p = jnp.asarray(_rope_table_np(np.arange(t)))
    rope_s = jnp.asarray(_rope_table_np(past_len + np.arange(dt)))
    tri = jnp.asarray(np.triu(np.ones((TRI_W, TRI_W), np.float32)), BF16)

    yp = x_prompt.reshape(bsz * t, D_MODEL)
    ys = x_sample.reshape(dbsz, D_MODEL)
    c_all = jnp.concatenate([c_prompt, c_sample], axis=0)
    outs_p, outs_s = [], []
    for l in range(depth):
        p = _layer_params(l, ln1_g, w_in, conv_w, conv_b, lru_wa, lru_ba, lru_wx, lru_bx, lru_lambda,
                          q_norm_g, k_norm_g, gn_lru_g, gn_att_g, w_out, ln2_g, w_gate, w_up, w_down)
        mod = _ada(c_all, ada_w[l], ada_b[l][None, :])
        mod_p = mod[:bsz].reshape(bsz, 1, 6 * D_MODEL)
        mod_s = mod[bsz:]

        (na, qh, kt, vt, ikt, ktb, vb, ikcat, iqcat, iw, h_last, conv_new) = _prompt_stage(
            yp.reshape(bsz, t, D_MODEL), mod_p, p, rope_p)
        nb = _prompt_attention(qh, iqcat, iw, ktb, vb, ikcat, p["gna"], tri)
        yp = _post(yp, na.reshape(bsz * t, LRU_W), nb.reshape(bsz * t, ATT_W), mod_p, p, t, POST_ROW_TILE)
        outs_p.append((jnp.transpose(kt, (0, 3, 1, 2)), jnp.transpose(vt, (0, 3, 1, 2)),
                       jnp.transpose(ikt, (0, 2, 1)), h_last.reshape(bsz, LRU_W), conv_new))

        (na_s, q_s, k_s, v_s, ik_s, iq_s, iw_s, h_s, conv_s) = _sample_stage(
            ys, mod_s, p, rope_s, state_conv[l].reshape(dbsz, (CONV_W - 1) * LRU_W), state_h[l])
        score = _sample_scores(page_table, iq_s.reshape(dbsz, IDX_HEADS, IDX_DIM),
                               iw_s[:, IW_LANE:IW_LANE + IDX_HEADS].reshape(dbsz, IDX_HEADS, 1),
                               jnp.transpose(cache_ik[l], (0, 2, 1)))
        bias, bias_new = _sample_select(score.reshape(dbsz, past_len), iq_s, ik_s, iw_s, tri)
        nb_s = _sample_attention(page_table, q_s.reshape(dbsz, N_HEADS, HEAD_DIM),
                                 k_s.reshape(dbsz, 1, KV_W), v_s.reshape(dbsz, 1, KV_W),
                                 bias.reshape(dbsz, 1, past_len), bias_new.reshape(dbsz, 1, LANES), p["gna8"],
                                 jnp.transpose(cache_k[l], (0, 2, 3, 1)), jnp.transpose(cache_v[l], (0, 2, 3, 1)))
        ys = _post(ys, na_s, nb_s.reshape(dbsz, ATT_W), mod_s, p, 1, dbsz)
        outs_s.append((k_s.reshape(dbsz, dt, N_KV_HEADS, HEAD_DIM), v_s.reshape(dbsz, dt, N_KV_HEADS, HEAD_DIM),
                       ik_s.reshape(dbsz, dt, IDX_DIM), h_s,
                       conv_s.reshape(dbsz, CONV_W - 1, LRU_W)))

    def stack(outs, i):
        return jnp.stack([o[i] for o in outs])

    return (yp.reshape(bsz, t, D_MODEL), ys.reshape(dbsz, dt, D_MODEL),
            stack(outs_p, 0), stack(outs_p, 1), stack(outs_p, 2), stack(outs_p, 3), stack(outs_p, 4),
            stack(outs_s, 0), stack(outs_s, 1), stack(outs_s, 2), stack(outs_s, 3), stack(outs_s, 4))
```

```python
import functools

import numpy as np
import jax
import jax.numpy as jnp
from jax import lax
from jax.experimental import pallas as pl
from jax.experimental.pallas import tpu as pltpu

F32 = jnp.float32
BF16 = jnp.bfloat16
I32 = jnp.int32
I16 = jnp.int16

D_MODEL = 1024
LRU_W = 512
LRU_BLOCKS = 8
LRU_BW = LRU_W // LRU_BLOCKS
CONV_W = 4
LRU_C = 8.0
N_HEADS = 8
HEAD_DIM = 64
ATT_W = N_HEADS * HEAD_DIM
N_KV_HEADS = 2
KV_GROUP = N_HEADS // N_KV_HEADS
KV_W = N_KV_HEADS * HEAD_DIM
ROT_DIM = HEAD_DIM // 4
ROPE_THETA = 500000.0
IDX_HEADS = 4
IDX_DIM = 64
IDX_W = IDX_HEADS * IDX_DIM
TOPK_MAX = 256
PAGE_SIZE = 128
D_FF = 2816
EPS = 1e-6

COL_XR = 0
COL_GR = COL_XR + LRU_W
COL_Q = COL_GR + LRU_W
COL_K = COL_Q + ATT_W
COL_V = COL_K + KV_W
COL_IQ = COL_V + KV_W
MAIN_W = COL_IQ
TAIL_W = 128
IDXP_W = IDX_W + TAIL_W
IW_LANE = IDX_DIM
IDX_CAT = 6 * IDX_DIM

LANES = 128
SUBLANES = 8
TRI_W = 256
INT_MIN = -2147483648
HALF_MIN = -32768
LOG2E = 1.4426950408889634
NEG_BIG = -1e30
VMEM_LIMIT = 48 * 1024 * 1024

PROMPT_ROW_TILE = 256
Q_BLOCK = 256
POST_ROW_TILE = 512
FF_CHUNK = D_FF // 2
ADA_COL_TILE = 512
DECODE_GROUP = 4


def _dot(a, b):
    return jnp.dot(a, b, preferred_element_type=F32)


def _dot_nt(a, b):
    return lax.dot_general(a, b, (((1,), (1,)), ((), ())), preferred_element_type=F32)


def _split_bf16(x):
    hi = x.astype(BF16)
    lo = (x - hi.astype(F32)).astype(BF16)
    return hi, lo


def _split3(x):
    p1 = x.astype(BF16)
    r1 = x - p1.astype(F32)
    p2 = r1.astype(BF16)
    p3 = (r1 - p2.astype(F32)).astype(BF16)
    return p1, p2, p3


def _dot_f32(a, b):
    a1, a2, a3 = a
    b1, b2, b3 = b
    return ((_dot(a1, b3) + _dot(a2, b2) + _dot(a3, b1)) + (_dot(a1, b2) + _dot(a2, b1))) + _dot(a1, b1)


def _cat6_lhs(x, axis):
    r1 = x - x.astype(BF16).astype(F32)
    r2 = r1 - r1.astype(BF16).astype(F32)
    return jnp.concatenate([x, x, r1, x, r1, r2], axis=axis).astype(BF16)


def _cat6_rhs(x, axis):
    p1, p2, p3 = _split3(x)
    return jnp.concatenate([p1, p2, p1, p3, p2, p1], axis=axis)


def _rms(x, g):
    return x * lax.rsqrt(jnp.mean(x * x, axis=-1, keepdims=True) + EPS) * g


def _head_rms(x, ones_bd, g):
    hi, lo = _split_bf16(x * x)
    ss = _dot(hi, ones_bd) + _dot(lo, ones_bd)
    return x * lax.rsqrt(ss * (1.0 / HEAD_DIM) + EPS) * g


def _rope(x, cos, sin_next, sin_prev):
    w = x.shape[-1]
    half = ROT_DIM // 2
    return x * cos + pltpu.roll(x, w - half, 1) * sin_next + pltpu.roll(x, half, 1) * sin_prev


def _rope_tables(tab, reps):
    cos = tab[:, 0:LANES]
    sa = tab[:, LANES:2 * LANES]
    sb = tab[:, 2 * LANES:3 * LANES]
    if reps > 1:
        cos = jnp.concatenate([cos] * reps, axis=1)
        sa = jnp.concatenate([sa] * reps, axis=1)
        sb = jnp.concatenate([sb] * reps, axis=1)
    return cos, sa, sb


def _rope_tail(zt, tab):
    cos, sa, sb = _rope_tables(tab, 1)
    lane = lax.broadcasted_iota(I32, zt.shape, 1)
    first = lane < IDX_DIM
    return _rope(zt, jnp.where(first, cos, 1.0), jnp.where(first, sa, 0.0), jnp.where(first, sb, 0.0))


def _gelu_tanh(x):
    c = float(np.sqrt(2.0 / np.pi))
    return x * (0.5 * (1.0 + jnp.tanh(c * (x + 0.044715 * (x * x * x)))))


def _softplus(x):
    return jnp.maximum(x, 0.0) + jnp.log1p(jnp.exp(-jnp.abs(x)))


def _project_in(x, sh1, sc1, ln1, w_main, w_idx_parts):
    h = _rms(x, ln1) * (1.0 + sc1) + sh1
    hp = _split3(h)
    return _dot(hp[0], w_main), _dot_f32(hp, w_idx_parts)


def _lru_gates(xc, wa, ba, wx, bx, lam):
    xcb = xc.astype(BF16)
    r = jax.nn.sigmoid(_dot(xcb, wa) + ba)
    gi = jax.nn.sigmoid(_dot(xcb, wx) + bx)
    log_a = (-LRU_C * _softplus(-lam)) * r
    a = jnp.exp(log_a)
    y = jnp.tanh(-log_a) * (1.0 + a * a)
    inp = jnp.where(y > 0.0, y * lax.rsqrt(y), 0.0) * gi * xc
    return a, inp


def _attention_inputs(z, zi, tab, qg, kg, ones_q, ones_k):
    cq, saq, sbq = _rope_tables(tab, ATT_W // LANES)
    ck, sak, sbk = _rope_tables(tab, KV_W // LANES)
    ci, sai, sbi = _rope_tables(tab, IDX_W // LANES)
    q = _rope(_head_rms(z[:, COL_Q:COL_K], ones_q, qg), cq, saq, sbq) * (HEAD_DIM ** -0.5 * LOG2E)
    k = _rope(_head_rms(z[:, COL_K:COL_V], ones_k, kg), ck, sak, sbk)
    v = z[:, COL_V:COL_IQ]
    iq = _rope(zi[:, 0:IDX_W], ci, sai, sbi)
    tail = _rope_tail(zi[:, IDX_W:IDXP_W], tab)
    return q, k, v, iq, tail


def _ada_kernel(c_ref, w_ref, b_ref, o_ref):
    c = c_ref[...]
    s = c * jax.nn.sigmoid(c)
    o_ref[...] = _dot_f32(_split3(s), _split3(w_ref[...])) + b_ref[...]


def _ada(c_all, ada_w, ada_b):
    rows = c_all.shape[0]
    n = ada_w.shape[1]
    return pl.pallas_call(
        _ada_kernel,
        grid=(n // ADA_COL_TILE,),
        in_specs=[
            pl.BlockSpec((rows, D_MODEL), lambda j: (0, 0)),
            pl.BlockSpec((D_MODEL, ADA_COL_TILE), lambda j: (0, j)),
            pl.BlockSpec((1, ADA_COL_TILE), lambda j: (0, j)),
        ],
        out_specs=pl.BlockSpec((rows, ADA_COL_TILE), lambda j: (0, j)),
        out_shape=jax.ShapeDtypeStruct((rows, n), F32),
        compiler_params=pltpu.CompilerParams(dimension_semantics=("arbitrary",), vmem_limit_bytes=VMEM_LIMIT),
        name="ada_modulation",
    )(c_all, ada_w, ada_b)


def _prompt_stage_kernel(x_ref, mod_ref, ln1_ref, wmain_ref, widx_ref, convw_ref, convb_ref,
                         wa_ref, ba_ref, wx_ref, bx_ref, lam_ref, qg_ref, kg_ref, gnl_ref,
                         onesq_ref, onesk_ref, rope_ref,
                         na_ref, q_ref, kt_ref, vt_ref, ikt_ref, ktb_ref, vb_ref,
                         ikcat_ref, iqcat_ref, iw_ref, h_ref, conv_ref,
                         xr_buf, hcarry, widx_parts):
    tt = x_ref.shape[1]
    t = pl.program_id(1)

    @pl.when((t == 0) & (pl.program_id(0) == 0))
    def _():
        for j, part in enumerate(_split3(widx_ref[...])):
            widx_parts[j] = part

    @pl.when(t == 0)
    def _():
        xr_buf[0:SUBLANES, :] = jnp.zeros((SUBLANES, LRU_W), F32)
        hcarry[...] = jnp.zeros_like(hcarry)

    mod = mod_ref[0]
    h1, h2, h3 = _split3(_rms(x_ref[0], ln1_ref[...]) * (1.0 + mod[:, D_MODEL:2 * D_MODEL]) + mod[:, 0:D_MODEL])
    z = _dot(h1, wmain_ref[...])
    zi_small = _dot(h1, widx_parts[2]) + _dot(h2, widx_parts[1])

    xr = z[:, COL_XR:COL_GR]
    xr_buf[SUBLANES:SUBLANES + tt, :] = xr
    convw = convw_ref[...]
    xc = convb_ref[...]
    for i in range(CONV_W):
        xc = xc + xr_buf[pl.ds(SUBLANES - (CONV_W - 1) + i, tt), :] * convw[i:i + 1, :]
    conv_ref[0] = xr_buf[pl.ds(SUBLANES + tt - (CONV_W - 1), CONV_W - 1), :]
    xr_buf[0:SUBLANES, :] = xr_buf[tt:tt + SUBLANES, :]

    zi_small = zi_small + _dot(h3, widx_parts[0])
    zi_mid = _dot(h1, widx_parts[1])
    a, b = _lru_gates(xc, wa_ref[...], ba_ref[...], wx_ref[...], bx_ref[...], lam_ref[...])
    zi_mid = zi_mid + _dot(h2, widx_parts[0])
    zi_big = _dot(h1, widx_parts[0])
    row = lax.broadcasted_iota(I32, a.shape, 0) % SUBLANES
    d = 1
    while d < SUBLANES:
        keep = row >= d
        a_s = jnp.where(keep, pltpu.roll(a, d, 0), 1.0)
        b_s = jnp.where(keep, pltpu.roll(b, d, 0), 0.0)
        b = a * b_s + b
        a = a * a_s
        d *= 2
    h_in = hcarry[...]
    groups = []
    for g in range(tt // SUBLANES):
        rows = slice(g * SUBLANES, (g + 1) * SUBLANES)
        hg = a[rows] * h_in + b[rows]
        h_in = hg[SUBLANES - 1:SUBLANES, :]
        groups.append(hg)
    hr = jnp.concatenate(groups, axis=0)
    hcarry[...] = h_in
    h_ref[0] = h_in

    na_ref[0] = _rms(_gelu_tanh(z[:, COL_GR:COL_Q]) * hr, gnl_ref[...]).astype(BF16)

    zi = (zi_small + zi_mid) + zi_big
    q, k, v, iq, tail = _attention_inputs(z, zi, rope_ref[...], qg_ref[...], kg_ref[...],
                                          onesq_ref[...], onesk_ref[...])
    for h in range(N_HEADS):
        q_ref[0, h] = q[:, h * HEAD_DIM:(h + 1) * HEAD_DIM].astype(BF16)
    kt = jnp.transpose(k)
    vt = jnp.transpose(v)
    ikt = jnp.transpose(tail)[0:IDX_DIM, :]
    kt_ref[0] = kt.reshape(N_KV_HEADS, HEAD_DIM, tt)
    vt_ref[0] = vt.reshape(N_KV_HEADS, HEAD_DIM, tt)
    ikt_ref[0] = ikt
    ktb_ref[0] = kt.astype(BF16).reshape(N_KV_HEADS, HEAD_DIM, tt)
    for n in range(N_KV_HEADS):
        vb_ref[0, n] = v[:, n * HEAD_DIM:(n + 1) * HEAD_DIM].astype(BF16)
    ikcat_ref[0] = _cat6_rhs(ikt, 0)
    for h in range(IDX_HEADS):
        iqcat_ref[0, h] = _cat6_lhs(iq[:, h * IDX_DIM:(h + 1) * IDX_DIM], 1)
    iw_ref[0] = zi[:, IDX_W:IDXP_W]


def _const_spec(shape):
    nd = len(shape)
    return pl.BlockSpec(shape, lambda *_: (0,) * nd)


def _prompt_stage(x, mod_p, p, rope_tab):
    bsz, t, _ = x.shape
    tt = PROMPT_ROW_TILE

    def rows(w):
        return pl.BlockSpec((1, tt, w), lambda b, i: (b, i, 0))

    in_specs = [
        rows(D_MODEL),
        pl.BlockSpec((1, 1, 6 * D_MODEL), lambda b, i: (b, 0, 0)),
        _const_spec((1, D_MODEL)),
        _const_spec((D_MODEL, MAIN_W)),
        _const_spec((D_MODEL, IDXP_W)),
        _const_spec((CONV_W, LRU_W)),
        _const_spec((1, LRU_W)),
        _const_spec((LRU_W, LRU_W)),
        _const_spec((1, LRU_W)),
        _const_spec((LRU_W, LRU_W)),
        _const_spec((1, LRU_W)),
        _const_spec((1, LRU_W)),
        _const_spec((1, ATT_W)),
        _const_spec((1, KV_W)),
        _const_spec((1, LRU_W)),
        _const_spec((ATT_W, ATT_W)),
        _const_spec((KV_W, KV_W)),
        pl.BlockSpec((tt, 3 * LANES), lambda b, i: (i, 0)),
    ]

    def heads(n, w):
        return pl.BlockSpec((1, n, tt, w), lambda b, i: (b, 0, i, 0))

    def heads_t(n):
        return pl.BlockSpec((1, n, HEAD_DIM, tt), lambda b, i: (b, 0, 0, i))

    def feat_t(w):
        return pl.BlockSpec((1, w, tt), lambda b, i: (b, 0, i))

    out_specs = [
        rows(LRU_W), heads(N_HEADS, HEAD_DIM), heads_t(N_KV_HEADS), heads_t(N_KV_HEADS), feat_t(IDX_DIM),
        heads_t(N_KV_HEADS), heads(N_KV_HEADS, HEAD_DIM), feat_t(IDX_CAT), heads(IDX_HEADS, IDX_CAT), rows(TAIL_W),
        pl.BlockSpec((1, 1, LRU_W), lambda b, i: (b, 0, 0)),
        pl.BlockSpec((1, CONV_W - 1, LRU_W), lambda b, i: (b, 0, 0)),
    ]
    out_shape = [
        jax.ShapeDtypeStruct((bsz, t, LRU_W), BF16),
        jax.ShapeDtypeStruct((bsz, N_HEADS, t, HEAD_DIM), BF16),
        jax.ShapeDtypeStruct((bsz, N_KV_HEADS, HEAD_DIM, t), F32),
        jax.ShapeDtypeStruct((bsz, N_KV_HEADS, HEAD_DIM, t), F32),
        jax.ShapeDtypeStruct((bsz, IDX_DIM, t), F32),
        jax.ShapeDtypeStruct((bsz, N_KV_HEADS, HEAD_DIM, t), BF16),
        jax.ShapeDtypeStruct((bsz, N_KV_HEADS, t, HEAD_DIM), BF16),
        jax.ShapeDtypeStruct((bsz, IDX_CAT, t), BF16),
        jax.ShapeDtypeStruct((bsz, IDX_HEADS, t, IDX_CAT), BF16),
        jax.ShapeDtypeStruct((bsz, t, TAIL_W), F32),
        jax.ShapeDtypeStruct((bsz, 1, LRU_W), F32),
        jax.ShapeDtypeStruct((bsz, CONV_W - 1, LRU_W), F32),
    ]
    return pl.pallas_call(
        _prompt_stage_kernel,
        grid=(bsz, t // tt),
        in_specs=in_specs,
        out_specs=out_specs,
        out_shape=out_shape,
        scratch_shapes=[pltpu.VMEM((tt + SUBLANES, LRU_W), F32), pltpu.VMEM((1, LRU_W), F32),
                        pltpu.VMEM((3, D_MODEL, IDXP_W), BF16)],
        compiler_params=pltpu.CompilerParams(dimension_semantics=("arbitrary", "arbitrary"),
                                             vmem_limit_bytes=VMEM_LIMIT),
        name="prompt_stage",
    )(x, mod_p, p["ln1"], p["w_main"], p["w_idx"], p["conv_w"], p["conv_b"], p["wa"], p["ba"],
      p["wx"], p["bx"], p["lam"], p["qg"], p["kg"], p["gnl"], p["ones_q"], p["ones_k"], rope_tab)


def _sortable_key(score):
    score = jnp.where(score == 0.0, 0.0, score)
    bits = lax.bitcast_convert_type(score, I32)
    return jnp.where(bits < 0, bits ^ jnp.int32(0x7FFFFFFF), bits)


def _count_ge16(ref, c, strict=False):
    c16 = c.astype(I16)
    hit = ref[...] > c16 if strict else ref[...] >= c16
    ones = jnp.where(hit, jnp.int16(1), jnp.int16(0))
    acc = ones[:, 0:LANES]
    for j in range(1, ref.shape[1] // LANES):
        acc = acc + ones[:, j * LANES:(j + 1) * LANES]
    return jnp.sum(acc.astype(F32), axis=1, keepdims=True)


def _bitwise_max16(count_ge, k):
    def body(i, t):
        c = t + lax.shift_left(jnp.int32(1), jnp.int32(15) - i)
        return jnp.where(count_ge(c) >= k, c, t)

    return lax.fori_loop(0, 16, body, jnp.full(k.shape, HALF_MIN, I32), unroll=True)


def _kth_largest(key_ref, hi_ref, lo_ref, k, extra=None):
    key = key_ref[...]
    hi_ref[...] = lax.shift_right_arithmetic(key, 16).astype(I16)
    lo_ref[...] = ((key & 0xFFFF) + HALF_MIN).astype(I16)
    kf = jnp.full((key_ref.shape[0], 1), k, F32)
    if extra is not None:
        hi_x = lax.shift_right_arithmetic(extra, 16)
        lo_x = (extra & 0xFFFF) + HALF_MIN

    def count_hi(c):
        cnt = _count_ge16(hi_ref, c)
        return cnt if extra is None else cnt + jnp.where(hi_x >= c, 1.0, 0.0)

    t_hi = _bitwise_max16(count_hi, kf)
    above = _count_ge16(hi_ref, t_hi, strict=True)
    if extra is not None:
        above = above + jnp.where(hi_x > t_hi, 1.0, 0.0)
    k_lo = kf - above
    lo_ref[...] = jnp.where(hi_ref[...] == t_hi.astype(I16), lo_ref[...], jnp.int16(HALF_MIN))
    if extra is not None:
        lo_x = jnp.where(hi_x == t_hi, lo_x, HALF_MIN)

    def count_lo(c):
        cnt = _count_ge16(lo_ref, c)
        return cnt if extra is None else cnt + jnp.where(lo_x >= c, 1.0, 0.0)

    t_lo = _bitwise_max16(count_lo, k_lo)
    return t_hi * 65536 + (t_lo - HALF_MIN)


def _selection_bias(key_ref, hi_ref, lo_ref, bias_ref, tri, k, extra=None):
    thr = _kth_largest(key_ref, hi_ref, lo_ref, k, extra)
    n_gt = jnp.sum(jnp.where(key_ref[...] > thr, 1.0, 0.0), axis=1, keepdims=True)
    if extra is not None:
        n_gt = n_gt + jnp.where(extra > thr, 1.0, 0.0)
    need = k - n_gt
    offset = jnp.zeros_like(need)
    for c in range(key_ref.shape[1] // TRI_W):
        kc = key_ref[:, c * TRI_W:(c + 1) * TRI_W]
        eq = kc == thr
        e = jnp.where(eq, 1.0, 0.0)
        incl = _dot(e.astype(BF16), tri)
        rank = incl - e + offset
        tie = jnp.where(rank < need, 0.0, NEG_BIG)
        bias_ref[:, c * TRI_W:(c + 1) * TRI_W] = jnp.where(kc > thr, 0.0, jnp.where(eq, tie, NEG_BIG))
        offset = offset + incl[:, TRI_W - 1:TRI_W]
    if extra is None:
        return None
    tie = jnp.where(offset < need, 0.0, NEG_BIG)
    return jnp.where(extra > thr, 0.0, jnp.where(extra == thr, tie, NEG_BIG))


def _prompt_attn_kernel(q_ref, iq_ref, iw_ref, kt_ref, v_ref, ikt_ref, gn_ref, tri_ref, o_ref,
                        key_ref, hi_ref, lo_ref, bias_ref, *, q_start):
    qb, n_keys = bias_ref.shape
    qpos = q_start + lax.broadcasted_iota(I32, (qb, n_keys), 0)
    kpos = lax.broadcasted_iota(I32, (qb, n_keys), 1)
    causal = kpos <= qpos
    if n_keys > TOPK_MAX:
        iw = iw_ref[0]
        ikt = ikt_ref[0]
        score = jnp.zeros((qb, n_keys), F32)
        s_next = _dot(iq_ref[0, 0], ikt)
        for h in range(IDX_HEADS):
            s, s_next = s_next, (_dot(iq_ref[0, h + 1], ikt) if h + 1 < IDX_HEADS else None)
            score = score + iw[:, IW_LANE + h:IW_LANE + h + 1] * jnp.maximum(s, 0.0)
        key_ref[...] = jnp.where(causal, _sortable_key(score), INT_MIN)
        _selection_bias(key_ref, hi_ref, lo_ref, bias_ref, tri_ref[...], float(TOPK_MAX))
        bias_ref[...] = jnp.where(causal, bias_ref[...], NEG_BIG)
    else:
        bias_ref[...] = jnp.where(causal, 0.0, NEG_BIG)

    def qk(h):
        return _dot(q_ref[0, h], kt_ref[0, h // KV_GROUP]) + bias_ref[...]

    outs = []
    logits_next = qk(0)
    for h in range(N_HEADS):
        logits, logits_next = logits_next, (qk(h + 1) if h + 1 < N_HEADS else None)
        m = jnp.max(logits, axis=1, keepdims=True)
        p = jnp.exp2(logits - m)
        den = jnp.sum(p, axis=1, keepdims=True)
        outs.append(_dot(p.astype(BF16), v_ref[0, h // KV_GROUP]) / den)
    o_ref[0] = _rms(jnp.concatenate(outs, axis=1), gn_ref[...]).astype(BF16)


def _prompt_attention_block(qh, iqcat, iw, ktb, vb, ikcat, gn_att, tri, q_block):
    bsz, _, t, _ = qh.shape
    qb = Q_BLOCK
    n_keys = (q_block + 1) * qb

    def qheads(n, w):
        return pl.BlockSpec((1, n, qb, w), lambda b: (b, 0, q_block, 0))

    return pl.pallas_call(
        functools.partial(_prompt_attn_kernel, q_start=q_block * qb),
        grid=(bsz,),
        in_specs=[qheads(N_HEADS, HEAD_DIM), qheads(IDX_HEADS, IDX_CAT),
                  pl.BlockSpec((1, qb, TAIL_W), lambda b: (b, q_block, 0)),
                  pl.BlockSpec((1, N_KV_HEADS, HEAD_DIM, n_keys), lambda b: (b, 0, 0, 0)),
                  pl.BlockSpec((1, N_KV_HEADS, n_keys, HEAD_DIM), lambda b: (b, 0, 0, 0)),
                  pl.BlockSpec((1, IDX_CAT, n_keys), lambda b: (b, 0, 0)),
                  _const_spec((1, ATT_W)), _const_spec((TRI_W, TRI_W))],
        out_specs=pl.BlockSpec((1, qb, ATT_W), lambda b: (b, 0, 0)),
        out_shape=jax.ShapeDtypeStruct((bsz, qb, ATT_W), BF16),
        scratch_shapes=[pltpu.VMEM((qb, n_keys), I32), pltpu.VMEM((qb, n_keys), I16), pltpu.VMEM((qb, n_keys), I16),
                        pltpu.VMEM((qb, n_keys), F32)],
        compiler_params=pltpu.CompilerParams(dimension_semantics=("arbitrary",), vmem_limit_bytes=VMEM_LIMIT),
        name=f"prompt_attention_{q_block}",
    )(qh, iqcat, iw, ktb, vb, ikcat, gn_att, tri)


def _prompt_attention(qh, iqcat, iw, ktb, vb, ikcat, gn_att, tri):
    t = qh.shape[2]
    blocks = [_prompt_attention_block(qh, iqcat, iw, ktb, vb, ikcat, gn_att, tri, i) for i in range(t // Q_BLOCK)]
    return jnp.concatenate(blocks, axis=1)


def _post_kernel(x_ref, na_ref, nb_ref, mod_ref, woa_ref, wob_ref, ln2_ref, wg_ref, wu_ref, wd_ref,
                 y_ref, *, shared_mod):
    mod = mod_ref[0] if shared_mod else mod_ref[...]
    g1 = mod[:, 2 * D_MODEL:3 * D_MODEL]
    sh2 = mod[:, 3 * D_MODEL:4 * D_MODEL]
    sc2 = mod[:, 4 * D_MODEL:5 * D_MODEL]
    g2 = mod[:, 5 * D_MODEL:6 * D_MODEL]
    x = x_ref[...]
    mix = _dot(na_ref[...], woa_ref[...]) + _dot(nb_ref[...], wob_ref[...])
    x1 = x + g1 * mix
    h2 = (_rms(x1, ln2_ref[...]) * (1.0 + sc2) + sh2).astype(BF16)
    def gate_up(c):
        return _dot(h2, wg_ref[:, c * FF_CHUNK:(c + 1) * FF_CHUNK]), _dot(h2, wu_ref[:, c * FF_CHUNK:(c + 1) * FF_CHUNK])

    n_chunks = D_FF // FF_CHUNK
    ff = jnp.zeros_like(x1)
    gu_next = gate_up(0)
    for c in range(n_chunks):
        (g, u), gu_next = gu_next, (gate_up(c + 1) if c + 1 < n_chunks else None)
        act = (g * jax.nn.sigmoid(g) * u).astype(BF16)
        ff = ff + _dot(act, wd_ref[c * FF_CHUNK:(c + 1) * FF_CHUNK, :])
    y_ref[...] = x1 + g2 * ff


def _post(x, na, nb, mod, p, rows_per_mod, tile):
    n = x.shape[0]
    shared = rows_per_mod > 1
    if shared:
        mod_spec = pl.BlockSpec((1, 1, 6 * D_MODEL), lambda i: (i * tile // rows_per_mod, 0, 0))
    else:
        mod_spec = pl.BlockSpec((tile, 6 * D_MODEL), lambda i: (i, 0))

    def rows(w):
        return pl.BlockSpec((tile, w), lambda i: (i, 0))

    def weight(shape):
        return pl.BlockSpec(shape, lambda i: (0, 0), pipeline_mode=pl.Buffered(1))

    return pl.pallas_call(
        functools.partial(_post_kernel, shared_mod=shared),
        grid=(n // tile,),
        in_specs=[rows(D_MODEL), rows(LRU_W), rows(ATT_W), mod_spec,
                  weight((LRU_W, D_MODEL)), weight((ATT_W, D_MODEL)), weight((1, D_MODEL)),
                  weight((D_MODEL, D_FF)), weight((D_MODEL, D_FF)), weight((D_FF, D_MODEL))],
        out_specs=rows(D_MODEL),
        out_shape=jax.ShapeDtypeStruct((n, D_MODEL), F32),
        compiler_params=pltpu.CompilerParams(dimension_semantics=("arbitrary",), vmem_limit_bytes=VMEM_LIMIT),
        name="post_ffn",
    )(x, na, nb, mod, p["wo_a"], p["wo_b"], p["ln2"], p["wg"], p["wu"], p["wd"])


def _sample_stage_kernel(x_ref, mod_ref, ln1_ref, wmain_ref, widx_ref, convw_ref, convb_ref,
                         wa_ref, ba_ref, wx_ref, bx_ref, lam_ref, qg_ref, kg_ref, gnl_ref,
                         onesq_ref, onesk_ref, rope_ref, sconv_ref, sh_ref,
                         na_ref, q_ref, k_ref, v_ref, ik_ref, iq_ref, iw_ref, h_ref, conv_ref):
    mod = mod_ref[...]
    z, zi = _project_in(x_ref[...], mod[:, 0:D_MODEL], mod[:, D_MODEL:2 * D_MODEL], ln1_ref[...],
                        wmain_ref[...], _split3(widx_ref[...]))
    xr = z[:, COL_XR:COL_GR]
    convw = convw_ref[...]
    sconv = sconv_ref[...]
    xc = convb_ref[...]
    for i in range(CONV_W - 1):
        xc = xc + sconv[:, i * LRU_W:(i + 1) * LRU_W] * convw[i:i + 1, :]
    xc = xc + xr * convw[CONV_W - 1:CONV_W, :]
    conv_ref[:, 0:(CONV_W - 2) * LRU_W] = sconv[:, LRU_W:(CONV_W - 1) * LRU_W]
    conv_ref[:, (CONV_W - 2) * LRU_W:(CONV_W - 1) * LRU_W] = xr

    a, b = _lru_gates(xc, wa_ref[...], ba_ref[...], wx_ref[...], bx_ref[...], lam_ref[...])
    hr = a * sh_ref[...] + b
    h_ref[...] = hr
    na_ref[...] = _rms(_gelu_tanh(z[:, COL_GR:COL_Q]) * hr, gnl_ref[...]).astype(BF16)

    q, k, v, iq, tail = _attention_inputs(z, zi, rope_ref[...], qg_ref[...], kg_ref[...],
                                          onesq_ref[...], onesk_ref[...])
    q_ref[...] = q
    k_ref[...] = k
    v_ref[...] = v
    ik_ref[...] = tail[:, 0:IDX_DIM]
    iq_ref[...] = iq
    iw_ref[...] = zi[:, IDX_W:IDXP_W]


def _sample_stage(x, mod_s, p, rope_row, sconv, sh):
    n = x.shape[0]

    def out(w, dt=F32):
        return jax.ShapeDtypeStruct((n, w), dt)

    return pl.pallas_call(
        _sample_stage_kernel,
        out_shape=[out(LRU_W, BF16), out(ATT_W), out(KV_W), out(KV_W), out(IDX_DIM), out(IDX_W),
                   out(TAIL_W), out(LRU_W), out((CONV_W - 1) * LRU_W)],
        compiler_params=pltpu.CompilerParams(vmem_limit_bytes=VMEM_LIMIT),
        name="sample_stage",
    )(x, mod_s, p["ln1"], p["w_main"], p["w_idx"], p["conv_w"], p["conv_b"], p["wa"], p["ba"],
      p["wx"], p["bx"], p["lam"], p["qg"], p["kg"], p["gnl"], p["ones_q"], p["ones_k"], rope_row, sconv, sh)


def _page_copy(cache_ref, buf_ref, sem, page, slot, j, pg):
    return pltpu.make_async_copy(cache_ref.at[page], buf_ref.at[slot, j, pg], sem)


def _paged_prefetch(pt_ref, caches, bufs, sems, group, n_pages):
    s = pl.program_id(0)
    slot = s % 2

    def start(grp, half):
        for cache_ref, buf_ref, sem in zip(caches, bufs, sems):
            for j in range(group):
                for pg in range(n_pages):
                    page = pt_ref[grp * group + j, pg]
                    _page_copy(cache_ref, buf_ref, sem.at[half], page, half, j, pg).start()

    @pl.when(s == 0)
    def _():
        start(0, 0)

    @pl.when(s + 1 < pl.num_programs(0))
    def _():
        start(s + 1, 1 - slot)

    for cache_ref, buf_ref, sem in zip(caches, bufs, sems):
        for j in range(group):
            for pg in range(n_pages):
                _page_copy(cache_ref, buf_ref, sem.at[slot], 0, slot, j, pg).wait()
    return slot


def _sample_score_kernel(pt_ref, iq_ref, iw_ref, ik_hbm, o_ref, ik_buf, sem, *, n_pages):
    group = iq_ref.shape[0]
    slot = _paged_prefetch(pt_ref, (ik_hbm,), (ik_buf,), (sem,), group, n_pages)
    iqcat = [_cat6_lhs(iq_ref[j], 1) for j in range(group)]
    for pg in range(n_pages):
        for j in range(group):
            s = _dot(iqcat[j], _cat6_rhs(ik_buf[slot, j, pg], 0))
            o_ref[j, :, pg * PAGE_SIZE:(pg + 1) * PAGE_SIZE] = jnp.sum(iw_ref[j] * jnp.maximum(s, 0.0), axis=0,
                                                                        keepdims=True)


def _sample_scores(page_table, iq3, iw3, cache_ik):
    n, n_pages = page_table.shape
    g = DECODE_GROUP
    grid_spec = pltpu.PrefetchScalarGridSpec(
        num_scalar_prefetch=1,
        grid=(n // g,),
        in_specs=[pl.BlockSpec((g, IDX_HEADS, IDX_DIM), lambda s, pt: (s, 0, 0)),
                  pl.BlockSpec((g, IDX_HEADS, 1), lambda s, pt: (s, 0, 0)),
                  pl.BlockSpec(memory_space=pl.ANY)],
        out_specs=pl.BlockSpec((g, 1, n_pages * PAGE_SIZE), lambda s, pt: (s, 0, 0)),
        scratch_shapes=[pltpu.VMEM((2, g, n_pages, IDX_DIM, PAGE_SIZE), F32), pltpu.SemaphoreType.DMA((2,))],
    )
    return pl.pallas_call(
        functools.partial(_sample_score_kernel, n_pages=n_pages),
        grid_spec=grid_spec,
        out_shape=jax.ShapeDtypeStruct((n, 1, n_pages * PAGE_SIZE), F32),
        compiler_params=pltpu.CompilerParams(dimension_semantics=("arbitrary",), vmem_limit_bytes=VMEM_LIMIT),
        name="sample_scores",
    )(page_table, iq3, iw3, cache_ik)


def _sample_select_kernel(score_ref, iq_ref, ik_ref, iw_ref, tri_ref, bias_ref, bias_new_ref,
                          key_ref, hi_ref, lo_ref):
    iq = iq_ref[...]
    ik = ik_ref[...]
    iw = iw_ref[...]
    new = jnp.zeros((iq.shape[0], 1), F32)
    for h in range(IDX_HEADS):
        s = jnp.sum(iq[:, h * IDX_DIM:(h + 1) * IDX_DIM] * ik, axis=1, keepdims=True)
        new = new + iw[:, IW_LANE + h:IW_LANE + h + 1] * jnp.maximum(s, 0.0)
    key_ref[...] = _sortable_key(score_ref[...])
    b_new = _selection_bias(key_ref, hi_ref, lo_ref, bias_ref, tri_ref[...], float(TOPK_MAX),
                            extra=_sortable_key(new))
    bias_new_ref[...] = jnp.broadcast_to(b_new, bias_new_ref.shape)


def _sample_select(score, iq, ik, iw, tri):
    n, n_keys = score.shape
    return pl.pallas_call(
        _sample_select_kernel,
        out_shape=[jax.ShapeDtypeStruct((n, n_keys), F32), jax.ShapeDtypeStruct((n, LANES), F32)],
        scratch_shapes=[pltpu.VMEM((n, n_keys), I32), pltpu.VMEM((n, n_keys), I16), pltpu.VMEM((n, n_keys), I16)],
        compiler_params=pltpu.CompilerParams(vmem_limit_bytes=VMEM_LIMIT),
        name="sample_select",
    )(score, iq, ik, iw, tri)


def _sample_attn_kernel(pt_ref, q_ref, kn_ref, vn_ref, bias_ref, bnew_ref, gn_ref, k_hbm, v_hbm, o_ref,
                        k_buf, v_buf, k_sem, v_sem, *, n_pages):
    group = q_ref.shape[0]
    seqs = range(group)
    slot = _paged_prefetch(pt_ref, (k_hbm, v_hbm), (k_buf, v_buf), (k_sem, v_sem), group, n_pages)
    row = lax.broadcasted_iota(I32, (N_HEADS, HEAD_DIM), 0)
    first = row < KV_GROUP
    q = [q_ref[j] for j in seqs]
    q2 = [jnp.concatenate([jnp.where(first, q[j], 0.0), jnp.where(first, 0.0, q[j])], axis=1).astype(BF16)
          for j in seqs]
    logits = [[None] * n_pages for _ in seqs]
    for pg in range(n_pages):
        for j in seqs:
            kp = k_buf[slot, j, pg].astype(BF16).reshape(KV_W, PAGE_SIZE)
            logits[j][pg] = _dot(q2[j], kp) + bias_ref[j][:, pg * PAGE_SIZE:(pg + 1) * PAGE_SIZE]
    m, p_new, den, acc, v_sel = [], [], [], [], []
    for j in seqs:
        kn = kn_ref[j]
        vn = vn_ref[j]
        k_sel = jnp.where(first, kn[:, 0:HEAD_DIM], kn[:, HEAD_DIM:KV_W])
        v_sel.append(jnp.where(first, vn[:, 0:HEAD_DIM], vn[:, HEAD_DIM:KV_W]))
        l_new = jnp.sum(q[j] * k_sel, axis=1, keepdims=True) + bnew_ref[j][:, 0:1]
        mj = l_new
        for l in logits[j]:
            mj = jnp.maximum(mj, jnp.max(l, axis=1, keepdims=True))
        m.append(mj)
        p_new.append(jnp.exp2(l_new - mj))
        den.append(p_new[j])
        acc.append(jnp.zeros((N_HEADS, KV_W), F32))
    for pg in range(n_pages):
        for j in seqs:
            p = jnp.exp2(logits[j][pg] - m[j])
            den[j] = den[j] + jnp.sum(p, axis=1, keepdims=True)
            vp = v_buf[slot, j, pg].astype(BF16).reshape(KV_W, PAGE_SIZE)
            acc[j] = acc[j] + _dot_nt(p.astype(BF16), vp)
    for j in seqs:
        out = (p_new[j] * v_sel[j] + jnp.where(first, acc[j][:, 0:HEAD_DIM], acc[j][:, HEAD_DIM:KV_W])) / den[j]
        ms = jnp.sum(jnp.sum(out * out, axis=1, keepdims=True), axis=0, keepdims=True) * (1.0 / ATT_W)
        o_ref[j] = (out * lax.rsqrt(ms + EPS) * gn_ref[...]).astype(BF16)


def _sample_attention(page_table, q3, k_new, v_new, bias, bias_new, gn8, cache_k, cache_v):
    n, n_pages = page_table.shape
    n_keys = n_pages * PAGE_SIZE

    g = DECODE_GROUP

    def per_seq(shape):
        return pl.BlockSpec((g,) + shape, lambda s, pt: (s, 0, 0))

    page_buf = pltpu.VMEM((2, g, n_pages, N_KV_HEADS, HEAD_DIM, PAGE_SIZE), F32)
    grid_spec = pltpu.PrefetchScalarGridSpec(
        num_scalar_prefetch=1,
        grid=(n // g,),
        in_specs=[per_seq((N_HEADS, HEAD_DIM)), per_seq((1, KV_W)), per_seq((1, KV_W)),
                  per_seq((1, n_keys)), per_seq((1, LANES)),
                  pl.BlockSpec((N_HEADS, HEAD_DIM), lambda s, pt: (0, 0)),
                  pl.BlockSpec(memory_space=pl.ANY), pl.BlockSpec(memory_space=pl.ANY)],
        out_specs=per_seq((N_HEADS, HEAD_DIM)),
        scratch_shapes=[page_buf, page_buf, pltpu.SemaphoreType.DMA((2,)), pltpu.SemaphoreType.DMA((2,))],
    )
    return pl.pallas_call(
        functools.partial(_sample_attn_kernel, n_pages=n_pages),
        grid_spec=grid_spec,
        out_shape=jax.ShapeDtypeStruct((n, N_HEADS, HEAD_DIM), BF16),
        compiler_params=pltpu.CompilerParams(dimension_semantics=("arbitrary",), vmem_limit_bytes=VMEM_LIMIT),
        name="sample_attention",
    )(page_table, q3, k_new, v_new, bias, bias_new, gn8, cache_k, cache_v)


def _rope_table_np(positions):
    half = ROT_DIM // 2
    freq = ROPE_THETA ** (-(np.arange(half, dtype=np.float64) / half))
    ang = np.asarray(positions, np.float64)[:, None] * freq[None, :]
    cos, sin = np.cos(ang), np.sin(ang)
    n = len(positions)
    tab = np.zeros((n, 3, LANES), np.float64)
    tab[:, 0, :] = 1.0
    for base in range(0, LANES, HEAD_DIM):
        tab[:, 0, base:base + half] = cos
        tab[:, 0, base + half:base + ROT_DIM] = cos
        tab[:, 1, base:base + half] = -sin
        tab[:, 2, base + half:base + ROT_DIM] = sin
    return tab.reshape(n, 3 * LANES).astype(np.float32)


def _block_diag(w):
    n, a, b = w.shape
    return jnp.einsum("nij,nm->nimj", w, jnp.eye(n, dtype=w.dtype)).reshape(n * a, n * b)


def _layer_params(l, ln1_g, w_in, conv_w, conv_b, lru_wa, lru_ba, lru_wx, lru_bx, lru_lambda, q_norm_g,
                  k_norm_g, gn_lru_g, gn_att_g, w_out, ln2_g, w_gate, w_up, w_down):
    w_idx = jnp.pad(w_in[l][:, COL_IQ:], ((0, 0), (0, IDXP_W - (w_in.shape[2] - COL_IQ))))
    head_id = np.arange(ATT_W) // HEAD_DIM
    ones_q = (head_id[:, None] == head_id[None, :]).astype(np.float32)
    return {
        "ln1": ln1_g[l][None, :],
        "w_main": w_in[l][:, :COL_IQ].astype(BF16),
        "w_idx": w_idx,
        "conv_w": conv_w[l],
        "conv_b": conv_b[l][None, :],
        "wa": _block_diag(lru_wa[l]).astype(BF16),
        "ba": lru_ba[l][None, :],
        "wx": _block_diag(lru_wx[l]).astype(BF16),
        "bx": lru_bx[l][None, :],
        "lam": lru_lambda[l][None, :],
        "qg": jnp.tile(q_norm_g[l], N_HEADS)[None, :],
        "kg": jnp.tile(k_norm_g[l], N_KV_HEADS)[None, :],
        "gnl": gn_lru_g[l][None, :],
        "gna": gn_att_g[l][None, :],
        "gna8": gn_att_g[l].reshape(N_HEADS, HEAD_DIM),
        "ones_q": jnp.asarray(ones_q, BF16),
        "ones_k": jnp.asarray(ones_q[:KV_W, :KV_W], BF16),
        "wo_a": w_out[l][:LRU_W].astype(BF16),
        "wo_b": w_out[l][LRU_W:].astype(BF16),
        "ln2": ln2_g[l][None, :],
        "wg": w_gate[l].astype(BF16),
        "wu": w_up[l].astype(BF16),
        "wd": w_down[l].astype(BF16),
    }


def kernel(x_prompt, x_sample, cache_k, cache_v, cache_ik, state_h, state_conv, page_table, c_prompt, c_sample, ada_w, ada_b, ln1_g, w_in, conv_w, conv_b, lru_wa, lru_ba, lru_wx, lru_bx, lru_lambda, q_norm_g, k_norm_g, gn_lru_g, gn_att_g, w_out, ln2_g, w_gate, w_up, w_down):
    bsz, t, _ = x_prompt.shape
    dbsz, dt, _ = x_sample.shape
    depth = ada_w.shape[0]
    n_pages = page_table.shape[1]
    past_len = n_pages * PAGE_SIZE
    n_pool = cache_k.shape[1]
    assert dt == 1 and t % PROMPT_ROW_TILE == 0 and t % POST_ROW_TILE == 0 and t // 4 >= TOPK_MAX
    assert (past_len + dt) // 4 >= TOPK_MAX

    rope_p = jnp.asarray(_rope_table_np(np.arange(t)))
    rope_s = jnp.asarray(_rope_table_np(past_len + np.arange(dt)))
    tri = jnp.asarray(np.triu(np.ones((TRI_W, TRI_W), np.float32)), BF16)

    yp = x_prompt.reshape(bsz * t, D_MODEL)
    ys = x_sample.reshape(dbsz, D_MODEL)
    c_all = jnp.concatenate([c_prompt, c_sample], axis=0)
    outs_p, outs_s = [], []
    for l in range(depth):
        p = _layer_params(l, ln1_g, w_in, conv_w, conv_b, lru_wa, lru_ba, lru_wx, lru_bx, lru_lambda,
                          q_norm_g, k_norm_g, gn_lru_g, gn_att_g, w_out, ln2_g, w_gate, w_up, w_down)
        mod = _ada(c_all, ada_w[l], ada_b[l][None, :])
        mod_p = mod[:bsz].reshape(bsz, 1, 6 * D_MODEL)
        mod_s = mod[bsz:]

        (na, qh, kt, vt, ikt, ktb, vb, ikcat, iqcat, iw, h_last, conv_new) = _prompt_stage(
            yp.reshape(bsz, t, D_MODEL), mod_p, p, rope_p)
        nb = _prompt_attention(qh, iqcat, iw, ktb, vb, ikcat, p["gna"], tri)
        yp = _post(yp, na.reshape(bsz * t, LRU_W), nb.reshape(bsz * t, ATT_W), mod_p, p, t, POST_ROW_TILE)
        outs_p.append((jnp.transpose(kt, (0, 3, 1, 2)), jnp.transpose(vt, (0, 3, 1, 2)),
                       jnp.transpose(ikt, (0, 2, 1)), h_last.reshape(bsz, LRU_W), conv_new))

        (na_s, q_s, k_s, v_s, ik_s, iq_s, iw_s, h_s, conv_s) = _sample_stage(
            ys, mod_s, p, rope_s, state_conv[l].reshape(dbsz, (CONV_W - 1) * LRU_W), state_h[l])
        score = _sample_scores(page_table, iq_s.reshape(dbsz, IDX_HEADS, IDX_DIM),
                               iw_s[:, IW_LANE:IW_LANE + IDX_HEADS].reshape(dbsz, IDX_HEADS, 1),
                               jnp.transpose(cache_ik[l], (0, 2, 1)))
        bias, bias_new = _sample_select(score.reshape(dbsz, past_len), iq_s, ik_s, iw_s, tri)
        nb_s = _sample_attention(page_table, q_s.reshape(dbsz, N_HEADS, HEAD_DIM),
                                 k_s.reshape(dbsz, 1, KV_W), v_s.reshape(dbsz, 1, KV_W),
                                 bias.reshape(dbsz, 1, past_len), bias_new.reshape(dbsz, 1, LANES), p["gna8"],
                                 jnp.transpose(cache_k[l], (0, 2, 3, 1)), jnp.transpose(cache_v[l], (0, 2, 3, 1)))
        ys = _post(ys, na_s, nb_s.reshape(dbsz, ATT_W), mod_s, p, 1, dbsz)
        outs_s.append((k_s.reshape(dbsz, dt, N_KV_HEADS, HEAD_DIM), v_s.reshape(dbsz, dt, N_KV_HEADS, HEAD_DIM),
                       ik_s.reshape(dbsz, dt, IDX_DIM), h_s,
                       conv_s.reshape(dbsz, CONV_W - 1, LRU_W)))

    def stack(outs, i):
        return jnp.stack([o[i] for o in outs])

    return (yp.reshape(bsz, t, D_MODEL), ys.reshape(dbsz, dt, D_MODEL),
            stack(outs_p, 0), stack(outs_p, 1), stack(outs_p, 2), stack(outs_p, 3), stack(outs_p, 4),
            stack(outs_s, 0), stack(outs_s, 1), stack(outs_s, 2), stack(outs_s, 3), stack(outs_s, 4))
```

```python
import functools

import numpy as np
import jax
import jax.numpy as jnp
from jax import lax
from jax.experimental import pallas as pl
from jax.experimental.pallas import tpu as pltpu

F32 = jnp.float32
BF16 = jnp.bfloat16
I32 = jnp.int32
I16 = jnp.int16

D_MODEL = 1024
LRU_W = 512
LRU_BLOCKS = 8
LRU_BW = LRU_W // LRU_BLOCKS
CONV_W = 4
LRU_C = 8.0
N_HEADS = 8
HEAD_DIM = 64
ATT_W = N_HEADS * HEAD_DIM
N_KV_HEADS = 2
KV_GROUP = N_HEADS // N_KV_HEADS
KV_W = N_KV_HEADS * HEAD_DIM
ROT_DIM = HEAD_DIM // 4
ROPE_THETA = 500000.0
IDX_HEADS = 4
IDX_DIM = 64
IDX_W = IDX_HEADS * IDX_DIM
TOPK_MAX = 256
PAGE_SIZE = 128
D_FF = 2816
EPS = 1e-6

COL_XR = 0
COL_GR = COL_XR + LRU_W
COL_Q = COL_GR + LRU_W
COL_K = COL_Q + ATT_W
COL_V = COL_K + KV_W
COL_IQ = COL_V + KV_W
MAIN_W = COL_IQ
TAIL_W = 128
IDXP_W = IDX_W + TAIL_W
IW_LANE = IDX_DIM
IDX_CAT = 6 * IDX_DIM

LANES = 128
SUBLANES = 8
TRI_W = 256
INT_MIN = -2147483648
HALF_MIN = -32768
LOG2E = 1.4426950408889634
NEG_BIG = -1e30
VMEM_LIMIT = 48 * 1024 * 1024

PROMPT_ROW_TILE = 256
Q_BLOCK = 256
POST_ROW_TILE = 512
FF_CHUNK = D_FF // 2
ADA_COL_TILE = 512
DECODE_GROUP = 4


def _dot(a, b):
    return jnp.dot(a, b, preferred_element_type=F32)


def _dot_nt(a, b):
    return lax.dot_general(a, b, (((1,), (1,)), ((), ())), preferred_element_type=F32)


def _split_bf16(x):
    hi = x.astype(BF16)
    lo = (x - hi.astype(F32)).astype(BF16)
    return hi, lo


def _split3(x):
    p1 = x.astype(BF16)
    r1 = x - p1.astype(F32)
    p2 = r1.astype(BF16)
    p3 = (r1 - p2.astype(F32)).astype(BF16)
    return p1, p2, p3


def _dot_f32(a, b, dot=_dot):
    a1, a2, a3 = a
    b1, b2, b3 = b
    return ((dot(a1, b3) + dot(a2, b2) + dot(a3, b1)) + (dot(a1, b2) + dot(a2, b1))) + dot(a1, b1)


def _cat6_lhs(x, axis):
    r1 = x - x.astype(BF16).astype(F32)
    r2 = r1 - r1.astype(BF16).astype(F32)
    return jnp.concatenate([x, x, r1, x, r1, r2], axis=axis).astype(BF16)


def _cat6_rhs(x, axis):
    p1, p2, p3 = _split3(x)
    return jnp.concatenate([p1, p2, p1, p3, p2, p1], axis=axis)


def _rms(x, g):
    return x * lax.rsqrt(jnp.mean(x * x, axis=-1, keepdims=True) + EPS) * g


def _head_rms(x, ones_bd, g):
    hi, lo = _split_bf16(x * x)
    ss = _dot(hi, ones_bd) + _dot(lo, ones_bd)
    return x * lax.rsqrt(ss * (1.0 / HEAD_DIM) + EPS) * g


def _rope(x, cos, sin_next, sin_prev):
    w = x.shape[-1]
    half = ROT_DIM // 2
    return x * cos + pltpu.roll(x, w - half, 1) * sin_next + pltpu.roll(x, half, 1) * sin_prev


def _rope_tables(tab, reps):
    cos = tab[:, 0:LANES]
    sa = tab[:, LANES:2 * LANES]
    sb = tab[:, 2 * LANES:3 * LANES]
    if reps > 1:
        cos = jnp.concatenate([cos] * reps, axis=1)
        sa = jnp.concatenate([sa] * reps, axis=1)
        sb = jnp.concatenate([sb] * reps, axis=1)
    return cos, sa, sb


def _rope_tail(zt, tab):
    cos, sa, sb = _rope_tables(tab, 1)
    lane = lax.broadcasted_iota(I32, zt.shape, 1)
    first = lane < IDX_DIM
    return _rope(zt, jnp.where(first, cos, 1.0), jnp.where(first, sa, 0.0), jnp.where(first, sb, 0.0))


def _gelu_tanh(x):
    c = float(np.sqrt(2.0 / np.pi))
    return x * (0.5 * (1.0 + jnp.tanh(c * (x + 0.044715 * (x * x * x)))))


def _softplus(x):
    return jnp.maximum(x, 0.0) + jnp.log1p(jnp.exp(-jnp.abs(x)))


def _project_in(x, sh1, sc1, ln1, w_main_t, w_idx_t_parts):
    h = _rms(x, ln1) * (1.0 + sc1) + sh1
    hp = _split3(h)
    return _dot_nt(hp[0], w_main_t), _dot_f32(hp, w_idx_t_parts, _dot_nt)


def _lru_gates(xc, wa, ba, wx, bx, lam):
    xcb = xc.astype(BF16)
    r = jax.nn.sigmoid(_dot(xcb, wa) + ba)
    gi = jax.nn.sigmoid(_dot(xcb, wx) + bx)
    log_a = (-LRU_C * _softplus(-lam)) * r
    a = jnp.exp(log_a)
    y = jnp.tanh(-log_a) * (1.0 + a * a)
    inp = jnp.where(y > 0.0, y * lax.rsqrt(y), 0.0) * gi * xc
    return a, inp


def _attention_inputs(z, zi, tab, qg, kg, ones_q, ones_k):
    cq, saq, sbq = _rope_tables(tab, ATT_W // LANES)
    ck, sak, sbk = _rope_tables(tab, KV_W // LANES)
    ci, sai, sbi = _rope_tables(tab, IDX_W // LANES)
    q = _rope(_head_rms(z[:, COL_Q:COL_K], ones_q, qg), cq, saq, sbq) * (HEAD_DIM ** -0.5 * LOG2E)
    k = _rope(_head_rms(z[:, COL_K:COL_V], ones_k, kg), ck, sak, sbk)
    v = z[:, COL_V:COL_IQ]
    iq = _rope(zi[:, 0:IDX_W], ci, sai, sbi)
    tail = _rope_tail(zi[:, IDX_W:IDXP_W], tab)
    return q, k, v, iq, tail


def _ada_kernel(c_ref, w_ref, b_ref, o_ref):
    c = c_ref[...]
    s = c * jax.nn.sigmoid(c)
    o_ref[...] = _dot_f32(_split3(s), _split3(w_ref[...])) + b_ref[...]


def _ada(c_all, ada_w, ada_b):
    rows = c_all.shape[0]
    n = ada_w.shape[1]
    return pl.pallas_call(
        _ada_kernel,
        grid=(n // ADA_COL_TILE,),
        in_specs=[
            pl.BlockSpec((rows, D_MODEL), lambda j: (0, 0)),
            pl.BlockSpec((D_MODEL, ADA_COL_TILE), lambda j: (0, j)),
            pl.BlockSpec((1, ADA_COL_TILE), lambda j: (0, j)),
        ],
        out_specs=pl.BlockSpec((rows, ADA_COL_TILE), lambda j: (0, j)),
        out_shape=jax.ShapeDtypeStruct((rows, n), F32),
        compiler_params=pltpu.CompilerParams(dimension_semantics=("arbitrary",), vmem_limit_bytes=VMEM_LIMIT),
        name="ada_modulation",
    )(c_all, ada_w, ada_b)


def _prompt_stage_kernel(x_ref, mod_ref, ln1_ref, wmain_ref, widx_ref, convw_ref, convb_ref,
                         wa_ref, ba_ref, wx_ref, bx_ref, lam_ref, qg_ref, kg_ref, gnl_ref,
                         onesq_ref, onesk_ref, rope_ref,
                         na_ref, q_ref, kt_ref, vt_ref, ikt_ref, ktb_ref, vb_ref,
                         ikcat_ref, iqcat_ref, iw_ref, h_ref, conv_ref,
                         xr_buf, hcarry, widx_parts):
    tt = x_ref.shape[1]
    t = pl.program_id(1)

    @pl.when((t == 0) & (pl.program_id(0) == 0))
    def _():
        for j, part in enumerate(_split3(widx_ref[...])):
            widx_parts[j] = part

    @pl.when(t == 0)
    def _():
        xr_buf[0:SUBLANES, :] = jnp.zeros((SUBLANES, LRU_W), F32)
        hcarry[...] = jnp.zeros_like(hcarry)

    mod = mod_ref[0]
    h1, h2, h3 = _split3(_rms(x_ref[0], ln1_ref[...]) * (1.0 + mod[:, D_MODEL:2 * D_MODEL]) + mod[:, 0:D_MODEL])
    z = _dot_nt(h1, wmain_ref[...])
    zi_small = _dot_nt(h1, widx_parts[2]) + _dot_nt(h2, widx_parts[1])

    xr = z[:, COL_XR:COL_GR]
    xr_buf[SUBLANES:SUBLANES + tt, :] = xr
    convw = convw_ref[...]
    xc = convb_ref[...]
    for i in range(CONV_W):
        xc = xc + xr_buf[pl.ds(SUBLANES - (CONV_W - 1) + i, tt), :] * convw[i:i + 1, :]
    conv_ref[0] = xr_buf[pl.ds(SUBLANES + tt - (CONV_W - 1), CONV_W - 1), :]
    xr_buf[0:SUBLANES, :] = xr_buf[tt:tt + SUBLANES, :]

    zi_small = zi_small + _dot_nt(h3, widx_parts[0])
    zi_mid = _dot_nt(h1, widx_parts[1])
    a, b = _lru_gates(xc, wa_ref[...], ba_ref[...], wx_ref[...], bx_ref[...], lam_ref[...])
    zi_mid = zi_mid + _dot_nt(h2, widx_parts[0])
    zi_big = _dot_nt(h1, widx_parts[0])
    row = lax.broadcasted_iota(I32, a.shape, 0) % SUBLANES
    d = 1
    while d < SUBLANES:
        keep = row >= d
        a_s = jnp.where(keep, pltpu.roll(a, d, 0), 1.0)
        b_s = jnp.where(keep, pltpu.roll(b, d, 0), 0.0)
        b = a * b_s + b
        a = a * a_s
        d *= 2
    h_in = hcarry[...]
    groups = []
    for g in range(tt // SUBLANES):
        rows = slice(g * SUBLANES, (g + 1) * SUBLANES)
        hg = a[rows] * h_in + b[rows]
        h_in = hg[SUBLANES - 1:SUBLANES, :]
        groups.append(hg)
    hr = jnp.concatenate(groups, axis=0)
    hcarry[...] = h_in
    h_ref[0] = h_in

    na_ref[0] = _rms(_gelu_tanh(z[:, COL_GR:COL_Q]) * hr, gnl_ref[...]).astype(BF16)

    zi = (zi_small + zi_mid) + zi_big
    q, k, v, iq, tail = _attention_inputs(z, zi, rope_ref[...], qg_ref[...], kg_ref[...],
                                          onesq_ref[...], onesk_ref[...])
    for h in range(N_HEADS):
        q_ref[0, h] = q[:, h * HEAD_DIM:(h + 1) * HEAD_DIM].astype(BF16)
    kt = jnp.transpose(k)
    vt = jnp.transpose(v)
    ikt = jnp.transpose(tail)[0:IDX_DIM, :]
    kt_ref[0] = kt.reshape(N_KV_HEADS, HEAD_DIM, tt)
    vt_ref[0] = vt.reshape(N_KV_HEADS, HEAD_DIM, tt)
    ikt_ref[0] = ikt
    ktb_ref[0] = kt.astype(BF16).reshape(N_KV_HEADS, HEAD_DIM, tt)
    lane = lax.broadcasted_iota(I32, v.shape, 1)
    pad = jnp.where(lane == HEAD_DIM, 1.0, 0.0)
    for n in range(N_KV_HEADS):
        vn = v if n == 0 else pltpu.roll(v, KV_W - n * HEAD_DIM, 1)
        vb_ref[0, n] = jnp.where(lane < HEAD_DIM, vn, pad).astype(BF16)
    ikcat_ref[0] = _cat6_rhs(ikt, 0)
    for h in range(IDX_HEADS):
        iqcat_ref[0, h] = _cat6_lhs(iq[:, h * IDX_DIM:(h + 1) * IDX_DIM], 1)
    iw_ref[0] = zi[:, IDX_W:IDXP_W]


def _const_spec(shape):
    nd = len(shape)
    return pl.BlockSpec(shape, lambda *_: (0,) * nd)


def _prompt_stage(x, mod_p, p, rope_tab):
    bsz, t, _ = x.shape
    tt = PROMPT_ROW_TILE

    def rows(w):
        return pl.BlockSpec((1, tt, w), lambda b, i: (b, i, 0))

    in_specs = [
        rows(D_MODEL),
        pl.BlockSpec((1, 1, 6 * D_MODEL), lambda b, i: (b, 0, 0)),
        _const_spec((1, D_MODEL)),
        _const_spec((MAIN_W, D_MODEL)),
        _const_spec((IDXP_W, D_MODEL)),
        _const_spec((CONV_W, LRU_W)),
        _const_spec((1, LRU_W)),
        _const_spec((LRU_W, LRU_W)),
        _const_spec((1, LRU_W)),
        _const_spec((LRU_W, LRU_W)),
        _const_spec((1, LRU_W)),
        _const_spec((1, LRU_W)),
        _const_spec((1, ATT_W)),
        _const_spec((1, KV_W)),
        _const_spec((1, LRU_W)),
        _const_spec((ATT_W, ATT_W)),
        _const_spec((KV_W, KV_W)),
        pl.BlockSpec((tt, 3 * LANES), lambda b, i: (i, 0)),
    ]

    def heads(n, w):
        return pl.BlockSpec((1, n, tt, w), lambda b, i: (b, 0, i, 0))

    def heads_t(n):
        return pl.BlockSpec((1, n, HEAD_DIM, tt), lambda b, i: (b, 0, 0, i))

    def feat_t(w):
        return pl.BlockSpec((1, w, tt), lambda b, i: (b, 0, i))

    out_specs = [
        rows(LRU_W), heads(N_HEADS, HEAD_DIM), heads_t(N_KV_HEADS), heads_t(N_KV_HEADS), feat_t(IDX_DIM),
        heads_t(N_KV_HEADS), heads(N_KV_HEADS, KV_W), feat_t(IDX_CAT), heads(IDX_HEADS, IDX_CAT), rows(TAIL_W),
        pl.BlockSpec((1, 1, LRU_W), lambda b, i: (b, 0, 0)),
        pl.BlockSpec((1, CONV_W - 1, LRU_W), lambda b, i: (b, 0, 0)),
    ]
    out_shape = [
        jax.ShapeDtypeStruct((bsz, t, LRU_W), BF16),
        jax.ShapeDtypeStruct((bsz, N_HEADS, t, HEAD_DIM), BF16),
        jax.ShapeDtypeStruct((bsz, N_KV_HEADS, HEAD_DIM, t), F32),
        jax.ShapeDtypeStruct((bsz, N_KV_HEADS, HEAD_DIM, t), F32),
        jax.ShapeDtypeStruct((bsz, IDX_DIM, t), F32),
        jax.ShapeDtypeStruct((bsz, N_KV_HEADS, HEAD_DIM, t), BF16),
        jax.ShapeDtypeStruct((bsz, N_KV_HEADS, t, KV_W), BF16),
        jax.ShapeDtypeStruct((bsz, IDX_CAT, t), BF16),
        jax.ShapeDtypeStruct((bsz, IDX_HEADS, t, IDX_CAT), BF16),
        jax.ShapeDtypeStruct((bsz, t, TAIL_W), F32),
        jax.ShapeDtypeStruct((bsz, 1, LRU_W), F32),
        jax.ShapeDtypeStruct((bsz, CONV_W - 1, LRU_W), F32),
    ]
    return pl.pallas_call(
        _prompt_stage_kernel,
        grid=(bsz, t // tt),
        in_specs=in_specs,
        out_specs=out_specs,
        out_shape=out_shape,
        scratch_shapes=[pltpu.VMEM((tt + SUBLANES, LRU_W), F32), pltpu.VMEM((1, LRU_W), F32),
                        pltpu.VMEM((3, IDXP_W, D_MODEL), BF16)],
        compiler_params=pltpu.CompilerParams(dimension_semantics=("arbitrary", "arbitrary"),
                                             vmem_limit_bytes=VMEM_LIMIT),
        name="prompt_stage",
    )(x, mod_p, p["ln1"], p["w_main"], p["w_idx"], p["conv_w"], p["conv_b"], p["wa"], p["ba"],
      p["wx"], p["bx"], p["lam"], p["qg"], p["kg"], p["gnl"], p["ones_q"], p["ones_k"], rope_tab)


def _sortable_key(score):
    score = jnp.where(score == 0.0, 0.0, score)
    bits = lax.bitcast_convert_type(score, I32)
    return jnp.where(bits < 0, bits ^ jnp.int32(0x7FFFFFFF), bits)


def _count_ge16(ref, c, strict=False):
    c16 = c.astype(I16)
    hit = ref[...] > c16 if strict else ref[...] >= c16
    ones = jnp.where(hit, jnp.int16(1), jnp.int16(0))
    acc = ones[:, 0:LANES]
    for j in range(1, ref.shape[1] // LANES):
        acc = acc + ones[:, j * LANES:(j + 1) * LANES]
    return jnp.sum(acc.astype(F32), axis=1, keepdims=True)


def _bitwise_max16(count_ge, k, n_all):
    def body(i, carry):
        t, ct = carry
        c = t + lax.shift_left(jnp.int32(1), jnp.int32(15) - i)
        cn = count_ge(c)
        ok = cn >= k
        return jnp.where(ok, c, t), jnp.where(ok, cn, ct)

    init = (jnp.full(k.shape, HALF_MIN, I32), jnp.full(k.shape, float(n_all), F32))
    return lax.fori_loop(0, 16, body, init, unroll=True)


def _kth_largest(key_ref, hi_ref, lo_ref, k, extra=None):
    key = key_ref[...]
    hi_ref[...] = lax.shift_right_arithmetic(key, 16).astype(I16)
    lo_ref[...] = ((key & 0xFFFF) + HALF_MIN).astype(I16)
    kf = jnp.full((key_ref.shape[0], 1), k, F32)
    n_all = key_ref.shape[1] + (0 if extra is None else 1)
    if extra is not None:
        hi_x = lax.shift_right_arithmetic(extra, 16)
        lo_x = (extra & 0xFFFF) + HALF_MIN

    def count_hi(c):
        cnt = _count_ge16(hi_ref, c)
        return cnt if extra is None else cnt + jnp.where(hi_x >= c, 1.0, 0.0)

    t_hi, n_bucket_up = _bitwise_max16(count_hi, kf, n_all)
    above = _count_ge16(hi_ref, t_hi, strict=True)
    if extra is not None:
        above = above + jnp.where(hi_x > t_hi, 1.0, 0.0)
    k_lo = kf - above
    lo_ref[...] = jnp.where(hi_ref[...] == t_hi.astype(I16), lo_ref[...], jnp.int16(HALF_MIN))
    if extra is not None:
        lo_x = jnp.where(hi_x == t_hi, lo_x, HALF_MIN)

    def count_lo(c):
        cnt = _count_ge16(lo_ref, c)
        return cnt if extra is None else cnt + jnp.where(lo_x >= c, 1.0, 0.0)

    t_lo, n_lo = _bitwise_max16(count_lo, k_lo, n_all)
    n_ge = jnp.where(t_lo > HALF_MIN, above + n_lo, n_bucket_up)
    return t_hi * 65536 + (t_lo - HALF_MIN), n_ge


def _selection_bias(key_ref, hi_ref, lo_ref, bias_ref, tri, k, extra=None):
    thr, n_ge = _kth_largest(key_ref, hi_ref, lo_ref, k, extra)
    has_ties = jnp.max(jnp.where(n_ge > k, 1.0, 0.0)) > 0.0

    @pl.when(jnp.logical_not(has_ties))
    def _():
        bias_ref[...] = jnp.where(key_ref[...] >= thr, 0.0, NEG_BIG)

    @pl.when(has_ties)
    def _():
        n_gt = jnp.sum(jnp.where(key_ref[...] > thr, 1.0, 0.0), axis=1, keepdims=True)
        if extra is not None:
            n_gt = n_gt + jnp.where(extra > thr, 1.0, 0.0)
        need = k - n_gt
        offset = jnp.zeros_like(need)
        for c in range(key_ref.shape[1] // TRI_W):
            kc = key_ref[:, c * TRI_W:(c + 1) * TRI_W]
            eq = kc == thr
            e = jnp.where(eq, 1.0, 0.0)
            incl = _dot(e.astype(BF16), tri)
            rank = incl - e + offset
            tie = jnp.where(rank < need, 0.0, NEG_BIG)
            bias_ref[:, c * TRI_W:(c + 1) * TRI_W] = jnp.where(kc > thr, 0.0, jnp.where(eq, tie, NEG_BIG))
            offset = offset + incl[:, TRI_W - 1:TRI_W]

    if extra is None:
        return None
    n_gt = jnp.sum(jnp.where(key_ref[...] > thr, 1.0, 0.0), axis=1, keepdims=True) + jnp.where(extra > thr, 1.0, 0.0)
    n_eq_main = n_ge - n_gt - jnp.where(extra == thr, 1.0, 0.0)
    tie = jnp.where(n_eq_main < k - n_gt, 0.0, NEG_BIG)
    return jnp.where(extra > thr, 0.0, jnp.where(extra == thr, tie, NEG_BIG))


def _prompt_attn_kernel(q_ref, iq_ref, iw_ref, kt_ref, v_ref, ikt_ref, gn_ref, tri_ref, o_ref,
                        key_ref, hi_ref, lo_ref, bias_ref, *, q_start):
    qb, n_keys = bias_ref.shape
    qpos = q_start + lax.broadcasted_iota(I32, (qb, n_keys), 0)
    kpos = lax.broadcasted_iota(I32, (qb, n_keys), 1)
    causal = kpos <= qpos
    if n_keys > TOPK_MAX:
        iw = iw_ref[0]
        ikt = ikt_ref[0]
        score = jnp.zeros((qb, n_keys), F32)
        s_next = _dot(iq_ref[0, 0], ikt)
        for h in range(IDX_HEADS):
            s, s_next = s_next, (_dot(iq_ref[0, h + 1], ikt) if h + 1 < IDX_HEADS else None)
            score = score + iw[:, IW_LANE + h:IW_LANE + h + 1] * jnp.maximum(s, 0.0)
        key_ref[...] = jnp.where(causal, _sortable_key(score), INT_MIN)
        _selection_bias(key_ref, hi_ref, lo_ref, bias_ref, tri_ref[...], float(TOPK_MAX))
        bias_ref[...] = jnp.where(causal, bias_ref[...], NEG_BIG)
    else:
        bias_ref[...] = jnp.where(causal, 0.0, NEG_BIG)

    def qk(h):
        return _dot(q_ref[0, h], kt_ref[0, h // KV_GROUP]) + bias_ref[...]

    outs = []
    logits_next = qk(0)
    for h in range(N_HEADS):
        logits, logits_next = logits_next, (qk(h + 1) if h + 1 < N_HEADS else None)
        m = jnp.max(logits, axis=1, keepdims=True)
        p = jnp.exp2((logits - m).astype(BF16))
        pv = _dot(p, v_ref[0, h // KV_GROUP])
        outs.append(pv[:, 0:HEAD_DIM] / pv[:, HEAD_DIM:HEAD_DIM + 1])
    o_ref[0] = _rms(jnp.concatenate(outs, axis=1), gn_ref[...]).astype(BF16)


def _prompt_attention_block(qh, iqcat, iw, ktb, vb, ikcat, gn_att, tri, q_block):
    bsz, _, t, _ = qh.shape
    qb = Q_BLOCK
    n_keys = (q_block + 1) * qb

    def qheads(n, w):
        return pl.BlockSpec((1, n, qb, w), lambda b: (b, 0, q_block, 0))

    return pl.pallas_call(
        functools.partial(_prompt_attn_kernel, q_start=q_block * qb),
        grid=(bsz,),
        in_specs=[qheads(N_HEADS, HEAD_DIM), qheads(IDX_HEADS, IDX_CAT),
                  pl.BlockSpec((1, qb, TAIL_W), lambda b: (b, q_block, 0)),
                  pl.BlockSpec((1, N_KV_HEADS, HEAD_DIM, n_keys), lambda b: (b, 0, 0, 0)),
                  pl.BlockSpec((1, N_KV_HEADS, n_keys, KV_W), lambda b: (b, 0, 0, 0)),
                  pl.BlockSpec((1, IDX_CAT, n_keys), lambda b: (b, 0, 0)),
                  _const_spec((1, ATT_W)), _const_spec((TRI_W, TRI_W))],
        out_specs=pl.BlockSpec((1, qb, ATT_W), lambda b: (b, 0, 0)),
        out_shape=jax.ShapeDtypeStruct((bsz, qb, ATT_W), BF16),
        scratch_shapes=[pltpu.VMEM((qb, n_keys), I32), pltpu.VMEM((qb, n_keys), I16), pltpu.VMEM((qb, n_keys), I16),
                        pltpu.VMEM((qb, n_keys), F32)],
        compiler_params=pltpu.CompilerParams(dimension_semantics=("arbitrary",), vmem_limit_bytes=VMEM_LIMIT),
        name=f"prompt_attention_{q_block}",
    )(qh, iqcat, iw, ktb, vb, ikcat, gn_att, tri)


def _prompt_attention(qh, iqcat, iw, ktb, vb, ikcat, gn_att, tri):
    t = qh.shape[2]
    blocks = [_prompt_attention_block(qh, iqcat, iw, ktb, vb, ikcat, gn_att, tri, i) for i in range(t // Q_BLOCK)]
    return jnp.concatenate(blocks, axis=1)


def _post_kernel(x_ref, na_ref, nb_ref, mod_ref, woa_ref, wob_ref, ln2_ref, wg_ref, wu_ref, wd_ref,
                 y_ref, *, shared_mod):
    mod = mod_ref[0] if shared_mod else mod_ref[...]
    g1 = mod[:, 2 * D_MODEL:3 * D_MODEL]
    sh2 = mod[:, 3 * D_MODEL:4 * D_MODEL]
    sc2 = mod[:, 4 * D_MODEL:5 * D_MODEL]
    g2 = mod[:, 5 * D_MODEL:6 * D_MODEL]
    x = x_ref[...]
    mix = _dot(na_ref[...], woa_ref[...]) + _dot(nb_ref[...], wob_ref[...])
    x1 = x + g1 * mix
    h2 = (_rms(x1, ln2_ref[...]) * (1.0 + sc2) + sh2).astype(BF16)
    def gate_up(c):
        return _dot(h2, wg_ref[:, c * FF_CHUNK:(c + 1) * FF_CHUNK]), _dot(h2, wu_ref[:, c * FF_CHUNK:(c + 1) * FF_CHUNK])

    n_chunks = D_FF // FF_CHUNK
    ff = jnp.zeros_like(x1)
    gu_next = gate_up(0)
    for c in range(n_chunks):
        (g, u), gu_next = gu_next, (gate_up(c + 1) if c + 1 < n_chunks else None)
        act = (g * jax.nn.sigmoid(g) * u).astype(BF16)
        ff = ff + _dot(act, wd_ref[c * FF_CHUNK:(c + 1) * FF_CHUNK, :])
    y_ref[...] = x1 + g2 * ff


def _post(x, na, nb, mod, p, rows_per_mod, tile):
    n = x.shape[0]
    shared = rows_per_mod > 1
    if shared:
        mod_spec = pl.BlockSpec((1, 1, 6 * D_MODEL), lambda i: (i * tile // rows_per_mod, 0, 0))
    else:
        mod_spec = pl.BlockSpec((tile, 6 * D_MODEL), lambda i: (i, 0))

    def rows(w):
        return pl.BlockSpec((tile, w), lambda i: (i, 0))

    def weight(shape):
        return pl.BlockSpec(shape, lambda i: (0, 0), pipeline_mode=pl.Buffered(1))

    return pl.pallas_call(
        functools.partial(_post_kernel, shared_mod=shared),
        grid=(n // tile,),
        in_specs=[rows(D_MODEL), rows(LRU_W), rows(ATT_W), mod_spec,
                  weight((LRU_W, D_MODEL)), weight((ATT_W, D_MODEL)), weight((1, D_MODEL)),
                  weight((D_MODEL, D_FF)), weight((D_MODEL, D_FF)), weight((D_FF, D_MODEL))],
        out_specs=rows(D_MODEL),
        out_shape=jax.ShapeDtypeStruct((n, D_MODEL), F32),
        compiler_params=pltpu.CompilerParams(dimension_semantics=("arbitrary",), vmem_limit_bytes=VMEM_LIMIT),
        name="post_ffn",
    )(x, na, nb, mod, p["wo_a"], p["wo_b"], p["ln2"], p["wg"], p["wu"], p["wd"])


def _sample_stage_kernel(x_ref, mod_ref, ln1_ref, wmain_ref, widx_ref, convw_ref, convb_ref,
                         wa_ref, ba_ref, wx_ref, bx_ref, lam_ref, qg_ref, kg_ref, gnl_ref,
                         onesq_ref, onesk_ref, rope_ref, sconv_ref, sh_ref,
                         na_ref, q_ref, k_ref, v_ref, ik_ref, iq_ref, iw_ref, h_ref, conv_ref):
    mod = mod_ref[...]
    z, zi = _project_in(x_ref[...], mod[:, 0:D_MODEL], mod[:, D_MODEL:2 * D_MODEL], ln1_ref[...],
                        wmain_ref[...], _split3(widx_ref[...]))
    xr = z[:, COL_XR:COL_GR]
    convw = convw_ref[...]
    sconv = sconv_ref[...]
    xc = convb_ref[...]
    for i in range(CONV_W - 1):
        xc = xc + sconv[:, i * LRU_W:(i + 1) * LRU_W] * convw[i:i + 1, :]
    xc = xc + xr * convw[CONV_W - 1:CONV_W, :]
    conv_ref[:, 0:(CONV_W - 2) * LRU_W] = sconv[:, LRU_W:(CONV_W - 1) * LRU_W]
    conv_ref[:, (CONV_W - 2) * LRU_W:(CONV_W - 1) * LRU_W] = xr

    a, b = _lru_gates(xc, wa_ref[...], ba_ref[...], wx_ref[...], bx_ref[...], lam_ref[...])
    hr = a * sh_ref[...] + b
    h_ref[...] = hr
    na_ref[...] = _rms(_gelu_tanh(z[:, COL_GR:COL_Q]) * hr, gnl_ref[...]).astype(BF16)

    q, k, v, iq, tail = _attention_inputs(z, zi, rope_ref[...], qg_ref[...], kg_ref[...],
                                          onesq_ref[...], onesk_ref[...])
    q_ref[...] = q
    k_ref[...] = k
    v_ref[...] = v
    ik_ref[...] = tail[:, 0:IDX_DIM]
    iq_ref[...] = iq
    iw_ref[...] = zi[:, IDX_W:IDXP_W]


def _sample_stage(x, mod_s, p, rope_row, sconv, sh):
    n = x.shape[0]

    def out(w, dt=F32):
        return jax.ShapeDtypeStruct((n, w), dt)

    return pl.pallas_call(
        _sample_stage_kernel,
        out_shape=[out(LRU_W, BF16), out(ATT_W), out(KV_W), out(KV_W), out(IDX_DIM), out(IDX_W),
                   out(TAIL_W), out(LRU_W), out((CONV_W - 1) * LRU_W)],
        compiler_params=pltpu.CompilerParams(vmem_limit_bytes=VMEM_LIMIT),
        name="sample_stage",
    )(x, mod_s, p["ln1"], p["w_main"], p["w_idx"], p["conv_w"], p["conv_b"], p["wa"], p["ba"],
      p["wx"], p["bx"], p["lam"], p["qg"], p["kg"], p["gnl"], p["ones_q"], p["ones_k"], rope_row, sconv, sh)


def _page_copy(cache_ref, buf_ref, sem, page, slot, j, pg):
    return pltpu.make_async_copy(cache_ref.at[page], buf_ref.at[slot, j, pg], sem)


def _paged_prefetch(pt_ref, caches, bufs, sems, group, n_pages):
    s = pl.program_id(0)
    slot = s % 2

    def start(grp, half):
        for cache_ref, buf_ref, sem in zip(caches, bufs, sems):
            for j in range(group):
                for pg in range(n_pages):
                    page = pt_ref[grp * group + j, pg]
                    _page_copy(cache_ref, buf_ref, sem.at[half], page, half, j, pg).start()

    @pl.when(s == 0)
    def _():
        start(0, 0)

    @pl.when(s + 1 < pl.num_programs(0))
    def _():
        start(s + 1, 1 - slot)

    for cache_ref, buf_ref, sem in zip(caches, bufs, sems):
        for j in range(group):
            for pg in range(n_pages):
                _page_copy(cache_ref, buf_ref, sem.at[slot], 0, slot, j, pg).wait()
    return slot


def _sample_score_kernel(pt_ref, iq_ref, iw_ref, ik_hbm, o_ref, ik_buf, sem, *, n_pages):
    group = iq_ref.shape[0]
    slot = _paged_prefetch(pt_ref, (ik_hbm,), (ik_buf,), (sem,), group, n_pages)
    iqcat = [_cat6_lhs(iq_ref[j], 1) for j in range(group)]
    for pg in range(n_pages):
        for j in range(group):
            s = _dot(iqcat[j], _cat6_rhs(ik_buf[slot, j, pg], 0))
            o_ref[j, :, pg * PAGE_SIZE:(pg + 1) * PAGE_SIZE] = jnp.sum(iw_ref[j] * jnp.maximum(s, 0.0), axis=0,
                                                                        keepdims=True)


def _sample_scores(page_table, iq3, iw3, cache_ik):
    n, n_pages = page_table.shape
    g = DECODE_GROUP
    grid_spec = pltpu.PrefetchScalarGridSpec(
        num_scalar_prefetch=1,
        grid=(n // g,),
        in_specs=[pl.BlockSpec((g, IDX_HEADS, IDX_DIM), lambda s, pt: (s, 0, 0)),
                  pl.BlockSpec((g, IDX_HEADS, 1), lambda s, pt: (s, 0, 0)),
                  pl.BlockSpec(memory_space=pl.ANY)],
        out_specs=pl.BlockSpec((g, 1, n_pages * PAGE_SIZE), lambda s, pt: (s, 0, 0)),
        scratch_shapes=[pltpu.VMEM((2, g, n_pages, IDX_DIM, PAGE_SIZE), F32), pltpu.SemaphoreType.DMA((2,))],
    )
    return pl.pallas_call(
        functools.partial(_sample_score_kernel, n_pages=n_pages),
        grid_spec=grid_spec,
        out_shape=jax.ShapeDtypeStruct((n, 1, n_pages * PAGE_SIZE), F32),
        compiler_params=pltpu.CompilerParams(dimension_semantics=("arbitrary",), vmem_limit_bytes=VMEM_LIMIT),
        name="sample_scores",
    )(page_table, iq3, iw3, cache_ik)


def _sample_select_kernel(score_ref, iq_ref, ik_ref, iw_ref, tri_ref, bias_ref, bias_new_ref,
                          key_ref, hi_ref, lo_ref):
    iq = iq_ref[...]
    ik = ik_ref[...]
    iw = iw_ref[...]
    new = jnp.zeros((iq.shape[0], 1), F32)
    for h in range(IDX_HEADS):
        s = jnp.sum(iq[:, h * IDX_DIM:(h + 1) * IDX_DIM] * ik, axis=1, keepdims=True)
        new = new + iw[:, IW_LANE + h:IW_LANE + h + 1] * jnp.maximum(s, 0.0)
    key_ref[...] = _sortable_key(score_ref[...])
    b_new = _selection_bias(key_ref, hi_ref, lo_ref, bias_ref, tri_ref[...], float(TOPK_MAX),
                            extra=_sortable_key(new))
    bias_new_ref[...] = jnp.broadcast_to(b_new, bias_new_ref.shape)


def _sample_select(score, iq, ik, iw, tri):
    n, n_keys = score.shape
    return pl.pallas_call(
        _sample_select_kernel,
        out_shape=[jax.ShapeDtypeStruct((n, n_keys), F32), jax.ShapeDtypeStruct((n, LANES), F32)],
        scratch_shapes=[pltpu.VMEM((n, n_keys), I32), pltpu.VMEM((n, n_keys), I16), pltpu.VMEM((n, n_keys), I16)],
        compiler_params=pltpu.CompilerParams(vmem_limit_bytes=VMEM_LIMIT),
        name="sample_select",
    )(score, iq, ik, iw, tri)


def _sample_attn_kernel(pt_ref, q_ref, kn_ref, vn_ref, bias_ref, bnew_ref, gn_ref, k_hbm, v_hbm, o_ref,
                        k_buf, v_buf, k_sem, v_sem, *, n_pages):
    group = q_ref.shape[0]
    seqs = range(group)
    slot = _paged_prefetch(pt_ref, (k_hbm, v_hbm), (k_buf, v_buf), (k_sem, v_sem), group, n_pages)
    row = lax.broadcasted_iota(I32, (N_HEADS, HEAD_DIM), 0)
    first = row < KV_GROUP
    q = [q_ref[j] for j in seqs]
    q2 = [jnp.concatenate([jnp.where(first, q[j], 0.0), jnp.where(first, 0.0, q[j])], axis=1).astype(BF16)
          for j in seqs]
    logits = [[None] * n_pages for _ in seqs]
    for pg in range(n_pages):
        for j in seqs:
            kp = k_buf[slot, j, pg].astype(BF16).reshape(KV_W, PAGE_SIZE)
            logits[j][pg] = _dot(q2[j], kp) + bias_ref[j][:, pg * PAGE_SIZE:(pg + 1) * PAGE_SIZE]
    m, p_new, den, acc, v_sel = [], [], [], [], []
    for j in seqs:
        kn = kn_ref[j]
        vn = vn_ref[j]
        k_sel = jnp.where(first, kn[:, 0:HEAD_DIM], kn[:, HEAD_DIM:KV_W])
        v_sel.append(jnp.where(first, vn[:, 0:HEAD_DIM], vn[:, HEAD_DIM:KV_W]))
        l_new = jnp.sum(q[j] * k_sel, axis=1, keepdims=True) + bnew_ref[j][:, 0:1]
        mj = l_new
        for l in logits[j]:
            mj = jnp.maximum(mj, jnp.max(l, axis=1, keepdims=True))
        m.append(mj)
        p_new.append(jnp.exp2(l_new - mj))
        den.append(p_new[j])
        acc.append(jnp.zeros((N_HEADS, KV_W), F32))
    for pg in range(n_pages):
        for j in seqs:
            p = jnp.exp2(logits[j][pg] - m[j])
            den[j] = den[j] + jnp.sum(p, axis=1, keepdims=True)
            vp = v_buf[slot, j, pg].astype(BF16).reshape(KV_W, PAGE_SIZE)
            acc[j] = acc[j] + _dot_nt(p.astype(BF16), vp)
    for j in seqs:
        out = (p_new[j] * v_sel[j] + jnp.where(first, acc[j][:, 0:HEAD_DIM], acc[j][:, HEAD_DIM:KV_W])) / den[j]
        ms = jnp.sum(jnp.sum(out * out, axis=1, keepdims=True), axis=0, keepdims=True) * (1.0 / ATT_W)
        o_ref[j] = (out * lax.rsqrt(ms + EPS) * gn_ref[...]).astype(BF16)


def _sample_attention(page_table, q3, k_new, v_new, bias, bias_new, gn8, cache_k, cache_v):
    n, n_pages = page_table.shape
    n_keys = n_pages * PAGE_SIZE

    g = DECODE_GROUP

    def per_seq(shape):
        return pl.BlockSpec((g,) + shape, lambda s, pt: (s, 0, 0))

    page_buf = pltpu.VMEM((2, g, n_pages, N_KV_HEADS, HEAD_DIM, PAGE_SIZE), F32)
    grid_spec = pltpu.PrefetchScalarGridSpec(
        num_scalar_prefetch=1,
        grid=(n // g,),
        in_specs=[per_seq((N_HEADS, HEAD_DIM)), per_seq((1, KV_W)), per_seq((1, KV_W)),
                  per_seq((1, n_keys)), per_seq((1, LANES)),
                  pl.BlockSpec((N_HEADS, HEAD_DIM), lambda s, pt: (0, 0)),
                  pl.BlockSpec(memory_space=pl.ANY), pl.BlockSpec(memory_space=pl.ANY)],
        out_specs=per_seq((N_HEADS, HEAD_DIM)),
        scratch_shapes=[page_buf, page_buf, pltpu.SemaphoreType.DMA((2,)), pltpu.SemaphoreType.DMA((2,))],
    )
    return pl.pallas_call(
        functools.partial(_sample_attn_kernel, n_pages=n_pages),
        grid_spec=grid_spec,
        out_shape=jax.ShapeDtypeStruct((n, N_HEADS, HEAD_DIM), BF16),
        compiler_params=pltpu.CompilerParams(dimension_semantics=("arbitrary",), vmem_limit_bytes=VMEM_LIMIT),
        name="sample_attention",
    )(page_table, q3, k_new, v_new, bias, bias_new, gn8, cache_k, cache_v)


def _rope_table_np(positions):
    half = ROT_DIM // 2
    freq = ROPE_THETA ** (-(np.arange(half, dtype=np.float64) / half))
    ang = np.asarray(positions, np.float64)[:, None] * freq[None, :]
    cos, sin = np.cos(ang), np.sin(ang)
    n = len(positions)
    tab = np.zeros((n, 3, LANES), np.float64)
    tab[:, 0, :] = 1.0
    for base in range(0, LANES, HEAD_DIM):
        tab[:, 0, base:base + half] = cos
        tab[:, 0, base + half:base + ROT_DIM] = cos
        tab[:, 1, base:base + half] = -sin
        tab[:, 2, base + half:base + ROT_DIM] = sin
    return tab.reshape(n, 3 * LANES).astype(np.float32)


def _block_diag(w):
    n, a, b = w.shape
    return jnp.einsum("nij,nm->nimj", w, jnp.eye(n, dtype=w.dtype)).reshape(n * a, n * b)


def _layer_params(l, ln1_g, w_in, conv_w, conv_b, lru_wa, lru_ba, lru_wx, lru_bx, lru_lambda, q_norm_g,
                  k_norm_g, gn_lru_g, gn_att_g, w_out, ln2_g, w_gate, w_up, w_down):
    w_in_t = jnp.transpose(w_in[l])
    w_idx = jnp.pad(w_in_t[COL_IQ:], ((0, IDXP_W - (w_in.shape[2] - COL_IQ)), (0, 0)))
    head_id = np.arange(ATT_W) // HEAD_DIM
    ones_q = (head_id[:, None] == head_id[None, :]).astype(np.float32)
    return {
        "ln1": ln1_g[l][None, :],
        "w_main": w_in_t[:COL_IQ].astype(BF16),
        "w_idx": w_idx,
        "conv_w": conv_w[l],
        "conv_b": conv_b[l][None, :],
        "wa": _block_diag(lru_wa[l]).astype(BF16),
        "ba": lru_ba[l][None, :],
        "wx": _block_diag(lru_wx[l]).astype(BF16),
        "bx": lru_bx[l][None, :],
        "lam": lru_lambda[l][None, :],
        "qg": jnp.tile(q_norm_g[l], N_HEADS)[None, :],
        "kg": jnp.tile(k_norm_g[l], N_KV_HEADS)[None, :],
        "gnl": gn_lru_g[l][None, :],
        "gna": gn_att_g[l][None, :],
        "gna8": gn_att_g[l].reshape(N_HEADS, HEAD_DIM),
        "ones_q": jnp.asarray(ones_q, BF16),
        "ones_k": jnp.asarray(ones_q[:KV_W, :KV_W], BF16),
        "wo_a": w_out[l][:LRU_W].astype(BF16),
        "wo_b": w_out[l][LRU_W:].astype(BF16),
        "ln2": ln2_g[l][None, :],
        "wg": w_gate[l].astype(BF16),
        "wu": w_up[l].astype(BF16),
        "wd": w_down[l].astype(BF16),
    }


def kernel(x_prompt, x_sample, cache_k, cache_v, cache_ik, state_h, state_conv, page_table, c_prompt, c_sample, ada_w, ada_b, ln1_g, w_in, conv_w, conv_b, lru_wa, lru_ba, lru_wx, lru_bx, lru_lambda, q_norm_g, k_norm_g, gn_lru_g, gn_att_g, w_out, ln2_g, w_gate, w_up, w_down):
    bsz, t, _ = x_prompt.shape
    dbsz, dt, _ = x_sample.shape
    depth = ada_w.shape[0]
    n_pages = page_table.shape[1]
    past_len = n_pages * PAGE_SIZE
    n_pool = cache_k.shape[1]
    assert dt == 1 and t % PROMPT_ROW_TILE == 0 and t % POST_ROW_TILE == 0 and t // 4 >= TOPK_MAX
    assert (past_len + dt) // 4 >= TOPK_MAX

    rope_p = jnp.asarray(_rope_table_np(np.arange(t)))
    rope_s = jnp.asarray(_rope_table_np(past_len + np.arange(dt)))
    tri = jnp.asarray(np.triu(np.ones((TRI_W, TRI_W), np.float32)), BF16)

    yp = x_prompt.reshape(bsz * t, D_MODEL)
    ys = x_sample.reshape(dbsz, D_MODEL)
    c_all = jnp.concatenate([c_prompt, c_sample], axis=0)
    outs_p, outs_s = [], []
    for l in range(depth):
        p = _layer_params(l, ln1_g, w_in, conv_w, conv_b, lru_wa, lru_ba, lru_wx, lru_bx, lru_lambda,
                          q_norm_g, k_norm_g, gn_lru_g, gn_att_g, w_out, ln2_g, w_gate, w_up, w_down)
        mod = _ada(c_all, ada_w[l], ada_b[l][None, :])
        mod_p = mod[:bsz].reshape(bsz, 1, 6 * D_MODEL)
        mod_s = mod[bsz:]

        (na, qh, kt, vt, ikt, ktb, vb, ikcat, iqcat, iw, h_last, conv_new) = _prompt_stage(
            yp.reshape(bsz, t, D_MODEL), mod_p, p, rope_p)
        nb = _prompt_attention(qh, iqcat, iw, ktb, vb, ikcat, p["gna"], tri)
        yp = _post(yp, na.reshape(bsz * t, LRU_W), nb.reshape(bsz * t, ATT_W), mod_p, p, t, POST_ROW_TILE)
        outs_p.append((jnp.transpose(kt, (0, 3, 1, 2)), jnp.transpose(vt, (0, 3, 1, 2)),
                       jnp.transpose(ikt, (0, 2, 1)), h_last.reshape(bsz, LRU_W), conv_new))

        (na_s, q_s, k_s, v_s, ik_s, iq_s, iw_s, h_s, conv_s) = _sample_stage(
            ys, mod_s, p, rope_s, state_conv[l].reshape(dbsz, (CONV_W - 1) * LRU_W), state_h[l])
        score = _sample_scores(page_table, iq_s.reshape(dbsz, IDX_HEADS, IDX_DIM),
                               iw_s[:, IW_LANE:IW_LANE + IDX_HEADS].reshape(dbsz, IDX_HEADS, 1),
                               jnp.transpose(cache_ik[l], (0, 2, 1)))
        bias, bias_new = _sample_select(score.reshape(dbsz, past_len), iq_s, ik_s, iw_s, tri)
        nb_s = _sample_attention(page_table, q_s.reshape(dbsz, N_HEADS, HEAD_DIM),
                                 k_s.reshape(dbsz, 1, KV_W), v_s.reshape(dbsz, 1, KV_W),
                                 bias.reshape(dbsz, 1, past_len), bias_new.reshape(dbsz, 1, LANES), p["gna8"],
                                 jnp.transpose(cache_k[l], (0, 2, 3, 1)), jnp.transpose(cache_v[l], (0, 2, 3, 1)))
        ys = _post(ys, na_s, nb_s.reshape(dbsz, ATT_W), mod_s, p, 1, dbsz)
        outs_s.append((k_s.reshape(dbsz, dt, N_KV_HEADS, HEAD_DIM), v_s.reshape(dbsz, dt, N_KV_HEADS, HEAD_DIM),
                       ik_s.reshape(dbsz, dt, IDX_DIM), h_s,
                       conv_s.reshape(dbsz, CONV_W - 1, LRU_W)))

    def stack(outs, i):
        return jnp.stack([o[i] for o in outs])

    return (yp.reshape(bsz, t, D_MODEL), ys.reshape(dbsz, dt, D_MODEL),
            stack(outs_p, 0), stack(outs_p, 1), stack(outs_p, 2), stack(outs_p, 3), stack(outs_p, 4),
            stack(outs_s, 0), stack(outs_s, 1), stack(outs_s, 2), stack(outs_s, 3), stack(outs_s, 4))
```

```python
import functools

import numpy as np
import jax
import jax.numpy as jnp
from jax import lax
from jax.experimental import pallas as pl
from jax.experimental.pallas import tpu as pltpu

F32 = jnp.float32
BF16 = jnp.bfloat16
I32 = jnp.int32
HALF_T = jnp.int32

D_MODEL = 1024
LRU_W = 512
LRU_BLOCKS = 8
LRU_BW = LRU_W // LRU_BLOCKS
CONV_W = 4
LRU_C = 8.0
N_HEADS = 8
HEAD_DIM = 64
ATT_W = N_HEADS * HEAD_DIM
N_KV_HEADS = 2
KV_GROUP = N_HEADS // N_KV_HEADS
KV_W = N_KV_HEADS * HEAD_DIM
ROT_DIM = HEAD_DIM // 4
ROPE_THETA = 500000.0
IDX_HEADS = 4
IDX_DIM = 64
IDX_W = IDX_HEADS * IDX_DIM
TOPK_MAX = 256
PAGE_SIZE = 128
D_FF = 2816
EPS = 1e-6

COL_XR = 0
COL_GR = COL_XR + LRU_W
COL_Q = COL_GR + LRU_W
COL_K = COL_Q + ATT_W
COL_V = COL_K + KV_W
COL_IQ = COL_V + KV_W
MAIN_W = COL_IQ
TAIL_W = 128
IDXP_W = IDX_W + TAIL_W
IW_LANE = IDX_DIM
IDX_CAT = 6 * IDX_DIM

LANES = 128
SUBLANES = 8
TRI_W = 256
INT_MIN = -2147483648
HALF_MIN = -32768
LOG2E = 1.4426950408889634
NEG_BIG = -1e30
VMEM_LIMIT = 48 * 1024 * 1024

PROMPT_ROW_TILE = 256
Q_BLOCK = 256
POST_ROW_TILE = 512
FF_CHUNK = D_FF // 2
ADA_COL_TILE = 512
DECODE_GROUP = 4


def _dot(a, b):
    return jnp.dot(a, b, preferred_element_type=F32)


def _dot_nt(a, b):
    return lax.dot_general(a, b, (((1,), (1,)), ((), ())), preferred_element_type=F32)


def _split_bf16(x):
    hi = x.astype(BF16)
    lo = (x - hi.astype(F32)).astype(BF16)
    return hi, lo


def _split3(x):
    p1 = x.astype(BF16)
    r1 = x - p1.astype(F32)
    p2 = r1.astype(BF16)
    p3 = (r1 - p2.astype(F32)).astype(BF16)
    return p1, p2, p3


def _dot_f32(a, b, dot=_dot):
    a1, a2, a3 = a
    b1, b2, b3 = b
    return ((dot(a1, b3) + dot(a2, b2) + dot(a3, b1)) + (dot(a1, b2) + dot(a2, b1))) + dot(a1, b1)


def _cat6_lhs(x, axis):
    r1 = x - x.astype(BF16).astype(F32)
    r2 = r1 - r1.astype(BF16).astype(F32)
    return jnp.concatenate([x, x, r1, x, r1, r2], axis=axis).astype(BF16)


def _cat6_rhs(x, axis):
    p1, p2, p3 = _split3(x)
    return jnp.concatenate([p1, p2, p1, p3, p2, p1], axis=axis)


def _rms(x, g):
    return x * lax.rsqrt(jnp.mean(x * x, axis=-1, keepdims=True) + EPS) * g


def _head_rms(x, ones_bd, g):
    hi, lo = _split_bf16(x * x)
    ss = _dot(hi, ones_bd) + _dot(lo, ones_bd)
    return x * lax.rsqrt(ss * (1.0 / HEAD_DIM) + EPS) * g


def _rope(x, cos, sin_next, sin_prev):
    w = x.shape[-1]
    half = ROT_DIM // 2
    return x * cos + pltpu.roll(x, w - half, 1) * sin_next + pltpu.roll(x, half, 1) * sin_prev


def _rope_tables(tab, reps):
    cos = tab[:, 0:LANES]
    sa = tab[:, LANES:2 * LANES]
    sb = tab[:, 2 * LANES:3 * LANES]
    if reps > 1:
        cos = jnp.concatenate([cos] * reps, axis=1)
        sa = jnp.concatenate([sa] * reps, axis=1)
        sb = jnp.concatenate([sb] * reps, axis=1)
    return cos, sa, sb


def _rope_tail(zt, tab):
    cos, sa, sb = _rope_tables(tab, 1)
    lane = lax.broadcasted_iota(I32, zt.shape, 1)
    first = lane < IDX_DIM
    return _rope(zt, jnp.where(first, cos, 1.0), jnp.where(first, sa, 0.0), jnp.where(first, sb, 0.0))


def _gelu_tanh(x):
    c = float(np.sqrt(2.0 / np.pi))
    return x * (0.5 * (1.0 + jnp.tanh(c * (x + 0.044715 * (x * x * x)))))


def _softplus(x):
    return jnp.maximum(x, 0.0) + jnp.log1p(jnp.exp(-jnp.abs(x)))


def _project_in(x, sh1, sc1, ln1, w_main_t, w_idx_t_parts):
    h = _rms(x, ln1) * (1.0 + sc1) + sh1
    hp = _split3(h)
    return _dot_nt(hp[0], w_main_t), _dot_f32(hp, w_idx_t_parts, _dot_nt)


def _lru_gates(xc, wa, ba, wx, bx, lam):
    xcb = xc.astype(BF16)
    r = jax.nn.sigmoid(_dot(xcb, wa) + ba)
    gi = jax.nn.sigmoid(_dot(xcb, wx) + bx)
    log_a = (-LRU_C * _softplus(-lam)) * r
    a = jnp.exp(log_a)
    y = jnp.tanh(-log_a) * (1.0 + a * a)
    inp = jnp.where(y > 0.0, y * lax.rsqrt(y), 0.0) * gi * xc
    return a, inp


def _attention_inputs(z, zi, tab, qg, kg, ones_q, ones_k):
    cq, saq, sbq = _rope_tables(tab, ATT_W // LANES)
    ck, sak, sbk = _rope_tables(tab, KV_W // LANES)
    ci, sai, sbi = _rope_tables(tab, IDX_W // LANES)
    q = _rope(_head_rms(z[:, COL_Q:COL_K], ones_q, qg), cq, saq, sbq) * (HEAD_DIM ** -0.5 * LOG2E)
    k = _rope(_head_rms(z[:, COL_K:COL_V], ones_k, kg), ck, sak, sbk)
    v = z[:, COL_V:COL_IQ]
    iq = _rope(zi[:, 0:IDX_W], ci, sai, sbi)
    tail = _rope_tail(zi[:, IDX_W:IDXP_W], tab)
    return q, k, v, iq, tail


def _ada_kernel(c_ref, w_ref, b_ref, o_ref):
    c = c_ref[...]
    s = c * jax.nn.sigmoid(c)
    o_ref[...] = _dot_f32(_split3(s), _split3(w_ref[...])) + b_ref[...]


def _ada(c_all, ada_w, ada_b):
    rows = c_all.shape[0]
    n = ada_w.shape[1]
    return pl.pallas_call(
        _ada_kernel,
        grid=(n // ADA_COL_TILE,),
        in_specs=[
            pl.BlockSpec((rows, D_MODEL), lambda j: (0, 0)),
            pl.BlockSpec((D_MODEL, ADA_COL_TILE), lambda j: (0, j)),
            pl.BlockSpec((1, ADA_COL_TILE), lambda j: (0, j)),
        ],
        out_specs=pl.BlockSpec((rows, ADA_COL_TILE), lambda j: (0, j)),
        out_shape=jax.ShapeDtypeStruct((rows, n), F32),
        compiler_params=pltpu.CompilerParams(dimension_semantics=("arbitrary",), vmem_limit_bytes=VMEM_LIMIT),
        name="ada_modulation",
    )(c_all, ada_w, ada_b)


def _prompt_stage_kernel(x_ref, mod_ref, ln1_ref, wmain_ref, widx_ref, convw_ref, convb_ref,
                         wa_ref, ba_ref, wx_ref, bx_ref, lam_ref, qg_ref, kg_ref, gnl_ref,
                         onesq_ref, onesk_ref, rope_ref,
                         na_ref, q_ref, kt_ref, vt_ref, ikt_ref, ktb_ref, vb_ref,
                         ikcat_ref, iqcat_ref, iw_ref, h_ref, conv_ref,
                         xr_buf, hcarry, widx_parts):
    tt = x_ref.shape[1]
    t = pl.program_id(1)

    @pl.when((t == 0) & (pl.program_id(0) == 0))
    def _():
        for j, part in enumerate(_split3(widx_ref[...])):
            widx_parts[j] = part

    @pl.when(t == 0)
    def _():
        xr_buf[0:SUBLANES, :] = jnp.zeros((SUBLANES, LRU_W), F32)
        hcarry[...] = jnp.zeros_like(hcarry)

    mod = mod_ref[0]
    h1, h2, h3 = _split3(_rms(x_ref[0], ln1_ref[...]) * (1.0 + mod[:, D_MODEL:2 * D_MODEL]) + mod[:, 0:D_MODEL])
    z = _dot_nt(h1, wmain_ref[...])
    zi_small = _dot_nt(h1, widx_parts[2]) + _dot_nt(h2, widx_parts[1])

    xr = z[:, COL_XR:COL_GR]
    xr_buf[SUBLANES:SUBLANES + tt, :] = xr
    convw = convw_ref[...]
    xc = convb_ref[...]
    for i in range(CONV_W):
        xc = xc + xr_buf[pl.ds(SUBLANES - (CONV_W - 1) + i, tt), :] * convw[i:i + 1, :]
    conv_ref[0] = xr_buf[pl.ds(SUBLANES + tt - (CONV_W - 1), CONV_W - 1), :]
    xr_buf[0:SUBLANES, :] = xr_buf[tt:tt + SUBLANES, :]

    zi_small = zi_small + _dot_nt(h3, widx_parts[0])
    zi_mid = _dot_nt(h1, widx_parts[1])
    a, b = _lru_gates(xc, wa_ref[...], ba_ref[...], wx_ref[...], bx_ref[...], lam_ref[...])
    zi_mid = zi_mid + _dot_nt(h2, widx_parts[0])
    zi_big = _dot_nt(h1, widx_parts[0])
    row = lax.broadcasted_iota(I32, a.shape, 0) % SUBLANES
    d = 1
    while d < SUBLANES:
        keep = row >= d
        a_s = jnp.where(keep, pltpu.roll(a, d, 0), 1.0)
        b_s = jnp.where(keep, pltpu.roll(b, d, 0), 0.0)
        b = a * b_s + b
        a = a * a_s
        d *= 2
    h_in = hcarry[...]
    groups = []
    for g in range(tt // SUBLANES):
        rows = slice(g * SUBLANES, (g + 1) * SUBLANES)
        hg = a[rows] * h_in + b[rows]
        h_in = hg[SUBLANES - 1:SUBLANES, :]
        groups.append(hg)
    hr = jnp.concatenate(groups, axis=0)
    hcarry[...] = h_in
    h_ref[0] = h_in

    na_ref[0] = _rms(_gelu_tanh(z[:, COL_GR:COL_Q]) * hr, gnl_ref[...]).astype(BF16)

    zi = (zi_small + zi_mid) + zi_big
    q, k, v, iq, tail = _attention_inputs(z, zi, rope_ref[...], qg_ref[...], kg_ref[...],
                                          onesq_ref[...], onesk_ref[...])
    for h in range(N_HEADS):
        q_ref[0, h] = q[:, h * HEAD_DIM:(h + 1) * HEAD_DIM].astype(BF16)
    kt = jnp.transpose(k)
    vt = jnp.transpose(v)
    ikt = jnp.transpose(tail)[0:IDX_DIM, :]
    kt_ref[0] = kt.reshape(N_KV_HEADS, HEAD_DIM, tt)
    vt_ref[0] = vt.reshape(N_KV_HEADS, HEAD_DIM, tt)
    ikt_ref[0] = ikt
    ktb_ref[0] = kt.astype(BF16).reshape(N_KV_HEADS, HEAD_DIM, tt)
    lane = lax.broadcasted_iota(I32, v.shape, 1)
    pad = jnp.where(lane == HEAD_DIM, 1.0, 0.0)
    for n in range(N_KV_HEADS):
        vn = v if n == 0 else pltpu.roll(v, KV_W - n * HEAD_DIM, 1)
        vb_ref[0, n] = jnp.where(lane < HEAD_DIM, vn, pad).astype(BF16)
    ikcat_ref[0] = _cat6_rhs(ikt, 0)
    for h in range(IDX_HEADS):
        iqcat_ref[0, h] = _cat6_lhs(iq[:, h * IDX_DIM:(h + 1) * IDX_DIM], 1)
    iw_ref[0] = zi[:, IDX_W:IDXP_W]


def _const_spec(shape):
    nd = len(shape)
    return pl.BlockSpec(shape, lambda *_: (0,) * nd)


def _prompt_stage(x, mod_p, p, rope_tab):
    bsz, t, _ = x.shape
    tt = PROMPT_ROW_TILE

    def rows(w):
        return pl.BlockSpec((1, tt, w), lambda b, i: (b, i, 0))

    in_specs = [
        rows(D_MODEL),
        pl.BlockSpec((1, 1, 6 * D_MODEL), lambda b, i: (b, 0, 0)),
        _const_spec((1, D_MODEL)),
        _const_spec((MAIN_W, D_MODEL)),
        _const_spec((IDXP_W, D_MODEL)),
        _const_spec((CONV_W, LRU_W)),
        _const_spec((1, LRU_W)),
        _const_spec((LRU_W, LRU_W)),
        _const_spec((1, LRU_W)),
        _const_spec((LRU_W, LRU_W)),
        _const_spec((1, LRU_W)),
        _const_spec((1, LRU_W)),
        _const_spec((1, ATT_W)),
        _const_spec((1, KV_W)),
        _const_spec((1, LRU_W)),
        _const_spec((ATT_W, ATT_W)),
        _const_spec((KV_W, KV_W)),
        pl.BlockSpec((tt, 3 * LANES), lambda b, i: (i, 0)),
    ]

    def heads(n, w):
        return pl.BlockSpec((1, n, tt, w), lambda b, i: (b, 0, i, 0))

    def heads_t(n):
        return pl.BlockSpec((1, n, HEAD_DIM, tt), lambda b, i: (b, 0, 0, i))

    def feat_t(w):
        return pl.BlockSpec((1, w, tt), lambda b, i: (b, 0, i))

    out_specs = [
        rows(LRU_W), heads(N_HEADS, HEAD_DIM), heads_t(N_KV_HEADS), heads_t(N_KV_HEADS), feat_t(IDX_DIM),
        heads_t(N_KV_HEADS), heads(N_KV_HEADS, KV_W), feat_t(IDX_CAT), heads(IDX_HEADS, IDX_CAT), rows(TAIL_W),
        pl.BlockSpec((1, 1, LRU_W), lambda b, i: (b, 0, 0)),
        pl.BlockSpec((1, CONV_W - 1, LRU_W), lambda b, i: (b, 0, 0)),
    ]
    out_shape = [
        jax.ShapeDtypeStruct((bsz, t, LRU_W), BF16),
        jax.ShapeDtypeStruct((bsz, N_HEADS, t, HEAD_DIM), BF16),
        jax.ShapeDtypeStruct((bsz, N_KV_HEADS, HEAD_DIM, t), F32),
        jax.ShapeDtypeStruct((bsz, N_KV_HEADS, HEAD_DIM, t), F32),
        jax.ShapeDtypeStruct((bsz, IDX_DIM, t), F32),
        jax.ShapeDtypeStruct((bsz, N_KV_HEADS, HEAD_DIM, t), BF16),
        jax.ShapeDtypeStruct((bsz, N_KV_HEADS, t, KV_W), BF16),
        jax.ShapeDtypeStruct((bsz, IDX_CAT, t), BF16),
        jax.ShapeDtypeStruct((bsz, IDX_HEADS, t, IDX_CAT), BF16),
        jax.ShapeDtypeStruct((bsz, t, TAIL_W), F32),
        jax.ShapeDtypeStruct((bsz, 1, LRU_W), F32),
        jax.ShapeDtypeStruct((bsz, CONV_W - 1, LRU_W), F32),
    ]
    return pl.pallas_call(
        _prompt_stage_kernel,
        grid=(bsz, t // tt),
        in_specs=in_specs,
        out_specs=out_specs,
        out_shape=out_shape,
        scratch_shapes=[pltpu.VMEM((tt + SUBLANES, LRU_W), F32), pltpu.VMEM((1, LRU_W), F32),
                        pltpu.VMEM((3, IDXP_W, D_MODEL), BF16)],
        compiler_params=pltpu.CompilerParams(dimension_semantics=("arbitrary", "arbitrary"),
                                             vmem_limit_bytes=VMEM_LIMIT),
        name="prompt_stage",
    )(x, mod_p, p["ln1"], p["w_main"], p["w_idx"], p["conv_w"], p["conv_b"], p["wa"], p["ba"],
      p["wx"], p["bx"], p["lam"], p["qg"], p["kg"], p["gnl"], p["ones_q"], p["ones_k"], rope_tab)


def _sortable_key(score):
    score = jnp.where(score == 0.0, 0.0, score)
    bits = lax.bitcast_convert_type(score, I32)
    return jnp.where(bits < 0, bits ^ jnp.int32(0x7FFFFFFF), bits)


def _count_ge16(ref, c, strict=False):
    c16 = c.astype(HALF_T)
    hit = ref[...] > c16 if strict else ref[...] >= c16
    ones = jnp.where(hit, jnp.ones((), HALF_T), jnp.zeros((), HALF_T))
    acc = ones[:, 0:LANES]
    for j in range(1, ref.shape[1] // LANES):
        acc = acc + ones[:, j * LANES:(j + 1) * LANES]
    return jnp.sum(acc.astype(F32), axis=1, keepdims=True)


def _bitwise_max16(count_ge, k, n_all):
    def body(i, carry):
        t, ct = carry
        c = t + lax.shift_left(jnp.int32(1), jnp.int32(15) - i)
        cn = count_ge(c)
        ok = cn >= k
        return jnp.where(ok, c, t), jnp.where(ok, cn, ct)

    init = (jnp.full(k.shape, HALF_MIN, I32), jnp.full(k.shape, float(n_all), F32))
    return lax.fori_loop(0, 16, body, init, unroll=True)


def _kth_largest(key_ref, hi_ref, lo_ref, k, extra=None):
    key = key_ref[...]
    hi_ref[...] = lax.shift_right_arithmetic(key, 16).astype(HALF_T)
    lo_ref[...] = ((key & 0xFFFF) + HALF_MIN).astype(HALF_T)
    kf = jnp.full((key_ref.shape[0], 1), k, F32)
    n_all = key_ref.shape[1] + (0 if extra is None else 1)
    if extra is not None:
        hi_x = lax.shift_right_arithmetic(extra, 16)
        lo_x = (extra & 0xFFFF) + HALF_MIN

    def count_hi(c):
        cnt = _count_ge16(hi_ref, c)
        return cnt if extra is None else cnt + jnp.where(hi_x >= c, 1.0, 0.0)

    t_hi, n_bucket_up = _bitwise_max16(count_hi, kf, n_all)
    above = _count_ge16(hi_ref, t_hi, strict=True)
    if extra is not None:
        above = above + jnp.where(hi_x > t_hi, 1.0, 0.0)
    k_lo = kf - above
    lo_ref[...] = jnp.where(hi_ref[...] == t_hi.astype(HALF_T), lo_ref[...], jnp.full((), HALF_MIN, HALF_T))
    if extra is not None:
        lo_x = jnp.where(hi_x == t_hi, lo_x, HALF_MIN)

    def count_lo(c):
        cnt = _count_ge16(lo_ref, c)
        return cnt if extra is None else cnt + jnp.where(lo_x >= c, 1.0, 0.0)

    t_lo, n_lo = _bitwise_max16(count_lo, k_lo, n_all)
    n_ge = jnp.where(t_lo > HALF_MIN, above + n_lo, n_bucket_up)
    return t_hi * 65536 + (t_lo - HALF_MIN), n_ge


def _selection_bias(key_ref, hi_ref, lo_ref, bias_ref, tri, k, extra=None):
    thr, n_ge = _kth_largest(key_ref, hi_ref, lo_ref, k, extra)
    has_ties = jnp.max(jnp.where(n_ge > k, 1.0, 0.0)) > 0.0

    @pl.when(jnp.logical_not(has_ties))
    def _():
        bias_ref[...] = jnp.where(key_ref[...] >= thr, 0.0, NEG_BIG)

    @pl.when(has_ties)
    def _():
        n_gt = jnp.sum(jnp.where(key_ref[...] > thr, 1.0, 0.0), axis=1, keepdims=True)
        if extra is not None:
            n_gt = n_gt + jnp.where(extra > thr, 1.0, 0.0)
        need = k - n_gt
        offset = jnp.zeros_like(need)
        for c in range(key_ref.shape[1] // TRI_W):
            kc = key_ref[:, c * TRI_W:(c + 1) * TRI_W]
            eq = kc == thr
            e = jnp.where(eq, 1.0, 0.0)
            incl = _dot(e.astype(BF16), tri)
            rank = incl - e + offset
            tie = jnp.where(rank < need, 0.0, NEG_BIG)
            bias_ref[:, c * TRI_W:(c + 1) * TRI_W] = jnp.where(kc > thr, 0.0, jnp.where(eq, tie, NEG_BIG))
            offset = offset + incl[:, TRI_W - 1:TRI_W]

    if extra is None:
        return None
    n_gt = jnp.sum(jnp.where(key_ref[...] > thr, 1.0, 0.0), axis=1, keepdims=True) + jnp.where(extra > thr, 1.0, 0.0)
    n_eq_main = n_ge - n_gt - jnp.where(extra == thr, 1.0, 0.0)
    tie = jnp.where(n_eq_main < k - n_gt, 0.0, NEG_BIG)
    return jnp.where(extra > thr, 0.0, jnp.where(extra == thr, tie, NEG_BIG))


def _prompt_attn_kernel(q_ref, iq_ref, iw_ref, kt_ref, v_ref, ikt_ref, gn_ref, tri_ref, o_ref,
                        key_ref, hi_ref, lo_ref, bias_ref, *, q_start):
    qb, n_keys = bias_ref.shape
    qpos = q_start + lax.broadcasted_iota(I32, (qb, n_keys), 0)
    kpos = lax.broadcasted_iota(I32, (qb, n_keys), 1)
    causal = kpos <= qpos
    if n_keys > TOPK_MAX:
        iw = iw_ref[0]
        ikt = ikt_ref[0]
        score = jnp.zeros((qb, n_keys), F32)
        s_next = _dot(iq_ref[0, 0], ikt)
        for h in range(IDX_HEADS):
            s, s_next = s_next, (_dot(iq_ref[0, h + 1], ikt) if h + 1 < IDX_HEADS else None)
            score = score + iw[:, IW_LANE + h:IW_LANE + h + 1] * jnp.maximum(s, 0.0)
        key_ref[...] = jnp.where(causal, _sortable_key(score), INT_MIN)
        _selection_bias(key_ref, hi_ref, lo_ref, bias_ref, tri_ref[...], float(TOPK_MAX))
        bias_ref[...] = jnp.where(causal, bias_ref[...], NEG_BIG)
    else:
        bias_ref[...] = jnp.where(causal, 0.0, NEG_BIG)

    def qk(h):
        return _dot(q_ref[0, h], kt_ref[0, h // KV_GROUP]) + bias_ref[...]

    outs = []
    logits_next = qk(0)
    for h in range(N_HEADS):
        logits, logits_next = logits_next, (qk(h + 1) if h + 1 < N_HEADS else None)
        m = jnp.max(logits, axis=1, keepdims=True)
        p = jnp.exp2((logits - m).astype(BF16))
        pv = _dot(p, v_ref[0, h // KV_GROUP])
        outs.append(pv[:, 0:HEAD_DIM] / pv[:, HEAD_DIM:HEAD_DIM + 1])
    o_ref[0] = _rms(jnp.concatenate(outs, axis=1), gn_ref[...]).astype(BF16)


def _prompt_attention_block(qh, iqcat, iw, ktb, vb, ikcat, gn_att, tri, q_block):
    bsz, _, t, _ = qh.shape
    qb = Q_BLOCK
    n_keys = (q_block + 1) * qb

    def qheads(n, w):
        return pl.BlockSpec((1, n, qb, w), lambda b: (b, 0, q_block, 0))

    return pl.pallas_call(
        functools.partial(_prompt_attn_kernel, q_start=q_block * qb),
        grid=(bsz,),
        in_specs=[qheads(N_HEADS, HEAD_DIM), qheads(IDX_HEADS, IDX_CAT),
                  pl.BlockSpec((1, qb, TAIL_W), lambda b: (b, q_block, 0)),
                  pl.BlockSpec((1, N_KV_HEADS, HEAD_DIM, n_keys), lambda b: (b, 0, 0, 0)),
                  pl.BlockSpec((1, N_KV_HEADS, n_keys, KV_W), lambda b: (b, 0, 0, 0)),
                  pl.BlockSpec((1, IDX_CAT, n_keys), lambda b: (b, 0, 0)),
                  _const_spec((1, ATT_W)), _const_spec((TRI_W, TRI_W))],
        out_specs=pl.BlockSpec((1, qb, ATT_W), lambda b: (b, 0, 0)),
        out_shape=jax.ShapeDtypeStruct((bsz, qb, ATT_W), BF16),
        scratch_shapes=[pltpu.VMEM((qb, n_keys), I32), pltpu.VMEM((qb, n_keys), HALF_T), pltpu.VMEM((qb, n_keys), HALF_T),
                        pltpu.VMEM((qb, n_keys), F32)],
        compiler_params=pltpu.CompilerParams(dimension_semantics=("arbitrary",), vmem_limit_bytes=VMEM_LIMIT),
        name=f"prompt_attention_{q_block}",
    )(qh, iqcat, iw, ktb, vb, ikcat, gn_att, tri)


def _prompt_attention(qh, iqcat, iw, ktb, vb, ikcat, gn_att, tri):
    t = qh.shape[2]
    blocks = [_prompt_attention_block(qh, iqcat, iw, ktb, vb, ikcat, gn_att, tri, i) for i in range(t // Q_BLOCK)]
    return jnp.concatenate(blocks, axis=1)


def _post_kernel(x_ref, na_ref, nb_ref, mod_ref, woa_ref, wob_ref, ln2_ref, wg_ref, wu_ref, wd_ref,
                 y_ref, *, shared_mod):
    mod = mod_ref[0] if shared_mod else mod_ref[...]
    g1 = mod[:, 2 * D_MODEL:3 * D_MODEL]
    sh2 = mod[:, 3 * D_MODEL:4 * D_MODEL]
    sc2 = mod[:, 4 * D_MODEL:5 * D_MODEL]
    g2 = mod[:, 5 * D_MODEL:6 * D_MODEL]
    x = x_ref[...]
    mix = _dot(na_ref[...], woa_ref[...]) + _dot(nb_ref[...], wob_ref[...])
    x1 = x + g1 * mix
    h2 = (_rms(x1, ln2_ref[...]) * (1.0 + sc2) + sh2).astype(BF16)
    def gate_up(c):
        return _dot(h2, wg_ref[:, c * FF_CHUNK:(c + 1) * FF_CHUNK]), _dot(h2, wu_ref[:, c * FF_CHUNK:(c + 1) * FF_CHUNK])

    n_chunks = D_FF // FF_CHUNK
    ff = jnp.zeros_like(x1)
    gu_next = gate_up(0)
    for c in range(n_chunks):
        (g, u), gu_next = gu_next, (gate_up(c + 1) if c + 1 < n_chunks else None)
        act = (g * jax.nn.sigmoid(g) * u).astype(BF16)
        ff = ff + _dot(act, wd_ref[c * FF_CHUNK:(c + 1) * FF_CHUNK, :])
    y_ref[...] = x1 + g2 * ff


def _post(x, na, nb, mod, p, rows_per_mod, tile):
    n = x.shape[0]
    shared = rows_per_mod > 1
    if shared:
        mod_spec = pl.BlockSpec((1, 1, 6 * D_MODEL), lambda i: (i * tile // rows_per_mod, 0, 0))
    else:
        mod_spec = pl.BlockSpec((tile, 6 * D_MODEL), lambda i: (i, 0))

    def rows(w):
        return pl.BlockSpec((tile, w), lambda i: (i, 0))

    def weight(shape):
        return pl.BlockSpec(shape, lambda i: (0, 0), pipeline_mode=pl.Buffered(1))

    return pl.pallas_call(
        functools.partial(_post_kernel, shared_mod=shared),
        grid=(n // tile,),
        in_specs=[rows(D_MODEL), rows(LRU_W), rows(ATT_W), mod_spec,
                  weight((LRU_W, D_MODEL)), weight((ATT_W, D_MODEL)), weight((1, D_MODEL)),
                  weight((D_MODEL, D_FF)), weight((D_MODEL, D_FF)), weight((D_FF, D_MODEL))],
        out_specs=rows(D_MODEL),
        out_shape=jax.ShapeDtypeStruct((n, D_MODEL), F32),
        compiler_params=pltpu.CompilerParams(dimension_semantics=("arbitrary",), vmem_limit_bytes=VMEM_LIMIT),
        name="post_ffn",
    )(x, na, nb, mod, p["wo_a"], p["wo_b"], p["ln2"], p["wg"], p["wu"], p["wd"])


def _sample_stage_kernel(x_ref, mod_ref, ln1_ref, wmain_ref, widx_ref, convw_ref, convb_ref,
                         wa_ref, ba_ref, wx_ref, bx_ref, lam_ref, qg_ref, kg_ref, gnl_ref,
                         onesq_ref, onesk_ref, rope_ref, sconv_ref, sh_ref,
                         na_ref, q_ref, k_ref, v_ref, ik_ref, iq_ref, iw_ref, h_ref, conv_ref):
    mod = mod_ref[...]
    z, zi = _project_in(x_ref[...], mod[:, 0:D_MODEL], mod[:, D_MODEL:2 * D_MODEL], ln1_ref[...],
                        wmain_ref[...], _split3(widx_ref[...]))
    xr = z[:, COL_XR:COL_GR]
    convw = convw_ref[...]
    sconv = sconv_ref[...]
    xc = convb_ref[...]
    for i in range(CONV_W - 1):
        xc = xc + sconv[:, i * LRU_W:(i + 1) * LRU_W] * convw[i:i + 1, :]
    xc = xc + xr * convw[CONV_W - 1:CONV_W, :]
    conv_ref[:, 0:(CONV_W - 2) * LRU_W] = sconv[:, LRU_W:(CONV_W - 1) * LRU_W]
    conv_ref[:, (CONV_W - 2) * LRU_W:(CONV_W - 1) * LRU_W] = xr

    a, b = _lru_gates(xc, wa_ref[...], ba_ref[...], wx_ref[...], bx_ref[...], lam_ref[...])
    hr = a * sh_ref[...] + b
    h_ref[...] = hr
    na_ref[...] = _rms(_gelu_tanh(z[:, COL_GR:COL_Q]) * hr, gnl_ref[...]).astype(BF16)

    q, k, v, iq, tail = _attention_inputs(z, zi, rope_ref[...], qg_ref[...], kg_ref[...],
                                          onesq_ref[...], onesk_ref[...])
    q_ref[...] = q
    k_ref[...] = k
    v_ref[...] = v
    ik_ref[...] = tail[:, 0:IDX_DIM]
    iq_ref[...] = iq
    iw_ref[...] = zi[:, IDX_W:IDXP_W]


def _sample_stage(x, mod_s, p, rope_row, sconv, sh):
    n = x.shape[0]

    def out(w, dt=F32):
        return jax.ShapeDtypeStruct((n, w), dt)

    return pl.pallas_call(
        _sample_stage_kernel,
        out_shape=[out(LRU_W, BF16), out(ATT_W), out(KV_W), out(KV_W), out(IDX_DIM), out(IDX_W),
                   out(TAIL_W), out(LRU_W), out((CONV_W - 1) * LRU_W)],
        compiler_params=pltpu.CompilerParams(vmem_limit_bytes=VMEM_LIMIT),
        name="sample_stage",
    )(x, mod_s, p["ln1"], p["w_main"], p["w_idx"], p["conv_w"], p["conv_b"], p["wa"], p["ba"],
      p["wx"], p["bx"], p["lam"], p["qg"], p["kg"], p["gnl"], p["ones_q"], p["ones_k"], rope_row, sconv, sh)


def _page_copy(cache_ref, buf_ref, sem, page, slot, j, pg):
    return pltpu.make_async_copy(cache_ref.at[page], buf_ref.at[slot, j, pg], sem)


def _paged_prefetch(pt_ref, caches, bufs, sems, group, n_pages):
    s = pl.program_id(0)
    slot = s % 2

    def start(grp, half):
        for cache_ref, buf_ref, sem in zip(caches, bufs, sems):
            for j in range(group):
                for pg in range(n_pages):
                    page = pt_ref[grp * group + j, pg]
                    _page_copy(cache_ref, buf_ref, sem.at[half], page, half, j, pg).start()

    @pl.when(s == 0)
    def _():
        start(0, 0)

    @pl.when(s + 1 < pl.num_programs(0))
    def _():
        start(s + 1, 1 - slot)

    for cache_ref, buf_ref, sem in zip(caches, bufs, sems):
        for j in range(group):
            for pg in range(n_pages):
                _page_copy(cache_ref, buf_ref, sem.at[slot], 0, slot, j, pg).wait()
    return slot


def _sample_score_kernel(pt_ref, iq_ref, iw_ref, ik_hbm, o_ref, ik_buf, sem, *, n_pages):
    group = iq_ref.shape[0]
    slot = _paged_prefetch(pt_ref, (ik_hbm,), (ik_buf,), (sem,), group, n_pages)
    iqcat = [_cat6_lhs(iq_ref[j], 1) for j in range(group)]
    for pg in range(n_pages):
        for j in range(group):
            s = _dot(iqcat[j], _cat6_rhs(ik_buf[slot, j, pg], 0))
            o_ref[j, :, pg * PAGE_SIZE:(pg + 1) * PAGE_SIZE] = jnp.sum(iw_ref[j] * jnp.maximum(s, 0.0), axis=0,
                                                                        keepdims=True)


def _sample_scores(page_table, iq3, iw3, cache_ik):
    n, n_pages = page_table.shape
    g = DECODE_GROUP
    grid_spec = pltpu.PrefetchScalarGridSpec(
        num_scalar_prefetch=1,
        grid=(n // g,),
        in_specs=[pl.BlockSpec((g, IDX_HEADS, IDX_DIM), lambda s, pt: (s, 0, 0)),
                  pl.BlockSpec((g, IDX_HEADS, 1), lambda s, pt: (s, 0, 0)),
                  pl.BlockSpec(memory_space=pl.ANY)],
        out_specs=pl.BlockSpec((g, 1, n_pages * PAGE_SIZE), lambda s, pt: (s, 0, 0)),
        scratch_shapes=[pltpu.VMEM((2, g, n_pages, IDX_DIM, PAGE_SIZE), F32), pltpu.SemaphoreType.DMA((2,))],
    )
    return pl.pallas_call(
        functools.partial(_sample_score_kernel, n_pages=n_pages),
        grid_spec=grid_spec,
        out_shape=jax.ShapeDtypeStruct((n, 1, n_pages * PAGE_SIZE), F32),
        compiler_params=pltpu.CompilerParams(dimension_semantics=("arbitrary",), vmem_limit_bytes=VMEM_LIMIT),
        name="sample_scores",
    )(page_table, iq3, iw3, cache_ik)


def _sample_select_kernel(score_ref, iq_ref, ik_ref, iw_ref, tri_ref, bias_ref, bias_new_ref,
                          key_ref, hi_ref, lo_ref):
    iq = iq_ref[...]
    ik = ik_ref[...]
    iw = iw_ref[...]
    new = jnp.zeros((iq.shape[0], 1), F32)
    for h in range(IDX_HEADS):
        s = jnp.sum(iq[:, h * IDX_DIM:(h + 1) * IDX_DIM] * ik, axis=1, keepdims=True)
        new = new + iw[:, IW_LANE + h:IW_LANE + h + 1] * jnp.maximum(s, 0.0)
    key_ref[...] = _sortable_key(score_ref[...])
    b_new = _selection_bias(key_ref, hi_ref, lo_ref, bias_ref, tri_ref[...], float(TOPK_MAX),
                            extra=_sortable_key(new))
    bias_new_ref[...] = jnp.broadcast_to(b_new, bias_new_ref.shape)


def _sample_select(score, iq, ik, iw, tri):
    n, n_keys = score.shape
    return pl.pallas_call(
        _sample_select_kernel,
        out_shape=[jax.ShapeDtypeStruct((n, n_keys), F32), jax.ShapeDtypeStruct((n, LANES), F32)],
        scratch_shapes=[pltpu.VMEM((n, n_keys), I32), pltpu.VMEM((n, n_keys), HALF_T), pltpu.VMEM((n, n_keys), HALF_T)],
        compiler_params=pltpu.CompilerParams(vmem_limit_bytes=VMEM_LIMIT),
        name="sample_select",
    )(score, iq, ik, iw, tri)


def _sample_attn_kernel(pt_ref, q_ref, kn_ref, vn_ref, bias_ref, bnew_ref, gn_ref, k_hbm, v_hbm, o_ref,
                        k_buf, v_buf, k_sem, v_sem, *, n_pages):
    group = q_ref.shape[0]
    seqs = range(group)
    slot = _paged_prefetch(pt_ref, (k_hbm, v_hbm), (k_buf, v_buf), (k_sem, v_sem), group, n_pages)
    row = lax.broadcasted_iota(I32, (N_HEADS, HEAD_DIM), 0)
    first = row < KV_GROUP
    q = [q_ref[j] for j in seqs]
    q2 = [jnp.concatenate([jnp.where(first, q[j], 0.0), jnp.where(first, 0.0, q[j])], axis=1).astype(BF16)
          for j in seqs]
    logits = [[None] * n_pages for _ in seqs]
    for pg in range(n_pages):
        for j in seqs:
            kp = k_buf[slot, j, pg].astype(BF16).reshape(KV_W, PAGE_SIZE)
            logits[j][pg] = _dot(q2[j], kp) + bias_ref[j][:, pg * PAGE_SIZE:(pg + 1) * PAGE_SIZE]
    m, p_new, den, acc, v_sel = [], [], [], [], []
    for j in seqs:
        kn = kn_ref[j]
        vn = vn_ref[j]
        k_sel = jnp.where(first, kn[:, 0:HEAD_DIM], kn[:, HEAD_DIM:KV_W])
        v_sel.append(jnp.where(first, vn[:, 0:HEAD_DIM], vn[:, HEAD_DIM:KV_W]))
        l_new = jnp.sum(q[j] * k_sel, axis=1, keepdims=True) + bnew_ref[j][:, 0:1]
        mj = l_new
        for l in logits[j]:
            mj = jnp.maximum(mj, jnp.max(l, axis=1, keepdims=True))
        m.append(mj)
        p_new.append(jnp.exp2(l_new - mj))
        den.append(p_new[j])
        acc.append(jnp.zeros((N_HEADS, KV_W), F32))
    for pg in range(n_pages):
        for j in seqs:
            p = jnp.exp2(logits[j][pg] - m[j])
            den[j] = den[j] + jnp.sum(p, axis=1, keepdims=True)
            vp = v_buf[slot, j, pg].astype(BF16).reshape(KV_W, PAGE_SIZE)
            acc[j] = acc[j] + _dot_nt(p.astype(BF16), vp)
    for j in seqs:
        out = (p_new[j] * v_sel[j] + jnp.where(first, acc[j][:, 0:HEAD_DIM], acc[j][:, HEAD_DIM:KV_W])) / den[j]
        ms = jnp.sum(jnp.sum(out * out, axis=1, keepdims=True), axis=0, keepdims=True) * (1.0 / ATT_W)
        o_ref[j] = (out * lax.rsqrt(ms + EPS) * gn_ref[...]).astype(BF16)


def _sample_attention(page_table, q3, k_new, v_new, bias, bias_new, gn8, cache_k, cache_v):
    n, n_pages = page_table.shape
    n_keys = n_pages * PAGE_SIZE

    g = DECODE_GROUP

    def per_seq(shape):
        return pl.BlockSpec((g,) + shape, lambda s, pt: (s, 0, 0))

    page_buf = pltpu.VMEM((2, g, n_pages, N_KV_HEADS, HEAD_DIM, PAGE_SIZE), F32)
    grid_spec = pltpu.PrefetchScalarGridSpec(
        num_scalar_prefetch=1,
        grid=(n // g,),
        in_specs=[per_seq((N_HEADS, HEAD_DIM)), per_seq((1, KV_W)), per_seq((1, KV_W)),
                  per_seq((1, n_keys)), per_seq((1, LANES)),
                  pl.BlockSpec((N_HEADS, HEAD_DIM), lambda s, pt: (0, 0)),
                  pl.BlockSpec(memory_space=pl.ANY), pl.BlockSpec(memory_space=pl.ANY)],
        out_specs=per_seq((N_HEADS, HEAD_DIM)),
        scratch_shapes=[page_buf, page_buf, pltpu.SemaphoreType.DMA((2,)), pltpu.SemaphoreType.DMA((2,))],
    )
    return pl.pallas_call(
        functools.partial(_sample_attn_kernel, n_pages=n_pages),
        grid_spec=grid_spec,
        out_shape=jax.ShapeDtypeStruct((n, N_HEADS, HEAD_DIM), BF16),
        compiler_params=pltpu.CompilerParams(dimension_semantics=("arbitrary",), vmem_limit_bytes=VMEM_LIMIT),
        name="sample_attention",
    )(page_table, q3, k_new, v_new, bias, bias_new, gn8, cache_k, cache_v)


def _rope_table_np(positions):
    half = ROT_DIM // 2
    freq = ROPE_THETA ** (-(np.arange(half, dtype=np.float64) / half))
    ang = np.asarray(positions, np.float64)[:, None] * freq[None, :]
    cos, sin = np.cos(ang), np.sin(ang)
    n = len(positions)
    tab = np.zeros((n, 3, LANES), np.float64)
    tab[:, 0, :] = 1.0
    for base in range(0, LANES, HEAD_DIM):
        tab[:, 0, base:base + half] = cos
        tab[:, 0, base + half:base + ROT_DIM] = cos
        tab[:, 1, base:base + half] = -sin
        tab[:, 2, base + half:base + ROT_DIM] = sin
    return tab.reshape(n, 3 * LANES).astype(np.float32)


def _block_diag(w):
    n, a, b = w.shape
    return jnp.einsum("nij,nm->nimj", w, jnp.eye(n, dtype=w.dtype)).reshape(n * a, n * b)


def _layer_params(l, ln1_g, w_in, conv_w, conv_b, lru_wa, lru_ba, lru_wx, lru_bx, lru_lambda, q_norm_g,
                  k_norm_g, gn_lru_g, gn_att_g, w_out, ln2_g, w_gate, w_up, w_down):
    w_in_t = jnp.transpose(w_in[l])
    w_idx = jnp.pad(w_in_t[COL_IQ:], ((0, IDXP_W - (w_in.shape[2] - COL_IQ)), (0, 0)))
    head_id = np.arange(ATT_W) // HEAD_DIM
    ones_q = (head_id[:, None] == head_id[None, :]).astype(np.float32)
    return {
        "ln1": ln1_g[l][None, :],
        "w_main": w_in_t[:COL_IQ].astype(BF16),
        "w_idx": w_idx,
        "conv_w": conv_w[l],
        "conv_b": conv_b[l][None, :],
        "wa": _block_diag(lru_wa[l]).astype(BF16),
        "ba": lru_ba[l][None, :],
        "wx": _block_diag(lru_wx[l]).astype(BF16),
        "bx": lru_bx[l][None, :],
        "lam": lru_lambda[l][None, :],
        "qg": jnp.tile(q_norm_g[l], N_HEADS)[None, :],
        "kg": jnp.tile(k_norm_g[l], N_KV_HEADS)[None, :],
        "gnl": gn_lru_g[l][None, :],
        "gna": gn_att_g[l][None, :],
        "gna8": gn_att_g[l].reshape(N_HEADS, HEAD_DIM),
        "ones_q": jnp.asarray(ones_q, BF16),
        "ones_k": jnp.asarray(ones_q[:KV_W, :KV_W], BF16),
        "wo_a": w_out[l][:LRU_W].astype(BF16),
        "wo_b": w_out[l][LRU_W:].astype(BF16),
        "ln2": ln2_g[l][None, :],
        "wg": w_gate[l].astype(BF16),
        "wu": w_up[l].astype(BF16),
        "wd": w_down[l].astype(BF16),
    }


def kernel(x_prompt, x_sample, cache_k, cache_v, cache_ik, state_h, state_conv, page_table, c_prompt, c_sample, ada_w, ada_b, ln1_g, w_in, conv_w, conv_b, lru_wa, lru_ba, lru_wx, lru_bx, lru_lambda, q_norm_g, k_norm_g, gn_lru_g, gn_att_g, w_out, ln2_g, w_gate, w_up, w_down):
    bsz, t, _ = x_prompt.shape
    dbsz, dt, _ = x_sample.shape
    depth = ada_w.shape[0]
    n_pages = page_table.shape[1]
    past_len = n_pages * PAGE_SIZE
    n_pool = cache_k.shape[1]
    assert dt == 1 and t % PROMPT_ROW_TILE == 0 and t % POST_ROW_TILE == 0 and t // 4 >= TOPK_MAX
    assert (past_len + dt) // 4 >= TOPK_MAX

    rope_p = jnp.asarray(_rope_table_np(np.arange(t)))
    rope_s = jnp.asarray(_rope_table_np(past_len + np.arange(dt)))
    tri = jnp.asarray(np.triu(np.ones((TRI_W, TRI_W), np.float32)), BF16)

    yp = x_prompt.reshape(bsz * t, D_MODEL)
    ys = x_sample.reshape(dbsz, D_MODEL)
    c_all = jnp.concatenate([c_prompt, c_sample], axis=0)
    outs_p, outs_s = [], []
    for l in range(depth):
        p = _layer_params(l, ln1_g, w_in, conv_w, conv_b, lru_wa, lru_ba, lru_wx, lru_bx, lru_lambda,
                          q_norm_g, k_norm_g, gn_lru_g, gn_att_g, w_out, ln2_g, w_gate, w_up, w_down)
        mod = _ada(c_all, ada_w[l], ada_b[l][None, :])
        mod_p = mod[:bsz].reshape(bsz, 1, 6 * D_MODEL)
        mod_s = mod[bsz:]

        (na, qh, kt, vt, ikt, ktb, vb, ikcat, iqcat, iw, h_last, conv_new) = _prompt_stage(
            yp.reshape(bsz, t, D_MODEL), mod_p, p, rope_p)
        nb = _prompt_attention(qh, iqcat, iw, ktb, vb, ikcat, p["gna"], tri)
        yp = _post(yp, na.reshape(bsz * t, LRU_W), nb.reshape(bsz * t, ATT_W), mod_p, p, t, POST_ROW_TILE)
        outs_p.append((jnp.transpose(kt, (0, 3, 1, 2)), jnp.transpose(vt, (0, 3, 1, 2)),
                       jnp.transpose(ikt, (0, 2, 1)), h_last.reshape(bsz, LRU_W), conv_new))

        (na_s, q_s, k_s, v_s, ik_s, iq_s, iw_s, h_s, conv_s) = _sample_stage(
            ys, mod_s, p, rope_s, state_conv[l].reshape(dbsz, (CONV_W - 1) * LRU_W), state_h[l])
        score = _sample_scores(page_table, iq_s.reshape(dbsz, IDX_HEADS, IDX_DIM),
                               iw_s[:, IW_LANE:IW_LANE + IDX_HEADS].reshape(dbsz, IDX_HEADS, 1),
                               jnp.transpose(cache_ik[l], (0, 2, 1)))
        bias, bias_new = _sample_select(score.reshape(dbsz, past_len), iq_s, ik_s, iw_s, tri)
        nb_s = _sample_attention(page_table, q_s.reshape(dbsz, N_HEADS, HEAD_DIM),
                                 k_s.reshape(dbsz, 1, KV_W), v_s.reshape(dbsz, 1, KV_W),
                                 bias.reshape(dbsz, 1, past_len), bias_new.reshape(dbsz, 1, LANES), p["gna8"],
                                 jnp.transpose(cache_k[l], (0, 2, 3, 1)), jnp.transpose(cache_v[l], (0, 2, 3, 1)))
        ys = _post(ys, na_s, nb_s.reshape(dbsz, ATT_W), mod_s, p, 1, dbsz)
        outs_s.append((k_s.reshape(dbsz, dt, N_KV_HEADS, HEAD_DIM), v_s.reshape(dbsz, dt, N_KV_HEADS, HEAD_DIM),
                       ik_s.reshape(dbsz, dt, IDX_DIM), h_s,
                       conv_s.reshape(dbsz, CONV_W - 1, LRU_W)))

    def stack(outs, i):
        return jnp.stack([o[i] for o in outs])

    return (yp.reshape(bsz, t, D_MODEL), ys.reshape(dbsz, dt, D_MODEL),
            stack(outs_p, 0), stack(outs_p, 1), stack(outs_p, 2), stack(outs_p, 3), stack(outs_p, 4),
            stack(outs_s, 0), stack(outs_s, 1), stack(outs_s, 2), stack(outs_s, 3), stack(outs_s, 4))
```

```python
import functools

import numpy as np
import jax
import jax.numpy as jnp
from jax import lax
from jax.experimental import pallas as pl
from jax.experimental.pallas import tpu as pltpu

F32 = jnp.float32
BF16 = jnp.bfloat16
I32 = jnp.int32
HALF_T = jnp.int16

D_MODEL = 1024
LRU_W = 512
LRU_BLOCKS = 8
LRU_BW = LRU_W // LRU_BLOCKS
CONV_W = 4
LRU_C = 8.0
N_HEADS = 8
HEAD_DIM = 64
ATT_W = N_HEADS * HEAD_DIM
N_KV_HEADS = 2
KV_GROUP = N_HEADS // N_KV_HEADS
KV_W = N_KV_HEADS * HEAD_DIM
ROT_DIM = HEAD_DIM // 4
ROPE_THETA = 500000.0
IDX_HEADS = 4
IDX_DIM = 64
IDX_W = IDX_HEADS * IDX_DIM
TOPK_MAX = 256
PAGE_SIZE = 128
D_FF = 2816
EPS = 1e-6

COL_XR = 0
COL_GR = COL_XR + LRU_W
COL_Q = COL_GR + LRU_W
COL_K = COL_Q + ATT_W
COL_V = COL_K + KV_W
COL_IQ = COL_V + KV_W
MAIN_W = COL_IQ
TAIL_W = 128
IDXP_W = IDX_W + TAIL_W
IW_LANE = IDX_DIM
IDX_CAT = 6 * IDX_DIM

LANES = 128
SUBLANES = 8
TRI_W = 256
INT_MIN = -2147483648
HALF_MIN = -32768
LOG2E = 1.4426950408889634
NEG_BIG = -1e30
VMEM_LIMIT = 48 * 1024 * 1024

PROMPT_ROW_TILE = 256
Q_BLOCK = 256
POST_ROW_TILE = 512
FF_CHUNK = D_FF // 2
ADA_COL_TILE = 512
DECODE_GROUP = 4
SEARCH_KEYS_PER_STEP = 2048


def _dot(a, b):
    return jnp.dot(a, b, preferred_element_type=F32)


def _dot_nt(a, b):
    return lax.dot_general(a, b, (((1,), (1,)), ((), ())), preferred_element_type=F32)


def _split_bf16(x):
    hi = x.astype(BF16)
    lo = (x - hi.astype(F32)).astype(BF16)
    return hi, lo


def _split3(x):
    p1 = x.astype(BF16)
    r1 = x - p1.astype(F32)
    p2 = r1.astype(BF16)
    p3 = (r1 - p2.astype(F32)).astype(BF16)
    return p1, p2, p3


def _dot_f32(a, b, dot=_dot):
    a1, a2, a3 = a
    b1, b2, b3 = b
    return ((dot(a1, b3) + dot(a2, b2) + dot(a3, b1)) + (dot(a1, b2) + dot(a2, b1))) + dot(a1, b1)


def _cat6_lhs(x, axis):
    r1 = x - x.astype(BF16).astype(F32)
    r2 = r1 - r1.astype(BF16).astype(F32)
    return jnp.concatenate([x, x, r1, x, r1, r2], axis=axis).astype(BF16)


def _cat6_rhs(x, axis):
    p1, p2, p3 = _split3(x)
    return jnp.concatenate([p1, p2, p1, p3, p2, p1], axis=axis)


def _rms(x, g):
    return x * lax.rsqrt(jnp.mean(x * x, axis=-1, keepdims=True) + EPS) * g


def _head_rms(x, ones_bd, g):
    hi, lo = _split_bf16(x * x)
    ss = _dot(hi, ones_bd) + _dot(lo, ones_bd)
    return x * lax.rsqrt(ss * (1.0 / HEAD_DIM) + EPS) * g


def _rope(x, cos, sin_next, sin_prev):
    w = x.shape[-1]
    half = ROT_DIM // 2
    return x * cos + pltpu.roll(x, w - half, 1) * sin_next + pltpu.roll(x, half, 1) * sin_prev


def _rope_tables(tab, reps):
    cos = tab[:, 0:LANES]
    sa = tab[:, LANES:2 * LANES]
    sb = tab[:, 2 * LANES:3 * LANES]
    if reps > 1:
        cos = jnp.concatenate([cos] * reps, axis=1)
        sa = jnp.concatenate([sa] * reps, axis=1)
        sb = jnp.concatenate([sb] * reps, axis=1)
    return cos, sa, sb


def _rope_tail(zt, tab):
    cos, sa, sb = _rope_tables(tab, 1)
    lane = lax.broadcasted_iota(I32, zt.shape, 1)
    first = lane < IDX_DIM
    return _rope(zt, jnp.where(first, cos, 1.0), jnp.where(first, sa, 0.0), jnp.where(first, sb, 0.0))


def _gelu_tanh(x):
    c = float(np.sqrt(2.0 / np.pi))
    return x * (0.5 * (1.0 + jnp.tanh(c * (x + 0.044715 * (x * x * x)))))


def _softplus(x):
    return jnp.maximum(x, 0.0) + jnp.log1p(jnp.exp(-jnp.abs(x)))


def _project_in(x, sh1, sc1, ln1, w_main_t, w_idx_t_parts):
    h = _rms(x, ln1) * (1.0 + sc1) + sh1
    hp = _split3(h)
    return _dot_nt(hp[0], w_main_t), _dot_f32(hp, w_idx_t_parts, _dot_nt)


def _lru_gates(xc, wa, ba, wx, bx, lam):
    xcb = xc.astype(BF16)
    r = jax.nn.sigmoid(_dot(xcb, wa) + ba)
    gi = jax.nn.sigmoid(_dot(xcb, wx) + bx)
    log_a = (-LRU_C * _softplus(-lam)) * r
    a = jnp.exp(log_a)
    y = jnp.tanh(-log_a) * (1.0 + a * a)
    inp = jnp.where(y > 0.0, y * lax.rsqrt(y), 0.0) * gi * xc
    return a, inp


def _qkv_inputs(z, tab, qg, kg, ones_q, ones_k):
    cq, saq, sbq = _rope_tables(tab, ATT_W // LANES)
    ck, sak, sbk = _rope_tables(tab, KV_W // LANES)
    q = _rope(_head_rms(z[:, COL_Q:COL_K], ones_q, qg), cq, saq, sbq) * (HEAD_DIM ** -0.5 * LOG2E)
    k = _rope(_head_rms(z[:, COL_K:COL_V], ones_k, kg), ck, sak, sbk)
    return q, k, z[:, COL_V:COL_IQ]


def _index_inputs(zi, tab):
    ci, sai, sbi = _rope_tables(tab, IDX_W // LANES)
    return _rope(zi[:, 0:IDX_W], ci, sai, sbi), _rope_tail(zi[:, IDX_W:IDXP_W], tab)


def _ada_kernel(c_ref, w_ref, b_ref, o_ref):
    c = c_ref[...]
    s = c * jax.nn.sigmoid(c)
    o_ref[...] = _dot_f32(_split3(s), _split3(w_ref[...])) + b_ref[...]


def _ada(c_all, ada_w, ada_b):
    rows = c_all.shape[0]
    n = ada_w.shape[1]
    return pl.pallas_call(
        _ada_kernel,
        grid=(n // ADA_COL_TILE,),
        in_specs=[
            pl.BlockSpec((rows, D_MODEL), lambda j: (0, 0)),
            pl.BlockSpec((D_MODEL, ADA_COL_TILE), lambda j: (0, j)),
            pl.BlockSpec((1, ADA_COL_TILE), lambda j: (0, j)),
        ],
        out_specs=pl.BlockSpec((rows, ADA_COL_TILE), lambda j: (0, j)),
        out_shape=jax.ShapeDtypeStruct((rows, n), F32),
        compiler_params=pltpu.CompilerParams(dimension_semantics=("arbitrary",), vmem_limit_bytes=VMEM_LIMIT),
        name="ada_modulation",
    )(c_all, ada_w, ada_b)


def _prompt_stage_kernel(x_ref, mod_ref, ln1_ref, wmain_ref, widx_ref, convw_ref, convb_ref,
                         wa_ref, ba_ref, wx_ref, bx_ref, lam_ref, qg_ref, kg_ref, gnl_ref,
                         onesq_ref, onesk_ref, rope_ref,
                         na_ref, q_ref, kt_ref, vt_ref, ikt_ref, ktb_ref, vb_ref,
                         ikcat_ref, iqcat_ref, iw_ref, h_ref, conv_ref,
                         xr_buf, hcarry, widx_parts):
    tt = x_ref.shape[1]
    t = pl.program_id(1)

    @pl.when((t == 0) & (pl.program_id(0) == 0))
    def _():
        for j, part in enumerate(_split3(widx_ref[...])):
            widx_parts[j] = part

    @pl.when(t == 0)
    def _():
        xr_buf[0:SUBLANES, :] = jnp.zeros((SUBLANES, LRU_W), F32)
        hcarry[...] = jnp.zeros_like(hcarry)

    mod = mod_ref[0]
    h1, h2, h3 = _split3(_rms(x_ref[0], ln1_ref[...]) * (1.0 + mod[:, D_MODEL:2 * D_MODEL]) + mod[:, 0:D_MODEL])
    z = _dot_nt(h1, wmain_ref[...])
    zi_small = _dot_nt(h1, widx_parts[2]) + _dot_nt(h2, widx_parts[1])

    xr = z[:, COL_XR:COL_GR]
    xr_buf[SUBLANES:SUBLANES + tt, :] = xr
    convw = convw_ref[...]
    xc = convb_ref[...]
    for i in range(CONV_W):
        xc = xc + xr_buf[pl.ds(SUBLANES - (CONV_W - 1) + i, tt), :] * convw[i:i + 1, :]
    conv_ref[0] = xr_buf[pl.ds(SUBLANES + tt - (CONV_W - 1), CONV_W - 1), :]
    xr_buf[0:SUBLANES, :] = xr_buf[tt:tt + SUBLANES, :]

    zi_small = zi_small + _dot_nt(h3, widx_parts[0])
    zi_mid = _dot_nt(h1, widx_parts[1])
    a, b = _lru_gates(xc, wa_ref[...], ba_ref[...], wx_ref[...], bx_ref[...], lam_ref[...])
    zi_mid = zi_mid + _dot_nt(h2, widx_parts[0])

    q, k, v = _qkv_inputs(z, rope_ref[...], qg_ref[...], kg_ref[...], onesq_ref[...], onesk_ref[...])
    for h in range(N_HEADS):
        q_ref[0, h] = q[:, h * HEAD_DIM:(h + 1) * HEAD_DIM].astype(BF16)
    kt = jnp.transpose(k)
    vt = jnp.transpose(v)
    kt_ref[0] = kt.reshape(N_KV_HEADS, HEAD_DIM, tt)
    vt_ref[0] = vt.reshape(N_KV_HEADS, HEAD_DIM, tt)
    ktb_ref[0] = kt.astype(BF16).reshape(N_KV_HEADS, HEAD_DIM, tt)
    lane = lax.broadcasted_iota(I32, v.shape, 1)
    pad = jnp.where(lane == HEAD_DIM, 1.0, 0.0)
    for n in range(N_KV_HEADS):
        vn = v if n == 0 else pltpu.roll(v, KV_W - n * HEAD_DIM, 1)
        vb_ref[0, n] = jnp.where(lane < HEAD_DIM, vn, pad).astype(BF16)

    zi_big = _dot_nt(h1, widx_parts[0])
    row = lax.broadcasted_iota(I32, a.shape, 0) % SUBLANES
    d = 1
    while d < SUBLANES:
        keep = row >= d
        a_s = jnp.where(keep, pltpu.roll(a, d, 0), 1.0)
        b_s = jnp.where(keep, pltpu.roll(b, d, 0), 0.0)
        b = a * b_s + b
        a = a * a_s
        d *= 2
    h_in = hcarry[...]
    groups = []
    for g in range(tt // SUBLANES):
        rows = slice(g * SUBLANES, (g + 1) * SUBLANES)
        hg = a[rows] * h_in + b[rows]
        h_in = hg[SUBLANES - 1:SUBLANES, :]
        groups.append(hg)
    hr = jnp.concatenate(groups, axis=0)
    hcarry[...] = h_in
    h_ref[0] = h_in

    na_ref[0] = _rms(_gelu_tanh(z[:, COL_GR:COL_Q]) * hr, gnl_ref[...]).astype(BF16)

    zi = (zi_small + zi_mid) + zi_big
    iq, tail = _index_inputs(zi, rope_ref[...])
    ikt = jnp.transpose(tail)[0:IDX_DIM, :]
    ikt_ref[0] = ikt
    ikcat_ref[0] = _cat6_rhs(ikt, 0)
    for h in range(IDX_HEADS):
        iqcat_ref[0, h] = _cat6_lhs(iq[:, h * IDX_DIM:(h + 1) * IDX_DIM], 1)
    iw_ref[0] = zi[:, IDX_W:IDXP_W]


def _const_spec(shape):
    nd = len(shape)
    return pl.BlockSpec(shape, lambda *_: (0,) * nd)


def _prompt_stage(x, mod_p, p, rope_tab):
    bsz, t, _ = x.shape
    tt = PROMPT_ROW_TILE

    def rows(w):
        return pl.BlockSpec((1, tt, w), lambda b, i: (b, i, 0))

    in_specs = [
        rows(D_MODEL),
        pl.BlockSpec((1, 1, 6 * D_MODEL), lambda b, i: (b, 0, 0)),
        _const_spec((1, D_MODEL)),
        _const_spec((MAIN_W, D_MODEL)),
        _const_spec((IDXP_W, D_MODEL)),
        _const_spec((CONV_W, LRU_W)),
        _const_spec((1, LRU_W)),
        _const_spec((LRU_W, LRU_W)),
        _const_spec((1, LRU_W)),
        _const_spec((LRU_W, LRU_W)),
        _const_spec((1, LRU_W)),
        _const_spec((1, LRU_W)),
        _const_spec((1, ATT_W)),
        _const_spec((1, KV_W)),
        _const_spec((1, LRU_W)),
        _const_spec((ATT_W, ATT_W)),
        _const_spec((KV_W, KV_W)),
        pl.BlockSpec((tt, 3 * LANES), lambda b, i: (i, 0)),
    ]

    def heads(n, w):
        return pl.BlockSpec((1, n, tt, w), lambda b, i: (b, 0, i, 0))

    def heads_t(n):
        return pl.BlockSpec((1, n, HEAD_DIM, tt), lambda b, i: (b, 0, 0, i))

    def feat_t(w):
        return pl.BlockSpec((1, w, tt), lambda b, i: (b, 0, i))

    out_specs = [
        rows(LRU_W), heads(N_HEADS, HEAD_DIM), heads_t(N_KV_HEADS), heads_t(N_KV_HEADS), feat_t(IDX_DIM),
        heads_t(N_KV_HEADS), heads(N_KV_HEADS, KV_W), feat_t(IDX_CAT), heads(IDX_HEADS, IDX_CAT), rows(TAIL_W),
        pl.BlockSpec((1, 1, LRU_W), lambda b, i: (b, 0, 0)),
        pl.BlockSpec((1, CONV_W - 1, LRU_W), lambda b, i: (b, 0, 0)),
    ]
    out_shape = [
        jax.ShapeDtypeStruct((bsz, t, LRU_W), BF16),
        jax.ShapeDtypeStruct((bsz, N_HEADS, t, HEAD_DIM), BF16),
        jax.ShapeDtypeStruct((bsz, N_KV_HEADS, HEAD_DIM, t), F32),
        jax.ShapeDtypeStruct((bsz, N_KV_HEADS, HEAD_DIM, t), F32),
        jax.ShapeDtypeStruct((bsz, IDX_DIM, t), F32),
        jax.ShapeDtypeStruct((bsz, N_KV_HEADS, HEAD_DIM, t), BF16),
        jax.ShapeDtypeStruct((bsz, N_KV_HEADS, t, KV_W), BF16),
        jax.ShapeDtypeStruct((bsz, IDX_CAT, t), BF16),
        jax.ShapeDtypeStruct((bsz, IDX_HEADS, t, IDX_CAT), BF16),
        jax.ShapeDtypeStruct((bsz, t, TAIL_W), F32),
        jax.ShapeDtypeStruct((bsz, 1, LRU_W), F32),
        jax.ShapeDtypeStruct((bsz, CONV_W - 1, LRU_W), F32),
    ]
    return pl.pallas_call(
        _prompt_stage_kernel,
        grid=(bsz, t // tt),
        in_specs=in_specs,
        out_specs=out_specs,
        out_shape=out_shape,
        scratch_shapes=[pltpu.VMEM((tt + SUBLANES, LRU_W), F32), pltpu.VMEM((1, LRU_W), F32),
                        pltpu.VMEM((3, IDXP_W, D_MODEL), BF16)],
        compiler_params=pltpu.CompilerParams(dimension_semantics=("arbitrary", "arbitrary"),
                                             vmem_limit_bytes=VMEM_LIMIT),
        name="prompt_stage",
    )(x, mod_p, p["ln1"], p["w_main"], p["w_idx"], p["conv_w"], p["conv_b"], p["wa"], p["ba"],
      p["wx"], p["bx"], p["lam"], p["qg"], p["kg"], p["gnl"], p["ones_q"], p["ones_k"], rope_tab)


def _sortable_key(score):
    score = jnp.where(score == 0.0, 0.0, score)
    bits = lax.bitcast_convert_type(score, I32)
    return jnp.where(bits < 0, bits ^ jnp.int32(0x7FFFFFFF), bits)


def _count_ge16(ref, c, strict=False):
    c16 = c.astype(HALF_T)
    hit = ref[...] > c16 if strict else ref[...] >= c16
    ones = jnp.where(hit, jnp.ones((), HALF_T), jnp.zeros((), HALF_T))
    acc = ones[:, 0:LANES]
    for j in range(1, ref.shape[1] // LANES):
        acc = acc + ones[:, j * LANES:(j + 1) * LANES]
    return jnp.sum(acc.astype(F32), axis=1, keepdims=True)


def _bitwise_max16(count_ge, k, n_all):
    def body(i, carry):
        t, ct = carry
        c = t + lax.shift_left(jnp.int32(1), jnp.int32(15) - i)
        cn = count_ge(c)
        ok = cn >= k
        return jnp.where(ok, c, t), jnp.where(ok, cn, ct)

    init = (jnp.full(k.shape, HALF_MIN, I32), jnp.full(k.shape, float(n_all), F32))
    return lax.fori_loop(0, 16, body, init, unroll=True)


def _kth_largest(key_ref, hi_ref, lo_ref, k, extra=None):
    key = key_ref[...]
    hi_ref[...] = lax.shift_right_arithmetic(key, 16).astype(HALF_T)
    lo_ref[...] = ((key & 0xFFFF) + HALF_MIN).astype(HALF_T)
    kf = jnp.full((key_ref.shape[0], 1), k, F32)
    n_all = key_ref.shape[1] + (0 if extra is None else 1)
    if extra is not None:
        hi_x = lax.shift_right_arithmetic(extra, 16)
        lo_x = (extra & 0xFFFF) + HALF_MIN

    def count_hi(c):
        cnt = _count_ge16(hi_ref, c)
        return cnt if extra is None else cnt + jnp.where(hi_x >= c, 1.0, 0.0)

    t_hi, n_bucket_up = _bitwise_max16(count_hi, kf, n_all)
    above = _count_ge16(hi_ref, t_hi, strict=True)
    if extra is not None:
        above = above + jnp.where(hi_x > t_hi, 1.0, 0.0)
    k_lo = kf - above
    lo_ref[...] = jnp.where(hi_ref[...] == t_hi.astype(HALF_T), lo_ref[...], jnp.full((), HALF_MIN, HALF_T))
    if extra is not None:
        lo_x = jnp.where(hi_x == t_hi, lo_x, HALF_MIN)

    def count_lo(c):
        cnt = _count_ge16(lo_ref, c)
        return cnt if extra is None else cnt + jnp.where(lo_x >= c, 1.0, 0.0)

    t_lo, n_lo = _bitwise_max16(count_lo, k_lo, n_all)
    n_ge = jnp.where(t_lo > HALF_MIN, above + n_lo, n_bucket_up)
    return t_hi * 65536 + (t_lo - HALF_MIN), n_ge


def _selection_bias(key_ref, hi_ref, lo_ref, bias_ref, tri, k, extra=None):
    thr, n_ge = _kth_largest(key_ref, hi_ref, lo_ref, k, extra)
    has_ties = jnp.max(jnp.where(n_ge > k, 1.0, 0.0)) > 0.0

    @pl.when(jnp.logical_not(has_ties))
    def _():
        bias_ref[...] = jnp.where(key_ref[...] >= thr, 0.0, NEG_BIG)

    @pl.when(has_ties)
    def _():
        n_gt = jnp.sum(jnp.where(key_ref[...] > thr, 1.0, 0.0), axis=1, keepdims=True)
        if extra is not None:
            n_gt = n_gt + jnp.where(extra > thr, 1.0, 0.0)
        need = k - n_gt
        offset = jnp.zeros_like(need)
        for c in range(key_ref.shape[1] // TRI_W):
            kc = key_ref[:, c * TRI_W:(c + 1) * TRI_W]
            eq = kc == thr
            e = jnp.where(eq, 1.0, 0.0)
            incl = _dot(e.astype(BF16), tri)
            rank = incl - e + offset
            tie = jnp.where(rank < need, 0.0, NEG_BIG)
            bias_ref[:, c * TRI_W:(c + 1) * TRI_W] = jnp.where(kc > thr, 0.0, jnp.where(eq, tie, NEG_BIG))
            offset = offset + incl[:, TRI_W - 1:TRI_W]

    if extra is None:
        return None
    n_gt = jnp.sum(jnp.where(key_ref[...] > thr, 1.0, 0.0), axis=1, keepdims=True) + jnp.where(extra > thr, 1.0, 0.0)
    n_eq_main = n_ge - n_gt - jnp.where(extra == thr, 1.0, 0.0)
    tie = jnp.where(n_eq_main < k - n_gt, 0.0, NEG_BIG)
    return jnp.where(extra > thr, 0.0, jnp.where(extra == thr, tie, NEG_BIG))


def _prompt_attn_kernel(q_ref, iq_ref, iw_ref, kt_ref, v_ref, ikt_ref, gn_ref, tri_ref, o_ref,
                        key_ref, hi_ref, lo_ref, bias_ref, *, q_start):
    nb, qb = o_ref.shape[0], o_ref.shape[1]
    n_keys = bias_ref.shape[1]
    qpos = q_start + lax.broadcasted_iota(I32, (qb, n_keys), 0)
    kpos = lax.broadcasted_iota(I32, (qb, n_keys), 1)
    causal = kpos <= qpos
    if n_keys > TOPK_MAX:
        for b in range(nb):
            iw = iw_ref[b]
            ikt = ikt_ref[b]
            score = jnp.zeros((qb, n_keys), F32)
            s_next = _dot(iq_ref[b, 0], ikt)
            for h in range(IDX_HEADS):
                s, s_next = s_next, (_dot(iq_ref[b, h + 1], ikt) if h + 1 < IDX_HEADS else None)
                score = score + iw[:, IW_LANE + h:IW_LANE + h + 1] * jnp.maximum(s, 0.0)
            key_ref[b * qb:(b + 1) * qb, :] = jnp.where(causal, _sortable_key(score), INT_MIN)
        _selection_bias(key_ref, hi_ref, lo_ref, bias_ref, tri_ref[...], float(TOPK_MAX))
        for b in range(nb):
            bias_ref[b * qb:(b + 1) * qb, :] = jnp.where(causal, bias_ref[b * qb:(b + 1) * qb, :], NEG_BIG)
    else:
        for b in range(nb):
            bias_ref[b * qb:(b + 1) * qb, :] = jnp.where(causal, 0.0, NEG_BIG)

    def qk(i):
        b, h = divmod(i, N_HEADS)
        return _dot(q_ref[b, h], kt_ref[b, h // KV_GROUP]) + bias_ref[b * qb:(b + 1) * qb, :]

    logits_next = qk(0)
    for b in range(nb):
        outs = []
        for h in range(N_HEADS):
            i = b * N_HEADS + h
            logits, logits_next = logits_next, (qk(i + 1) if i + 1 < nb * N_HEADS else None)
            m = jnp.max(logits, axis=1, keepdims=True)
            p = jnp.exp2((logits - m).astype(BF16))
            pv = _dot(p, v_ref[b, h // KV_GROUP])
            outs.append(pv[:, 0:HEAD_DIM] / pv[:, HEAD_DIM:HEAD_DIM + 1])
        o_ref[b] = _rms(jnp.concatenate(outs, axis=1), gn_ref[...]).astype(BF16)


def _prompt_attention_block(qh, iqcat, iw, ktb, vb, ikcat, gn_att, tri, q_block):
    bsz, _, t, _ = qh.shape
    qb = Q_BLOCK
    n_keys = (q_block + 1) * qb
    nb = max(1, min(bsz, SEARCH_KEYS_PER_STEP // n_keys))
    while bsz % nb:
        nb -= 1
    rows = nb * qb

    def qheads(n, w):
        return pl.BlockSpec((nb, n, qb, w), lambda b: (b, 0, q_block, 0))

    return pl.pallas_call(
        functools.partial(_prompt_attn_kernel, q_start=q_block * qb),
        grid=(bsz // nb,),
        in_specs=[qheads(N_HEADS, HEAD_DIM), qheads(IDX_HEADS, IDX_CAT),
                  pl.BlockSpec((nb, qb, TAIL_W), lambda b: (b, q_block, 0)),
                  pl.BlockSpec((nb, N_KV_HEADS, HEAD_DIM, n_keys), lambda b: (b, 0, 0, 0)),
                  pl.BlockSpec((nb, N_KV_HEADS, n_keys, KV_W), lambda b: (b, 0, 0, 0)),
                  pl.BlockSpec((nb, IDX_CAT, n_keys), lambda b: (b, 0, 0)),
                  _const_spec((1, ATT_W)), _const_spec((TRI_W, TRI_W))],
        out_specs=pl.BlockSpec((nb, qb, ATT_W), lambda b: (b, 0, 0)),
        out_shape=jax.ShapeDtypeStruct((bsz, qb, ATT_W), BF16),
        scratch_shapes=[pltpu.VMEM((rows, n_keys), I32), pltpu.VMEM((rows, n_keys), HALF_T),
                        pltpu.VMEM((rows, n_keys), HALF_T), pltpu.VMEM((rows, n_keys), F32)],
        compiler_params=pltpu.CompilerParams(dimension_semantics=("arbitrary",), vmem_limit_bytes=VMEM_LIMIT),
        name=f"prompt_attention_{q_block}",
    )(qh, iqcat, iw, ktb, vb, ikcat, gn_att, tri)


def _prompt_attention(qh, iqcat, iw, ktb, vb, ikcat, gn_att, tri):
    t = qh.shape[2]
    blocks = [_prompt_attention_block(qh, iqcat, iw, ktb, vb, ikcat, gn_att, tri, i) for i in range(t // Q_BLOCK)]
    return jnp.concatenate(blocks, axis=1)


def _post_kernel(x_ref, na_ref, nb_ref, mod_ref, woa_ref, wob_ref, ln2_ref, wg_ref, wu_ref, wd_ref,
                 y_ref, *, shared_mod):
    mod = mod_ref[0] if shared_mod else mod_ref[...]
    g1 = mod[:, 2 * D_MODEL:3 * D_MODEL]
    sh2 = mod[:, 3 * D_MODEL:4 * D_MODEL]
    sc2 = mod[:, 4 * D_MODEL:5 * D_MODEL]
    g2 = mod[:, 5 * D_MODEL:6 * D_MODEL]
    x = x_ref[...]
    mix = _dot(na_ref[...], woa_ref[...]) + _dot(nb_ref[...], wob_ref[...])
    x1 = x + g1 * mix
    h2 = (_rms(x1, ln2_ref[...]) * (1.0 + sc2) + sh2).astype(BF16)
    def gate_up(c):
        return _dot(h2, wg_ref[:, c * FF_CHUNK:(c + 1) * FF_CHUNK]), _dot(h2, wu_ref[:, c * FF_CHUNK:(c + 1) * FF_CHUNK])

    n_chunks = D_FF // FF_CHUNK
    ff = jnp.zeros_like(x1)
    gu_next = gate_up(0)
    for c in range(n_chunks):
        (g, u), gu_next = gu_next, (gate_up(c + 1) if c + 1 < n_chunks else None)
        act = (g * jax.nn.sigmoid(g) * u).astype(BF16)
        ff = ff + _dot(act, wd_ref[c * FF_CHUNK:(c + 1) * FF_CHUNK, :])
    y_ref[...] = x1 + g2 * ff


def _post(x, na, nb, mod, p, rows_per_mod, tile):
    n = x.shape[0]
    shared = rows_per_mod > 1
    if shared:
        mod_spec = pl.BlockSpec((1, 1, 6 * D_MODEL), lambda i: (i * tile // rows_per_mod, 0, 0))
    else:
        mod_spec = pl.BlockSpec((tile, 6 * D_MODEL), lambda i: (i, 0))

    def rows(w):
        return pl.BlockSpec((tile, w), lambda i: (i, 0))

    def weight(shape):
        return pl.BlockSpec(shape, lambda i: (0, 0), pipeline_mode=pl.Buffered(1))

    return pl.pallas_call(
        functools.partial(_post_kernel, shared_mod=shared),
        grid=(n // tile,),
        in_specs=[rows(D_MODEL), rows(LRU_W), rows(ATT_W), mod_spec,
                  weight((LRU_W, D_MODEL)), weight((ATT_W, D_MODEL)), weight((1, D_MODEL)),
                  weight((D_MODEL, D_FF)), weight((D_MODEL, D_FF)), weight((D_FF, D_MODEL))],
        out_specs=rows(D_MODEL),
        out_shape=jax.ShapeDtypeStruct((n, D_MODEL), F32),
        compiler_params=pltpu.CompilerParams(dimension_semantics=("arbitrary",), vmem_limit_bytes=VMEM_LIMIT),
        name="post_ffn",
    )(x, na, nb, mod, p["wo_a"], p["wo_b"], p["ln2"], p["wg"], p["wu"], p["wd"])


def _sample_stage_kernel(x_ref, mod_ref, ln1_ref, wmain_ref, widx_ref, convw_ref, convb_ref,
                         wa_ref, ba_ref, wx_ref, bx_ref, lam_ref, qg_ref, kg_ref, gnl_ref,
                         onesq_ref, onesk_ref, rope_ref, sconv_ref, sh_ref,
                         na_ref, q_ref, k_ref, v_ref, ik_ref, iq_ref, iw_ref, h_ref, conv_ref):
    mod = mod_ref[...]
    z, zi = _project_in(x_ref[...], mod[:, 0:D_MODEL], mod[:, D_MODEL:2 * D_MODEL], ln1_ref[...],
                        wmain_ref[...], _split3(widx_ref[...]))
    xr = z[:, COL_XR:COL_GR]
    convw = convw_ref[...]
    sconv = sconv_ref[...]
    xc = convb_ref[...]
    for i in range(CONV_W - 1):
        xc = xc + sconv[:, i * LRU_W:(i + 1) * LRU_W] * convw[i:i + 1, :]
    xc = xc + xr * convw[CONV_W - 1:CONV_W, :]
    conv_ref[:, 0:(CONV_W - 2) * LRU_W] = sconv[:, LRU_W:(CONV_W - 1) * LRU_W]
    conv_ref[:, (CONV_W - 2) * LRU_W:(CONV_W - 1) * LRU_W] = xr

    a, b = _lru_gates(xc, wa_ref[...], ba_ref[...], wx_ref[...], bx_ref[...], lam_ref[...])
    hr = a * sh_ref[...] + b
    h_ref[...] = hr
    na_ref[...] = _rms(_gelu_tanh(z[:, COL_GR:COL_Q]) * hr, gnl_ref[...]).astype(BF16)

    q, k, v = _qkv_inputs(z, rope_ref[...], qg_ref[...], kg_ref[...], onesq_ref[...], onesk_ref[...])
    iq, tail = _index_inputs(zi, rope_ref[...])
    q_ref[...] = q
    k_ref[...] = k
    v_ref[...] = v
    ik_ref[...] = tail[:, 0:IDX_DIM]
    iq_ref[...] = iq
    iw_ref[...] = zi[:, IDX_W:IDXP_W]


def _sample_stage(x, mod_s, p, rope_row, sconv, sh):
    n = x.shape[0]

    def out(w, dt=F32):
        return jax.ShapeDtypeStruct((n, w), dt)

    return pl.pallas_call(
        _sample_stage_kernel,
        out_shape=[out(LRU_W, BF16), out(ATT_W), out(KV_W), out(KV_W), out(IDX_DIM), out(IDX_W),
                   out(TAIL_W), out(LRU_W), out((CONV_W - 1) * LRU_W)],
        compiler_params=pltpu.CompilerParams(vmem_limit_bytes=VMEM_LIMIT),
        name="sample_stage",
    )(x, mod_s, p["ln1"], p["w_main"], p["w_idx"], p["conv_w"], p["conv_b"], p["wa"], p["ba"],
      p["wx"], p["bx"], p["lam"], p["qg"], p["kg"], p["gnl"], p["ones_q"], p["ones_k"], rope_row, sconv, sh)


def _page_copy(cache_ref, buf_ref, sem, page, slot, j, pg):
    return pltpu.make_async_copy(cache_ref.at[page], buf_ref.at[slot, j, pg], sem)


def _paged_prefetch(pt_ref, caches, bufs, sems, group, n_pages):
    s = pl.program_id(0)
    slot = s % 2

    def start(grp, half):
        for cache_ref, buf_ref, sem in zip(caches, bufs, sems):
            for j in range(group):
                for pg in range(n_pages):
                    page = pt_ref[grp * group + j, pg]
                    _page_copy(cache_ref, buf_ref, sem.at[half], page, half, j, pg).start()

    @pl.when(s == 0)
    def _():
        start(0, 0)

    @pl.when(s + 1 < pl.num_programs(0))
    def _():
        start(s + 1, 1 - slot)

    for cache_ref, buf_ref, sem in zip(caches, bufs, sems):
        for j in range(group):
            for pg in range(n_pages):
                _page_copy(cache_ref, buf_ref, sem.at[slot], 0, slot, j, pg).wait()
    return slot


def _sample_score_kernel(pt_ref, iq_ref, iw_ref, ik_hbm, o_ref, ik_buf, sem, *, n_pages):
    group = iq_ref.shape[0]
    slot = _paged_prefetch(pt_ref, (ik_hbm,), (ik_buf,), (sem,), group, n_pages)
    iqcat = [_cat6_lhs(iq_ref[j], 1) for j in range(group)]
    for pg in range(n_pages):
        for j in range(group):
            s = _dot(iqcat[j], _cat6_rhs(ik_buf[slot, j, pg], 0))
            o_ref[j, :, pg * PAGE_SIZE:(pg + 1) * PAGE_SIZE] = jnp.sum(iw_ref[j] * jnp.maximum(s, 0.0), axis=0,
                                                                        keepdims=True)


def _sample_scores(page_table, iq3, iw3, cache_ik):
    n, n_pages = page_table.shape
    g = DECODE_GROUP
    grid_spec = pltpu.PrefetchScalarGridSpec(
        num_scalar_prefetch=1,
        grid=(n // g,),
        in_specs=[pl.BlockSpec((g, IDX_HEADS, IDX_DIM), lambda s, pt: (s, 0, 0)),
                  pl.BlockSpec((g, IDX_HEADS, 1), lambda s, pt: (s, 0, 0)),
                  pl.BlockSpec(memory_space=pl.ANY)],
        out_specs=pl.BlockSpec((g, 1, n_pages * PAGE_SIZE), lambda s, pt: (s, 0, 0)),
        scratch_shapes=[pltpu.VMEM((2, g, n_pages, IDX_DIM, PAGE_SIZE), F32), pltpu.SemaphoreType.DMA((2,))],
    )
    return pl.pallas_call(
        functools.partial(_sample_score_kernel, n_pages=n_pages),
        grid_spec=grid_spec,
        out_shape=jax.ShapeDtypeStruct((n, 1, n_pages * PAGE_SIZE), F32),
        compiler_params=pltpu.CompilerParams(dimension_semantics=("arbitrary",), vmem_limit_bytes=VMEM_LIMIT),
        name="sample_scores",
    )(page_table, iq3, iw3, cache_ik)


def _sample_select_kernel(score_ref, iq_ref, ik_ref, iw_ref, tri_ref, bias_ref, bias_new_ref,
                          key_ref, hi_ref, lo_ref):
    iq = iq_ref[...]
    ik = ik_ref[...]
    iw = iw_ref[...]
    new = jnp.zeros((iq.shape[0], 1), F32)
    for h in range(IDX_HEADS):
        s = jnp.sum(iq[:, h * IDX_DIM:(h + 1) * IDX_DIM] * ik, axis=1, keepdims=True)
        new = new + iw[:, IW_LANE + h:IW_LANE + h + 1] * jnp.maximum(s, 0.0)
    key_ref[...] = _sortable_key(score_ref[...])
    b_new = _selection_bias(key_ref, hi_ref, lo_ref, bias_ref, tri_ref[...], float(TOPK_MAX),
                            extra=_sortable_key(new))
    bias_new_ref[...] = jnp.broadcast_to(b_new, bias_new_ref.shape)


def _sample_select(score, iq, ik, iw, tri):
    n, n_keys = score.shape
    return pl.pallas_call(
        _sample_select_kernel,
        out_shape=[jax.ShapeDtypeStruct((n, n_keys), F32), jax.ShapeDtypeStruct((n, LANES), F32)],
        scratch_shapes=[pltpu.VMEM((n, n_keys), I32), pltpu.VMEM((n, n_keys), HALF_T), pltpu.VMEM((n, n_keys), HALF_T)],
        compiler_params=pltpu.CompilerParams(vmem_limit_bytes=VMEM_LIMIT),
        name="sample_select",
    )(score, iq, ik, iw, tri)


def _sample_attn_kernel(pt_ref, q_ref, kn_ref, vn_ref, bias_ref, bnew_ref, gn_ref, k_hbm, v_hbm, o_ref,
                        k_buf, v_buf, k_sem, v_sem, *, n_pages):
    group = q_ref.shape[0]
    seqs = range(group)
    slot = _paged_prefetch(pt_ref, (k_hbm, v_hbm), (k_buf, v_buf), (k_sem, v_sem), group, n_pages)
    row = lax.broadcasted_iota(I32, (N_HEADS, HEAD_DIM), 0)
    first = row < KV_GROUP
    q = [q_ref[j] for j in seqs]
    q2 = [jnp.concatenate([jnp.where(first, q[j], 0.0), jnp.where(first, 0.0, q[j])], axis=1).astype(BF16)
          for j in seqs]
    logits = [[None] * n_pages for _ in seqs]
    for pg in range(n_pages):
        for j in seqs:
            kp = k_buf[slot, j, pg].astype(BF16).reshape(KV_W, PAGE_SIZE)
            logits[j][pg] = _dot(q2[j], kp) + bias_ref[j][:, pg * PAGE_SIZE:(pg + 1) * PAGE_SIZE]
    m, p_new, den, acc, v_sel = [], [], [], [], []
    for j in seqs:
        kn = kn_ref[j]
        vn = vn_ref[j]
        k_sel = jnp.where(first, kn[:, 0:HEAD_DIM], kn[:, HEAD_DIM:KV_W])
        v_sel.append(jnp.where(first, vn[:, 0:HEAD_DIM], vn[:, HEAD_DIM:KV_W]))
        l_new = jnp.sum(q[j] * k_sel, axis=1, keepdims=True) + bnew_ref[j][:, 0:1]
        mj = l_new
        for l in logits[j]:
            mj = jnp.maximum(mj, jnp.max(l, axis=1, keepdims=True))
        m.append(mj)
        p_new.append(jnp.exp2(l_new - mj))
        den.append(p_new[j])
        acc.append(jnp.zeros((N_HEADS, KV_W), F32))
    for pg in range(n_pages):
        for j in seqs:
            p = jnp.exp2(logits[j][pg] - m[j])
            den[j] = den[j] + jnp.sum(p, axis=1, keepdims=True)
            vp = v_buf[slot, j, pg].astype(BF16).reshape(KV_W, PAGE_SIZE)
            acc[j] = acc[j] + _dot_nt(p.astype(BF16), vp)
    for j in seqs:
        out = (p_new[j] * v_sel[j] + jnp.where(first, acc[j][:, 0:HEAD_DIM], acc[j][:, HEAD_DIM:KV_W])) / den[j]
        ms = jnp.sum(jnp.sum(out * out, axis=1, keepdims=True), axis=0, keepdims=True) * (1.0 / ATT_W)
        o_ref[j] = (out * lax.rsqrt(ms + EPS) * gn_ref[...]).astype(BF16)


def _sample_attention(page_table, q3, k_new, v_new, bias, bias_new, gn8, cache_k, cache_v):
    n, n_pages = page_table.shape
    n_keys = n_pages * PAGE_SIZE

    g = DECODE_GROUP

    def per_seq(shape):
        return pl.BlockSpec((g,) + shape, lambda s, pt: (s, 0, 0))

    page_buf = pltpu.VMEM((2, g, n_pages, N_KV_HEADS, HEAD_DIM, PAGE_SIZE), F32)
    grid_spec = pltpu.PrefetchScalarGridSpec(
        num_scalar_prefetch=1,
        grid=(n // g,),
        in_specs=[per_seq((N_HEADS, HEAD_DIM)), per_seq((1, KV_W)), per_seq((1, KV_W)),
                  per_seq((1, n_keys)), per_seq((1, LANES)),
                  pl.BlockSpec((N_HEADS, HEAD_DIM), lambda s, pt: (0, 0)),
                  pl.BlockSpec(memory_space=pl.ANY), pl.BlockSpec(memory_space=pl.ANY)],
        out_specs=per_seq((N_HEADS, HEAD_DIM)),
        scratch_shapes=[page_buf, page_buf, pltpu.SemaphoreType.DMA((2,)), pltpu.SemaphoreType.DMA((2,))],
    )
    return pl.pallas_call(
        functools.partial(_sample_attn_kernel, n_pages=n_pages),
        grid_spec=grid_spec,
        out_shape=jax.ShapeDtypeStruct((n, N_HEADS, HEAD_DIM), BF16),
        compiler_params=pltpu.CompilerParams(dimension_semantics=("arbitrary",), vmem_limit_bytes=VMEM_LIMIT),
        name="sample_attention",
    )(page_table, q3, k_new, v_new, bias, bias_new, gn8, cache_k, cache_v)


def _rope_table_np(positions):
    half = ROT_DIM // 2
    freq = ROPE_THETA ** (-(np.arange(half, dtype=np.float64) / half))
    ang = np.asarray(positions, np.float64)[:, None] * freq[None, :]
    cos, sin = np.cos(ang), np.sin(ang)
    n = len(positions)
    tab = np.zeros((n, 3, LANES), np.float64)
    tab[:, 0, :] = 1.0
    for base in range(0, LANES, HEAD_DIM):
        tab[:, 0, base:base + half] = cos
        tab[:, 0, base + half:base + ROT_DIM] = cos
        tab[:, 1, base:base + half] = -sin
        tab[:, 2, base + half:base + ROT_DIM] = sin
    return tab.reshape(n, 3 * LANES).astype(np.float32)


def _block_diag(w):
    n, a, b = w.shape
    return jnp.einsum("nij,nm->nimj", w, jnp.eye(n, dtype=w.dtype)).reshape(n * a, n * b)


def _layer_params(l, ln1_g, w_in, conv_w, conv_b, lru_wa, lru_ba, lru_wx, lru_bx, lru_lambda, q_norm_g,
                  k_norm_g, gn_lru_g, gn_att_g, w_out, ln2_g, w_gate, w_up, w_down):
    w_in_t = jnp.transpose(w_in[l])
    w_idx = jnp.pad(w_in_t[COL_IQ:], ((0, IDXP_W - (w_in.shape[2] - COL_IQ)), (0, 0)))
    head_id = np.arange(ATT_W) // HEAD_DIM
    ones_q = (head_id[:, None] == head_id[None, :]).astype(np.float32)
    return {
        "ln1": ln1_g[l][None, :],
        "w_main": w_in_t[:COL_IQ].astype(BF16),
        "w_idx": w_idx,
        "conv_w": conv_w[l],
        "conv_b": conv_b[l][None, :],
        "wa": _block_diag(lru_wa[l]).astype(BF16),
        "ba": lru_ba[l][None, :],
        "wx": _block_diag(lru_wx[l]).astype(BF16),
        "bx": lru_bx[l][None, :],
        "lam": lru_lambda[l][None, :],
        "qg": jnp.tile(q_norm_g[l], N_HEADS)[None, :],
        "kg": jnp.tile(k_norm_g[l], N_KV_HEADS)[None, :],
        "gnl": gn_lru_g[l][None, :],
        "gna": gn_att_g[l][None, :],
        "gna8": gn_att_g[l].reshape(N_HEADS, HEAD_DIM),
        "ones_q": jnp.asarray(ones_q, BF16),
        "ones_k": jnp.asarray(ones_q[:KV_W, :KV_W], BF16),
        "wo_a": w_out[l][:LRU_W].astype(BF16),
        "wo_b": w_out[l][LRU_W:].astype(BF16),
        "ln2": ln2_g[l][None, :],
        "wg": w_gate[l].astype(BF16),
        "wu": w_up[l].astype(BF16),
        "wd": w_down[l].astype(BF16),
    }


def kernel(x_prompt, x_sample, cache_k, cache_v, cache_ik, state_h, state_conv, page_table, c_prompt, c_sample, ada_w, ada_b, ln1_g, w_in, conv_w, conv_b, lru_wa, lru_ba, lru_wx, lru_bx, lru_lambda, q_norm_g, k_norm_g, gn_lru_g, gn_att_g, w_out, ln2_g, w_gate, w_up, w_down):
    bsz, t, _ = x_prompt.shape
    dbsz, dt, _ = x_sample.shape
    depth = ada_w.shape[0]
    n_pages = page_table.shape[1]
    past_len = n_pages * PAGE_SIZE
    n_pool = cache_k.shape[1]
    assert dt == 1 and t % PROMPT_ROW_TILE == 0 and t % POST_ROW_TILE == 0 and t // 4 >= TOPK_MAX
    assert (past_len + dt) // 4 >= TOPK_MAX

    rope_p = jnp.asarray(_rope_table_np(np.arange(t)))
    rope_s = jnp.asarray(_rope_table_np(past_len + np.arange(dt)))
    tri = jnp.asarray(np.triu(np.ones((TRI_W, TRI_W), np.float32)), BF16)

    yp = x_prompt.reshape(bsz * t, D_MODEL)
    ys = x_sample.reshape(dbsz, D_MODEL)
    c_all = jnp.concatenate([c_prompt, c_sample], axis=0)
    outs_p, outs_s = [], []
    for l in range(depth):
        p = _layer_params(l, ln1_g, w_in, conv_w, conv_b, lru_wa, lru_ba, lru_wx, lru_bx, lru_lambda,
                          q_norm_g, k_norm_g, gn_lru_g, gn_att_g, w_out, ln2_g, w_gate, w_up, w_down)
        mod = _ada(c_all, ada_w[l], ada_b[l][None, :])
        mod_p = mod[:bsz].reshape(bsz, 1, 6 * D_MODEL)
        mod_s = mod[bsz:]

        (na, qh, kt, vt, ikt, ktb, vb, ikcat, iqcat, iw, h_last, conv_new) = _prompt_stage(
            yp.reshape(bsz, t, D_MODEL), mod_p, p, rope_p)
        nb = _prompt_attention(qh, iqcat, iw, ktb, vb, ikcat, p["gna"], tri)
        yp = _post(yp, na.reshape(bsz * t, LRU_W), nb.reshape(bsz * t, ATT_W), mod_p, p, t, POST_ROW_TILE)
        outs_p.append((jnp.transpose(kt, (0, 3, 1, 2)), jnp.transpose(vt, (0, 3, 1, 2)),
                       jnp.transpose(ikt, (0, 2, 1)), h_last.reshape(bsz, LRU_W), conv_new))

        (na_s, q_s, k_s, v_s, ik_s, iq_s, iw_s, h_s, conv_s) = _sample_stage(
            ys, mod_s, p, rope_s, state_conv[l].reshape(dbsz, (CONV_W - 1) * LRU_W), state_h[l])
        score = _sample_scores(page_table, iq_s.reshape(dbsz, IDX_HEADS, IDX_DIM),
                               iw_s[:, IW_LANE:IW_LANE + IDX_HEADS].reshape(dbsz, IDX_HEADS, 1),
                               jnp.transpose(cache_ik[l], (0, 2, 1)))
        bias, bias_new = _sample_select(score.reshape(dbsz, past_len), iq_s, ik_s, iw_s, tri)
        nb_s = _sample_attention(page_table, q_s.reshape(dbsz, N_HEADS, HEAD_DIM),
                                 k_s.reshape(dbsz, 1, KV_W), v_s.reshape(dbsz, 1, KV_W),
                                 bias.reshape(dbsz, 1, past_len), bias_new.reshape(dbsz, 1, LANES), p["gna8"],
                                 jnp.transpose(cache_k[l], (0, 2, 3, 1)), jnp.transpose(cache_v[l], (0, 2, 3, 1)))
        ys = _post(ys, na_s, nb_s.reshape(dbsz, ATT_W), mod_s, p, 1, dbsz)
        outs_s.append((k_s.reshape(dbsz, dt, N_KV_HEADS, HEAD_DIM), v_s.reshape(dbsz, dt, N_KV_HEADS, HEAD_DIM),
                       ik_s.reshape(dbsz, dt, IDX_DIM), h_s,
                       conv_s.reshape(dbsz, CONV_W - 1, LRU_W)))

    def stack(outs, i):
        return jnp.stack([o[i] for o in outs])

    return (yp.reshape(bsz, t, D_MODEL), ys.reshape(dbsz, dt, D_MODEL),
            stack(outs_p, 0), stack(outs_p, 1), stack(outs_p, 2), stack(outs_p, 3), stack(outs_p, 4),
            stack(outs_s, 0), stack(outs_s, 1), stack(outs_s, 2), stack(outs_s, 3), stack(outs_s, 4))
```

```python
import functools

import numpy as np
import jax
import jax.numpy as jnp
from jax import lax
from jax.experimental import pallas as pl
from jax.experimental.pallas import tpu as pltpu

F32 = jnp.float32
BF16 = jnp.bfloat16
I32 = jnp.int32
HALF_T = jnp.int16

D_MODEL = 1024
LRU_W = 512
LRU_BLOCKS = 8
LRU_BW = LRU_W // LRU_BLOCKS
CONV_W = 4
LRU_C = 8.0
N_HEADS = 8
HEAD_DIM = 64
ATT_W = N_HEADS * HEAD_DIM
N_KV_HEADS = 2
KV_GROUP = N_HEADS // N_KV_HEADS
KV_W = N_KV_HEADS * HEAD_DIM
ROT_DIM = HEAD_DIM // 4
ROPE_THETA = 500000.0
IDX_HEADS = 4
IDX_DIM = 64
IDX_W = IDX_HEADS * IDX_DIM
TOPK_MAX = 256
PAGE_SIZE = 128
D_FF = 2816
EPS = 1e-6

COL_XR = 0
COL_GR = COL_XR + LRU_W
COL_Q = COL_GR + LRU_W
COL_K = COL_Q + ATT_W
COL_V = COL_K + KV_W
COL_IQ = COL_V + KV_W
MAIN_W = COL_IQ
TAIL_W = 128
IDXP_W = IDX_W + TAIL_W
IW_LANE = IDX_DIM
IDX_CAT = 6 * IDX_DIM

LANES = 128
SUBLANES = 8
TRI_W = 256
INT_MIN = -2147483648
HALF_MIN = -32768
LOG2E = 1.4426950408889634
NEG_BIG = -1e30
VMEM_LIMIT = 48 * 1024 * 1024

PROMPT_ROW_TILE = 256
Q_BLOCK = 256
POST_ROW_TILE = 512
FF_CHUNK = D_FF // 2
ADA_COL_TILE = 512
DECODE_GROUP = 4
SCORE_GROUP = 8
SEARCH_KEYS_PER_STEP = 2048


def _dot(a, b):
    return jnp.dot(a, b, preferred_element_type=F32)


def _dot_nt(a, b):
    return lax.dot_general(a, b, (((1,), (1,)), ((), ())), preferred_element_type=F32)


def _split_bf16(x):
    hi = x.astype(BF16)
    lo = (x - hi.astype(F32)).astype(BF16)
    return hi, lo


def _split3(x):
    p1 = x.astype(BF16)
    r1 = x - p1.astype(F32)
    p2 = r1.astype(BF16)
    p3 = (r1 - p2.astype(F32)).astype(BF16)
    return p1, p2, p3


def _dot_f32(a, b, dot=_dot):
    a1, a2, a3 = a
    b1, b2, b3 = b
    return ((dot(a1, b3) + dot(a2, b2) + dot(a3, b1)) + (dot(a1, b2) + dot(a2, b1))) + dot(a1, b1)


def _cat6_lhs(x, axis):
    r1 = x - x.astype(BF16).astype(F32)
    r2 = r1 - r1.astype(BF16).astype(F32)
    return jnp.concatenate([x, x, r1, x, r1, r2], axis=axis).astype(BF16)


def _cat6_rhs(x, axis):
    p1, p2, p3 = _split3(x)
    return jnp.concatenate([p1, p2, p1, p3, p2, p1], axis=axis)


def _rms(x, g):
    return x * lax.rsqrt(jnp.mean(x * x, axis=-1, keepdims=True) + EPS) * g


def _head_rms(x, ones_bd, g):
    hi, lo = _split_bf16(x * x)
    ss = _dot(hi, ones_bd) + _dot(lo, ones_bd)
    return x * lax.rsqrt(ss * (1.0 / HEAD_DIM) + EPS) * g


def _rope(x, cos, sin_next, sin_prev):
    w = x.shape[-1]
    half = ROT_DIM // 2
    return x * cos + pltpu.roll(x, w - half, 1) * sin_next + pltpu.roll(x, half, 1) * sin_prev


def _rope_tables(tab, reps):
    cos = tab[:, 0:LANES]
    sa = tab[:, LANES:2 * LANES]
    sb = tab[:, 2 * LANES:3 * LANES]
    if reps > 1:
        cos = jnp.concatenate([cos] * reps, axis=1)
        sa = jnp.concatenate([sa] * reps, axis=1)
        sb = jnp.concatenate([sb] * reps, axis=1)
    return cos, sa, sb


def _rope_tail(zt, tab):
    cos, sa, sb = _rope_tables(tab, 1)
    lane = lax.broadcasted_iota(I32, zt.shape, 1)
    first = lane < IDX_DIM
    return _rope(zt, jnp.where(first, cos, 1.0), jnp.where(first, sa, 0.0), jnp.where(first, sb, 0.0))


def _gelu_tanh(x):
    c = float(np.sqrt(2.0 / np.pi))
    return x * (0.5 * (1.0 + jnp.tanh(c * (x + 0.044715 * (x * x * x)))))


def _softplus(x):
    return jnp.maximum(x, 0.0) + jnp.log1p(jnp.exp(-jnp.abs(x)))


def _project_in(x, sh1, sc1, ln1, w_main_t, w_idx_t_parts):
    h = _rms(x, ln1) * (1.0 + sc1) + sh1
    hp = _split3(h)
    return _dot_nt(hp[0], w_main_t), _dot_f32(hp, w_idx_t_parts, _dot_nt)


def _lru_gates(xc, wa, ba, wx, bx, lam):
    xcb = xc.astype(BF16)
    r = jax.nn.sigmoid(_dot(xcb, wa) + ba)
    gi = jax.nn.sigmoid(_dot(xcb, wx) + bx)
    log_a = (-LRU_C * _softplus(-lam)) * r
    a = jnp.exp(log_a)
    y = jnp.tanh(-log_a) * (1.0 + a * a)
    inp = jnp.where(y > 0.0, y * lax.rsqrt(y), 0.0) * gi * xc
    return a, inp


def _qkv_inputs(z, tab, qg, kg, ones_q, ones_k):
    cq, saq, sbq = _rope_tables(tab, ATT_W // LANES)
    ck, sak, sbk = _rope_tables(tab, KV_W // LANES)
    q = _rope(_head_rms(z[:, COL_Q:COL_K], ones_q, qg), cq, saq, sbq) * (HEAD_DIM ** -0.5 * LOG2E)
    k = _rope(_head_rms(z[:, COL_K:COL_V], ones_k, kg), ck, sak, sbk)
    return q, k, z[:, COL_V:COL_IQ]


def _index_inputs(zi, tab):
    ci, sai, sbi = _rope_tables(tab, IDX_W // LANES)
    return _rope(zi[:, 0:IDX_W], ci, sai, sbi), _rope_tail(zi[:, IDX_W:IDXP_W], tab)


def _ada_kernel(c_ref, w_ref, b_ref, o_ref):
    c = c_ref[...]
    s = c * jax.nn.sigmoid(c)
    o_ref[...] = _dot_f32(_split3(s), _split3(w_ref[...])) + b_ref[...]


def _ada(c_all, ada_w, ada_b):
    rows = c_all.shape[0]
    n = ada_w.shape[1]
    return pl.pallas_call(
        _ada_kernel,
        grid=(n // ADA_COL_TILE,),
        in_specs=[
            pl.BlockSpec((rows, D_MODEL), lambda j: (0, 0)),
            pl.BlockSpec((D_MODEL, ADA_COL_TILE), lambda j: (0, j)),
            pl.BlockSpec((1, ADA_COL_TILE), lambda j: (0, j)),
        ],
        out_specs=pl.BlockSpec((rows, ADA_COL_TILE), lambda j: (0, j)),
        out_shape=jax.ShapeDtypeStruct((rows, n), F32),
        compiler_params=pltpu.CompilerParams(dimension_semantics=("arbitrary",), vmem_limit_bytes=VMEM_LIMIT),
        name="ada_modulation",
    )(c_all, ada_w, ada_b)


def _prompt_stage_kernel(x_ref, mod_ref, ln1_ref, wmain_ref, widx_ref, convw_ref, convb_ref,
                         wa_ref, ba_ref, wx_ref, bx_ref, lam_ref, qg_ref, kg_ref, gnl_ref,
                         onesq_ref, onesk_ref, rope_ref,
                         na_ref, q_ref, kt_ref, vt_ref, ikt_ref, ktb_ref, vb_ref,
                         ikcat_ref, iqcat_ref, iw_ref, h_ref, conv_ref,
                         xr_buf, hcarry, widx_parts):
    tt = x_ref.shape[1]
    t = pl.program_id(1)

    @pl.when((t == 0) & (pl.program_id(0) == 0))
    def _():
        for j, part in enumerate(_split3(widx_ref[...])):
            widx_parts[j] = part

    @pl.when(t == 0)
    def _():
        xr_buf[0:SUBLANES, :] = jnp.zeros((SUBLANES, LRU_W), F32)
        hcarry[...] = jnp.zeros_like(hcarry)

    mod = mod_ref[0]
    h1, h2, h3 = _split3(_rms(x_ref[0], ln1_ref[...]) * (1.0 + mod[:, D_MODEL:2 * D_MODEL]) + mod[:, 0:D_MODEL])
    z = _dot_nt(h1, wmain_ref[...])
    zi_small = _dot_nt(h1, widx_parts[2]) + _dot_nt(h2, widx_parts[1])

    xr = z[:, COL_XR:COL_GR]
    xr_buf[SUBLANES:SUBLANES + tt, :] = xr
    convw = convw_ref[...]
    xc = convb_ref[...]
    for i in range(CONV_W):
        xc = xc + xr_buf[pl.ds(SUBLANES - (CONV_W - 1) + i, tt), :] * convw[i:i + 1, :]
    conv_ref[0] = xr_buf[pl.ds(SUBLANES + tt - (CONV_W - 1), CONV_W - 1), :]
    xr_buf[0:SUBLANES, :] = xr_buf[tt:tt + SUBLANES, :]

    zi_small = zi_small + _dot_nt(h3, widx_parts[0])
    zi_mid = _dot_nt(h1, widx_parts[1])
    a, b = _lru_gates(xc, wa_ref[...], ba_ref[...], wx_ref[...], bx_ref[...], lam_ref[...])
    zi_mid = zi_mid + _dot_nt(h2, widx_parts[0])

    q, k, v = _qkv_inputs(z, rope_ref[...], qg_ref[...], kg_ref[...], onesq_ref[...], onesk_ref[...])
    for h in range(N_HEADS):
        q_ref[0, h] = q[:, h * HEAD_DIM:(h + 1) * HEAD_DIM].astype(BF16)
    kt = jnp.transpose(k)
    vt = jnp.transpose(v)
    kt_ref[0] = kt.reshape(N_KV_HEADS, HEAD_DIM, tt)
    vt_ref[0] = vt.reshape(N_KV_HEADS, HEAD_DIM, tt)
    ktb_ref[0] = kt.astype(BF16).reshape(N_KV_HEADS, HEAD_DIM, tt)
    lane = lax.broadcasted_iota(I32, v.shape, 1)
    pad = jnp.where(lane == HEAD_DIM, 1.0, 0.0)
    for n in range(N_KV_HEADS):
        vn = v if n == 0 else pltpu.roll(v, KV_W - n * HEAD_DIM, 1)
        vb_ref[0, n] = jnp.where(lane < HEAD_DIM, vn, pad).astype(BF16)

    zi_big = _dot_nt(h1, widx_parts[0])
    row = lax.broadcasted_iota(I32, a.shape, 0) % SUBLANES
    d = 1
    while d < SUBLANES:
        keep = row >= d
        a_s = jnp.where(keep, pltpu.roll(a, d, 0), 1.0)
        b_s = jnp.where(keep, pltpu.roll(b, d, 0), 0.0)
        b = a * b_s + b
        a = a * a_s
        d *= 2
    h_in = hcarry[...]
    groups = []
    for g in range(tt // SUBLANES):
        rows = slice(g * SUBLANES, (g + 1) * SUBLANES)
        hg = a[rows] * h_in + b[rows]
        h_in = hg[SUBLANES - 1:SUBLANES, :]
        groups.append(hg)
    hr = jnp.concatenate(groups, axis=0)
    hcarry[...] = h_in
    h_ref[0] = h_in

    na_ref[0] = _rms(_gelu_tanh(z[:, COL_GR:COL_Q]) * hr, gnl_ref[...]).astype(BF16)

    zi = (zi_small + zi_mid) + zi_big
    iq, tail = _index_inputs(zi, rope_ref[...])
    ikt = jnp.transpose(tail)[0:IDX_DIM, :]
    ikt_ref[0] = ikt
    ikcat_ref[0] = _cat6_rhs(ikt, 0)
    for h in range(IDX_HEADS):
        iqcat_ref[0, h] = _cat6_lhs(iq[:, h * IDX_DIM:(h + 1) * IDX_DIM], 1)
    iw_ref[0] = zi[:, IDX_W:IDXP_W]


def _const_spec(shape):
    nd = len(shape)
    return pl.BlockSpec(shape, lambda *_: (0,) * nd)


def _prompt_stage(x, mod_p, p, rope_tab):
    bsz, t, _ = x.shape
    tt = PROMPT_ROW_TILE

    def rows(w):
        return pl.BlockSpec((1, tt, w), lambda b, i: (b, i, 0))

    in_specs = [
        rows(D_MODEL),
        pl.BlockSpec((1, 1, 6 * D_MODEL), lambda b, i: (b, 0, 0)),
        _const_spec((1, D_MODEL)),
        _const_spec((MAIN_W, D_MODEL)),
        _const_spec((IDXP_W, D_MODEL)),
        _const_spec((CONV_W, LRU_W)),
        _const_spec((1, LRU_W)),
        _const_spec((LRU_W, LRU_W)),
        _const_spec((1, LRU_W)),
        _const_spec((LRU_W, LRU_W)),
        _const_spec((1, LRU_W)),
        _const_spec((1, LRU_W)),
        _const_spec((1, ATT_W)),
        _const_spec((1, KV_W)),
        _const_spec((1, LRU_W)),
        _const_spec((ATT_W, ATT_W)),
        _const_spec((KV_W, KV_W)),
        pl.BlockSpec((tt, 3 * LANES), lambda b, i: (i, 0)),
    ]

    def heads(n, w):
        return pl.BlockSpec((1, n, tt, w), lambda b, i: (b, 0, i, 0))

    def heads_t(n):
        return pl.BlockSpec((1, n, HEAD_DIM, tt), lambda b, i: (b, 0, 0, i))

    def feat_t(w):
        return pl.BlockSpec((1, w, tt), lambda b, i: (b, 0, i))

    out_specs = [
        rows(LRU_W), heads(N_HEADS, HEAD_DIM), heads_t(N_KV_HEADS), heads_t(N_KV_HEADS), feat_t(IDX_DIM),
        heads_t(N_KV_HEADS), heads(N_KV_HEADS, KV_W), feat_t(IDX_CAT), heads(IDX_HEADS, IDX_CAT), rows(TAIL_W),
        pl.BlockSpec((1, 1, LRU_W), lambda b, i: (b, 0, 0)),
        pl.BlockSpec((1, CONV_W - 1, LRU_W), lambda b, i: (b, 0, 0)),
    ]
    out_shape = [
        jax.ShapeDtypeStruct((bsz, t, LRU_W), BF16),
        jax.ShapeDtypeStruct((bsz, N_HEADS, t, HEAD_DIM), BF16),
        jax.ShapeDtypeStruct((bsz, N_KV_HEADS, HEAD_DIM, t), F32),
        jax.ShapeDtypeStruct((bsz, N_KV_HEADS, HEAD_DIM, t), F32),
        jax.ShapeDtypeStruct((bsz, IDX_DIM, t), F32),
        jax.ShapeDtypeStruct((bsz, N_KV_HEADS, HEAD_DIM, t), BF16),
        jax.ShapeDtypeStruct((bsz, N_KV_HEADS, t, KV_W), BF16),
        jax.ShapeDtypeStruct((bsz, IDX_CAT, t), BF16),
        jax.ShapeDtypeStruct((bsz, IDX_HEADS, t, IDX_CAT), BF16),
        jax.ShapeDtypeStruct((bsz, t, TAIL_W), F32),
        jax.ShapeDtypeStruct((bsz, 1, LRU_W), F32),
        jax.ShapeDtypeStruct((bsz, CONV_W - 1, LRU_W), F32),
    ]
    return pl.pallas_call(
        _prompt_stage_kernel,
        grid=(bsz, t // tt),
        in_specs=in_specs,
        out_specs=out_specs,
        out_shape=out_shape,
        scratch_shapes=[pltpu.VMEM((tt + SUBLANES, LRU_W), F32), pltpu.VMEM((1, LRU_W), F32),
                        pltpu.VMEM((3, IDXP_W, D_MODEL), BF16)],
        compiler_params=pltpu.CompilerParams(dimension_semantics=("arbitrary", "arbitrary"),
                                             vmem_limit_bytes=VMEM_LIMIT),
        name="prompt_stage",
    )(x, mod_p, p["ln1"], p["w_main"], p["w_idx"], p["conv_w"], p["conv_b"], p["wa"], p["ba"],
      p["wx"], p["bx"], p["lam"], p["qg"], p["kg"], p["gnl"], p["ones_q"], p["ones_k"], rope_tab)


def _sortable_key(score, fold_zero=True):
    if fold_zero:
        score = jnp.where(score == 0.0, 0.0, score)
    bits = lax.bitcast_convert_type(score, I32)
    return jnp.where(bits < 0, bits ^ jnp.int32(0x7FFFFFFF), bits)


def _count_ge16(ref, c, strict=False):
    c16 = c.astype(HALF_T)
    hit = ref[...] > c16 if strict else ref[...] >= c16
    ones = jnp.where(hit, jnp.ones((), HALF_T), jnp.zeros((), HALF_T))
    acc = ones[:, 0:LANES]
    for j in range(1, ref.shape[1] // LANES):
        acc = acc + ones[:, j * LANES:(j + 1) * LANES]
    return jnp.sum(acc.astype(F32), axis=1, keepdims=True)


def _bitwise_max16(count_ge, k, n_all):
    def body(i, carry):
        t, ct = carry
        c = t + lax.shift_left(jnp.int32(1), jnp.int32(15) - i)
        cn = count_ge(c)
        ok = cn >= k
        return jnp.where(ok, c, t), jnp.where(ok, cn, ct)

    init = (jnp.full(k.shape, HALF_MIN, I32), jnp.full(k.shape, float(n_all), F32))
    return lax.fori_loop(0, 16, body, init, unroll=True)


def _kth_largest(key_ref, hi_ref, lo_ref, k, extra=None):
    key = key_ref[...]
    hi_ref[...] = lax.shift_right_arithmetic(key, 16).astype(HALF_T)
    lo_ref[...] = ((key & 0xFFFF) + HALF_MIN).astype(HALF_T)
    kf = jnp.full((key_ref.shape[0], 1), k, F32)
    n_all = key_ref.shape[1] + (0 if extra is None else 1)
    if extra is not None:
        hi_x = lax.shift_right_arithmetic(extra, 16)
        lo_x = (extra & 0xFFFF) + HALF_MIN

    def count_hi(c):
        cnt = _count_ge16(hi_ref, c)
        return cnt if extra is None else cnt + jnp.where(hi_x >= c, 1.0, 0.0)

    t_hi, n_bucket_up = _bitwise_max16(count_hi, kf, n_all)
    above = _count_ge16(hi_ref, t_hi, strict=True)
    if extra is not None:
        above = above + jnp.where(hi_x > t_hi, 1.0, 0.0)
    k_lo = kf - above
    lo_ref[...] = jnp.where(hi_ref[...] == t_hi.astype(HALF_T), lo_ref[...], jnp.full((), HALF_MIN, HALF_T))
    if extra is not None:
        lo_x = jnp.where(hi_x == t_hi, lo_x, HALF_MIN)

    def count_lo(c):
        cnt = _count_ge16(lo_ref, c)
        return cnt if extra is None else cnt + jnp.where(lo_x >= c, 1.0, 0.0)

    t_lo, n_lo = _bitwise_max16(count_lo, k_lo, n_all)
    n_ge = jnp.where(t_lo > HALF_MIN, above + n_lo, n_bucket_up)
    return t_hi * 65536 + (t_lo - HALF_MIN), n_ge


def _selection_bias(key_ref, hi_ref, lo_ref, bias_ref, tri, k, extra=None):
    thr, n_ge = _kth_largest(key_ref, hi_ref, lo_ref, k, extra)
    has_ties = jnp.max(jnp.where(n_ge > k, 1.0, 0.0)) > 0.0

    @pl.when(jnp.logical_not(has_ties))
    def _():
        bias_ref[...] = jnp.where(key_ref[...] >= thr, 0.0, NEG_BIG)

    @pl.when(has_ties)
    def _():
        n_gt = jnp.sum(jnp.where(key_ref[...] > thr, 1.0, 0.0), axis=1, keepdims=True)
        if extra is not None:
            n_gt = n_gt + jnp.where(extra > thr, 1.0, 0.0)
        need = k - n_gt
        offset = jnp.zeros_like(need)
        for c in range(key_ref.shape[1] // TRI_W):
            kc = key_ref[:, c * TRI_W:(c + 1) * TRI_W]
            eq = kc == thr
            e = jnp.where(eq, 1.0, 0.0)
            incl = _dot(e.astype(BF16), tri)
            rank = incl - e + offset
            tie = jnp.where(rank < need, 0.0, NEG_BIG)
            bias_ref[:, c * TRI_W:(c + 1) * TRI_W] = jnp.where(kc > thr, 0.0, jnp.where(eq, tie, NEG_BIG))
            offset = offset + incl[:, TRI_W - 1:TRI_W]

    if extra is None:
        return None
    n_gt = jnp.sum(jnp.where(key_ref[...] > thr, 1.0, 0.0), axis=1, keepdims=True) + jnp.where(extra > thr, 1.0, 0.0)
    n_eq_main = n_ge - n_gt - jnp.where(extra == thr, 1.0, 0.0)
    tie = jnp.where(n_eq_main < k - n_gt, 0.0, NEG_BIG)
    return jnp.where(extra > thr, 0.0, jnp.where(extra == thr, tie, NEG_BIG))


def _prompt_attn_kernel(q_ref, iq_ref, iw_ref, kt_ref, v_ref, ikt_ref, gn_ref, tri_ref, o_ref,
                        key_ref, hi_ref, lo_ref, bias_ref):
    nb, qb = o_ref.shape[0], o_ref.shape[1]
    n_keys = bias_ref.shape[1]
    past = n_keys - qb
    causal = lax.broadcasted_iota(I32, (qb, qb), 1) <= lax.broadcasted_iota(I32, (qb, qb), 0)
    if n_keys > TOPK_MAX:
        for b in range(nb):
            rows = slice(b * qb, (b + 1) * qb)
            iw = iw_ref[b]
            ikt = ikt_ref[b]
            score = jnp.zeros((qb, n_keys), F32)
            s_next = _dot(iq_ref[b, 0], ikt)
            for h in range(IDX_HEADS):
                s, s_next = s_next, (_dot(iq_ref[b, h + 1], ikt) if h + 1 < IDX_HEADS else None)
                score = score + iw[:, IW_LANE + h:IW_LANE + h + 1] * jnp.maximum(s, 0.0)
            key = _sortable_key(score, fold_zero=False)
            key_ref[rows, 0:past] = key[:, 0:past]
            key_ref[rows, past:n_keys] = jnp.where(causal, key[:, past:n_keys], INT_MIN)
        _selection_bias(key_ref, hi_ref, lo_ref, bias_ref, tri_ref[...], float(TOPK_MAX))
        for b in range(nb):
            rows = slice(b * qb, (b + 1) * qb)
            bias_ref[rows, past:n_keys] = jnp.where(causal, bias_ref[rows, past:n_keys], NEG_BIG)
    else:
        for b in range(nb):
            bias_ref[b * qb:(b + 1) * qb, :] = jnp.where(causal, 0.0, NEG_BIG)

    def qk(i):
        b, h = divmod(i, N_HEADS)
        return _dot(q_ref[b, h], kt_ref[b, h // KV_GROUP]) + bias_ref[b * qb:(b + 1) * qb, :]

    logits_next = qk(0)
    for b in range(nb):
        outs = []
        for h in range(N_HEADS):
            i = b * N_HEADS + h
            logits, logits_next = logits_next, (qk(i + 1) if i + 1 < nb * N_HEADS else None)
            m = jnp.max(logits, axis=1, keepdims=True)
            p = jnp.exp2((logits - m).astype(BF16))
            pv = _dot(p, v_ref[b, h // KV_GROUP])
            outs.append(pv[:, 0:HEAD_DIM] / pv[:, HEAD_DIM:HEAD_DIM + 1])
        o_ref[b] = _rms(jnp.concatenate(outs, axis=1), gn_ref[...]).astype(BF16)


def _prompt_attention_block(qh, iqcat, iw, ktb, vb, ikcat, gn_att, tri, q_block):
    bsz, _, t, _ = qh.shape
    qb = Q_BLOCK
    n_keys = (q_block + 1) * qb
    nb = max(1, min(bsz, SEARCH_KEYS_PER_STEP // n_keys))
    while bsz % nb:
        nb -= 1
    rows = nb * qb

    def qheads(n, w):
        return pl.BlockSpec((nb, n, qb, w), lambda b: (b, 0, q_block, 0))

    return pl.pallas_call(
        _prompt_attn_kernel,
        grid=(bsz // nb,),
        in_specs=[qheads(N_HEADS, HEAD_DIM), qheads(IDX_HEADS, IDX_CAT),
                  pl.BlockSpec((nb, qb, TAIL_W), lambda b: (b, q_block, 0)),
                  pl.BlockSpec((nb, N_KV_HEADS, HEAD_DIM, n_keys), lambda b: (b, 0, 0, 0)),
                  pl.BlockSpec((nb, N_KV_HEADS, n_keys, KV_W), lambda b: (b, 0, 0, 0)),
                  pl.BlockSpec((nb, IDX_CAT, n_keys), lambda b: (b, 0, 0)),
                  _const_spec((1, ATT_W)), _const_spec((TRI_W, TRI_W))],
        out_specs=pl.BlockSpec((nb, qb, ATT_W), lambda b: (b, 0, 0)),
        out_shape=jax.ShapeDtypeStruct((bsz, qb, ATT_W), BF16),
        scratch_shapes=[pltpu.VMEM((rows, n_keys), I32), pltpu.VMEM((rows, n_keys), HALF_T),
                        pltpu.VMEM((rows, n_keys), HALF_T), pltpu.VMEM((rows, n_keys), F32)],
        compiler_params=pltpu.CompilerParams(dimension_semantics=("arbitrary",), vmem_limit_bytes=VMEM_LIMIT),
        name=f"prompt_attention_{q_block}",
    )(qh, iqcat, iw, ktb, vb, ikcat, gn_att, tri)


def _prompt_attention(qh, iqcat, iw, ktb, vb, ikcat, gn_att, tri):
    t = qh.shape[2]
    blocks = [_prompt_attention_block(qh, iqcat, iw, ktb, vb, ikcat, gn_att, tri, i) for i in range(t // Q_BLOCK)]
    return jnp.concatenate(blocks, axis=1)


def _post_kernel(x_ref, na_ref, nb_ref, mod_ref, woa_ref, wob_ref, ln2_ref, wg_ref, wu_ref, wd_ref,
                 y_ref, *, shared_mod):
    mod = mod_ref[0] if shared_mod else mod_ref[...]
    g1 = mod[:, 2 * D_MODEL:3 * D_MODEL]
    sh2 = mod[:, 3 * D_MODEL:4 * D_MODEL]
    sc2 = mod[:, 4 * D_MODEL:5 * D_MODEL]
    g2 = mod[:, 5 * D_MODEL:6 * D_MODEL]
    x = x_ref[...]
    mix = _dot(na_ref[...], woa_ref[...]) + _dot(nb_ref[...], wob_ref[...])
    x1 = x + g1 * mix
    h2 = (_rms(x1, ln2_ref[...]) * (1.0 + sc2) + sh2).astype(BF16)
    def gate_up(c):
        return _dot(h2, wg_ref[:, c * FF_CHUNK:(c + 1) * FF_CHUNK]), _dot(h2, wu_ref[:, c * FF_CHUNK:(c + 1) * FF_CHUNK])

    n_chunks = D_FF // FF_CHUNK
    ff = jnp.zeros_like(x1)
    gu_next = gate_up(0)
    for c in range(n_chunks):
        (g, u), gu_next = gu_next, (gate_up(c + 1) if c + 1 < n_chunks else None)
        act = (g * jax.nn.sigmoid(g) * u).astype(BF16)
        ff = ff + _dot(act, wd_ref[c * FF_CHUNK:(c + 1) * FF_CHUNK, :])
    y_ref[...] = x1 + g2 * ff


def _post(x, na, nb, mod, p, rows_per_mod, tile):
    n = x.shape[0]
    shared = rows_per_mod > 1
    if shared:
        mod_spec = pl.BlockSpec((1, 1, 6 * D_MODEL), lambda i: (i * tile // rows_per_mod, 0, 0))
    else:
        mod_spec = pl.BlockSpec((tile, 6 * D_MODEL), lambda i: (i, 0))

    def rows(w):
        return pl.BlockSpec((tile, w), lambda i: (i, 0))

    def weight(shape):
        return pl.BlockSpec(shape, lambda i: (0, 0), pipeline_mode=pl.Buffered(1))

    return pl.pallas_call(
        functools.partial(_post_kernel, shared_mod=shared),
        grid=(n // tile,),
        in_specs=[rows(D_MODEL), rows(LRU_W), rows(ATT_W), mod_spec,
                  weight((LRU_W, D_MODEL)), weight((ATT_W, D_MODEL)), weight((1, D_MODEL)),
                  weight((D_MODEL, D_FF)), weight((D_MODEL, D_FF)), weight((D_FF, D_MODEL))],
        out_specs=rows(D_MODEL),
        out_shape=jax.ShapeDtypeStruct((n, D_MODEL), F32),
        compiler_params=pltpu.CompilerParams(dimension_semantics=("arbitrary",), vmem_limit_bytes=VMEM_LIMIT),
        name="post_ffn",
    )(x, na, nb, mod, p["wo_a"], p["wo_b"], p["ln2"], p["wg"], p["wu"], p["wd"])


def _sample_stage_kernel(x_ref, mod_ref, ln1_ref, wmain_ref, widx_ref, convw_ref, convb_ref,
                         wa_ref, ba_ref, wx_ref, bx_ref, lam_ref, qg_ref, kg_ref, gnl_ref,
                         onesq_ref, onesk_ref, rope_ref, sconv_ref, sh_ref,
                         na_ref, q_ref, k_ref, v_ref, ik_ref, iq_ref, iw_ref, h_ref, conv_ref):
    mod = mod_ref[...]
    z, zi = _project_in(x_ref[...], mod[:, 0:D_MODEL], mod[:, D_MODEL:2 * D_MODEL], ln1_ref[...],
                        wmain_ref[...], _split3(widx_ref[...]))
    xr = z[:, COL_XR:COL_GR]
    convw = convw_ref[...]
    sconv = sconv_ref[...]
    xc = convb_ref[...]
    for i in range(CONV_W - 1):
        xc = xc + sconv[:, i * LRU_W:(i + 1) * LRU_W] * convw[i:i + 1, :]
    xc = xc + xr * convw[CONV_W - 1:CONV_W, :]
    conv_ref[:, 0:(CONV_W - 2) * LRU_W] = sconv[:, LRU_W:(CONV_W - 1) * LRU_W]
    conv_ref[:, (CONV_W - 2) * LRU_W:(CONV_W - 1) * LRU_W] = xr

    a, b = _lru_gates(xc, wa_ref[...], ba_ref[...], wx_ref[...], bx_ref[...], lam_ref[...])
    hr = a * sh_ref[...] + b
    h_ref[...] = hr
    na_ref[...] = _rms(_gelu_tanh(z[:, COL_GR:COL_Q]) * hr, gnl_ref[...]).astype(BF16)

    q, k, v = _qkv_inputs(z, rope_ref[...], qg_ref[...], kg_ref[...], onesq_ref[...], onesk_ref[...])
    iq, tail = _index_inputs(zi, rope_ref[...])
    q_ref[...] = q
    k_ref[...] = k
    v_ref[...] = v
    ik_ref[...] = tail[:, 0:IDX_DIM]
    iq_ref[...] = iq
    iw_ref[...] = zi[:, IDX_W:IDXP_W]


def _sample_stage(x, mod_s, p, rope_row, sconv, sh):
    n = x.shape[0]

    def out(w, dt=F32):
        return jax.ShapeDtypeStruct((n, w), dt)

    return pl.pallas_call(
        _sample_stage_kernel,
        out_shape=[out(LRU_W, BF16), out(ATT_W), out(KV_W), out(KV_W), out(IDX_DIM), out(IDX_W),
                   out(TAIL_W), out(LRU_W), out((CONV_W - 1) * LRU_W)],
        compiler_params=pltpu.CompilerParams(vmem_limit_bytes=VMEM_LIMIT),
        name="sample_stage",
    )(x, mod_s, p["ln1"], p["w_main"], p["w_idx"], p["conv_w"], p["conv_b"], p["wa"], p["ba"],
      p["wx"], p["bx"], p["lam"], p["qg"], p["kg"], p["gnl"], p["ones_q"], p["ones_k"], rope_row, sconv, sh)


def _page_copy(cache_ref, buf_ref, sem, page, slot, j, pg):
    return pltpu.make_async_copy(cache_ref.at[page], buf_ref.at[slot, j, pg], sem)


def _paged_prefetch(pt_ref, caches, bufs, sems, group, n_pages):
    s = pl.program_id(0)
    slot = s % 2

    def start(grp, half):
        for cache_ref, buf_ref, sem in zip(caches, bufs, sems):
            for j in range(group):
                for pg in range(n_pages):
                    page = pt_ref[grp * group + j, pg]
                    _page_copy(cache_ref, buf_ref, sem.at[half], page, half, j, pg).start()

    @pl.when(s == 0)
    def _():
        start(0, 0)

    @pl.when(s + 1 < pl.num_programs(0))
    def _():
        start(s + 1, 1 - slot)

    for cache_ref, buf_ref, sem in zip(caches, bufs, sems):
        for j in range(group):
            for pg in range(n_pages):
                _page_copy(cache_ref, buf_ref, sem.at[slot], 0, slot, j, pg).wait()
    return slot


def _sample_score_kernel(pt_ref, iq_ref, iw_ref, ik_hbm, o_ref, ik_buf, sem, *, n_pages):
    group = iq_ref.shape[0]
    slot = _paged_prefetch(pt_ref, (ik_hbm,), (ik_buf,), (sem,), group, n_pages)
    iqcat = [_cat6_lhs(iq_ref[j], 1) for j in range(group)]
    for pg in range(n_pages):
        for j in range(group):
            s = _dot(iqcat[j], _cat6_rhs(ik_buf[slot, j, pg], 0))
            o_ref[j, :, pg * PAGE_SIZE:(pg + 1) * PAGE_SIZE] = jnp.sum(iw_ref[j] * jnp.maximum(s, 0.0), axis=0,
                                                                        keepdims=True)


def _sample_scores(page_table, iq3, iw3, cache_ik):
    n, n_pages = page_table.shape
    g = SCORE_GROUP
    grid_spec = pltpu.PrefetchScalarGridSpec(
        num_scalar_prefetch=1,
        grid=(n // g,),
        in_specs=[pl.BlockSpec((g, IDX_HEADS, IDX_DIM), lambda s, pt: (s, 0, 0)),
                  pl.BlockSpec((g, IDX_HEADS, 1), lambda s, pt: (s, 0, 0)),
                  pl.BlockSpec(memory_space=pl.ANY)],
        out_specs=pl.BlockSpec((g, 1, n_pages * PAGE_SIZE), lambda s, pt: (s, 0, 0)),
        scratch_shapes=[pltpu.VMEM((2, g, n_pages, IDX_DIM, PAGE_SIZE), F32), pltpu.SemaphoreType.DMA((2,))],
    )
    return pl.pallas_call(
        functools.partial(_sample_score_kernel, n_pages=n_pages),
        grid_spec=grid_spec,
        out_shape=jax.ShapeDtypeStruct((n, 1, n_pages * PAGE_SIZE), F32),
        compiler_params=pltpu.CompilerParams(dimension_semantics=("arbitrary",), vmem_limit_bytes=VMEM_LIMIT),
        name="sample_scores",
    )(page_table, iq3, iw3, cache_ik)


def _sample_select_kernel(score_ref, iq_ref, ik_ref, iw_ref, tri_ref, bias_ref, bias_new_ref,
                          key_ref, hi_ref, lo_ref):
    iq = iq_ref[...]
    ik = ik_ref[...]
    iw = iw_ref[...]
    new = jnp.zeros((iq.shape[0], 1), F32)
    for h in range(IDX_HEADS):
        s = jnp.sum(iq[:, h * IDX_DIM:(h + 1) * IDX_DIM] * ik, axis=1, keepdims=True)
        new = new + iw[:, IW_LANE + h:IW_LANE + h + 1] * jnp.maximum(s, 0.0)
    key_ref[...] = _sortable_key(score_ref[...])
    b_new = _selection_bias(key_ref, hi_ref, lo_ref, bias_ref, tri_ref[...], float(TOPK_MAX),
                            extra=_sortable_key(new))
    bias_new_ref[...] = jnp.broadcast_to(b_new, bias_new_ref.shape)


def _sample_select(score, iq, ik, iw, tri):
    n, n_keys = score.shape
    return pl.pallas_call(
        _sample_select_kernel,
        out_shape=[jax.ShapeDtypeStruct((n, n_keys), F32), jax.ShapeDtypeStruct((n, LANES), F32)],
        scratch_shapes=[pltpu.VMEM((n, n_keys), I32), pltpu.VMEM((n, n_keys), HALF_T), pltpu.VMEM((n, n_keys), HALF_T)],
        compiler_params=pltpu.CompilerParams(vmem_limit_bytes=VMEM_LIMIT),
        name="sample_select",
    )(score, iq, ik, iw, tri)


def _sample_attn_kernel(pt_ref, q_ref, kn_ref, vn_ref, bias_ref, bnew_ref, gn_ref, k_hbm, v_hbm, o_ref,
                        k_buf, v_buf, k_sem, v_sem, *, n_pages):
    group = q_ref.shape[0]
    seqs = range(group)
    slot = _paged_prefetch(pt_ref, (k_hbm, v_hbm), (k_buf, v_buf), (k_sem, v_sem), group, n_pages)
    row = lax.broadcasted_iota(I32, (N_HEADS, HEAD_DIM), 0)
    first = row < KV_GROUP
    q = [q_ref[j] for j in seqs]
    q2 = [jnp.concatenate([jnp.where(first, q[j], 0.0), jnp.where(first, 0.0, q[j])], axis=1).astype(BF16)
          for j in seqs]
    logits = [[None] * n_pages for _ in seqs]
    for pg in range(n_pages):
        for j in seqs:
            kp = k_buf[slot, j, pg].astype(BF16).reshape(KV_W, PAGE_SIZE)
            logits[j][pg] = _dot(q2[j], kp) + bias_ref[j][:, pg * PAGE_SIZE:(pg + 1) * PAGE_SIZE]
    m, p_new, den, acc, v_sel = [], [], [], [], []
    for j in seqs:
        kn = kn_ref[j]
        vn = vn_ref[j]
        k_sel = jnp.where(first, kn[:, 0:HEAD_DIM], kn[:, HEAD_DIM:KV_W])
        v_sel.append(jnp.where(first, vn[:, 0:HEAD_DIM], vn[:, HEAD_DIM:KV_W]))
        l_new = jnp.sum(q[j] * k_sel, axis=1, keepdims=True) + bnew_ref[j][:, 0:1]
        mj = l_new
        for l in logits[j]:
            mj = jnp.maximum(mj, jnp.max(l, axis=1, keepdims=True))
        m.append(mj)
        p_new.append(jnp.exp2(l_new - mj))
        den.append(p_new[j])
        acc.append(jnp.zeros((N_HEADS, KV_W), F32))
    for pg in range(n_pages):
        for j in seqs:
            p = jnp.exp2(logits[j][pg] - m[j])
            den[j] = den[j] + jnp.sum(p, axis=1, keepdims=True)
            vp = v_buf[slot, j, pg].astype(BF16).reshape(KV_W, PAGE_SIZE)
            acc[j] = acc[j] + _dot_nt(p.astype(BF16), vp)
    for j in seqs:
        out = (p_new[j] * v_sel[j] + jnp.where(first, acc[j][:, 0:HEAD_DIM], acc[j][:, HEAD_DIM:KV_W])) / den[j]
        ms = jnp.sum(jnp.sum(out * out, axis=1, keepdims=True), axis=0, keepdims=True) * (1.0 / ATT_W)
        o_ref[j] = (out * lax.rsqrt(ms + EPS) * gn_ref[...]).astype(BF16)


def _sample_attention(page_table, q3, k_new, v_new, bias, bias_new, gn8, cache_k, cache_v):
    n, n_pages = page_table.shape
    n_keys = n_pages * PAGE_SIZE

    g = DECODE_GROUP

    def per_seq(shape):
        return pl.BlockSpec((g,) + shape, lambda s, pt: (s, 0, 0))

    page_buf = pltpu.VMEM((2, g, n_pages, N_KV_HEADS, HEAD_DIM, PAGE_SIZE), F32)
    grid_spec = pltpu.PrefetchScalarGridSpec(
        num_scalar_prefetch=1,
        grid=(n // g,),
        in_specs=[per_seq((N_HEADS, HEAD_DIM)), per_seq((1, KV_W)), per_seq((1, KV_W)),
                  per_seq((1, n_keys)), per_seq((1, LANES)),
                  pl.BlockSpec((N_HEADS, HEAD_DIM), lambda s, pt: (0, 0)),
                  pl.BlockSpec(memory_space=pl.ANY), pl.BlockSpec(memory_space=pl.ANY)],
        out_specs=per_seq((N_HEADS, HEAD_DIM)),
        scratch_shapes=[page_buf, page_buf, pltpu.SemaphoreType.DMA((2,)), pltpu.SemaphoreType.DMA((2,))],
    )
    return pl.pallas_call(
        functools.partial(_sample_attn_kernel, n_pages=n_pages),
        grid_spec=grid_spec,
        out_shape=jax.ShapeDtypeStruct((n, N_HEADS, HEAD_DIM), BF16),
        compiler_params=pltpu.CompilerParams(dimension_semantics=("arbitrary",), vmem_limit_bytes=VMEM_LIMIT),
        name="sample_attention",
    )(page_table, q3, k_new, v_new, bias, bias_new, gn8, cache_k, cache_v)


def _rope_table_np(positions):
    half = ROT_DIM // 2
    freq = ROPE_THETA ** (-(np.arange(half, dtype=np.float64) / half))
    ang = np.asarray(positions, np.float64)[:, None] * freq[None, :]
    cos, sin = np.cos(ang), np.sin(ang)
    n = len(positions)
    tab = np.zeros((n, 3, LANES), np.float64)
    tab[:, 0, :] = 1.0
    for base in range(0, LANES, HEAD_DIM):
        tab[:, 0, base:base + half] = cos
        tab[:, 0, base + half:base + ROT_DIM] = cos
        tab[:, 1, base:base + half] = -sin
        tab[:, 2, base + half:base + ROT_DIM] = sin
    return tab.reshape(n, 3 * LANES).astype(np.float32)


def _block_diag(w):
    n, a, b = w.shape
    return jnp.einsum("nij,nm->nimj", w, jnp.eye(n, dtype=w.dtype)).reshape(n * a, n * b)


def _layer_params(l, ln1_g, w_in, conv_w, conv_b, lru_wa, lru_ba, lru_wx, lru_bx, lru_lambda, q_norm_g,
                  k_norm_g, gn_lru_g, gn_att_g, w_out, ln2_g, w_gate, w_up, w_down):
    w_in_t = jnp.transpose(w_in[l])
    w_idx = jnp.pad(w_in_t[COL_IQ:], ((0, IDXP_W - (w_in.shape[2] - COL_IQ)), (0, 0)))
    head_id = np.arange(ATT_W) // HEAD_DIM
    ones_q = (head_id[:, None] == head_id[None, :]).astype(np.float32)
    return {
        "ln1": ln1_g[l][None, :],
        "w_main": w_in_t[:COL_IQ].astype(BF16),
        "w_idx": w_idx,
        "conv_w": conv_w[l],
        "conv_b": conv_b[l][None, :],
        "wa": _block_diag(lru_wa[l]).astype(BF16),
        "ba": lru_ba[l][None, :],
        "wx": _block_diag(lru_wx[l]).astype(BF16),
        "bx": lru_bx[l][None, :],
        "lam": lru_lambda[l][None, :],
        "qg": jnp.tile(q_norm_g[l], N_HEADS)[None, :],
        "kg": jnp.tile(k_norm_g[l], N_KV_HEADS)[None, :],
        "gnl": gn_lru_g[l][None, :],
        "gna": gn_att_g[l][None, :],
        "gna8": gn_att_g[l].reshape(N_HEADS, HEAD_DIM),
        "ones_q": jnp.asarray(ones_q, BF16),
        "ones_k": jnp.asarray(ones_q[:KV_W, :KV_W], BF16),
        "wo_a": w_out[l][:LRU_W].astype(BF16),
        "wo_b": w_out[l][LRU_W:].astype(BF16),
        "ln2": ln2_g[l][None, :],
        "wg": w_gate[l].astype(BF16),
        "wu": w_up[l].astype(BF16),
        "wd": w_down[l].astype(BF16),
    }


def kernel(x_prompt, x_sample, cache_k, cache_v, cache_ik, state_h, state_conv, page_table, c_prompt, c_sample, ada_w, ada_b, ln1_g, w_in, conv_w, conv_b, lru_wa, lru_ba, lru_wx, lru_bx, lru_lambda, q_norm_g, k_norm_g, gn_lru_g, gn_att_g, w_out, ln2_g, w_gate, w_up, w_down):
    bsz, t, _ = x_prompt.shape
    dbsz, dt, _ = x_sample.shape
    depth = ada_w.shape[0]
    n_pages = page_table.shape[1]
    past_len = n_pages * PAGE_SIZE
    assert dt == 1 and t % PROMPT_ROW_TILE == 0 and t % POST_ROW_TILE == 0 and t % Q_BLOCK == 0
    assert t // 4 >= TOPK_MAX and (past_len + dt) // 4 >= TOPK_MAX
    assert dbsz % DECODE_GROUP == 0 and dbsz % SCORE_GROUP == 0 and cache_k.shape[2] == PAGE_SIZE

    rope_p = jnp.asarray(_rope_table_np(np.arange(t)))
    rope_s = jnp.asarray(_rope_table_np(past_len + np.arange(dt)))
    tri = jnp.asarray(np.triu(np.ones((TRI_W, TRI_W), np.float32)), BF16)

    yp = x_prompt.reshape(bsz * t, D_MODEL)
    ys = x_sample.reshape(dbsz, D_MODEL)
    c_all = jnp.concatenate([c_prompt, c_sample], axis=0)
    outs_p, outs_s = [], []
    for l in range(depth):
        p = _layer_params(l, ln1_g, w_in, conv_w, conv_b, lru_wa, lru_ba, lru_wx, lru_bx, lru_lambda,
                          q_norm_g, k_norm_g, gn_lru_g, gn_att_g, w_out, ln2_g, w_gate, w_up, w_down)
        mod = _ada(c_all, ada_w[l], ada_b[l][None, :])
        mod_p = mod[:bsz].reshape(bsz, 1, 6 * D_MODEL)
        mod_s = mod[bsz:]

        (na, qh, kt, vt, ikt, ktb, vb, ikcat, iqcat, iw, h_last, conv_new) = _prompt_stage(
            yp.reshape(bsz, t, D_MODEL), mod_p, p, rope_p)
        nb = _prompt_attention(qh, iqcat, iw, ktb, vb, ikcat, p["gna"], tri)
        yp = _post(yp, na.reshape(bsz * t, LRU_W), nb.reshape(bsz * t, ATT_W), mod_p, p, t, POST_ROW_TILE)
        outs_p.append((jnp.transpose(kt, (0, 3, 1, 2)), jnp.transpose(vt, (0, 3, 1, 2)),
                       jnp.transpose(ikt, (0, 2, 1)), h_last.reshape(bsz, LRU_W), conv_new))

        (na_s, q_s, k_s, v_s, ik_s, iq_s, iw_s, h_s, conv_s) = _sample_stage(
            ys, mod_s, p, rope_s, state_conv[l].reshape(dbsz, (CONV_W - 1) * LRU_W), state_h[l])
        score = _sample_scores(page_table, iq_s.reshape(dbsz, IDX_HEADS, IDX_DIM),
                               iw_s[:, IW_LANE:IW_LANE + IDX_HEADS].reshape(dbsz, IDX_HEADS, 1),
                               jnp.transpose(cache_ik[l], (0, 2, 1)))
        bias, bias_new = _sample_select(score.reshape(dbsz, past_len), iq_s, ik_s, iw_s, tri)
        nb_s = _sample_attention(page_table, q_s.reshape(dbsz, N_HEADS, HEAD_DIM),
                                 k_s.reshape(dbsz, 1, KV_W), v_s.reshape(dbsz, 1, KV_W),
                                 bias.reshape(dbsz, 1, past_len), bias_new.reshape(dbsz, 1, LANES), p["gna8"],
                                 jnp.transpose(cache_k[l], (0, 2, 3, 1)), jnp.transpose(cache_v[l], (0, 2, 3, 1)))
        ys = _post(ys, na_s, nb_s.reshape(dbsz, ATT_W), mod_s, p, 1, dbsz)
        outs_s.append((k_s.reshape(dbsz, dt, N_KV_HEADS, HEAD_DIM), v_s.reshape(dbsz, dt, N_KV_HEADS, HEAD_DIM),
                       ik_s.reshape(dbsz, dt, IDX_DIM), h_s,
                       conv_s.reshape(dbsz, CONV_W - 1, LRU_W)))

    def stack(outs, i):
        return jnp.stack([o[i] for o in outs])

    return (yp.reshape(bsz, t, D_MODEL), ys.reshape(dbsz, dt, D_MODEL),
            stack(outs_p, 0), stack(outs_p, 1), stack(outs_p, 2), stack(outs_p, 3), stack(outs_p, 4),
            stack(outs_s, 0), stack(outs_s, 1), stack(outs_s, 2), stack(outs_s, 3), stack(outs_s, 4))
```

```python
import functools

import numpy as np
import jax
import jax.numpy as jnp
from jax import lax
from jax.experimental import pallas as pl
from jax.experimental.pallas import tpu as pltpu

F32 = jnp.float32
BF16 = jnp.bfloat16
I32 = jnp.int32
HALF_T = jnp.int16

D_MODEL = 1024
LRU_W = 512
LRU_BLOCKS = 8
LRU_BW = LRU_W // LRU_BLOCKS
CONV_W = 4
LRU_C = 8.0
N_HEADS = 8
HEAD_DIM = 64
ATT_W = N_HEADS * HEAD_DIM
N_KV_HEADS = 2
KV_GROUP = N_HEADS // N_KV_HEADS
KV_W = N_KV_HEADS * HEAD_DIM
ROT_DIM = HEAD_DIM // 4
ROPE_THETA = 500000.0
IDX_HEADS = 4
IDX_DIM = 64
IDX_W = IDX_HEADS * IDX_DIM
TOPK_MAX = 256
PAGE_SIZE = 128
D_FF = 2816
EPS = 1e-6

COL_XR = 0
COL_GR = COL_XR + LRU_W
COL_Q = COL_GR + LRU_W
COL_K = COL_Q + ATT_W
COL_V = COL_K + KV_W
COL_IQ = COL_V + KV_W
MAIN_W = COL_IQ
TAIL_W = 128
IDXP_W = IDX_W + TAIL_W
IW_LANE = IDX_DIM
IDX_CAT = 6 * IDX_DIM

LANES = 128
SUBLANES = 8
TRI_W = 256
INT_MIN = -2147483648
HALF_MIN = -32768
LOG2E = 1.4426950408889634
NEG_BIG = -1e30
VMEM_LIMIT = 48 * 1024 * 1024

PROMPT_ROW_TILE = 256
Q_BLOCK = 256
POST_ROW_TILE = 512
FF_CHUNK = D_FF // 2
ADA_COL_TILE = 512
DECODE_GROUP = 8
SCORE_GROUP = 8
SEARCH_KEYS_PER_STEP = 2048


def _dot(a, b):
    return jnp.dot(a, b, preferred_element_type=F32)


def _dot_nt(a, b):
    return lax.dot_general(a, b, (((1,), (1,)), ((), ())), preferred_element_type=F32)


def _split_bf16(x):
    hi = x.astype(BF16)
    lo = (x - hi.astype(F32)).astype(BF16)
    return hi, lo


def _split3(x):
    p1 = x.astype(BF16)
    r1 = x - p1.astype(F32)
    p2 = r1.astype(BF16)
    p3 = (r1 - p2.astype(F32)).astype(BF16)
    return p1, p2, p3


def _dot_f32(a, b, dot=_dot):
    a1, a2, a3 = a
    b1, b2, b3 = b
    return ((dot(a1, b3) + dot(a2, b2) + dot(a3, b1)) + (dot(a1, b2) + dot(a2, b1))) + dot(a1, b1)


def _cat6_lhs(x, axis):
    r1 = x - x.astype(BF16).astype(F32)
    r2 = r1 - r1.astype(BF16).astype(F32)
    return jnp.concatenate([x, x, r1, x, r1, r2], axis=axis).astype(BF16)


def _cat6_rhs(x, axis):
    p1, p2, p3 = _split3(x)
    return jnp.concatenate([p1, p2, p1, p3, p2, p1], axis=axis)


def _rms(x, g):
    return x * lax.rsqrt(jnp.mean(x * x, axis=-1, keepdims=True) + EPS) * g


def _head_rms(x, ones_bd, g):
    hi, lo = _split_bf16(x * x)
    ss = _dot(hi, ones_bd) + _dot(lo, ones_bd)
    return x * lax.rsqrt(ss * (1.0 / HEAD_DIM) + EPS) * g


def _rope(x, cos, sin_next, sin_prev):
    w = x.shape[-1]
    half = ROT_DIM // 2
    return x * cos + pltpu.roll(x, w - half, 1) * sin_next + pltpu.roll(x, half, 1) * sin_prev


def _rope_tables(tab, reps):
    cos = tab[:, 0:LANES]
    sa = tab[:, LANES:2 * LANES]
    sb = tab[:, 2 * LANES:3 * LANES]
    if reps > 1:
        cos = jnp.concatenate([cos] * reps, axis=1)
        sa = jnp.concatenate([sa] * reps, axis=1)
        sb = jnp.concatenate([sb] * reps, axis=1)
    return cos, sa, sb


def _rope_tail(zt, tab):
    cos, sa, sb = _rope_tables(tab, 1)
    lane = lax.broadcasted_iota(I32, zt.shape, 1)
    first = lane < IDX_DIM
    return _rope(zt, jnp.where(first, cos, 1.0), jnp.where(first, sa, 0.0), jnp.where(first, sb, 0.0))


def _gelu_tanh(x):
    c = float(np.sqrt(2.0 / np.pi))
    return x * (0.5 * (1.0 + jnp.tanh(c * (x + 0.044715 * (x * x * x)))))


def _softplus(x):
    return jnp.maximum(x, 0.0) + jnp.log1p(jnp.exp(-jnp.abs(x)))


def _project_in(x, sh1, sc1, ln1, w_main_t, w_idx_t_parts):
    h = _rms(x, ln1) * (1.0 + sc1) + sh1
    hp = _split3(h)
    return _dot_nt(hp[0], w_main_t), _dot_f32(hp, w_idx_t_parts, _dot_nt)


def _lru_gates(xc, wa, ba, wx, bx, lam):
    xcb = xc.astype(BF16)
    r = jax.nn.sigmoid(_dot(xcb, wa) + ba)
    gi = jax.nn.sigmoid(_dot(xcb, wx) + bx)
    log_a = (-LRU_C * _softplus(-lam)) * r
    a = jnp.exp(log_a)
    y = jnp.tanh(-log_a) * (1.0 + a * a)
    inp = jnp.where(y > 0.0, y * lax.rsqrt(y), 0.0) * gi * xc
    return a, inp


def _qkv_inputs(z, tab, qg, kg, ones_q, ones_k):
    cq, saq, sbq = _rope_tables(tab, ATT_W // LANES)
    ck, sak, sbk = _rope_tables(tab, KV_W // LANES)
    q = _rope(_head_rms(z[:, COL_Q:COL_K], ones_q, qg), cq, saq, sbq) * (HEAD_DIM ** -0.5 * LOG2E)
    k = _rope(_head_rms(z[:, COL_K:COL_V], ones_k, kg), ck, sak, sbk)
    return q, k, z[:, COL_V:COL_IQ]


def _index_inputs(zi, tab):
    ci, sai, sbi = _rope_tables(tab, IDX_W // LANES)
    return _rope(zi[:, 0:IDX_W], ci, sai, sbi), _rope_tail(zi[:, IDX_W:IDXP_W], tab)


def _ada_kernel(c_ref, w_ref, b_ref, o_ref):
    c = c_ref[...]
    s = c * jax.nn.sigmoid(c)
    o_ref[...] = _dot_f32(_split3(s), _split3(w_ref[...])) + b_ref[...]


def _ada(c_all, ada_w, ada_b):
    rows = c_all.shape[0]
    n = ada_w.shape[1]
    return pl.pallas_call(
        _ada_kernel,
        grid=(n // ADA_COL_TILE,),
        in_specs=[
            pl.BlockSpec((rows, D_MODEL), lambda j: (0, 0)),
            pl.BlockSpec((D_MODEL, ADA_COL_TILE), lambda j: (0, j)),
            pl.BlockSpec((1, ADA_COL_TILE), lambda j: (0, j)),
        ],
        out_specs=pl.BlockSpec((rows, ADA_COL_TILE), lambda j: (0, j)),
        out_shape=jax.ShapeDtypeStruct((rows, n), F32),
        compiler_params=pltpu.CompilerParams(dimension_semantics=("arbitrary",), vmem_limit_bytes=VMEM_LIMIT),
        name="ada_modulation",
    )(c_all, ada_w, ada_b)


def _prompt_stage_kernel(x_ref, mod_ref, ln1_ref, wmain_ref, widx_ref, convw_ref, convb_ref,
                         wa_ref, ba_ref, wx_ref, bx_ref, lam_ref, qg_ref, kg_ref, gnl_ref,
                         onesq_ref, onesk_ref, rope_ref,
                         na_ref, q_ref, kt_ref, vt_ref, ikt_ref, ktb_ref, vb_ref,
                         ikcat_ref, iqcat_ref, iw_ref, h_ref, conv_ref,
                         xr_buf, hcarry, widx_parts):
    tt = x_ref.shape[1]
    t = pl.program_id(1)

    @pl.when((t == 0) & (pl.program_id(0) == 0))
    def _():
        for j, part in enumerate(_split3(widx_ref[...])):
            widx_parts[j] = part

    @pl.when(t == 0)
    def _():
        xr_buf[0:SUBLANES, :] = jnp.zeros((SUBLANES, LRU_W), F32)
        hcarry[...] = jnp.zeros_like(hcarry)

    mod = mod_ref[0]
    h1, h2, h3 = _split3(_rms(x_ref[0], ln1_ref[...]) * (1.0 + mod[:, D_MODEL:2 * D_MODEL]) + mod[:, 0:D_MODEL])
    z = _dot_nt(h1, wmain_ref[...])
    zi_small = _dot_nt(h1, widx_parts[2]) + _dot_nt(h2, widx_parts[1])

    xr = z[:, COL_XR:COL_GR]
    xr_buf[SUBLANES:SUBLANES + tt, :] = xr
    convw = convw_ref[...]
    xc = convb_ref[...]
    for i in range(CONV_W):
        xc = xc + xr_buf[pl.ds(SUBLANES - (CONV_W - 1) + i, tt), :] * convw[i:i + 1, :]
    conv_ref[0] = xr_buf[pl.ds(SUBLANES + tt - (CONV_W - 1), CONV_W - 1), :]
    xr_buf[0:SUBLANES, :] = xr_buf[tt:tt + SUBLANES, :]

    zi_small = zi_small + _dot_nt(h3, widx_parts[0])
    zi_mid = _dot_nt(h1, widx_parts[1])
    a, b = _lru_gates(xc, wa_ref[...], ba_ref[...], wx_ref[...], bx_ref[...], lam_ref[...])
    zi_mid = zi_mid + _dot_nt(h2, widx_parts[0])

    q, k, v = _qkv_inputs(z, rope_ref[...], qg_ref[...], kg_ref[...], onesq_ref[...], onesk_ref[...])
    for h in range(N_HEADS):
        q_ref[0, h] = q[:, h * HEAD_DIM:(h + 1) * HEAD_DIM].astype(BF16)
    kt = jnp.transpose(k)
    vt = jnp.transpose(v)
    kt_ref[0] = kt.reshape(N_KV_HEADS, HEAD_DIM, tt)
    vt_ref[0] = vt.reshape(N_KV_HEADS, HEAD_DIM, tt)
    ktb_ref[0] = kt.astype(BF16).reshape(N_KV_HEADS, HEAD_DIM, tt)
    lane = lax.broadcasted_iota(I32, v.shape, 1)
    pad = jnp.where(lane == HEAD_DIM, 1.0, 0.0)
    for n in range(N_KV_HEADS):
        vn = v if n == 0 else pltpu.roll(v, KV_W - n * HEAD_DIM, 1)
        vb_ref[0, n] = jnp.where(lane < HEAD_DIM, vn, pad).astype(BF16)

    zi_big = _dot_nt(h1, widx_parts[0])
    row = lax.broadcasted_iota(I32, a.shape, 0) % SUBLANES
    d = 1
    while d < SUBLANES:
        keep = row >= d
        a_s = jnp.where(keep, pltpu.roll(a, d, 0), 1.0)
        b_s = jnp.where(keep, pltpu.roll(b, d, 0), 0.0)
        b = a * b_s + b
        a = a * a_s
        d *= 2
    h_in = hcarry[...]
    groups = []
    for g in range(tt // SUBLANES):
        rows = slice(g * SUBLANES, (g + 1) * SUBLANES)
        hg = a[rows] * h_in + b[rows]
        h_in = hg[SUBLANES - 1:SUBLANES, :]
        groups.append(hg)
    hr = jnp.concatenate(groups, axis=0)
    hcarry[...] = h_in
    h_ref[0] = h_in

    na_ref[0] = _rms(_gelu_tanh(z[:, COL_GR:COL_Q]) * hr, gnl_ref[...]).astype(BF16)

    zi = (zi_small + zi_mid) + zi_big
    iq, tail = _index_inputs(zi, rope_ref[...])
    ikt = jnp.transpose(tail)[0:IDX_DIM, :]
    ikt_ref[0] = ikt
    ikcat_ref[0] = _cat6_rhs(ikt, 0)
    for h in range(IDX_HEADS):
        iqcat_ref[0, h] = _cat6_lhs(iq[:, h * IDX_DIM:(h + 1) * IDX_DIM], 1)
    iw_ref[0] = zi[:, IDX_W:IDXP_W]


def _const_spec(shape):
    nd = len(shape)
    return pl.BlockSpec(shape, lambda *_: (0,) * nd)


def _prompt_stage(x, mod_p, p, rope_tab):
    bsz, t, _ = x.shape
    tt = PROMPT_ROW_TILE

    def rows(w):
        return pl.BlockSpec((1, tt, w), lambda b, i: (b, i, 0))

    in_specs = [
        rows(D_MODEL),
        pl.BlockSpec((1, 1, 6 * D_MODEL), lambda b, i: (b, 0, 0)),
        _const_spec((1, D_MODEL)),
        _const_spec((MAIN_W, D_MODEL)),
        _const_spec((IDXP_W, D_MODEL)),
        _const_spec((CONV_W, LRU_W)),
        _const_spec((1, LRU_W)),
        _const_spec((LRU_W, LRU_W)),
        _const_spec((1, LRU_W)),
        _const_spec((LRU_W, LRU_W)),
        _const_spec((1, LRU_W)),
        _const_spec((1, LRU_W)),
        _const_spec((1, ATT_W)),
        _const_spec((1, KV_W)),
        _const_spec((1, LRU_W)),
        _const_spec((ATT_W, ATT_W)),
        _const_spec((KV_W, KV_W)),
        pl.BlockSpec((tt, 3 * LANES), lambda b, i: (i, 0)),
    ]

    def heads(n, w):
        return pl.BlockSpec((1, n, tt, w), lambda b, i: (b, 0, i, 0))

    def heads_t(n):
        return pl.BlockSpec((1, n, HEAD_DIM, tt), lambda b, i: (b, 0, 0, i))

    def feat_t(w):
        return pl.BlockSpec((1, w, tt), lambda b, i: (b, 0, i))

    out_specs = [
        rows(LRU_W), heads(N_HEADS, HEAD_DIM), heads_t(N_KV_HEADS), heads_t(N_KV_HEADS), feat_t(IDX_DIM),
        heads_t(N_KV_HEADS), heads(N_KV_HEADS, KV_W), feat_t(IDX_CAT), heads(IDX_HEADS, IDX_CAT), rows(TAIL_W),
        pl.BlockSpec((1, 1, LRU_W), lambda b, i: (b, 0, 0)),
        pl.BlockSpec((1, CONV_W - 1, LRU_W), lambda b, i: (b, 0, 0)),
    ]
    out_shape = [
        jax.ShapeDtypeStruct((bsz, t, LRU_W), BF16),
        jax.ShapeDtypeStruct((bsz, N_HEADS, t, HEAD_DIM), BF16),
        jax.ShapeDtypeStruct((bsz, N_KV_HEADS, HEAD_DIM, t), F32),
        jax.ShapeDtypeStruct((bsz, N_KV_HEADS, HEAD_DIM, t), F32),
        jax.ShapeDtypeStruct((bsz, IDX_DIM, t), F32),
        jax.ShapeDtypeStruct((bsz, N_KV_HEADS, HEAD_DIM, t), BF16),
        jax.ShapeDtypeStruct((bsz, N_KV_HEADS, t, KV_W), BF16),
        jax.ShapeDtypeStruct((bsz, IDX_CAT, t), BF16),
        jax.ShapeDtypeStruct((bsz, IDX_HEADS, t, IDX_CAT), BF16),
        jax.ShapeDtypeStruct((bsz, t, TAIL_W), F32),
        jax.ShapeDtypeStruct((bsz, 1, LRU_W), F32),
        jax.ShapeDtypeStruct((bsz, CONV_W - 1, LRU_W), F32),
    ]
    return pl.pallas_call(
        _prompt_stage_kernel,
        grid=(bsz, t // tt),
        in_specs=in_specs,
        out_specs=out_specs,
        out_shape=out_shape,
        scratch_shapes=[pltpu.VMEM((tt + SUBLANES, LRU_W), F32), pltpu.VMEM((1, LRU_W), F32),
                        pltpu.VMEM((3, IDXP_W, D_MODEL), BF16)],
        compiler_params=pltpu.CompilerParams(dimension_semantics=("arbitrary", "arbitrary"),
                                             vmem_limit_bytes=VMEM_LIMIT),
        name="prompt_stage",
    )(x, mod_p, p["ln1"], p["w_main"], p["w_idx"], p["conv_w"], p["conv_b"], p["wa"], p["ba"],
      p["wx"], p["bx"], p["lam"], p["qg"], p["kg"], p["gnl"], p["ones_q"], p["ones_k"], rope_tab)


def _sortable_key(score, fold_zero=True):
    if fold_zero:
        score = jnp.where(score == 0.0, 0.0, score)
    bits = lax.bitcast_convert_type(score, I32)
    return jnp.where(bits < 0, bits ^ jnp.int32(0x7FFFFFFF), bits)


def _count_ge16(ref, c, strict=False):
    c16 = c.astype(HALF_T)
    hit = ref[...] > c16 if strict else ref[...] >= c16
    ones = jnp.where(hit, jnp.ones((), HALF_T), jnp.zeros((), HALF_T))
    acc = ones[:, 0:LANES]
    for j in range(1, ref.shape[1] // LANES):
        acc = acc + ones[:, j * LANES:(j + 1) * LANES]
    return jnp.sum(acc.astype(F32), axis=1, keepdims=True)


def _bitwise_max16(count_ge, k, n_all):
    def body(i, carry):
        t, ct = carry
        c = t + lax.shift_left(jnp.int32(1), jnp.int32(15) - i)
        cn = count_ge(c)
        ok = cn >= k
        return jnp.where(ok, c, t), jnp.where(ok, cn, ct)

    init = (jnp.full(k.shape, HALF_MIN, I32), jnp.full(k.shape, float(n_all), F32))
    return lax.fori_loop(0, 16, body, init, unroll=True)


def _kth_largest(key_ref, hi_ref, lo_ref, k, extra=None):
    key = key_ref[...]
    hi_ref[...] = lax.shift_right_arithmetic(key, 16).astype(HALF_T)
    lo_ref[...] = ((key & 0xFFFF) + HALF_MIN).astype(HALF_T)
    kf = jnp.full((key_ref.shape[0], 1), k, F32)
    n_all = key_ref.shape[1] + (0 if extra is None else 1)
    if extra is not None:
        hi_x = lax.shift_right_arithmetic(extra, 16)
        lo_x = (extra & 0xFFFF) + HALF_MIN

    def count_hi(c):
        cnt = _count_ge16(hi_ref, c)
        return cnt if extra is None else cnt + jnp.where(hi_x >= c, 1.0, 0.0)

    t_hi, n_bucket_up = _bitwise_max16(count_hi, kf, n_all)
    above = _count_ge16(hi_ref, t_hi, strict=True)
    if extra is not None:
        above = above + jnp.where(hi_x > t_hi, 1.0, 0.0)
    k_lo = kf - above
    lo_ref[...] = jnp.where(hi_ref[...] == t_hi.astype(HALF_T), lo_ref[...], jnp.full((), HALF_MIN, HALF_T))
    if extra is not None:
        lo_x = jnp.where(hi_x == t_hi, lo_x, HALF_MIN)

    def count_lo(c):
        cnt = _count_ge16(lo_ref, c)
        return cnt if extra is None else cnt + jnp.where(lo_x >= c, 1.0, 0.0)

    t_lo, n_lo = _bitwise_max16(count_lo, k_lo, n_all)
    n_ge = jnp.where(t_lo > HALF_MIN, above + n_lo, n_bucket_up)
    return t_hi * 65536 + (t_lo - HALF_MIN), n_ge


def _selection_bias(key_ref, hi_ref, lo_ref, bias_ref, tri, k, extra=None):
    thr, n_ge = _kth_largest(key_ref, hi_ref, lo_ref, k, extra)
    has_ties = jnp.max(jnp.where(n_ge > k, 1.0, 0.0)) > 0.0

    @pl.when(jnp.logical_not(has_ties))
    def _():
        bias_ref[...] = jnp.where(key_ref[...] >= thr, 0.0, NEG_BIG)

    @pl.when(has_ties)
    def _():
        n_gt = jnp.sum(jnp.where(key_ref[...] > thr, 1.0, 0.0), axis=1, keepdims=True)
        if extra is not None:
            n_gt = n_gt + jnp.where(extra > thr, 1.0, 0.0)
        need = k - n_gt
        offset = jnp.zeros_like(need)
        for c in range(key_ref.shape[1] // TRI_W):
            kc = key_ref[:, c * TRI_W:(c + 1) * TRI_W]
            eq = kc == thr
            e = jnp.where(eq, 1.0, 0.0)
            incl = _dot(e.astype(BF16), tri)
            rank = incl - e + offset
            tie = jnp.where(rank < need, 0.0, NEG_BIG)
            bias_ref[:, c * TRI_W:(c + 1) * TRI_W] = jnp.where(kc > thr, 0.0, jnp.where(eq, tie, NEG_BIG))
            offset = offset + incl[:, TRI_W - 1:TRI_W]

    if extra is None:
        return None
    n_gt = jnp.sum(jnp.where(key_ref[...] > thr, 1.0, 0.0), axis=1, keepdims=True) + jnp.where(extra > thr, 1.0, 0.0)
    n_eq_main = n_ge - n_gt - jnp.where(extra == thr, 1.0, 0.0)
    tie = jnp.where(n_eq_main < k - n_gt, 0.0, NEG_BIG)
    return jnp.where(extra > thr, 0.0, jnp.where(extra == thr, tie, NEG_BIG))


def _prompt_attn_kernel(q_ref, iq_ref, iw_ref, kt_ref, v_ref, ikt_ref, gn_ref, tri_ref, o_ref,
                        key_ref, hi_ref, lo_ref, bias_ref):
    nb, qb = o_ref.shape[0], o_ref.shape[1]
    n_keys = bias_ref.shape[1]
    past = n_keys - qb
    causal = lax.broadcasted_iota(I32, (qb, qb), 1) <= lax.broadcasted_iota(I32, (qb, qb), 0)
    if n_keys > TOPK_MAX:
        for b in range(nb):
            rows = slice(b * qb, (b + 1) * qb)
            iw = iw_ref[b]
            ikt = ikt_ref[b]
            score = jnp.zeros((qb, n_keys), F32)
            s_next = _dot(iq_ref[b, 0], ikt)
            for h in range(IDX_HEADS):
                s, s_next = s_next, (_dot(iq_ref[b, h + 1], ikt) if h + 1 < IDX_HEADS else None)
                score = score + iw[:, IW_LANE + h:IW_LANE + h + 1] * jnp.maximum(s, 0.0)
            key = _sortable_key(score, fold_zero=False)
            key_ref[rows, 0:past] = key[:, 0:past]
            key_ref[rows, past:n_keys] = jnp.where(causal, key[:, past:n_keys], INT_MIN)
        _selection_bias(key_ref, hi_ref, lo_ref, bias_ref, tri_ref[...], float(TOPK_MAX))
        for b in range(nb):
            rows = slice(b * qb, (b + 1) * qb)
            bias_ref[rows, past:n_keys] = jnp.where(causal, bias_ref[rows, past:n_keys], NEG_BIG)
    else:
        for b in range(nb):
            bias_ref[b * qb:(b + 1) * qb, :] = jnp.where(causal, 0.0, NEG_BIG)

    def qk(i):
        b, h = divmod(i, N_HEADS)
        return _dot(q_ref[b, h], kt_ref[b, h // KV_GROUP]) + bias_ref[b * qb:(b + 1) * qb, :]

    logits_next = qk(0)
    for b in range(nb):
        outs = []
        for h in range(N_HEADS):
            i = b * N_HEADS + h
            logits, logits_next = logits_next, (qk(i + 1) if i + 1 < nb * N_HEADS else None)
            m = jnp.max(logits, axis=1, keepdims=True)
            p = jnp.exp2(logits - m).astype(BF16)
            pv = _dot(p, v_ref[b, h // KV_GROUP])
            outs.append(pv[:, 0:HEAD_DIM] / pv[:, HEAD_DIM:HEAD_DIM + 1])
        o_ref[b] = _rms(jnp.concatenate(outs, axis=1), gn_ref[...]).astype(BF16)


def _prompt_attention_block(qh, iqcat, iw, ktb, vb, ikcat, gn_att, tri, q_block):
    bsz, _, t, _ = qh.shape
    qb = Q_BLOCK
    n_keys = (q_block + 1) * qb
    nb = max(1, min(bsz, SEARCH_KEYS_PER_STEP // n_keys))
    while bsz % nb:
        nb -= 1
    rows = nb * qb

    def qheads(n, w):
        return pl.BlockSpec((nb, n, qb, w), lambda b: (b, 0, q_block, 0))

    return pl.pallas_call(
        _prompt_attn_kernel,
        grid=(bsz // nb,),
        in_specs=[qheads(N_HEADS, HEAD_DIM), qheads(IDX_HEADS, IDX_CAT),
                  pl.BlockSpec((nb, qb, TAIL_W), lambda b: (b, q_block, 0)),
                  pl.BlockSpec((nb, N_KV_HEADS, HEAD_DIM, n_keys), lambda b: (b, 0, 0, 0)),
                  pl.BlockSpec((nb, N_KV_HEADS, n_keys, KV_W), lambda b: (b, 0, 0, 0)),
                  pl.BlockSpec((nb, IDX_CAT, n_keys), lambda b: (b, 0, 0)),
                  _const_spec((1, ATT_W)), _const_spec((TRI_W, TRI_W))],
        out_specs=pl.BlockSpec((nb, qb, ATT_W), lambda b: (b, 0, 0)),
        out_shape=jax.ShapeDtypeStruct((bsz, qb, ATT_W), BF16),
        scratch_shapes=[pltpu.VMEM((rows, n_keys), I32), pltpu.VMEM((rows, n_keys), HALF_T),
                        pltpu.VMEM((rows, n_keys), HALF_T), pltpu.VMEM((rows, n_keys), F32)],
        compiler_params=pltpu.CompilerParams(dimension_semantics=("arbitrary",), vmem_limit_bytes=VMEM_LIMIT),
        name=f"prompt_attention_{q_block}",
    )(qh, iqcat, iw, ktb, vb, ikcat, gn_att, tri)


def _prompt_attention(qh, iqcat, iw, ktb, vb, ikcat, gn_att, tri):
    t = qh.shape[2]
    blocks = [_prompt_attention_block(qh, iqcat, iw, ktb, vb, ikcat, gn_att, tri, i) for i in range(t // Q_BLOCK)]
    return jnp.concatenate(blocks, axis=1)


def _post_kernel(x_ref, na_ref, nb_ref, mod_ref, woa_ref, wob_ref, ln2_ref, wg_ref, wu_ref, wd_ref,
                 y_ref, *, shared_mod):
    mod = mod_ref[0] if shared_mod else mod_ref[...]
    g1 = mod[:, 2 * D_MODEL:3 * D_MODEL]
    sh2 = mod[:, 3 * D_MODEL:4 * D_MODEL]
    sc2 = mod[:, 4 * D_MODEL:5 * D_MODEL]
    g2 = mod[:, 5 * D_MODEL:6 * D_MODEL]
    x = x_ref[...]
    mix = _dot(na_ref[...], woa_ref[...]) + _dot(nb_ref[...], wob_ref[...])
    x1 = x + g1 * mix
    h2 = (_rms(x1, ln2_ref[...]) * (1.0 + sc2) + sh2).astype(BF16)
    def gate_up(c):
        return _dot(h2, wg_ref[:, c * FF_CHUNK:(c + 1) * FF_CHUNK]), _dot(h2, wu_ref[:, c * FF_CHUNK:(c + 1) * FF_CHUNK])

    n_chunks = D_FF // FF_CHUNK
    ff = jnp.zeros_like(x1)
    gu_next = gate_up(0)
    for c in range(n_chunks):
        (g, u), gu_next = gu_next, (gate_up(c + 1) if c + 1 < n_chunks else None)
        act = (g * jax.nn.sigmoid(g) * u).astype(BF16)
        ff = ff + _dot(act, wd_ref[c * FF_CHUNK:(c + 1) * FF_CHUNK, :])
    y_ref[...] = x1 + g2 * ff


def _post(x, na, nb, mod, p, rows_per_mod, tile):
    n = x.shape[0]
    shared = rows_per_mod > 1
    if shared:
        mod_spec = pl.BlockSpec((1, 1, 6 * D_MODEL), lambda i: (i * tile // rows_per_mod, 0, 0))
    else:
        mod_spec = pl.BlockSpec((tile, 6 * D_MODEL), lambda i: (i, 0))

    def rows(w):
        return pl.BlockSpec((tile, w), lambda i: (i, 0))

    def weight(shape):
        return pl.BlockSpec(shape, lambda i: (0, 0), pipeline_mode=pl.Buffered(1))

    return pl.pallas_call(
        functools.partial(_post_kernel, shared_mod=shared),
        grid=(n // tile,),
        in_specs=[rows(D_MODEL), rows(LRU_W), rows(ATT_W), mod_spec,
                  weight((LRU_W, D_MODEL)), weight((ATT_W, D_MODEL)), weight((1, D_MODEL)),
                  weight((D_MODEL, D_FF)), weight((D_MODEL, D_FF)), weight((D_FF, D_MODEL))],
        out_specs=rows(D_MODEL),
        out_shape=jax.ShapeDtypeStruct((n, D_MODEL), F32),
        compiler_params=pltpu.CompilerParams(dimension_semantics=("arbitrary",), vmem_limit_bytes=VMEM_LIMIT),
        name="post_ffn",
    )(x, na, nb, mod, p["wo_a"], p["wo_b"], p["ln2"], p["wg"], p["wu"], p["wd"])


def _sample_stage_kernel(x_ref, mod_ref, ln1_ref, wmain_ref, widx_ref, convw_ref, convb_ref,
                         wa_ref, ba_ref, wx_ref, bx_ref, lam_ref, qg_ref, kg_ref, gnl_ref,
                         onesq_ref, onesk_ref, rope_ref, sconv_ref, sh_ref,
                         na_ref, q_ref, k_ref, v_ref, ik_ref, iq_ref, iw_ref, h_ref, conv_ref):
    mod = mod_ref[...]
    z, zi = _project_in(x_ref[...], mod[:, 0:D_MODEL], mod[:, D_MODEL:2 * D_MODEL], ln1_ref[...],
                        wmain_ref[...], _split3(widx_ref[...]))
    xr = z[:, COL_XR:COL_GR]
    convw = convw_ref[...]
    sconv = sconv_ref[...]
    xc = convb_ref[...]
    for i in range(CONV_W - 1):
        xc = xc + sconv[:, i * LRU_W:(i + 1) * LRU_W] * convw[i:i + 1, :]
    xc = xc + xr * convw[CONV_W - 1:CONV_W, :]
    conv_ref[:, 0:(CONV_W - 2) * LRU_W] = sconv[:, LRU_W:(CONV_W - 1) * LRU_W]
    conv_ref[:, (CONV_W - 2) * LRU_W:(CONV_W - 1) * LRU_W] = xr

    a, b = _lru_gates(xc, wa_ref[...], ba_ref[...], wx_ref[...], bx_ref[...], lam_ref[...])
    hr = a * sh_ref[...] + b
    h_ref[...] = hr
    na_ref[...] = _rms(_gelu_tanh(z[:, COL_GR:COL_Q]) * hr, gnl_ref[...]).astype(BF16)

    q, k, v = _qkv_inputs(z, rope_ref[...], qg_ref[...], kg_ref[...], onesq_ref[...], onesk_ref[...])
    iq, tail = _index_inputs(zi, rope_ref[...])
    q_ref[...] = q
    k_ref[...] = k
    v_ref[...] = v
    ik_ref[...] = tail[:, 0:IDX_DIM]
    iq_ref[...] = iq
    iw_ref[...] = zi[:, IDX_W:IDXP_W]


def _sample_stage(x, mod_s, p, rope_row, sconv, sh):
    n = x.shape[0]

    def out(w, dt=F32):
        return jax.ShapeDtypeStruct((n, w), dt)

    return pl.pallas_call(
        _sample_stage_kernel,
        out_shape=[out(LRU_W, BF16), out(ATT_W), out(KV_W), out(KV_W), out(IDX_DIM), out(IDX_W),
                   out(TAIL_W), out(LRU_W), out((CONV_W - 1) * LRU_W)],
        compiler_params=pltpu.CompilerParams(vmem_limit_bytes=VMEM_LIMIT),
        name="sample_stage",
    )(x, mod_s, p["ln1"], p["w_main"], p["w_idx"], p["conv_w"], p["conv_b"], p["wa"], p["ba"],
      p["wx"], p["bx"], p["lam"], p["qg"], p["kg"], p["gnl"], p["ones_q"], p["ones_k"], rope_row, sconv, sh)


def _page_copy(cache_ref, buf_ref, sem, page, slot, j, pg):
    return pltpu.make_async_copy(cache_ref.at[page], buf_ref.at[slot, j, pg], sem)


def _paged_prefetch(pt_ref, caches, bufs, sems, group, n_pages):
    s = pl.program_id(0)
    slot = s % 2

    def start(grp, half):
        for cache_ref, buf_ref, sem in zip(caches, bufs, sems):
            for j in range(group):
                for pg in range(n_pages):
                    page = pt_ref[grp * group + j, pg]
                    _page_copy(cache_ref, buf_ref, sem.at[half], page, half, j, pg).start()

    @pl.when(s == 0)
    def _():
        start(0, 0)

    @pl.when(s + 1 < pl.num_programs(0))
    def _():
        start(s + 1, 1 - slot)

    for cache_ref, buf_ref, sem in zip(caches, bufs, sems):
        for j in range(group):
            for pg in range(n_pages):
                _page_copy(cache_ref, buf_ref, sem.at[slot], 0, slot, j, pg).wait()
    return slot


def _sample_score_kernel(pt_ref, iq_ref, iw_ref, ik_hbm, o_ref, ik_buf, sem, *, n_pages):
    group = iq_ref.shape[0]
    slot = _paged_prefetch(pt_ref, (ik_hbm,), (ik_buf,), (sem,), group, n_pages)
    iqcat = [_cat6_lhs(iq_ref[j], 1) for j in range(group)]
    for pg in range(n_pages):
        for j in range(group):
            s = _dot(iqcat[j], _cat6_rhs(ik_buf[slot, j, pg], 0))
            o_ref[j, :, pg * PAGE_SIZE:(pg + 1) * PAGE_SIZE] = jnp.sum(iw_ref[j] * jnp.maximum(s, 0.0), axis=0,
                                                                        keepdims=True)


def _sample_scores(page_table, iq3, iw3, cache_ik):
    n, n_pages = page_table.shape
    g = SCORE_GROUP
    grid_spec = pltpu.PrefetchScalarGridSpec(
        num_scalar_prefetch=1,
        grid=(n // g,),
        in_specs=[pl.BlockSpec((g, IDX_HEADS, IDX_DIM), lambda s, pt: (s, 0, 0)),
                  pl.BlockSpec((g, IDX_HEADS, 1), lambda s, pt: (s, 0, 0)),
                  pl.BlockSpec(memory_space=pl.ANY)],
        out_specs=pl.BlockSpec((g, 1, n_pages * PAGE_SIZE), lambda s, pt: (s, 0, 0)),
        scratch_shapes=[pltpu.VMEM((2, g, n_pages, IDX_DIM, PAGE_SIZE), F32), pltpu.SemaphoreType.DMA((2,))],
    )
    return pl.pallas_call(
        functools.partial(_sample_score_kernel, n_pages=n_pages),
        grid_spec=grid_spec,
        out_shape=jax.ShapeDtypeStruct((n, 1, n_pages * PAGE_SIZE), F32),
        compiler_params=pltpu.CompilerParams(dimension_semantics=("arbitrary",), vmem_limit_bytes=VMEM_LIMIT),
        name="sample_scores",
    )(page_table, iq3, iw3, cache_ik)


def _sample_select_kernel(score_ref, iq_ref, ik_ref, iw_ref, tri_ref, bias_ref, bias_new_ref,
                          key_ref, hi_ref, lo_ref):
    iq = iq_ref[...]
    ik = ik_ref[...]
    iw = iw_ref[...]
    new = jnp.zeros((iq.shape[0], 1), F32)
    for h in range(IDX_HEADS):
        s = jnp.sum(iq[:, h * IDX_DIM:(h + 1) * IDX_DIM] * ik, axis=1, keepdims=True)
        new = new + iw[:, IW_LANE + h:IW_LANE + h + 1] * jnp.maximum(s, 0.0)
    key_ref[...] = _sortable_key(score_ref[...])
    b_new = _selection_bias(key_ref, hi_ref, lo_ref, bias_ref, tri_ref[...], float(TOPK_MAX),
                            extra=_sortable_key(new))
    bias_new_ref[...] = jnp.broadcast_to(b_new, bias_new_ref.shape)


def _sample_select(score, iq, ik, iw, tri):
    n, n_keys = score.shape
    return pl.pallas_call(
        _sample_select_kernel,
        out_shape=[jax.ShapeDtypeStruct((n, n_keys), F32), jax.ShapeDtypeStruct((n, LANES), F32)],
        scratch_shapes=[pltpu.VMEM((n, n_keys), I32), pltpu.VMEM((n, n_keys), HALF_T), pltpu.VMEM((n, n_keys), HALF_T)],
        compiler_params=pltpu.CompilerParams(vmem_limit_bytes=VMEM_LIMIT),
        name="sample_select",
    )(score, iq, ik, iw, tri)


def _sample_attn_kernel(pt_ref, q_ref, kn_ref, vn_ref, bias_ref, bnew_ref, gn_ref, k_hbm, v_hbm, o_ref,
                        k_buf, v_buf, k_sem, v_sem, *, n_pages):
    group = q_ref.shape[0]
    seqs = range(group)
    slot = _paged_prefetch(pt_ref, (k_hbm, v_hbm), (k_buf, v_buf), (k_sem, v_sem), group, n_pages)
    row = lax.broadcasted_iota(I32, (N_HEADS, HEAD_DIM), 0)
    first = row < KV_GROUP
    q = [q_ref[j] for j in seqs]
    q2 = [jnp.concatenate([jnp.where(first, q[j], 0.0), jnp.where(first, 0.0, q[j])], axis=1).astype(BF16)
          for j in seqs]
    logits = [[None] * n_pages for _ in seqs]
    for pg in range(n_pages):
        for j in seqs:
            kp = k_buf[slot, j, pg].astype(BF16).reshape(KV_W, PAGE_SIZE)
            logits[j][pg] = _dot(q2[j], kp) + bias_ref[j][:, pg * PAGE_SIZE:(pg + 1) * PAGE_SIZE]
    m, p_new, den, acc, v_sel = [], [], [], [], []
    for j in seqs:
        kn = kn_ref[j]
        vn = vn_ref[j]
        k_sel = jnp.where(first, kn[:, 0:HEAD_DIM], kn[:, HEAD_DIM:KV_W])
        v_sel.append(jnp.where(first, vn[:, 0:HEAD_DIM], vn[:, HEAD_DIM:KV_W]))
        l_new = jnp.sum(q[j] * k_sel, axis=1, keepdims=True) + bnew_ref[j][:, 0:1]
        mj = l_new
        for l in logits[j]:
            mj = jnp.maximum(mj, jnp.max(l, axis=1, keepdims=True))
        m.append(mj)
        p_new.append(jnp.exp2(l_new - mj))
        den.append(p_new[j])
        acc.append(jnp.zeros((N_HEADS, KV_W), F32))
    for pg in range(n_pages):
        for j in seqs:
            p = jnp.exp2(logits[j][pg] - m[j])
            den[j] = den[j] + jnp.sum(p, axis=1, keepdims=True)
            vp = v_buf[slot, j, pg].astype(BF16).reshape(KV_W, PAGE_SIZE)
            acc[j] = acc[j] + _dot_nt(p.astype(BF16), vp)
    for j in seqs:
        out = (p_new[j] * v_sel[j] + jnp.where(first, acc[j][:, 0:HEAD_DIM], acc[j][:, HEAD_DIM:KV_W])) / den[j]
        ms = jnp.sum(jnp.sum(out * out, axis=1, keepdims=True), axis=0, keepdims=True) * (1.0 / ATT_W)
        o_ref[j] = (out * lax.rsqrt(ms + EPS) * gn_ref[...]).astype(BF16)


def _sample_attention(page_table, q3, k_new, v_new, bias, bias_new, gn8, cache_k, cache_v):
    n, n_pages = page_table.shape
    n_keys = n_pages * PAGE_SIZE

    g = DECODE_GROUP

    def per_seq(shape):
        return pl.BlockSpec((g,) + shape, lambda s, pt: (s, 0, 0))

    page_buf = pltpu.VMEM((2, g, n_pages, N_KV_HEADS, HEAD_DIM, PAGE_SIZE), F32)
    grid_spec = pltpu.PrefetchScalarGridSpec(
        num_scalar_prefetch=1,
        grid=(n // g,),
        in_specs=[per_seq((N_HEADS, HEAD_DIM)), per_seq((1, KV_W)), per_seq((1, KV_W)),
                  per_seq((1, n_keys)), per_seq((1, LANES)),
                  pl.BlockSpec((N_HEADS, HEAD_DIM), lambda s, pt: (0, 0)),
                  pl.BlockSpec(memory_space=pl.ANY), pl.BlockSpec(memory_space=pl.ANY)],
        out_specs=per_seq((N_HEADS, HEAD_DIM)),
        scratch_shapes=[page_buf, page_buf, pltpu.SemaphoreType.DMA((2,)), pltpu.SemaphoreType.DMA((2,))],
    )
    return pl.pallas_call(
        functools.partial(_sample_attn_kernel, n_pages=n_pages),
        grid_spec=grid_spec,
        out_shape=jax.ShapeDtypeStruct((n, N_HEADS, HEAD_DIM), BF16),
        compiler_params=pltpu.CompilerParams(dimension_semantics=("arbitrary",), vmem_limit_bytes=VMEM_LIMIT),
        name="sample_attention",
    )(page_table, q3, k_new, v_new, bias, bias_new, gn8, cache_k, cache_v)


def _rope_table_np(positions):
    half = ROT_DIM // 2
    freq = ROPE_THETA ** (-(np.arange(half, dtype=np.float64) / half))
    ang = np.asarray(positions, np.float64)[:, None] * freq[None, :]
    cos, sin = np.cos(ang), np.sin(ang)
    n = len(positions)
    tab = np.zeros((n, 3, LANES), np.float64)
    tab[:, 0, :] = 1.0
    for base in range(0, LANES, HEAD_DIM):
        tab[:, 0, base:base + half] = cos
        tab[:, 0, base + half:base + ROT_DIM] = cos
        tab[:, 1, base:base + half] = -sin
        tab[:, 2, base + half:base + ROT_DIM] = sin
    return tab.reshape(n, 3 * LANES).astype(np.float32)


def _block_diag(w):
    n, a, b = w.shape
    return jnp.einsum("nij,nm->nimj", w, jnp.eye(n, dtype=w.dtype)).reshape(n * a, n * b)


def _layer_params(l, ln1_g, w_in, conv_w, conv_b, lru_wa, lru_ba, lru_wx, lru_bx, lru_lambda, q_norm_g,
                  k_norm_g, gn_lru_g, gn_att_g, w_out, ln2_g, w_gate, w_up, w_down):
    w_in_t = jnp.transpose(w_in[l])
    w_idx = jnp.pad(w_in_t[COL_IQ:], ((0, IDXP_W - (w_in.shape[2] - COL_IQ)), (0, 0)))
    head_id = np.arange(ATT_W) // HEAD_DIM
    ones_q = (head_id[:, None] == head_id[None, :]).astype(np.float32)
    return {
        "ln1": ln1_g[l][None, :],
        "w_main": w_in_t[:COL_IQ].astype(BF16),
        "w_idx": w_idx,
        "conv_w": conv_w[l],
        "conv_b": conv_b[l][None, :],
        "wa": _block_diag(lru_wa[l]).astype(BF16),
        "ba": lru_ba[l][None, :],
        "wx": _block_diag(lru_wx[l]).astype(BF16),
        "bx": lru_bx[l][None, :],
        "lam": lru_lambda[l][None, :],
        "qg": jnp.tile(q_norm_g[l], N_HEADS)[None, :],
        "kg": jnp.tile(k_norm_g[l], N_KV_HEADS)[None, :],
        "gnl": gn_lru_g[l][None, :],
        "gna": gn_att_g[l][None, :],
        "gna8": gn_att_g[l].reshape(N_HEADS, HEAD_DIM),
        "ones_q": jnp.asarray(ones_q, BF16),
        "ones_k": jnp.asarray(ones_q[:KV_W, :KV_W], BF16),
        "wo_a": w_out[l][:LRU_W].astype(BF16),
        "wo_b": w_out[l][LRU_W:].astype(BF16),
        "ln2": ln2_g[l][None, :],
        "wg": w_gate[l].astype(BF16),
        "wu": w_up[l].astype(BF16),
        "wd": w_down[l].astype(BF16),
    }


def kernel(x_prompt, x_sample, cache_k, cache_v, cache_ik, state_h, state_conv, page_table, c_prompt, c_sample, ada_w, ada_b, ln1_g, w_in, conv_w, conv_b, lru_wa, lru_ba, lru_wx, lru_bx, lru_lambda, q_norm_g, k_norm_g, gn_lru_g, gn_att_g, w_out, ln2_g, w_gate, w_up, w_down):
    bsz, t, _ = x_prompt.shape
    dbsz, dt, _ = x_sample.shape
    depth = ada_w.shape[0]
    n_pages = page_table.shape[1]
    past_len = n_pages * PAGE_SIZE
    assert dt == 1 and t % PROMPT_ROW_TILE == 0 and t % POST_ROW_TILE == 0 and t % Q_BLOCK == 0
    assert t // 4 >= TOPK_MAX and (past_len + dt) // 4 >= TOPK_MAX
    assert dbsz % DECODE_GROUP == 0 and dbsz % SCORE_GROUP == 0 and cache_k.shape[2] == PAGE_SIZE

    rope_p = jnp.asarray(_rope_table_np(np.arange(t)))
    rope_s = jnp.asarray(_rope_table_np(past_len + np.arange(dt)))
    tri = jnp.asarray(np.triu(np.ones((TRI_W, TRI_W), np.float32)), BF16)

    yp = x_prompt.reshape(bsz * t, D_MODEL)
    ys = x_sample.reshape(dbsz, D_MODEL)
    c_all = jnp.concatenate([c_prompt, c_sample], axis=0)
    outs_p, outs_s = [], []
    for l in range(depth):
        p = _layer_params(l, ln1_g, w_in, conv_w, conv_b, lru_wa, lru_ba, lru_wx, lru_bx, lru_lambda,
                          q_norm_g, k_norm_g, gn_lru_g, gn_att_g, w_out, ln2_g, w_gate, w_up, w_down)
        mod = _ada(c_all, ada_w[l], ada_b[l][None, :])
        mod_p = mod[:bsz].reshape(bsz, 1, 6 * D_MODEL)
        mod_s = mod[bsz:]

        (na, qh, kt, vt, ikt, ktb, vb, ikcat, iqcat, iw, h_last, conv_new) = _prompt_stage(
            yp.reshape(bsz, t, D_MODEL), mod_p, p, rope_p)
        nb = _prompt_attention(qh, iqcat, iw, ktb, vb, ikcat, p["gna"], tri)
        yp = _post(yp, na.reshape(bsz * t, LRU_W), nb.reshape(bsz * t, ATT_W), mod_p, p, t, POST_ROW_TILE)
        outs_p.append((jnp.transpose(kt, (0, 3, 1, 2)), jnp.transpose(vt, (0, 3, 1, 2)),
                       jnp.transpose(ikt, (0, 2, 1)), h_last.reshape(bsz, LRU_W), conv_new))

        (na_s, q_s, k_s, v_s, ik_s, iq_s, iw_s, h_s, conv_s) = _sample_stage(
            ys, mod_s, p, rope_s, state_conv[l].reshape(dbsz, (CONV_W - 1) * LRU_W), state_h[l])
        score = _sample_scores(page_table, iq_s.reshape(dbsz, IDX_HEADS, IDX_DIM),
                               iw_s[:, IW_LANE:IW_LANE + IDX_HEADS].reshape(dbsz, IDX_HEADS, 1),
                               jnp.transpose(cache_ik[l], (0, 2, 1)))
        bias, bias_new = _sample_select(score.reshape(dbsz, past_len), iq_s, ik_s, iw_s, tri)
        nb_s = _sample_attention(page_table, q_s.reshape(dbsz, N_HEADS, HEAD_DIM),
                                 k_s.reshape(dbsz, 1, KV_W), v_s.reshape(dbsz, 1, KV_W),
                                 bias.reshape(dbsz, 1, past_len), bias_new.reshape(dbsz, 1, LANES), p["gna8"],
                                 jnp.transpose(cache_k[l], (0, 2, 3, 1)), jnp.transpose(cache_v[l], (0, 2, 3, 1)))
        ys = _post(ys, na_s, nb_s.reshape(dbsz, ATT_W), mod_s, p, 1, dbsz)
        outs_s.append((k_s.reshape(dbsz, dt, N_KV_HEADS, HEAD_DIM), v_s.reshape(dbsz, dt, N_KV_HEADS, HEAD_DIM),
                       ik_s.reshape(dbsz, dt, IDX_DIM), h_s,
                       conv_s.reshape(dbsz, CONV_W - 1, LRU_W)))

    def stack(outs, i):
        return jnp.stack([o[i] for o in outs])

    return (yp.reshape(bsz, t, D_MODEL), ys.reshape(dbsz, dt, D_MODEL),
            stack(outs_p, 0), stack(outs_p, 1), stack(outs_p, 2), stack(outs_p, 3), stack(outs_p, 4),
            stack(outs_s, 0), stack(outs_s, 1), stack(outs_s, 2), stack(outs_s, 3), stack(outs_s, 4))
```

```python
import functools

import numpy as np
import jax
import jax.numpy as jnp
from jax import lax
from jax.experimental import pallas as pl
from jax.experimental.pallas import tpu as pltpu

F32 = jnp.float32
BF16 = jnp.bfloat16
I32 = jnp.int32
HALF_T = jnp.int16

D_MODEL = 1024
LRU_W = 512
LRU_BLOCKS = 8
LRU_BW = LRU_W // LRU_BLOCKS
CONV_W = 4
LRU_C = 8.0
N_HEADS = 8
HEAD_DIM = 64
ATT_W = N_HEADS * HEAD_DIM
N_KV_HEADS = 2
KV_GROUP = N_HEADS // N_KV_HEADS
KV_W = N_KV_HEADS * HEAD_DIM
ROT_DIM = HEAD_DIM // 4
ROPE_THETA = 500000.0
IDX_HEADS = 4
IDX_DIM = 64
IDX_W = IDX_HEADS * IDX_DIM
TOPK_MAX = 256
PAGE_SIZE = 128
D_FF = 2816
EPS = 1e-6

COL_XR = 0
COL_GR = COL_XR + LRU_W
COL_Q = COL_GR + LRU_W
COL_K = COL_Q + ATT_W
COL_V = COL_K + KV_W
COL_IQ = COL_V + KV_W
MAIN_W = COL_IQ
TAIL_W = 128
IDXP_W = IDX_W + TAIL_W
IW_LANE = IDX_DIM
IDX_CAT = 6 * IDX_DIM

LANES = 128
SUBLANES = 8
TRI_W = 256
INT_MIN = -2147483648
HALF_MIN = -32768
LOG2E = 1.4426950408889634
NEG_BIG = -1e30
VMEM_LIMIT = 48 * 1024 * 1024

PROMPT_ROW_TILE = 256
Q_BLOCK = 256
POST_ROW_TILE = 512
FF_CHUNK = D_FF
ADA_COL_TILE = 512
DECODE_GROUP = 8
SCORE_GROUP = 8
SEARCH_KEYS_PER_STEP = 2048


def _dot(a, b):
    return jnp.dot(a, b, preferred_element_type=F32)


def _dot_nt(a, b):
    return lax.dot_general(a, b, (((1,), (1,)), ((), ())), preferred_element_type=F32)


def _split_bf16(x):
    hi = x.astype(BF16)
    lo = (x - hi.astype(F32)).astype(BF16)
    return hi, lo


def _split3(x):
    p1 = x.astype(BF16)
    r1 = x - p1.astype(F32)
    p2 = r1.astype(BF16)
    p3 = (r1 - p2.astype(F32)).astype(BF16)
    return p1, p2, p3


def _dot_f32(a, b, dot=_dot):
    a1, a2, a3 = a
    b1, b2, b3 = b
    return ((dot(a1, b3) + dot(a2, b2) + dot(a3, b1)) + (dot(a1, b2) + dot(a2, b1))) + dot(a1, b1)


def _cat6_lhs(x, axis):
    r1 = x - x.astype(BF16).astype(F32)
    r2 = r1 - r1.astype(BF16).astype(F32)
    return jnp.concatenate([x, x, r1, x, r1, r2], axis=axis).astype(BF16)


def _cat6_rhs(x, axis):
    p1, p2, p3 = _split3(x)
    return jnp.concatenate([p1, p2, p1, p3, p2, p1], axis=axis)


def _rms(x, g):
    return x * lax.rsqrt(jnp.mean(x * x, axis=-1, keepdims=True) + EPS) * g


def _head_rms(x, ones_bd, g):
    hi, lo = _split_bf16(x * x)
    ss = _dot(hi, ones_bd) + _dot(lo, ones_bd)
    return x * lax.rsqrt(ss * (1.0 / HEAD_DIM) + EPS) * g


def _rope(x, cos, sin_next, sin_prev):
    w = x.shape[-1]
    half = ROT_DIM // 2
    return x * cos + pltpu.roll(x, w - half, 1) * sin_next + pltpu.roll(x, half, 1) * sin_prev


def _rope_tables(tab, reps):
    cos = tab[:, 0:LANES]
    sa = tab[:, LANES:2 * LANES]
    sb = tab[:, 2 * LANES:3 * LANES]
    if reps > 1:
        cos = jnp.concatenate([cos] * reps, axis=1)
        sa = jnp.concatenate([sa] * reps, axis=1)
        sb = jnp.concatenate([sb] * reps, axis=1)
    return cos, sa, sb


def _rope_tail(zt, tab):
    cos, sa, sb = _rope_tables(tab, 1)
    lane = lax.broadcasted_iota(I32, zt.shape, 1)
    first = lane < IDX_DIM
    return _rope(zt, jnp.where(first, cos, 1.0), jnp.where(first, sa, 0.0), jnp.where(first, sb, 0.0))


def _gelu_tanh(x):
    c = float(np.sqrt(2.0 / np.pi))
    return x * (0.5 * (1.0 + jnp.tanh(c * (x + 0.044715 * (x * x * x)))))


def _softplus(x):
    return jnp.maximum(x, 0.0) + jnp.log1p(jnp.exp(-jnp.abs(x)))


def _project_in(x, sh1, sc1, ln1, w_main_t, w_idx_t_parts):
    h = _rms(x, ln1) * (1.0 + sc1) + sh1
    hp = _split3(h)
    return _dot_nt(hp[0], w_main_t), _dot_f32(hp, w_idx_t_parts, _dot_nt)


def _lru_gates(xc, wa, ba, wx, bx, lam):
    xcb = xc.astype(BF16)
    r = jax.nn.sigmoid(_dot(xcb, wa) + ba)
    gi = jax.nn.sigmoid(_dot(xcb, wx) + bx)
    log_a = (-LRU_C * _softplus(-lam)) * r
    a = jnp.exp(log_a)
    y = jnp.tanh(-log_a) * (1.0 + a * a)
    inp = jnp.where(y > 0.0, y * lax.rsqrt(y), 0.0) * gi * xc
    return a, inp


def _qkv_inputs(z, tab, qg, kg, ones_q, ones_k):
    cq, saq, sbq = _rope_tables(tab, ATT_W // LANES)
    ck, sak, sbk = _rope_tables(tab, KV_W // LANES)
    q = _rope(_head_rms(z[:, COL_Q:COL_K], ones_q, qg), cq, saq, sbq) * (HEAD_DIM ** -0.5 * LOG2E)
    k = _rope(_head_rms(z[:, COL_K:COL_V], ones_k, kg), ck, sak, sbk)
    return q, k, z[:, COL_V:COL_IQ]


def _index_inputs(zi, tab):
    ci, sai, sbi = _rope_tables(tab, IDX_W // LANES)
    return _rope(zi[:, 0:IDX_W], ci, sai, sbi), _rope_tail(zi[:, IDX_W:IDXP_W], tab)


def _ada_kernel(c_ref, w_ref, b_ref, o_ref):
    c = c_ref[...]
    s = c * jax.nn.sigmoid(c)
    o_ref[...] = _dot_f32(_split3(s), _split3(w_ref[...])) + b_ref[...]


def _ada(c_all, ada_w, ada_b):
    rows = c_all.shape[0]
    n = ada_w.shape[1]
    return pl.pallas_call(
        _ada_kernel,
        grid=(n // ADA_COL_TILE,),
        in_specs=[
            pl.BlockSpec((rows, D_MODEL), lambda j: (0, 0)),
            pl.BlockSpec((D_MODEL, ADA_COL_TILE), lambda j: (0, j)),
            pl.BlockSpec((1, ADA_COL_TILE), lambda j: (0, j)),
        ],
        out_specs=pl.BlockSpec((rows, ADA_COL_TILE), lambda j: (0, j)),
        out_shape=jax.ShapeDtypeStruct((rows, n), F32),
        compiler_params=pltpu.CompilerParams(dimension_semantics=("arbitrary",), vmem_limit_bytes=VMEM_LIMIT),
        name="ada_modulation",
    )(c_all, ada_w, ada_b)


def _prompt_stage_kernel(x_ref, mod_ref, ln1_ref, wmain_ref, widx_ref, convw_ref, convb_ref,
                         wa_ref, ba_ref, wx_ref, bx_ref, lam_ref, qg_ref, kg_ref, gnl_ref,
                         onesq_ref, onesk_ref, rope_ref,
                         na_ref, q_ref, kt_ref, vt_ref, ikt_ref, ktb_ref, vb_ref,
                         ikcat_ref, iqcat_ref, iw_ref, h_ref, conv_ref,
                         xr_buf, hcarry, widx_parts):
    tt = x_ref.shape[1]
    t = pl.program_id(1)

    @pl.when((t == 0) & (pl.program_id(0) == 0))
    def _():
        for j, part in enumerate(_split3(widx_ref[...])):
            widx_parts[j] = part

    @pl.when(t == 0)
    def _():
        xr_buf[0:SUBLANES, :] = jnp.zeros((SUBLANES, LRU_W), F32)
        hcarry[...] = jnp.zeros_like(hcarry)

    mod = mod_ref[0]
    h1, h2, h3 = _split3(_rms(x_ref[0], ln1_ref[...]) * (1.0 + mod[:, D_MODEL:2 * D_MODEL]) + mod[:, 0:D_MODEL])
    z = _dot_nt(h1, wmain_ref[...])
    zi_small = _dot_nt(h1, widx_parts[2]) + _dot_nt(h2, widx_parts[1])

    xr = z[:, COL_XR:COL_GR]
    xr_buf[SUBLANES:SUBLANES + tt, :] = xr
    convw = convw_ref[...]
    xc = convb_ref[...]
    for i in range(CONV_W):
        xc = xc + xr_buf[pl.ds(SUBLANES - (CONV_W - 1) + i, tt), :] * convw[i:i + 1, :]
    conv_ref[0] = xr_buf[pl.ds(SUBLANES + tt - (CONV_W - 1), CONV_W - 1), :]
    xr_buf[0:SUBLANES, :] = xr_buf[tt:tt + SUBLANES, :]

    zi_small = zi_small + _dot_nt(h3, widx_parts[0])
    zi_mid = _dot_nt(h1, widx_parts[1])
    a, b = _lru_gates(xc, wa_ref[...], ba_ref[...], wx_ref[...], bx_ref[...], lam_ref[...])
    zi_mid = zi_mid + _dot_nt(h2, widx_parts[0])

    q, k, v = _qkv_inputs(z, rope_ref[...], qg_ref[...], kg_ref[...], onesq_ref[...], onesk_ref[...])
    for h in range(N_HEADS):
        q_ref[0, h] = q[:, h * HEAD_DIM:(h + 1) * HEAD_DIM].astype(BF16)
    kt = jnp.transpose(k)
    vt = jnp.transpose(v)
    kt_ref[0] = kt.reshape(N_KV_HEADS, HEAD_DIM, tt)
    vt_ref[0] = vt.reshape(N_KV_HEADS, HEAD_DIM, tt)
    ktb_ref[0] = kt.astype(BF16).reshape(N_KV_HEADS, HEAD_DIM, tt)
    lane = lax.broadcasted_iota(I32, v.shape, 1)
    pad = jnp.where(lane == HEAD_DIM, 1.0, 0.0)
    for n in range(N_KV_HEADS):
        vn = v if n == 0 else pltpu.roll(v, KV_W - n * HEAD_DIM, 1)
        vb_ref[0, n] = jnp.where(lane < HEAD_DIM, vn, pad).astype(BF16)

    zi_big = _dot_nt(h1, widx_parts[0])
    row = lax.broadcasted_iota(I32, a.shape, 0) % SUBLANES
    d = 1
    while d < SUBLANES:
        keep = row >= d
        a_s = jnp.where(keep, pltpu.roll(a, d, 0), 1.0)
        b_s = jnp.where(keep, pltpu.roll(b, d, 0), 0.0)
        b = a * b_s + b
        a = a * a_s
        d *= 2
    h_in = hcarry[...]
    groups = []
    for g in range(tt // SUBLANES):
        rows = slice(g * SUBLANES, (g + 1) * SUBLANES)
        hg = a[rows] * h_in + b[rows]
        h_in = hg[SUBLANES - 1:SUBLANES, :]
        groups.append(hg)
    hr = jnp.concatenate(groups, axis=0)
    hcarry[...] = h_in
    h_ref[0] = h_in

    na_ref[0] = _rms(_gelu_tanh(z[:, COL_GR:COL_Q]) * hr, gnl_ref[...]).astype(BF16)

    zi = (zi_small + zi_mid) + zi_big
    iq, tail = _index_inputs(zi, rope_ref[...])
    ikt = jnp.transpose(tail)[0:IDX_DIM, :]
    ikt_ref[0] = ikt
    ikcat_ref[0] = _cat6_rhs(ikt, 0)
    for h in range(IDX_HEADS):
        iqcat_ref[0, h] = _cat6_lhs(iq[:, h * IDX_DIM:(h + 1) * IDX_DIM], 1)
    iw_ref[0] = zi[:, IDX_W:IDXP_W]


def _const_spec(shape):
    nd = len(shape)
    return pl.BlockSpec(shape, lambda *_: (0,) * nd)


def _prompt_stage(x, mod_p, p, rope_tab):
    bsz, t, _ = x.shape
    tt = PROMPT_ROW_TILE

    def rows(w):
        return pl.BlockSpec((1, tt, w), lambda b, i: (b, i, 0))

    in_specs = [
        rows(D_MODEL),
        pl.BlockSpec((1, 1, 6 * D_MODEL), lambda b, i: (b, 0, 0)),
        _const_spec((1, D_MODEL)),
        _const_spec((MAIN_W, D_MODEL)),
        _const_spec((IDXP_W, D_MODEL)),
        _const_spec((CONV_W, LRU_W)),
        _const_spec((1, LRU_W)),
        _const_spec((LRU_W, LRU_W)),
        _const_spec((1, LRU_W)),
        _const_spec((LRU_W, LRU_W)),
        _const_spec((1, LRU_W)),
        _const_spec((1, LRU_W)),
        _const_spec((1, ATT_W)),
        _const_spec((1, KV_W)),
        _const_spec((1, LRU_W)),
        _const_spec((ATT_W, ATT_W)),
        _const_spec((KV_W, KV_W)),
        pl.BlockSpec((tt, 3 * LANES), lambda b, i: (i, 0)),
    ]

    def heads(n, w):
        return pl.BlockSpec((1, n, tt, w), lambda b, i: (b, 0, i, 0))

    def heads_t(n):
        return pl.BlockSpec((1, n, HEAD_DIM, tt), lambda b, i: (b, 0, 0, i))

    def feat_t(w):
        return pl.BlockSpec((1, w, tt), lambda b, i: (b, 0, i))

    out_specs = [
        rows(LRU_W), heads(N_HEADS, HEAD_DIM), heads_t(N_KV_HEADS), heads_t(N_KV_HEADS), feat_t(IDX_DIM),
        heads_t(N_KV_HEADS), heads(N_KV_HEADS, KV_W), feat_t(IDX_CAT), heads(IDX_HEADS, IDX_CAT), rows(TAIL_W),
        pl.BlockSpec((1, 1, LRU_W), lambda b, i: (b, 0, 0)),
        pl.BlockSpec((1, CONV_W - 1, LRU_W), lambda b, i: (b, 0, 0)),
    ]
    out_shape = [
        jax.ShapeDtypeStruct((bsz, t, LRU_W), BF16),
        jax.ShapeDtypeStruct((bsz, N_HEADS, t, HEAD_DIM), BF16),
        jax.ShapeDtypeStruct((bsz, N_KV_HEADS, HEAD_DIM, t), F32),
        jax.ShapeDtypeStruct((bsz, N_KV_HEADS, HEAD_DIM, t), F32),
        jax.ShapeDtypeStruct((bsz, IDX_DIM, t), F32),
        jax.ShapeDtypeStruct((bsz, N_KV_HEADS, HEAD_DIM, t), BF16),
        jax.ShapeDtypeStruct((bsz, N_KV_HEADS, t, KV_W), BF16),
        jax.ShapeDtypeStruct((bsz, IDX_CAT, t), BF16),
        jax.ShapeDtypeStruct((bsz, IDX_HEADS, t, IDX_CAT), BF16),
        jax.ShapeDtypeStruct((bsz, t, TAIL_W), F32),
        jax.ShapeDtypeStruct((bsz, 1, LRU_W), F32),
        jax.ShapeDtypeStruct((bsz, CONV_W - 1, LRU_W), F32),
    ]
    return pl.pallas_call(
        _prompt_stage_kernel,
        grid=(bsz, t // tt),
        in_specs=in_specs,
        out_specs=out_specs,
        out_shape=out_shape,
        scratch_shapes=[pltpu.VMEM((tt + SUBLANES, LRU_W), F32), pltpu.VMEM((1, LRU_W), F32),
                        pltpu.VMEM((3, IDXP_W, D_MODEL), BF16)],
        compiler_params=pltpu.CompilerParams(dimension_semantics=("arbitrary", "arbitrary"),
                                             vmem_limit_bytes=VMEM_LIMIT),
        name="prompt_stage",
    )(x, mod_p, p["ln1"], p["w_main"], p["w_idx"], p["conv_w"], p["conv_b"], p["wa"], p["ba"],
      p["wx"], p["bx"], p["lam"], p["qg"], p["kg"], p["gnl"], p["ones_q"], p["ones_k"], rope_tab)


def _sortable_key(score, fold_zero=True):
    if fold_zero:
        score = jnp.where(score == 0.0, 0.0, score)
    bits = lax.bitcast_convert_type(score, I32)
    return jnp.where(bits < 0, bits ^ jnp.int32(0x7FFFFFFF), bits)


def _count_ge16(ref, c, strict=False):
    c16 = c.astype(HALF_T)
    hit = ref[...] > c16 if strict else ref[...] >= c16
    ones = jnp.where(hit, jnp.ones((), HALF_T), jnp.zeros((), HALF_T))
    acc = ones[:, 0:LANES]
    for j in range(1, ref.shape[1] // LANES):
        acc = acc + ones[:, j * LANES:(j + 1) * LANES]
    return jnp.sum(acc.astype(F32), axis=1, keepdims=True)


def _bitwise_max16(count_ge, k, n_all):
    def body(i, carry):
        t, ct = carry
        c = t + lax.shift_left(jnp.int32(1), jnp.int32(15) - i)
        cn = count_ge(c)
        ok = cn >= k
        return jnp.where(ok, c, t), jnp.where(ok, cn, ct)

    init = (jnp.full(k.shape, HALF_MIN, I32), jnp.full(k.shape, float(n_all), F32))
    return lax.fori_loop(0, 16, body, init, unroll=True)


def _kth_largest(key_ref, hi_ref, lo_ref, k, extra=None):
    key = key_ref[...]
    hi_ref[...] = lax.shift_right_arithmetic(key, 16).astype(HALF_T)
    lo_ref[...] = ((key & 0xFFFF) + HALF_MIN).astype(HALF_T)
    kf = jnp.full((key_ref.shape[0], 1), k, F32)
    n_all = key_ref.shape[1] + (0 if extra is None else 1)
    if extra is not None:
        hi_x = lax.shift_right_arithmetic(extra, 16)
        lo_x = (extra & 0xFFFF) + HALF_MIN

    def count_hi(c):
        cnt = _count_ge16(hi_ref, c)
        return cnt if extra is None else cnt + jnp.where(hi_x >= c, 1.0, 0.0)

    t_hi, n_bucket_up = _bitwise_max16(count_hi, kf, n_all)
    above = _count_ge16(hi_ref, t_hi, strict=True)
    if extra is not None:
        above = above + jnp.where(hi_x > t_hi, 1.0, 0.0)
    k_lo = kf - above
    lo_ref[...] = jnp.where(hi_ref[...] == t_hi.astype(HALF_T), lo_ref[...], jnp.full((), HALF_MIN, HALF_T))
    if extra is not None:
        lo_x = jnp.where(hi_x == t_hi, lo_x, HALF_MIN)

    def count_lo(c):
        cnt = _count_ge16(lo_ref, c)
        return cnt if extra is None else cnt + jnp.where(lo_x >= c, 1.0, 0.0)

    t_lo, n_lo = _bitwise_max16(count_lo, k_lo, n_all)
    n_ge = jnp.where(t_lo > HALF_MIN, above + n_lo, n_bucket_up)
    return t_hi * 65536 + (t_lo - HALF_MIN), n_ge


def _selection_bias(key_ref, hi_ref, lo_ref, bias_ref, tri, k, extra=None):
    thr, n_ge = _kth_largest(key_ref, hi_ref, lo_ref, k, extra)
    has_ties = jnp.max(jnp.where(n_ge > k, 1.0, 0.0)) > 0.0

    @pl.when(jnp.logical_not(has_ties))
    def _():
        bias_ref[...] = jnp.where(key_ref[...] >= thr, 0.0, NEG_BIG)

    @pl.when(has_ties)
    def _():
        n_gt = jnp.sum(jnp.where(key_ref[...] > thr, 1.0, 0.0), axis=1, keepdims=True)
        if extra is not None:
            n_gt = n_gt + jnp.where(extra > thr, 1.0, 0.0)
        need = k - n_gt
        offset = jnp.zeros_like(need)
        for c in range(key_ref.shape[1] // TRI_W):
            kc = key_ref[:, c * TRI_W:(c + 1) * TRI_W]
            eq = kc == thr
            e = jnp.where(eq, 1.0, 0.0)
            incl = _dot(e.astype(BF16), tri)
            rank = incl - e + offset
            tie = jnp.where(rank < need, 0.0, NEG_BIG)
            bias_ref[:, c * TRI_W:(c + 1) * TRI_W] = jnp.where(kc > thr, 0.0, jnp.where(eq, tie, NEG_BIG))
            offset = offset + incl[:, TRI_W - 1:TRI_W]

    if extra is None:
        return None
    n_gt = jnp.sum(jnp.where(key_ref[...] > thr, 1.0, 0.0), axis=1, keepdims=True) + jnp.where(extra > thr, 1.0, 0.0)
    n_eq_main = n_ge - n_gt - jnp.where(extra == thr, 1.0, 0.0)
    tie = jnp.where(n_eq_main < k - n_gt, 0.0, NEG_BIG)
    return jnp.where(extra > thr, 0.0, jnp.where(extra == thr, tie, NEG_BIG))


def _prompt_attn_kernel(q_ref, iq_ref, iw_ref, kt_ref, v_ref, ikt_ref, gn_ref, tri_ref, o_ref,
                        key_ref, hi_ref, lo_ref, bias_ref):
    nb, qb = o_ref.shape[0], o_ref.shape[1]
    n_keys = bias_ref.shape[1]
    past = n_keys - qb
    causal = lax.broadcasted_iota(I32, (qb, qb), 1) <= lax.broadcasted_iota(I32, (qb, qb), 0)
    if n_keys > TOPK_MAX:
        for b in range(nb):
            rows = slice(b * qb, (b + 1) * qb)
            iw = iw_ref[b]
            ikt = ikt_ref[b]
            score = jnp.zeros((qb, n_keys), F32)
            s_next = _dot(iq_ref[b, 0], ikt)
            for h in range(IDX_HEADS):
                s, s_next = s_next, (_dot(iq_ref[b, h + 1], ikt) if h + 1 < IDX_HEADS else None)
                score = score + iw[:, IW_LANE + h:IW_LANE + h + 1] * jnp.maximum(s, 0.0)
            key = _sortable_key(score, fold_zero=False)
            key_ref[rows, 0:past] = key[:, 0:past]
            key_ref[rows, past:n_keys] = jnp.where(causal, key[:, past:n_keys], INT_MIN)
        _selection_bias(key_ref, hi_ref, lo_ref, bias_ref, tri_ref[...], float(TOPK_MAX))
        for b in range(nb):
            rows = slice(b * qb, (b + 1) * qb)
            bias_ref[rows, past:n_keys] = jnp.where(causal, bias_ref[rows, past:n_keys], NEG_BIG)
    else:
        for b in range(nb):
            bias_ref[b * qb:(b + 1) * qb, :] = jnp.where(causal, 0.0, NEG_BIG)

    def qk(i):
        b, h = divmod(i, N_HEADS)
        return _dot(q_ref[b, h], kt_ref[b, h // KV_GROUP]) + bias_ref[b * qb:(b + 1) * qb, :]

    logits_next = qk(0)
    for b in range(nb):
        outs = []
        for h in range(N_HEADS):
            i = b * N_HEADS + h
            logits, logits_next = logits_next, (qk(i + 1) if i + 1 < nb * N_HEADS else None)
            m = jnp.max(logits, axis=1, keepdims=True)
            p = jnp.exp2(logits - m).astype(BF16)
            pv = _dot(p, v_ref[b, h // KV_GROUP])
            outs.append(pv[:, 0:HEAD_DIM] / pv[:, HEAD_DIM:HEAD_DIM + 1])
        o_ref[b] = _rms(jnp.concatenate(outs, axis=1), gn_ref[...]).astype(BF16)


def _prompt_attention_block(qh, iqcat, iw, ktb, vb, ikcat, gn_att, tri, q_block):
    bsz, _, t, _ = qh.shape
    qb = Q_BLOCK
    n_keys = (q_block + 1) * qb
    nb = max(1, min(bsz, SEARCH_KEYS_PER_STEP // n_keys))
    while bsz % nb:
        nb -= 1
    rows = nb * qb

    def qheads(n, w):
        return pl.BlockSpec((nb, n, qb, w), lambda b: (b, 0, q_block, 0))

    return pl.pallas_call(
        _prompt_attn_kernel,
        grid=(bsz // nb,),
        in_specs=[qheads(N_HEADS, HEAD_DIM), qheads(IDX_HEADS, IDX_CAT),
                  pl.BlockSpec((nb, qb, TAIL_W), lambda b: (b, q_block, 0)),
                  pl.BlockSpec((nb, N_KV_HEADS, HEAD_DIM, n_keys), lambda b: (b, 0, 0, 0)),
                  pl.BlockSpec((nb, N_KV_HEADS, n_keys, KV_W), lambda b: (b, 0, 0, 0)),
                  pl.BlockSpec((nb, IDX_CAT, n_keys), lambda b: (b, 0, 0)),
                  _const_spec((1, ATT_W)), _const_spec((TRI_W, TRI_W))],
        out_specs=pl.BlockSpec((nb, qb, ATT_W), lambda b: (b, 0, 0)),
        out_shape=jax.ShapeDtypeStruct((bsz, qb, ATT_W), BF16),
        scratch_shapes=[pltpu.VMEM((rows, n_keys), I32), pltpu.VMEM((rows, n_keys), HALF_T),
                        pltpu.VMEM((rows, n_keys), HALF_T), pltpu.VMEM((rows, n_keys), F32)],
        compiler_params=pltpu.CompilerParams(dimension_semantics=("arbitrary",), vmem_limit_bytes=VMEM_LIMIT),
        name=f"prompt_attention_{q_block}",
    )(qh, iqcat, iw, ktb, vb, ikcat, gn_att, tri)


def _prompt_attention(qh, iqcat, iw, ktb, vb, ikcat, gn_att, tri):
    t = qh.shape[2]
    blocks = [_prompt_attention_block(qh, iqcat, iw, ktb, vb, ikcat, gn_att, tri, i) for i in range(t // Q_BLOCK)]
    return jnp.concatenate(blocks, axis=1)


def _post_kernel(x_ref, na_ref, nb_ref, mod_ref, woa_ref, wob_ref, ln2_ref, wg_ref, wu_ref, wd_ref,
                 y_ref, *, shared_mod):
    mod = mod_ref[0] if shared_mod else mod_ref[...]
    g1 = mod[:, 2 * D_MODEL:3 * D_MODEL]
    sh2 = mod[:, 3 * D_MODEL:4 * D_MODEL]
    sc2 = mod[:, 4 * D_MODEL:5 * D_MODEL]
    g2 = mod[:, 5 * D_MODEL:6 * D_MODEL]
    x = x_ref[...]
    mix = _dot(na_ref[...], woa_ref[...]) + _dot(nb_ref[...], wob_ref[...])
    x1 = x + g1 * mix
    h2 = (_rms(x1, ln2_ref[...]) * (1.0 + sc2) + sh2).astype(BF16)
    def gate_up(c):
        return _dot(h2, wg_ref[:, c * FF_CHUNK:(c + 1) * FF_CHUNK]), _dot(h2, wu_ref[:, c * FF_CHUNK:(c + 1) * FF_CHUNK])

    n_chunks = D_FF // FF_CHUNK
    ff = jnp.zeros_like(x1)
    gu_next = gate_up(0)
    for c in range(n_chunks):
        (g, u), gu_next = gu_next, (gate_up(c + 1) if c + 1 < n_chunks else None)
        act = (g * jax.nn.sigmoid(g) * u).astype(BF16)
        ff = ff + _dot(act, wd_ref[c * FF_CHUNK:(c + 1) * FF_CHUNK, :])
    y_ref[...] = x1 + g2 * ff


def _post(x, na, nb, mod, p, rows_per_mod, tile):
    n = x.shape[0]
    shared = rows_per_mod > 1
    if shared:
        mod_spec = pl.BlockSpec((1, 1, 6 * D_MODEL), lambda i: (i * tile // rows_per_mod, 0, 0))
    else:
        mod_spec = pl.BlockSpec((tile, 6 * D_MODEL), lambda i: (i, 0))

    def rows(w):
        return pl.BlockSpec((tile, w), lambda i: (i, 0))

    def weight(shape):
        return pl.BlockSpec(shape, lambda i: (0, 0), pipeline_mode=pl.Buffered(1))

    return pl.pallas_call(
        functools.partial(_post_kernel, shared_mod=shared),
        grid=(n // tile,),
        in_specs=[rows(D_MODEL), rows(LRU_W), rows(ATT_W), mod_spec,
                  weight((LRU_W, D_MODEL)), weight((ATT_W, D_MODEL)), weight((1, D_MODEL)),
                  weight((D_MODEL, D_FF)), weight((D_MODEL, D_FF)), weight((D_FF, D_MODEL))],
        out_specs=rows(D_MODEL),
        out_shape=jax.ShapeDtypeStruct((n, D_MODEL), F32),
        compiler_params=pltpu.CompilerParams(dimension_semantics=("arbitrary",), vmem_limit_bytes=VMEM_LIMIT),
        name="post_ffn",
    )(x, na, nb, mod, p["wo_a"], p["wo_b"], p["ln2"], p["wg"], p["wu"], p["wd"])


def _sample_stage_kernel(x_ref, mod_ref, ln1_ref, wmain_ref, widx_ref, convw_ref, convb_ref,
                         wa_ref, ba_ref, wx_ref, bx_ref, lam_ref, qg_ref, kg_ref, gnl_ref,
                         onesq_ref, onesk_ref, rope_ref, sconv_ref, sh_ref,
                         na_ref, q_ref, k_ref, v_ref, ik_ref, iq_ref, iw_ref, h_ref, conv_ref):
    mod = mod_ref[...]
    z, zi = _project_in(x_ref[...], mod[:, 0:D_MODEL], mod[:, D_MODEL:2 * D_MODEL], ln1_ref[...],
                        wmain_ref[...], _split3(widx_ref[...]))
    xr = z[:, COL_XR:COL_GR]
    convw = convw_ref[...]
    sconv = sconv_ref[...]
    xc = convb_ref[...]
    for i in range(CONV_W - 1):
        xc = xc + sconv[:, i * LRU_W:(i + 1) * LRU_W] * convw[i:i + 1, :]
    xc = xc + xr * convw[CONV_W - 1:CONV_W, :]
    conv_ref[:, 0:(CONV_W - 2) * LRU_W] = sconv[:, LRU_W:(CONV_W - 1) * LRU_W]
    conv_ref[:, (CONV_W - 2) * LRU_W:(CONV_W - 1) * LRU_W] = xr

    a, b = _lru_gates(xc, wa_ref[...], ba_ref[...], wx_ref[...], bx_ref[...], lam_ref[...])
    hr = a * sh_ref[...] + b
    h_ref[...] = hr
    na_ref[...] = _rms(_gelu_tanh(z[:, COL_GR:COL_Q]) * hr, gnl_ref[...]).astype(BF16)

    q, k, v = _qkv_inputs(z, rope_ref[...], qg_ref[...], kg_ref[...], onesq_ref[...], onesk_ref[...])
    iq, tail = _index_inputs(zi, rope_ref[...])
    q_ref[...] = q
    k_ref[...] = k
    v_ref[...] = v
    ik_ref[...] = tail[:, 0:IDX_DIM]
    iq_ref[...] = iq
    iw_ref[...] = zi[:, IDX_W:IDXP_W]


def _sample_stage(x, mod_s, p, rope_row, sconv, sh):
    n = x.shape[0]

    def out(w, dt=F32):
        return jax.ShapeDtypeStruct((n, w), dt)

    return pl.pallas_call(
        _sample_stage_kernel,
        out_shape=[out(LRU_W, BF16), out(ATT_W), out(KV_W), out(KV_W), out(IDX_DIM), out(IDX_W),
                   out(TAIL_W), out(LRU_W), out((CONV_W - 1) * LRU_W)],
        compiler_params=pltpu.CompilerParams(vmem_limit_bytes=VMEM_LIMIT),
        name="sample_stage",
    )(x, mod_s, p["ln1"], p["w_main"], p["w_idx"], p["conv_w"], p["conv_b"], p["wa"], p["ba"],
      p["wx"], p["bx"], p["lam"], p["qg"], p["kg"], p["gnl"], p["ones_q"], p["ones_k"], rope_row, sconv, sh)


def _page_copy(cache_ref, buf_ref, sem, page, slot, j, pg):
    return pltpu.make_async_copy(cache_ref.at[page], buf_ref.at[slot, j, pg], sem)


def _paged_prefetch(pt_ref, caches, bufs, sems, group, n_pages):
    s = pl.program_id(0)
    slot = s % 2

    def start(grp, half):
        for cache_ref, buf_ref, sem in zip(caches, bufs, sems):
            for j in range(group):
                for pg in range(n_pages):
                    page = pt_ref[grp * group + j, pg]
                    _page_copy(cache_ref, buf_ref, sem.at[half], page, half, j, pg).start()

    @pl.when(s == 0)
    def _():
        start(0, 0)

    @pl.when(s + 1 < pl.num_programs(0))
    def _():
        start(s + 1, 1 - slot)

    for cache_ref, buf_ref, sem in zip(caches, bufs, sems):
        for j in range(group):
            for pg in range(n_pages):
                _page_copy(cache_ref, buf_ref, sem.at[slot], 0, slot, j, pg).wait()
    return slot


def _sample_score_kernel(pt_ref, iq_ref, iw_ref, ik_hbm, o_ref, ik_buf, sem, *, n_pages):
    group = iq_ref.shape[0]
    slot = _paged_prefetch(pt_ref, (ik_hbm,), (ik_buf,), (sem,), group, n_pages)
    iqcat = [_cat6_lhs(iq_ref[j], 1) for j in range(group)]
    for pg in range(n_pages):
        for j in range(group):
            s = _dot(iqcat[j], _cat6_rhs(ik_buf[slot, j, pg], 0))
            o_ref[j, :, pg * PAGE_SIZE:(pg + 1) * PAGE_SIZE] = jnp.sum(iw_ref[j] * jnp.maximum(s, 0.0), axis=0,
                                                                        keepdims=True)


def _sample_scores(page_table, iq3, iw3, cache_ik):
    n, n_pages = page_table.shape
    g = SCORE_GROUP
    grid_spec = pltpu.PrefetchScalarGridSpec(
        num_scalar_prefetch=1,
        grid=(n // g,),
        in_specs=[pl.BlockSpec((g, IDX_HEADS, IDX_DIM), lambda s, pt: (s, 0, 0)),
                  pl.BlockSpec((g, IDX_HEADS, 1), lambda s, pt: (s, 0, 0)),
                  pl.BlockSpec(memory_space=pl.ANY)],
        out_specs=pl.BlockSpec((g, 1, n_pages * PAGE_SIZE), lambda s, pt: (s, 0, 0)),
        scratch_shapes=[pltpu.VMEM((2, g, n_pages, IDX_DIM, PAGE_SIZE), F32), pltpu.SemaphoreType.DMA((2,))],
    )
    return pl.pallas_call(
        functools.partial(_sample_score_kernel, n_pages=n_pages),
        grid_spec=grid_spec,
        out_shape=jax.ShapeDtypeStruct((n, 1, n_pages * PAGE_SIZE), F32),
        compiler_params=pltpu.CompilerParams(dimension_semantics=("arbitrary",), vmem_limit_bytes=VMEM_LIMIT),
        name="sample_scores",
    )(page_table, iq3, iw3, cache_ik)


def _sample_select_kernel(score_ref, iq_ref, ik_ref, iw_ref, tri_ref, bias_ref, bias_new_ref,
                          key_ref, hi_ref, lo_ref):
    iq = iq_ref[...]
    ik = ik_ref[...]
    iw = iw_ref[...]
    new = jnp.zeros((iq.shape[0], 1), F32)
    for h in range(IDX_HEADS):
        s = jnp.sum(iq[:, h * IDX_DIM:(h + 1) * IDX_DIM] * ik, axis=1, keepdims=True)
        new = new + iw[:, IW_LANE + h:IW_LANE + h + 1] * jnp.maximum(s, 0.0)
    key_ref[...] = _sortable_key(score_ref[...])
    b_new = _selection_bias(key_ref, hi_ref, lo_ref, bias_ref, tri_ref[...], float(TOPK_MAX),
                            extra=_sortable_key(new))
    bias_new_ref[...] = jnp.broadcast_to(b_new, bias_new_ref.shape)


def _sample_select(score, iq, ik, iw, tri):
    n, n_keys = score.shape
    return pl.pallas_call(
        _sample_select_kernel,
        out_shape=[jax.ShapeDtypeStruct((n, n_keys), F32), jax.ShapeDtypeStruct((n, LANES), F32)],
        scratch_shapes=[pltpu.VMEM((n, n_keys), I32), pltpu.VMEM((n, n_keys), HALF_T), pltpu.VMEM((n, n_keys), HALF_T)],
        compiler_params=pltpu.CompilerParams(vmem_limit_bytes=VMEM_LIMIT),
        name="sample_select",
    )(score, iq, ik, iw, tri)


def _sample_attn_kernel(pt_ref, q_ref, kn_ref, vn_ref, bias_ref, bnew_ref, gn_ref, k_hbm, v_hbm, o_ref,
                        k_buf, v_buf, k_sem, v_sem, *, n_pages):
    group = q_ref.shape[0]
    seqs = range(group)
    slot = _paged_prefetch(pt_ref, (k_hbm, v_hbm), (k_buf, v_buf), (k_sem, v_sem), group, n_pages)
    row = lax.broadcasted_iota(I32, (N_HEADS, HEAD_DIM), 0)
    first = row < KV_GROUP
    q = [q_ref[j] for j in seqs]
    q2 = [jnp.concatenate([jnp.where(first, q[j], 0.0), jnp.where(first, 0.0, q[j])], axis=1).astype(BF16)
          for j in seqs]
    logits = [[None] * n_pages for _ in seqs]
    for pg in range(n_pages):
        for j in seqs:
            kp = k_buf[slot, j, pg].astype(BF16).reshape(KV_W, PAGE_SIZE)
            logits[j][pg] = _dot(q2[j], kp) + bias_ref[j][:, pg * PAGE_SIZE:(pg + 1) * PAGE_SIZE]
    m, p_new, den, acc, v_sel = [], [], [], [], []
    for j in seqs:
        kn = kn_ref[j]
        vn = vn_ref[j]
        k_sel = jnp.where(first, kn[:, 0:HEAD_DIM], kn[:, HEAD_DIM:KV_W])
        v_sel.append(jnp.where(first, vn[:, 0:HEAD_DIM], vn[:, HEAD_DIM:KV_W]))
        l_new = jnp.sum(q[j] * k_sel, axis=1, keepdims=True) + bnew_ref[j][:, 0:1]
        mj = l_new
        for l in logits[j]:
            mj = jnp.maximum(mj, jnp.max(l, axis=1, keepdims=True))
        m.append(mj)
        p_new.append(jnp.exp2(l_new - mj))
        den.append(p_new[j])
        acc.append(jnp.zeros((N_HEADS, KV_W), F32))
    for pg in range(n_pages):
        for j in seqs:
            p = jnp.exp2(logits[j][pg] - m[j])
            den[j] = den[j] + jnp.sum(p, axis=1, keepdims=True)
            vp = v_buf[slot, j, pg].astype(BF16).reshape(KV_W, PAGE_SIZE)
            acc[j] = acc[j] + _dot_nt(p.astype(BF16), vp)
    for j in seqs:
        out = (p_new[j] * v_sel[j] + jnp.where(first, acc[j][:, 0:HEAD_DIM], acc[j][:, HEAD_DIM:KV_W])) / den[j]
        ms = jnp.sum(jnp.sum(out * out, axis=1, keepdims=True), axis=0, keepdims=True) * (1.0 / ATT_W)
        o_ref[j] = (out * lax.rsqrt(ms + EPS) * gn_ref[...]).astype(BF16)


def _sample_attention(page_table, q3, k_new, v_new, bias, bias_new, gn8, cache_k, cache_v):
    n, n_pages = page_table.shape
    n_keys = n_pages * PAGE_SIZE

    g = DECODE_GROUP

    def per_seq(shape):
        return pl.BlockSpec((g,) + shape, lambda s, pt: (s, 0, 0))

    page_buf = pltpu.VMEM((2, g, n_pages, N_KV_HEADS, HEAD_DIM, PAGE_SIZE), F32)
    grid_spec = pltpu.PrefetchScalarGridSpec(
        num_scalar_prefetch=1,
        grid=(n // g,),
        in_specs=[per_seq((N_HEADS, HEAD_DIM)), per_seq((1, KV_W)), per_seq((1, KV_W)),
                  per_seq((1, n_keys)), per_seq((1, LANES)),
                  pl.BlockSpec((N_HEADS, HEAD_DIM), lambda s, pt: (0, 0)),
                  pl.BlockSpec(memory_space=pl.ANY), pl.BlockSpec(memory_space=pl.ANY)],
        out_specs=per_seq((N_HEADS, HEAD_DIM)),
        scratch_shapes=[page_buf, page_buf, pltpu.SemaphoreType.DMA((2,)), pltpu.SemaphoreType.DMA((2,))],
    )
    return pl.pallas_call(
        functools.partial(_sample_attn_kernel, n_pages=n_pages),
        grid_spec=grid_spec,
        out_shape=jax.ShapeDtypeStruct((n, N_HEADS, HEAD_DIM), BF16),
        compiler_params=pltpu.CompilerParams(dimension_semantics=("arbitrary",), vmem_limit_bytes=VMEM_LIMIT),
        name="sample_attention",
    )(page_table, q3, k_new, v_new, bias, bias_new, gn8, cache_k, cache_v)


def _rope_table_np(positions):
    half = ROT_DIM // 2
    freq = ROPE_THETA ** (-(np.arange(half, dtype=np.float64) / half))
    ang = np.asarray(positions, np.float64)[:, None] * freq[None, :]
    cos, sin = np.cos(ang), np.sin(ang)
    n = len(positions)
    tab = np.zeros((n, 3, LANES), np.float64)
    tab[:, 0, :] = 1.0
    for base in range(0, LANES, HEAD_DIM):
        tab[:, 0, base:base + half] = cos
        tab[:, 0, base + half:base + ROT_DIM] = cos
        tab[:, 1, base:base + half] = -sin
        tab[:, 2, base + half:base + ROT_DIM] = sin
    return tab.reshape(n, 3 * LANES).astype(np.float32)


def _block_diag(w):
    n, a, b = w.shape
    return jnp.einsum("nij,nm->nimj", w, jnp.eye(n, dtype=w.dtype)).reshape(n * a, n * b)


def _layer_params(l, ln1_g, w_in, conv_w, conv_b, lru_wa, lru_ba, lru_wx, lru_bx, lru_lambda, q_norm_g,
                  k_norm_g, gn_lru_g, gn_att_g, w_out, ln2_g, w_gate, w_up, w_down):
    w_in_t = jnp.transpose(w_in[l])
    w_idx = jnp.pad(w_in_t[COL_IQ:], ((0, IDXP_W - (w_in.shape[2] - COL_IQ)), (0, 0)))
    head_id = np.arange(ATT_W) // HEAD_DIM
    ones_q = (head_id[:, None] == head_id[None, :]).astype(np.float32)
    return {
        "ln1": ln1_g[l][None, :],
        "w_main": w_in_t[:COL_IQ].astype(BF16),
        "w_idx": w_idx,
        "conv_w": conv_w[l],
        "conv_b": conv_b[l][None, :],
        "wa": _block_diag(lru_wa[l]).astype(BF16),
        "ba": lru_ba[l][None, :],
        "wx": _block_diag(lru_wx[l]).astype(BF16),
        "bx": lru_bx[l][None, :],
        "lam": lru_lambda[l][None, :],
        "qg": jnp.tile(q_norm_g[l], N_HEADS)[None, :],
        "kg": jnp.tile(k_norm_g[l], N_KV_HEADS)[None, :],
        "gnl": gn_lru_g[l][None, :],
        "gna": gn_att_g[l][None, :],
        "gna8": gn_att_g[l].reshape(N_HEADS, HEAD_DIM),
        "ones_q": jnp.asarray(ones_q, BF16),
        "ones_k": jnp.asarray(ones_q[:KV_W, :KV_W], BF16),
        "wo_a": w_out[l][:LRU_W].astype(BF16),
        "wo_b": w_out[l][LRU_W:].astype(BF16),
        "ln2": ln2_g[l][None, :],
        "wg": w_gate[l].astype(BF16),
        "wu": w_up[l].astype(BF16),
        "wd": w_down[l].astype(BF16),
    }


def kernel(x_prompt, x_sample, cache_k, cache_v, cache_ik, state_h, state_conv, page_table, c_prompt, c_sample, ada_w, ada_b, ln1_g, w_in, conv_w, conv_b, lru_wa, lru_ba, lru_wx, lru_bx, lru_lambda, q_norm_g, k_norm_g, gn_lru_g, gn_att_g, w_out, ln2_g, w_gate, w_up, w_down):
    bsz, t, _ = x_prompt.shape
    dbsz, dt, _ = x_sample.shape
    depth = ada_w.shape[0]
    n_pages = page_table.shape[1]
    past_len = n_pages * PAGE_SIZE
    assert dt == 1 and t % PROMPT_ROW_TILE == 0 and t % POST_ROW_TILE == 0 and t % Q_BLOCK == 0
    assert t // 4 >= TOPK_MAX and (past_len + dt) // 4 >= TOPK_MAX
    assert dbsz % DECODE_GROUP == 0 and dbsz % SCORE_GROUP == 0 and cache_k.shape[2] == PAGE_SIZE

    rope_p = jnp.asarray(_rope_table_np(np.arange(t)))
    rope_s = jnp.asarray(_rope_table_np(past_len + np.arange(dt)))
    tri = jnp.asarray(np.triu(np.ones((TRI_W, TRI_W), np.float32)), BF16)

    yp = x_prompt.reshape(bsz * t, D_MODEL)
    ys = x_sample.reshape(dbsz, D_MODEL)
    c_all = jnp.concatenate([c_prompt, c_sample], axis=0)
    outs_p, outs_s = [], []
    for l in range(depth):
        p = _layer_params(l, ln1_g, w_in, conv_w, conv_b, lru_wa, lru_ba, lru_wx, lru_bx, lru_lambda,
                          q_norm_g, k_norm_g, gn_lru_g, gn_att_g, w_out, ln2_g, w_gate, w_up, w_down)
        mod = _ada(c_all, ada_w[l], ada_b[l][None, :])
        mod_p = mod[:bsz].reshape(bsz, 1, 6 * D_MODEL)
        mod_s = mod[bsz:]

        (na, qh, kt, vt, ikt, ktb, vb, ikcat, iqcat, iw, h_last, conv_new) = _prompt_stage(
            yp.reshape(bsz, t, D_MODEL), mod_p, p, rope_p)
        nb = _prompt_attention(qh, iqcat, iw, ktb, vb, ikcat, p["gna"], tri)
        yp = _post(yp, na.reshape(bsz * t, LRU_W), nb.reshape(bsz * t, ATT_W), mod_p, p, t, POST_ROW_TILE)
        outs_p.append((jnp.transpose(kt, (0, 3, 1, 2)), jnp.transpose(vt, (0, 3, 1, 2)),
                       jnp.transpose(ikt, (0, 2, 1)), h_last.reshape(bsz, LRU_W), conv_new))

        (na_s, q_s, k_s, v_s, ik_s, iq_s, iw_s, h_s, conv_s) = _sample_stage(
            ys, mod_s, p, rope_s, state_conv[l].reshape(dbsz, (CONV_W - 1) * LRU_W), state_h[l])
        score = _sample_scores(page_table, iq_s.reshape(dbsz, IDX_HEADS, IDX_DIM),
                               iw_s[:, IW_LANE:IW_LANE + IDX_HEADS].reshape(dbsz, IDX_HEADS, 1),
                               jnp.transpose(cache_ik[l], (0, 2, 1)))
        bias, bias_new = _sample_select(score.reshape(dbsz, past_len), iq_s, ik_s, iw_s, tri)
        nb_s = _sample_attention(page_table, q_s.reshape(dbsz, N_HEADS, HEAD_DIM),
                                 k_s.reshape(dbsz, 1, KV_W), v_s.reshape(dbsz, 1, KV_W),
                                 bias.reshape(dbsz, 1, past_len), bias_new.reshape(dbsz, 1, LANES), p["gna8"],
                                 jnp.transpose(cache_k[l], (0, 2, 3, 1)), jnp.transpose(cache_v[l], (0, 2, 3, 1)))
        ys = _post(ys, na_s, nb_s.reshape(dbsz, ATT_W), mod_s, p, 1, dbsz)
        outs_s.append((k_s.reshape(dbsz, dt, N_KV_HEADS, HEAD_DIM), v_s.reshape(dbsz, dt, N_KV_HEADS, HEAD_DIM),
                       ik_s.reshape(dbsz, dt, IDX_DIM), h_s,
                       conv_s.reshape(dbsz, CONV_W - 1, LRU_W)))

    def stack(outs, i):
        return jnp.stack([o[i] for o in outs])

    return (yp.reshape(bsz, t, D_MODEL), ys.reshape(dbsz, dt, D_MODEL),
            stack(outs_p, 0), stack(outs_p, 1), stack(outs_p, 2), stack(outs_p, 3), stack(outs_p, 4),
            stack(outs_s, 0), stack(outs_s, 1), stack(outs_s, 2), stack(outs_s, 3), stack(outs_s, 4))
```

```python
import functools

import numpy as np
import jax
import jax.numpy as jnp
from jax import lax
from jax.experimental import pallas as pl
from jax.experimental.pallas import tpu as pltpu

F32 = jnp.float32
BF16 = jnp.bfloat16
I32 = jnp.int32
HALF_T = jnp.int16

D_MODEL = 1024
LRU_W = 512
LRU_BLOCKS = 8
LRU_BW = LRU_W // LRU_BLOCKS
CONV_W = 4
LRU_C = 8.0
N_HEADS = 8
HEAD_DIM = 64
ATT_W = N_HEADS * HEAD_DIM
N_KV_HEADS = 2
KV_GROUP = N_HEADS // N_KV_HEADS
KV_W = N_KV_HEADS * HEAD_DIM
ROT_DIM = HEAD_DIM // 4
ROPE_THETA = 500000.0
IDX_HEADS = 4
IDX_DIM = 64
IDX_W = IDX_HEADS * IDX_DIM
TOPK_MAX = 256
PAGE_SIZE = 128
D_FF = 2816
EPS = 1e-6

COL_XR = 0
COL_GR = COL_XR + LRU_W
COL_Q = COL_GR + LRU_W
COL_K = COL_Q + ATT_W
COL_V = COL_K + KV_W
COL_IQ = COL_V + KV_W
MAIN_W = COL_IQ
TAIL_W = 128
IDXP_W = IDX_W + TAIL_W
IW_LANE = IDX_DIM
IDX_CAT = 6 * IDX_DIM

LANES = 128
SUBLANES = 8
TRI_W = 256
GATE_TILE = 256
INT_MIN = -2147483648
HALF_MIN = -32768
LOG2E = 1.4426950408889634
NEG_BIG = -1e30
VMEM_LIMIT = 48 * 1024 * 1024

PROMPT_ROW_TILE = 512
Q_BLOCK = 256
POST_ROW_TILE = 512
FF_CHUNK = D_FF
ADA_COL_TILE = 512
DECODE_GROUP = 8
SCORE_GROUP = 8
SEARCH_KEYS_PER_STEP = 2048


def _dot(a, b):
    return jnp.dot(a, b, preferred_element_type=F32)


def _dot_nt(a, b):
    return lax.dot_general(a, b, (((1,), (1,)), ((), ())), preferred_element_type=F32)


def _split_bf16(x):
    hi = x.astype(BF16)
    lo = (x - hi.astype(F32)).astype(BF16)
    return hi, lo


def _split3(x):
    p1 = x.astype(BF16)
    r1 = x - p1.astype(F32)
    p2 = r1.astype(BF16)
    p3 = (r1 - p2.astype(F32)).astype(BF16)
    return p1, p2, p3


def _dot_f32(a, b, dot=_dot):
    a1, a2, a3 = a
    b1, b2, b3 = b
    return ((dot(a1, b3) + dot(a2, b2) + dot(a3, b1)) + (dot(a1, b2) + dot(a2, b1))) + dot(a1, b1)


def _cat6_lhs(x, axis):
    r1 = x - x.astype(BF16).astype(F32)
    r2 = r1 - r1.astype(BF16).astype(F32)
    return jnp.concatenate([x, x, r1, x, r1, r2], axis=axis).astype(BF16)


def _cat6_rhs(x, axis):
    p1, p2, p3 = _split3(x)
    return jnp.concatenate([p1, p2, p1, p3, p2, p1], axis=axis)


def _rms(x, g):
    return x * lax.rsqrt(jnp.mean(x * x, axis=-1, keepdims=True) + EPS) * g


def _head_rms(x, ones_bd, g):
    hi, lo = _split_bf16(x * x)
    ss = _dot(hi, ones_bd) + _dot(lo, ones_bd)
    return x * lax.rsqrt(ss * (1.0 / HEAD_DIM) + EPS) * g


def _rope(x, cos, sin_next, sin_prev):
    w = x.shape[-1]
    half = ROT_DIM // 2
    return x * cos + pltpu.roll(x, w - half, 1) * sin_next + pltpu.roll(x, half, 1) * sin_prev


def _rope_tables(tab, reps):
    cos = tab[:, 0:LANES]
    sa = tab[:, LANES:2 * LANES]
    sb = tab[:, 2 * LANES:3 * LANES]
    if reps > 1:
        cos = jnp.concatenate([cos] * reps, axis=1)
        sa = jnp.concatenate([sa] * reps, axis=1)
        sb = jnp.concatenate([sb] * reps, axis=1)
    return cos, sa, sb


def _rope_tail(zt, tab):
    cos, sa, sb = _rope_tables(tab, 1)
    lane = lax.broadcasted_iota(I32, zt.shape, 1)
    first = lane < IDX_DIM
    return _rope(zt, jnp.where(first, cos, 1.0), jnp.where(first, sa, 0.0), jnp.where(first, sb, 0.0))


def _gelu_tanh(x):
    c = float(np.sqrt(2.0 / np.pi))
    return x * (0.5 * (1.0 + jnp.tanh(c * (x + 0.044715 * (x * x * x)))))


def _softplus(x):
    return jnp.maximum(x, 0.0) + jnp.log1p(jnp.exp(-jnp.abs(x)))


def _project_in(x, sh1, sc1, ln1, w_main_t, w_idx_t_parts):
    h = _rms(x, ln1) * (1.0 + sc1) + sh1
    hp = _split3(h)
    return _dot_nt(hp[0], w_main_t), _dot_f32(hp, w_idx_t_parts, _dot_nt)


def _lru_gates(xc, wa, ba, wx, bx, lam):
    xcb = xc.astype(BF16)

    def block_diag_dot(w):
        return jnp.concatenate([_dot(xcb[:, j:j + GATE_TILE], w[j:j + GATE_TILE, j:j + GATE_TILE])
                                for j in range(0, LRU_W, GATE_TILE)], axis=1)

    r = jax.nn.sigmoid(block_diag_dot(wa) + ba)
    gi = jax.nn.sigmoid(block_diag_dot(wx) + bx)
    log_a = (-LRU_C * _softplus(-lam)) * r
    a = jnp.exp(log_a)
    y = jnp.tanh(-log_a) * (1.0 + a * a)
    inp = jnp.where(y > 0.0, y * lax.rsqrt(y), 0.0) * gi * xc
    return a, inp


def _qkv_inputs(z, tab, qg, kg, ones_q, ones_k):
    cq, saq, sbq = _rope_tables(tab, ATT_W // LANES)
    ck, sak, sbk = _rope_tables(tab, KV_W // LANES)
    q = _rope(_head_rms(z[:, COL_Q:COL_K], ones_q, qg), cq, saq, sbq) * (HEAD_DIM ** -0.5 * LOG2E)
    k = _rope(_head_rms(z[:, COL_K:COL_V], ones_k, kg), ck, sak, sbk)
    return q, k, z[:, COL_V:COL_IQ]


def _index_inputs(zi_iq, zi_tail, tab):
    ci, sai, sbi = _rope_tables(tab, IDX_W // LANES)
    return _rope(zi_iq, ci, sai, sbi), _rope_tail(zi_tail, tab)


def _ada_kernel(c_ref, w_ref, b_ref, o_ref):
    c = c_ref[...]
    s = c * jax.nn.sigmoid(c)
    o_ref[...] = _dot_f32(_split3(s), _split3(w_ref[...])) + b_ref[...]


def _ada(c_all, ada_w, ada_b):
    rows = c_all.shape[0]
    n = ada_w.shape[1]
    return pl.pallas_call(
        _ada_kernel,
        grid=(n // ADA_COL_TILE,),
        in_specs=[
            pl.BlockSpec((rows, D_MODEL), lambda j: (0, 0)),
            pl.BlockSpec((D_MODEL, ADA_COL_TILE), lambda j: (0, j)),
            pl.BlockSpec((1, ADA_COL_TILE), lambda j: (0, j)),
        ],
        out_specs=pl.BlockSpec((rows, ADA_COL_TILE), lambda j: (0, j)),
        out_shape=jax.ShapeDtypeStruct((rows, n), F32),
        compiler_params=pltpu.CompilerParams(dimension_semantics=("arbitrary",), vmem_limit_bytes=VMEM_LIMIT),
        name="ada_modulation",
    )(c_all, ada_w, ada_b)


def _prompt_stage_kernel(x_ref, mod_ref, ln1_ref, wmain_ref, widx_ref, convw_ref, convb_ref,
                         wa_ref, ba_ref, wx_ref, bx_ref, lam_ref, qg_ref, kg_ref, gnl_ref,
                         onesq_ref, onesk_ref, rope_ref,
                         na_ref, q_ref, kt_ref, vt_ref, ikt_ref, ktb_ref, vb_ref,
                         ikcat_ref, iqcat_ref, iw_ref, h_ref, conv_ref,
                         xr_buf, hcarry, wiq_parts, wtail_cat):
    tt = x_ref.shape[1]
    t = pl.program_id(1)

    @pl.when((t == 0) & (pl.program_id(0) == 0))
    def _():
        for j, part in enumerate(_split3(widx_ref[...])):
            wiq_parts[j] = part[0:IDX_W]
            wtail_cat[j * TAIL_W:(j + 1) * TAIL_W] = part[IDX_W:IDXP_W]

    @pl.when(t == 0)
    def _():
        xr_buf[0:SUBLANES, :] = jnp.zeros((SUBLANES, LRU_W), F32)
        hcarry[...] = jnp.zeros_like(hcarry)

    mod = mod_ref[0]
    h1, h2, h3 = _split3(_rms(x_ref[0], ln1_ref[...]) * (1.0 + mod[:, D_MODEL:2 * D_MODEL]) + mod[:, 0:D_MODEL])
    z = _dot_nt(h1, wmain_ref[...])
    iq_small = _dot_nt(h1, wiq_parts[2]) + _dot_nt(h2, wiq_parts[1])
    t1 = _dot_nt(h1, wtail_cat[...])
    t2 = _dot_nt(h2, wtail_cat[0:2 * TAIL_W])
    t3 = _dot_nt(h3, wtail_cat[0:TAIL_W])

    xr = z[:, COL_XR:COL_GR]
    xr_buf[SUBLANES:SUBLANES + tt, :] = xr
    convw = convw_ref[...]
    xc = convb_ref[...]
    for i in range(CONV_W):
        xc = xc + xr_buf[pl.ds(SUBLANES - (CONV_W - 1) + i, tt), :] * convw[i:i + 1, :]
    conv_ref[0] = xr_buf[pl.ds(SUBLANES + tt - (CONV_W - 1), CONV_W - 1), :]
    xr_buf[0:SUBLANES, :] = xr_buf[tt:tt + SUBLANES, :]

    iq_small = iq_small + _dot_nt(h3, wiq_parts[0])
    iq_mid = _dot_nt(h1, wiq_parts[1])
    a, b = _lru_gates(xc, wa_ref[...], ba_ref[...], wx_ref[...], bx_ref[...], lam_ref[...])
    iq_mid = iq_mid + _dot_nt(h2, wiq_parts[0])

    q, k, v = _qkv_inputs(z, rope_ref[...], qg_ref[...], kg_ref[...], onesq_ref[...], onesk_ref[...])
    for h in range(N_HEADS):
        q_ref[0, h] = q[:, h * HEAD_DIM:(h + 1) * HEAD_DIM].astype(BF16)
    kt = jnp.transpose(k)
    vt = jnp.transpose(v)
    kt_ref[0] = kt.reshape(N_KV_HEADS, HEAD_DIM, tt)
    vt_ref[0] = vt.reshape(N_KV_HEADS, HEAD_DIM, tt)
    ktb_ref[0] = kt.astype(BF16).reshape(N_KV_HEADS, HEAD_DIM, tt)
    lane = lax.broadcasted_iota(I32, v.shape, 1)
    pad = jnp.where(lane == HEAD_DIM, 1.0, 0.0)
    for n in range(N_KV_HEADS):
        vn = v if n == 0 else pltpu.roll(v, KV_W - n * HEAD_DIM, 1)
        vb_ref[0, n] = jnp.where(lane < HEAD_DIM, vn, pad).astype(BF16)

    iq_big = _dot_nt(h1, wiq_parts[0])
    row = lax.broadcasted_iota(I32, a.shape, 0) % SUBLANES
    d = 1
    while d < SUBLANES:
        keep = row >= d
        a_s = jnp.where(keep, pltpu.roll(a, d, 0), 1.0)
        b_s = jnp.where(keep, pltpu.roll(b, d, 0), 0.0)
        b = a * b_s + b
        a = a * a_s
        d *= 2
    h_in = hcarry[...]
    groups = []
    for g in range(tt // SUBLANES):
        rows = slice(g * SUBLANES, (g + 1) * SUBLANES)
        hg = a[rows] * h_in + b[rows]
        h_in = hg[SUBLANES - 1:SUBLANES, :]
        groups.append(hg)
    hr = jnp.concatenate(groups, axis=0)
    hcarry[...] = h_in
    h_ref[0] = h_in

    na_ref[0] = _rms(_gelu_tanh(z[:, COL_GR:COL_Q]) * hr, gnl_ref[...]).astype(BF16)

    zi_iq = (iq_small + iq_mid) + iq_big
    zi_tail = (((t1[:, 2 * TAIL_W:3 * TAIL_W] + t2[:, TAIL_W:2 * TAIL_W]) + t3)
               + (t1[:, TAIL_W:2 * TAIL_W] + t2[:, 0:TAIL_W])) + t1[:, 0:TAIL_W]
    iq, tail = _index_inputs(zi_iq, zi_tail, rope_ref[...])
    ikt = jnp.transpose(tail)[0:IDX_DIM, :]
    ikt_ref[0] = ikt
    ikcat_ref[0] = _cat6_rhs(ikt, 0)
    for h in range(IDX_HEADS):
        iqcat_ref[0, h] = _cat6_lhs(iq[:, h * IDX_DIM:(h + 1) * IDX_DIM], 1)
    iw_ref[0] = zi_tail


def _const_spec(shape):
    nd = len(shape)
    return pl.BlockSpec(shape, lambda *_: (0,) * nd)


def _prompt_stage(x, mod_p, p, rope_tab):
    bsz, t, _ = x.shape
    tt = PROMPT_ROW_TILE

    def rows(w):
        return pl.BlockSpec((1, tt, w), lambda b, i: (b, i, 0))

    in_specs = [
        rows(D_MODEL),
        pl.BlockSpec((1, 1, 6 * D_MODEL), lambda b, i: (b, 0, 0)),
        _const_spec((1, D_MODEL)),
        _const_spec((MAIN_W, D_MODEL)),
        _const_spec((IDXP_W, D_MODEL)),
        _const_spec((CONV_W, LRU_W)),
        _const_spec((1, LRU_W)),
        _const_spec((LRU_W, LRU_W)),
        _const_spec((1, LRU_W)),
        _const_spec((LRU_W, LRU_W)),
        _const_spec((1, LRU_W)),
        _const_spec((1, LRU_W)),
        _const_spec((1, ATT_W)),
        _const_spec((1, KV_W)),
        _const_spec((1, LRU_W)),
        _const_spec((ATT_W, ATT_W)),
        _const_spec((KV_W, KV_W)),
        pl.BlockSpec((tt, 3 * LANES), lambda b, i: (i, 0)),
    ]

    def heads(n, w):
        return pl.BlockSpec((1, n, tt, w), lambda b, i: (b, 0, i, 0))

    def heads_t(n):
        return pl.BlockSpec((1, n, HEAD_DIM, tt), lambda b, i: (b, 0, 0, i))

    def feat_t(w):
        return pl.BlockSpec((1, w, tt), lambda b, i: (b, 0, i))

    out_specs = [
        rows(LRU_W), heads(N_HEADS, HEAD_DIM), heads_t(N_KV_HEADS), heads_t(N_KV_HEADS), feat_t(IDX_DIM),
        heads_t(N_KV_HEADS), heads(N_KV_HEADS, KV_W), feat_t(IDX_CAT), heads(IDX_HEADS, IDX_CAT), rows(TAIL_W),
        pl.BlockSpec((1, 1, LRU_W), lambda b, i: (b, 0, 0)),
        pl.BlockSpec((1, CONV_W - 1, LRU_W), lambda b, i: (b, 0, 0)),
    ]
    out_shape = [
        jax.ShapeDtypeStruct((bsz, t, LRU_W), BF16),
        jax.ShapeDtypeStruct((bsz, N_HEADS, t, HEAD_DIM), BF16),
        jax.ShapeDtypeStruct((bsz, N_KV_HEADS, HEAD_DIM, t), F32),
        jax.ShapeDtypeStruct((bsz, N_KV_HEADS, HEAD_DIM, t), F32),
        jax.ShapeDtypeStruct((bsz, IDX_DIM, t), F32),
        jax.ShapeDtypeStruct((bsz, N_KV_HEADS, HEAD_DIM, t), BF16),
        jax.ShapeDtypeStruct((bsz, N_KV_HEADS, t, KV_W), BF16),
        jax.ShapeDtypeStruct((bsz, IDX_CAT, t), BF16),
        jax.ShapeDtypeStruct((bsz, IDX_HEADS, t, IDX_CAT), BF16),
        jax.ShapeDtypeStruct((bsz, t, TAIL_W), F32),
        jax.ShapeDtypeStruct((bsz, 1, LRU_W), F32),
        jax.ShapeDtypeStruct((bsz, CONV_W - 1, LRU_W), F32),
    ]
    return pl.pallas_call(
        _prompt_stage_kernel,
        grid=(bsz, t // tt),
        in_specs=in_specs,
        out_specs=out_specs,
        out_shape=out_shape,
        scratch_shapes=[pltpu.VMEM((tt + SUBLANES, LRU_W), F32), pltpu.VMEM((1, LRU_W), F32),
                        pltpu.VMEM((3, IDX_W, D_MODEL), BF16), pltpu.VMEM((3 * TAIL_W, D_MODEL), BF16)],
        compiler_params=pltpu.CompilerParams(dimension_semantics=("arbitrary", "arbitrary"),
                                             vmem_limit_bytes=VMEM_LIMIT),
        name="prompt_stage",
    )(x, mod_p, p["ln1"], p["w_main"], p["w_idx"], p["conv_w"], p["conv_b"], p["wa"], p["ba"],
      p["wx"], p["bx"], p["lam"], p["qg"], p["kg"], p["gnl"], p["ones_q"], p["ones_k"], rope_tab)


def _sortable_key(score, fold_zero=True):
    if fold_zero:
        score = jnp.where(score == 0.0, 0.0, score)
    bits = lax.bitcast_convert_type(score, I32)
    return jnp.where(bits < 0, bits ^ jnp.int32(0x7FFFFFFF), bits)


def _count_ge16(ref, c, strict=False):
    c16 = c.astype(HALF_T)
    hit = ref[...] > c16 if strict else ref[...] >= c16
    ones = jnp.where(hit, jnp.ones((), HALF_T), jnp.zeros((), HALF_T))
    acc = ones[:, 0:LANES]
    for j in range(1, ref.shape[1] // LANES):
        acc = acc + ones[:, j * LANES:(j + 1) * LANES]
    return jnp.sum(acc.astype(F32), axis=1, keepdims=True)


def _bitwise_max16(count_ge, k, n_all):
    def body(i, carry):
        t, ct = carry
        c = t + lax.shift_left(jnp.int32(1), jnp.int32(15) - i)
        cn = count_ge(c)
        ok = cn >= k
        return jnp.where(ok, c, t), jnp.where(ok, cn, ct)

    init = (jnp.full(k.shape, HALF_MIN, I32), jnp.full(k.shape, float(n_all), F32))
    return lax.fori_loop(0, 16, body, init, unroll=True)


def _kth_largest(key_ref, hi_ref, lo_ref, k, extra=None):
    key = key_ref[...]
    hi_ref[...] = lax.shift_right_arithmetic(key, 16).astype(HALF_T)
    lo_ref[...] = ((key & 0xFFFF) + HALF_MIN).astype(HALF_T)
    kf = jnp.full((key_ref.shape[0], 1), k, F32)
    n_all = key_ref.shape[1] + (0 if extra is None else 1)
    if extra is not None:
        hi_x = lax.shift_right_arithmetic(extra, 16)
        lo_x = (extra & 0xFFFF) + HALF_MIN

    def count_hi(c):
        cnt = _count_ge16(hi_ref, c)
        return cnt if extra is None else cnt + jnp.where(hi_x >= c, 1.0, 0.0)

    t_hi, n_bucket_up = _bitwise_max16(count_hi, kf, n_all)
    above = _count_ge16(hi_ref, t_hi, strict=True)
    if extra is not None:
        above = above + jnp.where(hi_x > t_hi, 1.0, 0.0)
    k_lo = kf - above
    lo_ref[...] = jnp.where(hi_ref[...] == t_hi.astype(HALF_T), lo_ref[...], jnp.full((), HALF_MIN, HALF_T))
    if extra is not None:
        lo_x = jnp.where(hi_x == t_hi, lo_x, HALF_MIN)

    def count_lo(c):
        cnt = _count_ge16(lo_ref, c)
        return cnt if extra is None else cnt + jnp.where(lo_x >= c, 1.0, 0.0)

    t_lo, n_lo = _bitwise_max16(count_lo, k_lo, n_all)
    n_ge = jnp.where(t_lo > HALF_MIN, above + n_lo, n_bucket_up)
    return t_hi * 65536 + (t_lo - HALF_MIN), n_ge


def _selection_bias(key_ref, hi_ref, lo_ref, bias_ref, tri, k, extra=None):
    thr, n_ge = _kth_largest(key_ref, hi_ref, lo_ref, k, extra)
    has_ties = jnp.max(jnp.where(n_ge > k, 1.0, 0.0)) > 0.0

    @pl.when(jnp.logical_not(has_ties))
    def _():
        bias_ref[...] = jnp.where(key_ref[...] >= thr, 0.0, NEG_BIG)

    @pl.when(has_ties)
    def _():
        n_gt = jnp.sum(jnp.where(key_ref[...] > thr, 1.0, 0.0), axis=1, keepdims=True)
        if extra is not None:
            n_gt = n_gt + jnp.where(extra > thr, 1.0, 0.0)
        need = k - n_gt
        offset = jnp.zeros_like(need)
        for c in range(key_ref.shape[1] // TRI_W):
            kc = key_ref[:, c * TRI_W:(c + 1) * TRI_W]
            eq = kc == thr
            e = jnp.where(eq, 1.0, 0.0)
            incl = _dot(e.astype(BF16), tri)
            rank = incl - e + offset
            tie = jnp.where(rank < need, 0.0, NEG_BIG)
            bias_ref[:, c * TRI_W:(c + 1) * TRI_W] = jnp.where(kc > thr, 0.0, jnp.where(eq, tie, NEG_BIG))
            offset = offset + incl[:, TRI_W - 1:TRI_W]

    if extra is None:
        return None
    n_gt = jnp.sum(jnp.where(key_ref[...] > thr, 1.0, 0.0), axis=1, keepdims=True) + jnp.where(extra > thr, 1.0, 0.0)
    n_eq_main = n_ge - n_gt - jnp.where(extra == thr, 1.0, 0.0)
    tie = jnp.where(n_eq_main < k - n_gt, 0.0, NEG_BIG)
    return jnp.where(extra > thr, 0.0, jnp.where(extra == thr, tie, NEG_BIG))


def _prompt_attn_kernel(q_ref, iq_ref, iw_ref, kt_ref, v_ref, ikt_ref, gn_ref, tri_ref, o_ref,
                        key_ref, hi_ref, lo_ref, bias_ref):
    nb, qb = o_ref.shape[0], o_ref.shape[1]
    n_keys = bias_ref.shape[1]
    past = n_keys - qb
    causal = lax.broadcasted_iota(I32, (qb, qb), 1) <= lax.broadcasted_iota(I32, (qb, qb), 0)
    if n_keys > TOPK_MAX:
        for b in range(nb):
            rows = slice(b * qb, (b + 1) * qb)
            iw = iw_ref[b]
            ikt = ikt_ref[b]
            score = jnp.zeros((qb, n_keys), F32)
            s_next = _dot(iq_ref[b, 0], ikt)
            for h in range(IDX_HEADS):
                s, s_next = s_next, (_dot(iq_ref[b, h + 1], ikt) if h + 1 < IDX_HEADS else None)
                score = score + iw[:, IW_LANE + h:IW_LANE + h + 1] * jnp.maximum(s, 0.0)
            key = _sortable_key(score, fold_zero=False)
            key_ref[rows, 0:past] = key[:, 0:past]
            key_ref[rows, past:n_keys] = jnp.where(causal, key[:, past:n_keys], INT_MIN)
        _selection_bias(key_ref, hi_ref, lo_ref, bias_ref, tri_ref[...], float(TOPK_MAX))
        for b in range(nb):
            rows = slice(b * qb, (b + 1) * qb)
            bias_ref[rows, past:n_keys] = jnp.where(causal, bias_ref[rows, past:n_keys], NEG_BIG)
    else:
        for b in range(nb):
            bias_ref[b * qb:(b + 1) * qb, :] = jnp.where(causal, 0.0, NEG_BIG)

    def qk(i):
        b, h = divmod(i, N_HEADS)
        return _dot(q_ref[b, h], kt_ref[b, h // KV_GROUP]) + bias_ref[b * qb:(b + 1) * qb, :]

    logits_next = qk(0)
    for b in range(nb):
        outs = []
        for h in range(N_HEADS):
            i = b * N_HEADS + h
            logits, logits_next = logits_next, (qk(i + 1) if i + 1 < nb * N_HEADS else None)
            m = jnp.max(logits, axis=1, keepdims=True)
            p = jnp.exp2(logits - m).astype(BF16)
            pv = _dot(p, v_ref[b, h // KV_GROUP])
            outs.append(pv[:, 0:HEAD_DIM] / pv[:, HEAD_DIM:HEAD_DIM + 1])
        o_ref[b] = _rms(jnp.concatenate(outs, axis=1), gn_ref[...]).astype(BF16)


def _prompt_attention_block(qh, iqcat, iw, ktb, vb, ikcat, gn_att, tri, q_block):
    bsz, _, t, _ = qh.shape
    qb = Q_BLOCK
    n_keys = (q_block + 1) * qb
    nb = max(1, min(bsz, SEARCH_KEYS_PER_STEP // n_keys))
    while bsz % nb:
        nb -= 1
    rows = nb * qb

    def qheads(n, w):
        return pl.BlockSpec((nb, n, qb, w), lambda b: (b, 0, q_block, 0))

    return pl.pallas_call(
        _prompt_attn_kernel,
        grid=(bsz // nb,),
        in_specs=[qheads(N_HEADS, HEAD_DIM), qheads(IDX_HEADS, IDX_CAT),
                  pl.BlockSpec((nb, qb, TAIL_W), lambda b: (b, q_block, 0)),
                  pl.BlockSpec((nb, N_KV_HEADS, HEAD_DIM, n_keys), lambda b: (b, 0, 0, 0)),
                  pl.BlockSpec((nb, N_KV_HEADS, n_keys, KV_W), lambda b: (b, 0, 0, 0)),
                  pl.BlockSpec((nb, IDX_CAT, n_keys), lambda b: (b, 0, 0)),
                  _const_spec((1, ATT_W)), _const_spec((TRI_W, TRI_W))],
        out_specs=pl.BlockSpec((nb, qb, ATT_W), lambda b: (b, 0, 0)),
        out_shape=jax.ShapeDtypeStruct((bsz, qb, ATT_W), BF16),
        scratch_shapes=[pltpu.VMEM((rows, n_keys), I32), pltpu.VMEM((rows, n_keys), HALF_T),
                        pltpu.VMEM((rows, n_keys), HALF_T), pltpu.VMEM((rows, n_keys), F32)],
        compiler_params=pltpu.CompilerParams(dimension_semantics=("arbitrary",), vmem_limit_bytes=VMEM_LIMIT),
        name=f"prompt_attention_{q_block}",
    )(qh, iqcat, iw, ktb, vb, ikcat, gn_att, tri)


def _prompt_attention(qh, iqcat, iw, ktb, vb, ikcat, gn_att, tri):
    t = qh.shape[2]
    blocks = [_prompt_attention_block(qh, iqcat, iw, ktb, vb, ikcat, gn_att, tri, i) for i in range(t // Q_BLOCK)]
    return jnp.concatenate(blocks, axis=1)


def _post_kernel(x_ref, na_ref, nb_ref, mod_ref, woa_ref, wob_ref, ln2_ref, wg_ref, wu_ref, wd_ref,
                 y_ref, *, shared_mod):
    mod = mod_ref[0] if shared_mod else mod_ref[...]
    g1 = mod[:, 2 * D_MODEL:3 * D_MODEL]
    sh2 = mod[:, 3 * D_MODEL:4 * D_MODEL]
    sc2 = mod[:, 4 * D_MODEL:5 * D_MODEL]
    g2 = mod[:, 5 * D_MODEL:6 * D_MODEL]
    x = x_ref[...]
    mix = _dot(na_ref[...], woa_ref[...]) + _dot(nb_ref[...], wob_ref[...])
    x1 = x + g1 * mix
    h2 = (_rms(x1, ln2_ref[...]) * (1.0 + sc2) + sh2).astype(BF16)
    def gate_up(c):
        return _dot(h2, wg_ref[:, c * FF_CHUNK:(c + 1) * FF_CHUNK]), _dot(h2, wu_ref[:, c * FF_CHUNK:(c + 1) * FF_CHUNK])

    n_chunks = D_FF // FF_CHUNK
    ff = jnp.zeros_like(x1)
    gu_next = gate_up(0)
    for c in range(n_chunks):
        (g, u), gu_next = gu_next, (gate_up(c + 1) if c + 1 < n_chunks else None)
        act = (g * jax.nn.sigmoid(g) * u).astype(BF16)
        ff = ff + _dot(act, wd_ref[c * FF_CHUNK:(c + 1) * FF_CHUNK, :])
    y_ref[...] = x1 + g2 * ff


def _post(x, na, nb, mod, p, rows_per_mod, tile):
    n = x.shape[0]
    shared = rows_per_mod > 1
    if shared:
        mod_spec = pl.BlockSpec((1, 1, 6 * D_MODEL), lambda i: (i * tile // rows_per_mod, 0, 0))
    else:
        mod_spec = pl.BlockSpec((tile, 6 * D_MODEL), lambda i: (i, 0))

    def rows(w):
        return pl.BlockSpec((tile, w), lambda i: (i, 0))

    def weight(shape):
        return pl.BlockSpec(shape, lambda i: (0, 0), pipeline_mode=pl.Buffered(1))

    return pl.pallas_call(
        functools.partial(_post_kernel, shared_mod=shared),
        grid=(n // tile,),
        in_specs=[rows(D_MODEL), rows(LRU_W), rows(ATT_W), mod_spec,
                  weight((LRU_W, D_MODEL)), weight((ATT_W, D_MODEL)), weight((1, D_MODEL)),
                  weight((D_MODEL, D_FF)), weight((D_MODEL, D_FF)), weight((D_FF, D_MODEL))],
        out_specs=rows(D_MODEL),
        out_shape=jax.ShapeDtypeStruct((n, D_MODEL), F32),
        compiler_params=pltpu.CompilerParams(dimension_semantics=("arbitrary",), vmem_limit_bytes=VMEM_LIMIT),
        name="post_ffn",
    )(x, na, nb, mod, p["wo_a"], p["wo_b"], p["ln2"], p["wg"], p["wu"], p["wd"])


def _sample_stage_kernel(x_ref, mod_ref, ln1_ref, wmain_ref, widx_ref, convw_ref, convb_ref,
                         wa_ref, ba_ref, wx_ref, bx_ref, lam_ref, qg_ref, kg_ref, gnl_ref,
                         onesq_ref, onesk_ref, rope_ref, sconv_ref, sh_ref,
                         na_ref, q_ref, k_ref, v_ref, ik_ref, iq_ref, iw_ref, h_ref, conv_ref):
    mod = mod_ref[...]
    z, zi = _project_in(x_ref[...], mod[:, 0:D_MODEL], mod[:, D_MODEL:2 * D_MODEL], ln1_ref[...],
                        wmain_ref[...], _split3(widx_ref[...]))
    xr = z[:, COL_XR:COL_GR]
    convw = convw_ref[...]
    sconv = sconv_ref[...]
    xc = convb_ref[...]
    for i in range(CONV_W - 1):
        xc = xc + sconv[:, i * LRU_W:(i + 1) * LRU_W] * convw[i:i + 1, :]
    xc = xc + xr * convw[CONV_W - 1:CONV_W, :]
    conv_ref[:, 0:(CONV_W - 2) * LRU_W] = sconv[:, LRU_W:(CONV_W - 1) * LRU_W]
    conv_ref[:, (CONV_W - 2) * LRU_W:(CONV_W - 1) * LRU_W] = xr

    a, b = _lru_gates(xc, wa_ref[...], ba_ref[...], wx_ref[...], bx_ref[...], lam_ref[...])
    hr = a * sh_ref[...] + b
    h_ref[...] = hr
    na_ref[...] = _rms(_gelu_tanh(z[:, COL_GR:COL_Q]) * hr, gnl_ref[...]).astype(BF16)

    q, k, v = _qkv_inputs(z, rope_ref[...], qg_ref[...], kg_ref[...], onesq_ref[...], onesk_ref[...])
    iq, tail = _index_inputs(zi[:, 0:IDX_W], zi[:, IDX_W:IDXP_W], rope_ref[...])
    q_ref[...] = q
    k_ref[...] = k
    v_ref[...] = v
    ik_ref[...] = tail[:, 0:IDX_DIM]
    iq_ref[...] = iq
    iw_ref[...] = zi[:, IDX_W:IDXP_W]


def _sample_stage(x, mod_s, p, rope_row, sconv, sh):
    n = x.shape[0]

    def out(w, dt=F32):
        return jax.ShapeDtypeStruct((n, w), dt)

    return pl.pallas_call(
        _sample_stage_kernel,
        out_shape=[out(LRU_W, BF16), out(ATT_W), out(KV_W), out(KV_W), out(IDX_DIM), out(IDX_W),
                   out(TAIL_W), out(LRU_W), out((CONV_W - 1) * LRU_W)],
        compiler_params=pltpu.CompilerParams(vmem_limit_bytes=VMEM_LIMIT),
        name="sample_stage",
    )(x, mod_s, p["ln1"], p["w_main"], p["w_idx"], p["conv_w"], p["conv_b"], p["wa"], p["ba"],
      p["wx"], p["bx"], p["lam"], p["qg"], p["kg"], p["gnl"], p["ones_q"], p["ones_k"], rope_row, sconv, sh)


def _page_copy(cache_ref, buf_ref, sem, page, slot, j, pg):
    return pltpu.make_async_copy(cache_ref.at[page], buf_ref.at[slot, j, pg], sem)


def _paged_prefetch(pt_ref, caches, bufs, sems, group, n_pages):
    s = pl.program_id(0)
    slot = s % 2

    def start(grp, half):
        for cache_ref, buf_ref, sem in zip(caches, bufs, sems):
            for j in range(group):
                for pg in range(n_pages):
                    page = pt_ref[grp * group + j, pg]
                    _page_copy(cache_ref, buf_ref, sem.at[half], page, half, j, pg).start()

    @pl.when(s == 0)
    def _():
        start(0, 0)

    @pl.when(s + 1 < pl.num_programs(0))
    def _():
        start(s + 1, 1 - slot)

    for cache_ref, buf_ref, sem in zip(caches, bufs, sems):
        for j in range(group):
            for pg in range(n_pages):
                _page_copy(cache_ref, buf_ref, sem.at[slot], 0, slot, j, pg).wait()
    return slot


def _sample_score_kernel(pt_ref, iq_ref, iw_ref, ik_hbm, o_ref, ik_buf, sem, *, n_pages):
    group = iq_ref.shape[0]
    slot = _paged_prefetch(pt_ref, (ik_hbm,), (ik_buf,), (sem,), group, n_pages)
    iqcat = [_cat6_lhs(iq_ref[j], 1) for j in range(group)]
    for pg in range(n_pages):
        for j in range(group):
            s = _dot(iqcat[j], _cat6_rhs(ik_buf[slot, j, pg], 0))
            o_ref[j, :, pg * PAGE_SIZE:(pg + 1) * PAGE_SIZE] = jnp.sum(iw_ref[j] * jnp.maximum(s, 0.0), axis=0,
                                                                        keepdims=True)


def _sample_scores(page_table, iq3, iw3, cache_ik):
    n, n_pages = page_table.shape
    g = SCORE_GROUP
    grid_spec = pltpu.PrefetchScalarGridSpec(
        num_scalar_prefetch=1,
        grid=(n // g,),
        in_specs=[pl.BlockSpec((g, IDX_HEADS, IDX_DIM), lambda s, pt: (s, 0, 0)),
                  pl.BlockSpec((g, IDX_HEADS, 1), lambda s, pt: (s, 0, 0)),
                  pl.BlockSpec(memory_space=pl.ANY)],
        out_specs=pl.BlockSpec((g, 1, n_pages * PAGE_SIZE), lambda s, pt: (s, 0, 0)),
        scratch_shapes=[pltpu.VMEM((2, g, n_pages, IDX_DIM, PAGE_SIZE), F32), pltpu.SemaphoreType.DMA((2,))],
    )
    return pl.pallas_call(
        functools.partial(_sample_score_kernel, n_pages=n_pages),
        grid_spec=grid_spec,
        out_shape=jax.ShapeDtypeStruct((n, 1, n_pages * PAGE_SIZE), F32),
        compiler_params=pltpu.CompilerParams(dimension_semantics=("arbitrary",), vmem_limit_bytes=VMEM_LIMIT),
        name="sample_scores",
    )(page_table, iq3, iw3, cache_ik)


def _sample_select_kernel(score_ref, iq_ref, ik_ref, iw_ref, tri_ref, bias_ref, bias_new_ref,
                          key_ref, hi_ref, lo_ref):
    iq = iq_ref[...]
    ik = ik_ref[...]
    iw = iw_ref[...]
    new = jnp.zeros((iq.shape[0], 1), F32)
    for h in range(IDX_HEADS):
        s = jnp.sum(iq[:, h * IDX_DIM:(h + 1) * IDX_DIM] * ik, axis=1, keepdims=True)
        new = new + iw[:, IW_LANE + h:IW_LANE + h + 1] * jnp.maximum(s, 0.0)
    key_ref[...] = _sortable_key(score_ref[...])
    b_new = _selection_bias(key_ref, hi_ref, lo_ref, bias_ref, tri_ref[...], float(TOPK_MAX),
                            extra=_sortable_key(new))
    bias_new_ref[...] = jnp.broadcast_to(b_new, bias_new_ref.shape)


def _sample_select(score, iq, ik, iw, tri):
    n, n_keys = score.shape
    return pl.pallas_call(
        _sample_select_kernel,
        out_shape=[jax.ShapeDtypeStruct((n, n_keys), F32), jax.ShapeDtypeStruct((n, LANES), F32)],
        scratch_shapes=[pltpu.VMEM((n, n_keys), I32), pltpu.VMEM((n, n_keys), HALF_T), pltpu.VMEM((n, n_keys), HALF_T)],
        compiler_params=pltpu.CompilerParams(vmem_limit_bytes=VMEM_LIMIT),
        name="sample_select",
    )(score, iq, ik, iw, tri)


def _sample_attn_kernel(pt_ref, q_ref, kn_ref, vn_ref, bias_ref, bnew_ref, gn_ref, k_hbm, v_hbm, o_ref,
                        k_buf, v_buf, k_sem, v_sem, *, n_pages):
    group = q_ref.shape[0]
    seqs = range(group)
    slot = _paged_prefetch(pt_ref, (k_hbm, v_hbm), (k_buf, v_buf), (k_sem, v_sem), group, n_pages)
    row = lax.broadcasted_iota(I32, (N_HEADS, HEAD_DIM), 0)
    first = row < KV_GROUP
    q = [q_ref[j] for j in seqs]
    q2 = [jnp.concatenate([jnp.where(first, q[j], 0.0), jnp.where(first, 0.0, q[j])], axis=1).astype(BF16)
          for j in seqs]
    logits = [[None] * n_pages for _ in seqs]
    for pg in range(n_pages):
        for j in seqs:
            kp = k_buf[slot, j, pg].astype(BF16).reshape(KV_W, PAGE_SIZE)
            logits[j][pg] = _dot(q2[j], kp) + bias_ref[j][:, pg * PAGE_SIZE:(pg + 1) * PAGE_SIZE]
    m, p_new, den, acc, v_sel = [], [], [], [], []
    for j in seqs:
        kn = kn_ref[j]
        vn = vn_ref[j]
        k_sel = jnp.where(first, kn[:, 0:HEAD_DIM], kn[:, HEAD_DIM:KV_W])
        v_sel.append(jnp.where(first, vn[:, 0:HEAD_DIM], vn[:, HEAD_DIM:KV_W]))
        l_new = jnp.sum(q[j] * k_sel, axis=1, keepdims=True) + bnew_ref[j][:, 0:1]
        mj = l_new
        for l in logits[j]:
            mj = jnp.maximum(mj, jnp.max(l, axis=1, keepdims=True))
        m.append(mj)
        p_new.append(jnp.exp2(l_new - mj))
        den.append(p_new[j])
        acc.append(jnp.zeros((N_HEADS, KV_W), F32))
    for pg in range(n_pages):
        for j in seqs:
            p = jnp.exp2(logits[j][pg] - m[j])
            den[j] = den[j] + jnp.sum(p, axis=1, keepdims=True)
            vp = v_buf[slot, j, pg].astype(BF16).reshape(KV_W, PAGE_SIZE)
            acc[j] = acc[j] + _dot_nt(p.astype(BF16), vp)
    for j in seqs:
        out = (p_new[j] * v_sel[j] + jnp.where(first, acc[j][:, 0:HEAD_DIM], acc[j][:, HEAD_DIM:KV_W])) / den[j]
        ms = jnp.sum(jnp.sum(out * out, axis=1, keepdims=True), axis=0, keepdims=True) * (1.0 / ATT_W)
        o_ref[j] = (out * lax.rsqrt(ms + EPS) * gn_ref[...]).astype(BF16)


def _sample_attention(page_table, q3, k_new, v_new, bias, bias_new, gn8, cache_k, cache_v):
    n, n_pages = page_table.shape
    n_keys = n_pages * PAGE_SIZE

    g = DECODE_GROUP

    def per_seq(shape):
        return pl.BlockSpec((g,) + shape, lambda s, pt: (s, 0, 0))

    page_buf = pltpu.VMEM((2, g, n_pages, N_KV_HEADS, HEAD_DIM, PAGE_SIZE), F32)
    grid_spec = pltpu.PrefetchScalarGridSpec(
        num_scalar_prefetch=1,
        grid=(n // g,),
        in_specs=[per_seq((N_HEADS, HEAD_DIM)), per_seq((1, KV_W)), per_seq((1, KV_W)),
                  per_seq((1, n_keys)), per_seq((1, LANES)),
                  pl.BlockSpec((N_HEADS, HEAD_DIM), lambda s, pt: (0, 0)),
                  pl.BlockSpec(memory_space=pl.ANY), pl.BlockSpec(memory_space=pl.ANY)],
        out_specs=per_seq((N_HEADS, HEAD_DIM)),
        scratch_shapes=[page_buf, page_buf, pltpu.SemaphoreType.DMA((2,)), pltpu.SemaphoreType.DMA((2,))],
    )
    return pl.pallas_call(
        functools.partial(_sample_attn_kernel, n_pages=n_pages),
        grid_spec=grid_spec,
        out_shape=jax.ShapeDtypeStruct((n, N_HEADS, HEAD_DIM), BF16),
        compiler_params=pltpu.CompilerParams(dimension_semantics=("arbitrary",), vmem_limit_bytes=VMEM_LIMIT),
        name="sample_attention",
    )(page_table, q3, k_new, v_new, bias, bias_new, gn8, cache_k, cache_v)


def _rope_table_np(positions):
    half = ROT_DIM // 2
    freq = ROPE_THETA ** (-(np.arange(half, dtype=np.float64) / half))
    ang = np.asarray(positions, np.float64)[:, None] * freq[None, :]
    cos, sin = np.cos(ang), np.sin(ang)
    n = len(positions)
    tab = np.zeros((n, 3, LANES), np.float64)
    tab[:, 0, :] = 1.0
    for base in range(0, LANES, HEAD_DIM):
        tab[:, 0, base:base + half] = cos
        tab[:, 0, base + half:base + ROT_DIM] = cos
        tab[:, 1, base:base + half] = -sin
        tab[:, 2, base + half:base + ROT_DIM] = sin
    return tab.reshape(n, 3 * LANES).astype(np.float32)


def _block_diag(w):
    n, a, b = w.shape
    return jnp.einsum("nij,nm->nimj", w, jnp.eye(n, dtype=w.dtype)).reshape(n * a, n * b)


def _layer_params(l, ln1_g, w_in, conv_w, conv_b, lru_wa, lru_ba, lru_wx, lru_bx, lru_lambda, q_norm_g,
                  k_norm_g, gn_lru_g, gn_att_g, w_out, ln2_g, w_gate, w_up, w_down):
    w_in_t = jnp.transpose(w_in[l])
    w_idx = jnp.pad(w_in_t[COL_IQ:], ((0, IDXP_W - (w_in.shape[2] - COL_IQ)), (0, 0)))
    head_id = np.arange(ATT_W) // HEAD_DIM
    ones_q = (head_id[:, None] == head_id[None, :]).astype(np.float32)
    return {
        "ln1": ln1_g[l][None, :],
        "w_main": w_in_t[:COL_IQ].astype(BF16),
        "w_idx": w_idx,
        "conv_w": conv_w[l],
        "conv_b": conv_b[l][None, :],
        "wa": _block_diag(lru_wa[l]).astype(BF16),
        "ba": lru_ba[l][None, :],
        "wx": _block_diag(lru_wx[l]).astype(BF16),
        "bx": lru_bx[l][None, :],
        "lam": lru_lambda[l][None, :],
        "qg": jnp.tile(q_norm_g[l], N_HEADS)[None, :],
        "kg": jnp.tile(k_norm_g[l], N_KV_HEADS)[None, :],
        "gnl": gn_lru_g[l][None, :],
        "gna": gn_att_g[l][None, :],
        "gna8": gn_att_g[l].reshape(N_HEADS, HEAD_DIM),
        "ones_q": jnp.asarray(ones_q, BF16),
        "ones_k": jnp.asarray(ones_q[:KV_W, :KV_W], BF16),
        "wo_a": w_out[l][:LRU_W].astype(BF16),
        "wo_b": w_out[l][LRU_W:].astype(BF16),
        "ln2": ln2_g[l][None, :],
        "wg": w_gate[l].astype(BF16),
        "wu": w_up[l].astype(BF16),
        "wd": w_down[l].astype(BF16),
    }


def kernel(x_prompt, x_sample, cache_k, cache_v, cache_ik, state_h, state_conv, page_table, c_prompt, c_sample, ada_w, ada_b, ln1_g, w_in, conv_w, conv_b, lru_wa, lru_ba, lru_wx, lru_bx, lru_lambda, q_norm_g, k_norm_g, gn_lru_g, gn_att_g, w_out, ln2_g, w_gate, w_up, w_down):
    bsz, t, _ = x_prompt.shape
    dbsz, dt, _ = x_sample.shape
    depth = ada_w.shape[0]
    n_pages = page_table.shape[1]
    past_len = n_pages * PAGE_SIZE
    assert dt == 1 and t % PROMPT_ROW_TILE == 0 and t % POST_ROW_TILE == 0 and t % Q_BLOCK == 0
    assert t // 4 >= TOPK_MAX and (past_len + dt) // 4 >= TOPK_MAX
    assert dbsz % DECODE_GROUP == 0 and dbsz % SCORE_GROUP == 0 and cache_k.shape[2] == PAGE_SIZE

    rope_p = jnp.asarray(_rope_table_np(np.arange(t)))
    rope_s = jnp.asarray(_rope_table_np(past_len + np.arange(dt)))
    tri = jnp.asarray(np.triu(np.ones((TRI_W, TRI_W), np.float32)), BF16)

    yp = x_prompt.reshape(bsz * t, D_MODEL)
    ys = x_sample.reshape(dbsz, D_MODEL)
    c_all = jnp.concatenate([c_prompt, c_sample], axis=0)
    outs_p, outs_s = [], []
    for l in range(depth):
        p = _layer_params(l, ln1_g, w_in, conv_w, conv_b, lru_wa, lru_ba, lru_wx, lru_bx, lru_lambda,
                          q_norm_g, k_norm_g, gn_lru_g, gn_att_g, w_out, ln2_g, w_gate, w_up, w_down)
        mod = _ada(c_all, ada_w[l], ada_b[l][None, :])
        mod_p = mod[:bsz].reshape(bsz, 1, 6 * D_MODEL)
        mod_s = mod[bsz:]

        (na, qh, kt, vt, ikt, ktb, vb, ikcat, iqcat, iw, h_last, conv_new) = _prompt_stage(
            yp.reshape(bsz, t, D_MODEL), mod_p, p, rope_p)
        nb = _prompt_attention(qh, iqcat, iw, ktb, vb, ikcat, p["gna"], tri)
        yp = _post(yp, na.reshape(bsz * t, LRU_W), nb.reshape(bsz * t, ATT_W), mod_p, p, t, POST_ROW_TILE)
        outs_p.append((jnp.transpose(kt, (0, 3, 1, 2)), jnp.transpose(vt, (0, 3, 1, 2)),
                       jnp.transpose(ikt, (0, 2, 1)), h_last.reshape(bsz, LRU_W), conv_new))

        (na_s, q_s, k_s, v_s, ik_s, iq_s, iw_s, h_s, conv_s) = _sample_stage(
            ys, mod_s, p, rope_s, state_conv[l].reshape(dbsz, (CONV_W - 1) * LRU_W), state_h[l])
        score = _sample_scores(page_table, iq_s.reshape(dbsz, IDX_HEADS, IDX_DIM),
                               iw_s[:, IW_LANE:IW_LANE + IDX_HEADS].reshape(dbsz, IDX_HEADS, 1),
                               jnp.transpose(cache_ik[l], (0, 2, 1)))
        bias, bias_new = _sample_select(score.reshape(dbsz, past_len), iq_s, ik_s, iw_s, tri)
        nb_s = _sample_attention(page_table, q_s.reshape(dbsz, N_HEADS, HEAD_DIM),
                                 k_s.reshape(dbsz, 1, KV_W), v_s.reshape(dbsz, 1, KV_W),
                                 bias.reshape(dbsz, 1, past_len), bias_new.reshape(dbsz, 1, LANES), p["gna8"],
                                 jnp.transpose(cache_k[l], (0, 2, 3, 1)), jnp.transpose(cache_v[l], (0, 2, 3, 1)))
        ys = _post(ys, na_s, nb_s.reshape(dbsz, ATT_W), mod_s, p, 1, dbsz)
        outs_s.append((k_s.reshape(dbsz, dt, N_KV_HEADS, HEAD_DIM), v_s.reshape(dbsz, dt, N_KV_HEADS, HEAD_DIM),
                       ik_s.reshape(dbsz, dt, IDX_DIM), h_s,
                       conv_s.reshape(dbsz, CONV_W - 1, LRU_W)))

    def stack(outs, i):
        return jnp.stack([o[i] for o in outs])

    return (yp.reshape(bsz, t, D_MODEL), ys.reshape(dbsz, dt, D_MODEL),
            stack(outs_p, 0), stack(outs_p, 1), stack(outs_p, 2), stack(outs_p, 3), stack(outs_p, 4),
            stack(outs_s, 0), stack(outs_s, 1), stack(outs_s, 2), stack(outs_s, 3), stack(outs_s, 4))
```

```python
import functools

import numpy as np
import jax
import jax.numpy as jnp
from jax import lax
from jax.experimental import pallas as pl
from jax.experimental.pallas import tpu as pltpu

F32 = jnp.float32
BF16 = jnp.bfloat16
I32 = jnp.int32
HALF_T = jnp.int16

D_MODEL = 1024
LRU_W = 512
LRU_BLOCKS = 8
LRU_BW = LRU_W // LRU_BLOCKS
CONV_W = 4
LRU_C = 8.0
N_HEADS = 8
HEAD_DIM = 64
ATT_W = N_HEADS * HEAD_DIM
N_KV_HEADS = 2
KV_GROUP = N_HEADS // N_KV_HEADS
KV_W = N_KV_HEADS * HEAD_DIM
ROT_DIM = HEAD_DIM // 4
ROPE_THETA = 500000.0
IDX_HEADS = 4
IDX_DIM = 64
IDX_W = IDX_HEADS * IDX_DIM
TOPK_MAX = 256
PAGE_SIZE = 128
D_FF = 2816
EPS = 1e-6

COL_XR = 0
COL_GR = COL_XR + LRU_W
COL_Q = COL_GR + LRU_W
COL_K = COL_Q + ATT_W
COL_V = COL_K + KV_W
COL_IQ = COL_V + KV_W
MAIN_W = COL_IQ
TAIL_W = 128
IDXP_W = IDX_W + TAIL_W
IW_LANE = IDX_DIM
IDX_CAT = 6 * IDX_DIM

LANES = 128
SUBLANES = 8
TRI_W = 256
GATE_TILE = 256
INT_MIN = -2147483648
HALF_MIN = -32768
LOG2E = 1.4426950408889634
NEG_BIG = -1e30
VMEM_LIMIT = 48 * 1024 * 1024

PROMPT_ROW_TILE = 512
Q_BLOCK = 256
POST_ROW_TILE = 512
FF_CHUNK = D_FF
ADA_COL_TILE = 512
DECODE_GROUP = 8
SCORE_GROUP = 8
SEARCH_KEYS_PER_STEP = 2048


def _dot(a, b):
    return jnp.dot(a, b, preferred_element_type=F32)


def _dot_nt(a, b):
    return lax.dot_general(a, b, (((1,), (1,)), ((), ())), preferred_element_type=F32)


def _split_bf16(x):
    hi = x.astype(BF16)
    lo = (x - hi.astype(F32)).astype(BF16)
    return hi, lo


def _split3(x):
    p1 = x.astype(BF16)
    r1 = x - p1.astype(F32)
    p2 = r1.astype(BF16)
    p3 = (r1 - p2.astype(F32)).astype(BF16)
    return p1, p2, p3


def _dot_f32(a, b, dot=_dot):
    a1, a2, a3 = a
    b1, b2, b3 = b
    return ((dot(a1, b3) + dot(a2, b2) + dot(a3, b1)) + (dot(a1, b2) + dot(a2, b1))) + dot(a1, b1)


def _cat6_lhs(x, axis):
    r1 = x - x.astype(BF16).astype(F32)
    r2 = r1 - r1.astype(BF16).astype(F32)
    return jnp.concatenate([x, x, r1, x, r1, r2], axis=axis).astype(BF16)


def _cat6_rhs(x, axis):
    p1, p2, p3 = _split3(x)
    return jnp.concatenate([p1, p2, p1, p3, p2, p1], axis=axis)


def _rms(x, g):
    return x * lax.rsqrt(jnp.mean(x * x, axis=-1, keepdims=True) + EPS) * g


def _head_rms(x, ones_bd, g):
    hi, lo = _split_bf16(x * x)
    ss = _dot(hi, ones_bd) + _dot(lo, ones_bd)
    return x * lax.rsqrt(ss * (1.0 / HEAD_DIM) + EPS) * g


def _rope(x, cos, sin_next, sin_prev):
    w = x.shape[-1]
    half = ROT_DIM // 2
    return x * cos + pltpu.roll(x, w - half, 1) * sin_next + pltpu.roll(x, half, 1) * sin_prev


def _rope_tables(tab, reps):
    cos = tab[:, 0:LANES]
    sa = tab[:, LANES:2 * LANES]
    sb = tab[:, 2 * LANES:3 * LANES]
    if reps > 1:
        cos = jnp.concatenate([cos] * reps, axis=1)
        sa = jnp.concatenate([sa] * reps, axis=1)
        sb = jnp.concatenate([sb] * reps, axis=1)
    return cos, sa, sb


def _rope_tail(zt, tab):
    cos, sa, sb = _rope_tables(tab, 1)
    lane = lax.broadcasted_iota(I32, zt.shape, 1)
    first = lane < IDX_DIM
    return _rope(zt, jnp.where(first, cos, 1.0), jnp.where(first, sa, 0.0), jnp.where(first, sb, 0.0))


def _gelu_tanh(x):
    c = float(np.sqrt(2.0 / np.pi))
    return x * (0.5 * (1.0 + jnp.tanh(c * (x + 0.044715 * (x * x * x)))))


def _softplus(x):
    return jnp.maximum(x, 0.0) + jnp.log1p(jnp.exp(-jnp.abs(x)))


def _project_in(x, sh1, sc1, ln1, w_main_t, w_idx_t_parts):
    h = _rms(x, ln1) * (1.0 + sc1) + sh1
    hp = _split3(h)
    return _dot_nt(hp[0], w_main_t), _dot_f32(hp, w_idx_t_parts, _dot_nt)


def _lru_gates(xc, wa, ba, wx, bx, lam):
    xcb = xc.astype(BF16)

    def block_diag_dot(w):
        return jnp.concatenate([_dot(xcb[:, j:j + GATE_TILE], w[j:j + GATE_TILE, j:j + GATE_TILE])
                                for j in range(0, LRU_W, GATE_TILE)], axis=1)

    r = jax.nn.sigmoid(block_diag_dot(wa) + ba)
    gi = jax.nn.sigmoid(block_diag_dot(wx) + bx)
    log_a = (-LRU_C * _softplus(-lam)) * r
    a = jnp.exp(log_a)
    y = jnp.tanh(-log_a) * (1.0 + a * a)
    inp = jnp.where(y > 0.0, y * lax.rsqrt(y), 0.0) * gi * xc
    return a, inp


def _qkv_inputs(z, tab, qg, kg, ones_q, ones_k):
    cq, saq, sbq = _rope_tables(tab, ATT_W // LANES)
    ck, sak, sbk = _rope_tables(tab, KV_W // LANES)
    q = _rope(_head_rms(z[:, COL_Q:COL_K], ones_q, qg), cq, saq, sbq) * (HEAD_DIM ** -0.5 * LOG2E)
    k = _rope(_head_rms(z[:, COL_K:COL_V], ones_k, kg), ck, sak, sbk)
    return q, k, z[:, COL_V:COL_IQ]


def _index_inputs(zi_iq, zi_tail, tab):
    ci, sai, sbi = _rope_tables(tab, IDX_W // LANES)
    return _rope(zi_iq, ci, sai, sbi), _rope_tail(zi_tail, tab)


def _ada_kernel(c_ref, w_ref, b_ref, o_ref):
    c = c_ref[...]
    s = c * jax.nn.sigmoid(c)
    o_ref[...] = _dot_f32(_split3(s), _split3(w_ref[...])) + b_ref[...]


def _ada(c_all, ada_w, ada_b):
    rows = c_all.shape[0]
    n = ada_w.shape[1]
    return pl.pallas_call(
        _ada_kernel,
        grid=(n // ADA_COL_TILE,),
        in_specs=[
            pl.BlockSpec((rows, D_MODEL), lambda j: (0, 0)),
            pl.BlockSpec((D_MODEL, ADA_COL_TILE), lambda j: (0, j)),
            pl.BlockSpec((1, ADA_COL_TILE), lambda j: (0, j)),
        ],
        out_specs=pl.BlockSpec((rows, ADA_COL_TILE), lambda j: (0, j)),
        out_shape=jax.ShapeDtypeStruct((rows, n), F32),
        compiler_params=pltpu.CompilerParams(dimension_semantics=("arbitrary",), vmem_limit_bytes=VMEM_LIMIT),
        name="ada_modulation",
    )(c_all, ada_w, ada_b)


def _prompt_stage_kernel(x_ref, mod_ref, ln1_ref, wmain_ref, widx_ref, convw_ref, convb_ref,
                         wa_ref, ba_ref, wx_ref, bx_ref, lam_ref, qg_ref, kg_ref, gnl_ref,
                         onesq_ref, onesk_ref, rope_ref,
                         na_ref, q_ref, kt_ref, vt_ref, ikt_ref, ktb_ref, vb_ref,
                         ikcat_ref, iqcat_ref, iw_ref, h_ref, conv_ref,
                         xr_buf, hcarry, wiq_parts, wtail_cat):
    tt = x_ref.shape[1]
    t = pl.program_id(1)

    @pl.when((t == 0) & (pl.program_id(0) == 0))
    def _():
        for j, part in enumerate(_split3(widx_ref[...])):
            wiq_parts[j] = part[0:IDX_W]
            wtail_cat[j * TAIL_W:(j + 1) * TAIL_W] = part[IDX_W:IDXP_W]

    @pl.when(t == 0)
    def _():
        xr_buf[0:SUBLANES, :] = jnp.zeros((SUBLANES, LRU_W), F32)
        hcarry[...] = jnp.zeros_like(hcarry)

    mod = mod_ref[0]
    h1, h2, h3 = _split3(_rms(x_ref[0], ln1_ref[...]) * (1.0 + mod[:, D_MODEL:2 * D_MODEL]) + mod[:, 0:D_MODEL])
    z = _dot_nt(h1, wmain_ref[...])
    iq_small = _dot_nt(h1, wiq_parts[2]) + _dot_nt(h2, wiq_parts[1])
    t1 = _dot_nt(h1, wtail_cat[...])
    t2 = _dot_nt(h2, wtail_cat[0:2 * TAIL_W])
    t3 = _dot_nt(h3, wtail_cat[0:TAIL_W])

    xr = z[:, COL_XR:COL_GR]
    xr_buf[SUBLANES:SUBLANES + tt, :] = xr
    convw = convw_ref[...]
    xc = convb_ref[...]
    for i in range(CONV_W):
        xc = xc + xr_buf[pl.ds(SUBLANES - (CONV_W - 1) + i, tt), :] * convw[i:i + 1, :]
    conv_ref[0] = xr_buf[pl.ds(SUBLANES + tt - (CONV_W - 1), CONV_W - 1), :]
    xr_buf[0:SUBLANES, :] = xr_buf[tt:tt + SUBLANES, :]

    iq_small = iq_small + _dot_nt(h3, wiq_parts[0])
    iq_mid = _dot_nt(h1, wiq_parts[1])
    a, b = _lru_gates(xc, wa_ref[...], ba_ref[...], wx_ref[...], bx_ref[...], lam_ref[...])
    iq_mid = iq_mid + _dot_nt(h2, wiq_parts[0])

    q, k, v = _qkv_inputs(z, rope_ref[...], qg_ref[...], kg_ref[...], onesq_ref[...], onesk_ref[...])
    for h in range(N_HEADS):
        q_ref[0, h] = q[:, h * HEAD_DIM:(h + 1) * HEAD_DIM].astype(BF16)
    kt = jnp.transpose(k)
    vt = jnp.transpose(v)
    kt_ref[0] = kt.reshape(N_KV_HEADS, HEAD_DIM, tt)
    vt_ref[0] = vt.reshape(N_KV_HEADS, HEAD_DIM, tt)
    ktb_ref[0] = kt.astype(BF16).reshape(N_KV_HEADS, HEAD_DIM, tt)
    lane = lax.broadcasted_iota(I32, v.shape, 1)
    pad = jnp.where(lane == HEAD_DIM, 1.0, 0.0)
    for n in range(N_KV_HEADS):
        vn = v if n == 0 else pltpu.roll(v, KV_W - n * HEAD_DIM, 1)
        vb_ref[0, n] = jnp.where(lane < HEAD_DIM, vn, pad).astype(BF16)

    iq_big = _dot_nt(h1, wiq_parts[0])
    row = lax.broadcasted_iota(I32, a.shape, 0) % SUBLANES
    d = 1
    while d < SUBLANES:
        keep = row >= d
        a_s = jnp.where(keep, pltpu.roll(a, d, 0), 1.0)
        b_s = jnp.where(keep, pltpu.roll(b, d, 0), 0.0)
        b = a * b_s + b
        a = a * a_s
        d *= 2
    h_in = hcarry[...]
    groups = []
    for g in range(tt // SUBLANES):
        rows = slice(g * SUBLANES, (g + 1) * SUBLANES)
        hg = a[rows] * h_in + b[rows]
        h_in = hg[SUBLANES - 1:SUBLANES, :]
        groups.append(hg)
    hr = jnp.concatenate(groups, axis=0)
    hcarry[...] = h_in
    h_ref[0] = h_in

    na_ref[0] = _rms(_gelu_tanh(z[:, COL_GR:COL_Q]) * hr, gnl_ref[...]).astype(BF16)

    zi_iq = (iq_small + iq_mid) + iq_big
    zi_tail = (((t1[:, 2 * TAIL_W:3 * TAIL_W] + t2[:, TAIL_W:2 * TAIL_W]) + t3)
               + (t1[:, TAIL_W:2 * TAIL_W] + t2[:, 0:TAIL_W])) + t1[:, 0:TAIL_W]
    iq, tail = _index_inputs(zi_iq, zi_tail, rope_ref[...])
    ikt = jnp.transpose(tail)[0:IDX_DIM, :]
    ikt_ref[0] = ikt
    ikcat_ref[0] = _cat6_rhs(ikt, 0)
    for h in range(IDX_HEADS):
        iqcat_ref[0, h] = _cat6_lhs(iq[:, h * IDX_DIM:(h + 1) * IDX_DIM], 1)
    iw_ref[0] = zi_tail


def _const_spec(shape):
    nd = len(shape)
    return pl.BlockSpec(shape, lambda *_: (0,) * nd)


def _prompt_stage(x, mod_p, p, rope_tab):
    bsz, t, _ = x.shape
    tt = PROMPT_ROW_TILE

    def rows(w):
        return pl.BlockSpec((1, tt, w), lambda b, i: (b, i, 0))

    in_specs = [
        rows(D_MODEL),
        pl.BlockSpec((1, 1, 6 * D_MODEL), lambda b, i: (b, 0, 0)),
        _const_spec((1, D_MODEL)),
        _const_spec((MAIN_W, D_MODEL)),
        _const_spec((IDXP_W, D_MODEL)),
        _const_spec((CONV_W, LRU_W)),
        _const_spec((1, LRU_W)),
        _const_spec((LRU_W, LRU_W)),
        _const_spec((1, LRU_W)),
        _const_spec((LRU_W, LRU_W)),
        _const_spec((1, LRU_W)),
        _const_spec((1, LRU_W)),
        _const_spec((1, ATT_W)),
        _const_spec((1, KV_W)),
        _const_spec((1, LRU_W)),
        _const_spec((ATT_W, ATT_W)),
        _const_spec((KV_W, KV_W)),
        pl.BlockSpec((tt, 3 * LANES), lambda b, i: (i, 0)),
    ]

    def heads(n, w):
        return pl.BlockSpec((1, n, tt, w), lambda b, i: (b, 0, i, 0))

    def heads_t(n):
        return pl.BlockSpec((1, n, HEAD_DIM, tt), lambda b, i: (b, 0, 0, i))

    def feat_t(w):
        return pl.BlockSpec((1, w, tt), lambda b, i: (b, 0, i))

    out_specs = [
        rows(LRU_W), heads(N_HEADS, HEAD_DIM), heads_t(N_KV_HEADS), heads_t(N_KV_HEADS), feat_t(IDX_DIM),
        heads_t(N_KV_HEADS), heads(N_KV_HEADS, KV_W), feat_t(IDX_CAT), heads(IDX_HEADS, IDX_CAT), rows(TAIL_W),
        pl.BlockSpec((1, 1, LRU_W), lambda b, i: (b, 0, 0)),
        pl.BlockSpec((1, CONV_W - 1, LRU_W), lambda b, i: (b, 0, 0)),
    ]
    out_shape = [
        jax.ShapeDtypeStruct((bsz, t, LRU_W), BF16),
        jax.ShapeDtypeStruct((bsz, N_HEADS, t, HEAD_DIM), BF16),
        jax.ShapeDtypeStruct((bsz, N_KV_HEADS, HEAD_DIM, t), F32),
        jax.ShapeDtypeStruct((bsz, N_KV_HEADS, HEAD_DIM, t), F32),
        jax.ShapeDtypeStruct((bsz, IDX_DIM, t), F32),
        jax.ShapeDtypeStruct((bsz, N_KV_HEADS, HEAD_DIM, t), BF16),
        jax.ShapeDtypeStruct((bsz, N_KV_HEADS, t, KV_W), BF16),
        jax.ShapeDtypeStruct((bsz, IDX_CAT, t), BF16),
        jax.ShapeDtypeStruct((bsz, IDX_HEADS, t, IDX_CAT), BF16),
        jax.ShapeDtypeStruct((bsz, t, TAIL_W), F32),
        jax.ShapeDtypeStruct((bsz, 1, LRU_W), F32),
        jax.ShapeDtypeStruct((bsz, CONV_W - 1, LRU_W), F32),
    ]
    return pl.pallas_call(
        _prompt_stage_kernel,
        grid=(bsz, t // tt),
        in_specs=in_specs,
        out_specs=out_specs,
        out_shape=out_shape,
        scratch_shapes=[pltpu.VMEM((tt + SUBLANES, LRU_W), F32), pltpu.VMEM((1, LRU_W), F32),
                        pltpu.VMEM((3, IDX_W, D_MODEL), BF16), pltpu.VMEM((3 * TAIL_W, D_MODEL), BF16)],
        compiler_params=pltpu.CompilerParams(dimension_semantics=("arbitrary", "arbitrary"),
                                             vmem_limit_bytes=VMEM_LIMIT),
        name="prompt_stage",
    )(x, mod_p, p["ln1"], p["w_main"], p["w_idx"], p["conv_w"], p["conv_b"], p["wa"], p["ba"],
      p["wx"], p["bx"], p["lam"], p["qg"], p["kg"], p["gnl"], p["ones_q"], p["ones_k"], rope_tab)


def _sortable_key(score, fold_zero=True):
    if fold_zero:
        score = jnp.where(score == 0.0, 0.0, score)
    bits = lax.bitcast_convert_type(score, I32)
    return jnp.where(bits < 0, bits ^ jnp.int32(0x7FFFFFFF), bits)


def _count_ge16(ref, c, strict=False):
    c16 = c.astype(HALF_T)
    hit = ref[...] > c16 if strict else ref[...] >= c16
    ones = jnp.where(hit, jnp.ones((), HALF_T), jnp.zeros((), HALF_T))
    acc = ones[:, 0:LANES]
    for j in range(1, ref.shape[1] // LANES):
        acc = acc + ones[:, j * LANES:(j + 1) * LANES]
    return jnp.sum(acc.astype(F32), axis=1, keepdims=True)


def _bitwise_max16(count_ge, k, n_all):
    def body(i, carry):
        t, ct = carry
        c = t + lax.shift_left(jnp.int32(1), jnp.int32(15) - i)
        cn = count_ge(c)
        ok = cn >= k
        return jnp.where(ok, c, t), jnp.where(ok, cn, ct)

    init = (jnp.full(k.shape, HALF_MIN, I32), jnp.full(k.shape, float(n_all), F32))
    return lax.fori_loop(0, 16, body, init, unroll=True)


def _kth_largest(key_ref, hi_ref, lo_ref, k, extra=None):
    key = key_ref[...]
    hi_ref[...] = lax.shift_right_arithmetic(key, 16).astype(HALF_T)
    lo_ref[...] = ((key & 0xFFFF) + HALF_MIN).astype(HALF_T)
    kf = jnp.full((key_ref.shape[0], 1), k, F32)
    n_all = key_ref.shape[1] + (0 if extra is None else 1)
    if extra is not None:
        hi_x = lax.shift_right_arithmetic(extra, 16)
        lo_x = (extra & 0xFFFF) + HALF_MIN

    def count_hi(c):
        cnt = _count_ge16(hi_ref, c)
        return cnt if extra is None else cnt + jnp.where(hi_x >= c, 1.0, 0.0)

    t_hi, n_bucket_up = _bitwise_max16(count_hi, kf, n_all)
    above = _count_ge16(hi_ref, t_hi, strict=True)
    if extra is not None:
        above = above + jnp.where(hi_x > t_hi, 1.0, 0.0)
    k_lo = kf - above
    lo_ref[...] = jnp.where(hi_ref[...] == t_hi.astype(HALF_T), lo_ref[...], jnp.full((), HALF_MIN, HALF_T))
    if extra is not None:
        lo_x = jnp.where(hi_x == t_hi, lo_x, HALF_MIN)

    def count_lo(c):
        cnt = _count_ge16(lo_ref, c)
        return cnt if extra is None else cnt + jnp.where(lo_x >= c, 1.0, 0.0)

    t_lo, n_lo = _bitwise_max16(count_lo, k_lo, n_all)
    n_ge = jnp.where(t_lo > HALF_MIN, above + n_lo, n_bucket_up)
    return t_hi * 65536 + (t_lo - HALF_MIN), n_ge


def _selection_bias(key_ref, hi_ref, lo_ref, bias_ref, tri, k, extra=None):
    thr, n_ge = _kth_largest(key_ref, hi_ref, lo_ref, k, extra)
    has_ties = jnp.max(jnp.where(n_ge > k, 1.0, 0.0)) > 0.0

    @pl.when(jnp.logical_not(has_ties))
    def _():
        bias_ref[...] = jnp.where(key_ref[...] >= thr, 0.0, NEG_BIG)

    @pl.when(has_ties)
    def _():
        n_gt = jnp.sum(jnp.where(key_ref[...] > thr, 1.0, 0.0), axis=1, keepdims=True)
        if extra is not None:
            n_gt = n_gt + jnp.where(extra > thr, 1.0, 0.0)
        need = k - n_gt
        offset = jnp.zeros_like(need)
        for c in range(key_ref.shape[1] // TRI_W):
            kc = key_ref[:, c * TRI_W:(c + 1) * TRI_W]
            eq = kc == thr
            e = jnp.where(eq, 1.0, 0.0)
            incl = _dot(e.astype(BF16), tri)
            rank = incl - e + offset
            tie = jnp.where(rank < need, 0.0, NEG_BIG)
            bias_ref[:, c * TRI_W:(c + 1) * TRI_W] = jnp.where(kc > thr, 0.0, jnp.where(eq, tie, NEG_BIG))
            offset = offset + incl[:, TRI_W - 1:TRI_W]

    if extra is None:
        return None
    n_gt = jnp.sum(jnp.where(key_ref[...] > thr, 1.0, 0.0), axis=1, keepdims=True) + jnp.where(extra > thr, 1.0, 0.0)
    n_eq_main = n_ge - n_gt - jnp.where(extra == thr, 1.0, 0.0)
    tie = jnp.where(n_eq_main < k - n_gt, 0.0, NEG_BIG)
    return jnp.where(extra > thr, 0.0, jnp.where(extra == thr, tie, NEG_BIG))


def _prompt_attn_kernel(q_ref, iq_ref, iw_ref, kt_ref, v_ref, ikt_ref, gn_ref, tri_ref, o_ref,
                        key_ref, hi_ref, lo_ref, bias_ref):
    nb, qb = o_ref.shape[0], o_ref.shape[1]
    n_keys = bias_ref.shape[1]
    past = n_keys - qb
    causal = lax.broadcasted_iota(I32, (qb, qb), 1) <= lax.broadcasted_iota(I32, (qb, qb), 0)
    if n_keys > TOPK_MAX:
        for b in range(nb):
            rows = slice(b * qb, (b + 1) * qb)
            iw = iw_ref[b]
            ikt = ikt_ref[b]
            score = jnp.zeros((qb, n_keys), F32)
            s_next = _dot(iq_ref[b, 0], ikt)
            for h in range(IDX_HEADS):
                s, s_next = s_next, (_dot(iq_ref[b, h + 1], ikt) if h + 1 < IDX_HEADS else None)
                score = score + iw[:, IW_LANE + h:IW_LANE + h + 1] * jnp.maximum(s, 0.0)
            key = _sortable_key(score, fold_zero=False)
            key_ref[rows, 0:past] = key[:, 0:past]
            key_ref[rows, past:n_keys] = jnp.where(causal, key[:, past:n_keys], INT_MIN)
        _selection_bias(key_ref, hi_ref, lo_ref, bias_ref, tri_ref[...], float(TOPK_MAX))
        for b in range(nb):
            rows = slice(b * qb, (b + 1) * qb)
            bias_ref[rows, past:n_keys] = jnp.where(causal, bias_ref[rows, past:n_keys], NEG_BIG)
    else:
        for b in range(nb):
            bias_ref[b * qb:(b + 1) * qb, :] = jnp.where(causal, 0.0, NEG_BIG)

    def qk(i):
        b, h = divmod(i, N_HEADS)
        return _dot(q_ref[b, h], kt_ref[b, h // KV_GROUP]) + bias_ref[b * qb:(b + 1) * qb, :]

    logits_next = qk(0)
    for b in range(nb):
        outs = []
        for h in range(N_HEADS):
            i = b * N_HEADS + h
            logits, logits_next = logits_next, (qk(i + 1) if i + 1 < nb * N_HEADS else None)
            m = jnp.max(logits, axis=1, keepdims=True)
            p = jnp.exp2(logits - m).astype(BF16)
            pv = _dot(p, v_ref[b, h // KV_GROUP])
            outs.append(pv[:, 0:HEAD_DIM] / pv[:, HEAD_DIM:HEAD_DIM + 1])
        o_ref[b] = _rms(jnp.concatenate(outs, axis=1), gn_ref[...]).astype(BF16)


def _prompt_attention_block(qh, iqcat, iw, ktb, vb, ikcat, gn_att, tri, q_block):
    bsz, _, t, _ = qh.shape
    qb = Q_BLOCK
    n_keys = (q_block + 1) * qb
    nb = max(1, min(bsz, SEARCH_KEYS_PER_STEP // n_keys))
    while bsz % nb:
        nb -= 1
    rows = nb * qb

    def qheads(n, w):
        return pl.BlockSpec((nb, n, qb, w), lambda b: (b, 0, q_block, 0))

    return pl.pallas_call(
        _prompt_attn_kernel,
        grid=(bsz // nb,),
        in_specs=[qheads(N_HEADS, HEAD_DIM), qheads(IDX_HEADS, IDX_CAT),
                  pl.BlockSpec((nb, qb, TAIL_W), lambda b: (b, q_block, 0)),
                  pl.BlockSpec((nb, N_KV_HEADS, HEAD_DIM, n_keys), lambda b: (b, 0, 0, 0)),
                  pl.BlockSpec((nb, N_KV_HEADS, n_keys, KV_W), lambda b: (b, 0, 0, 0)),
                  pl.BlockSpec((nb, IDX_CAT, n_keys), lambda b: (b, 0, 0)),
                  _const_spec((1, ATT_W)), _const_spec((TRI_W, TRI_W))],
        out_specs=pl.BlockSpec((nb, qb, ATT_W), lambda b: (b, 0, 0)),
        out_shape=jax.ShapeDtypeStruct((bsz, qb, ATT_W), BF16),
        scratch_shapes=[pltpu.VMEM((rows, n_keys), I32), pltpu.VMEM((rows, n_keys), HALF_T),
                        pltpu.VMEM((rows, n_keys), HALF_T), pltpu.VMEM((rows, n_keys), F32)],
        compiler_params=pltpu.CompilerParams(dimension_semantics=("arbitrary",), vmem_limit_bytes=VMEM_LIMIT),
        name=f"prompt_attention_{q_block}",
    )(qh, iqcat, iw, ktb, vb, ikcat, gn_att, tri)


def _prompt_attention(qh, iqcat, iw, ktb, vb, ikcat, gn_att, tri):
    t = qh.shape[2]
    blocks = [_prompt_attention_block(qh, iqcat, iw, ktb, vb, ikcat, gn_att, tri, i) for i in range(t // Q_BLOCK)]
    return jnp.concatenate(blocks, axis=1)


def _post_kernel(x_ref, na_ref, nb_ref, mod_ref, woa_ref, wob_ref, ln2_ref, wg_ref, wu_ref, wd_ref,
                 y_ref, *, shared_mod):
    mod = mod_ref[0] if shared_mod else mod_ref[...]
    g1 = mod[:, 2 * D_MODEL:3 * D_MODEL]
    sh2 = mod[:, 3 * D_MODEL:4 * D_MODEL]
    sc2 = mod[:, 4 * D_MODEL:5 * D_MODEL]
    g2 = mod[:, 5 * D_MODEL:6 * D_MODEL]
    x = x_ref[...]
    mix = _dot(na_ref[...], woa_ref[...]) + _dot(nb_ref[...], wob_ref[...])
    x1 = x + g1 * mix
    h2 = (_rms(x1, ln2_ref[...]) * (1.0 + sc2) + sh2).astype(BF16)
    def gate_up(c):
        return _dot(h2, wg_ref[:, c * FF_CHUNK:(c + 1) * FF_CHUNK]), _dot(h2, wu_ref[:, c * FF_CHUNK:(c + 1) * FF_CHUNK])

    n_chunks = D_FF // FF_CHUNK
    ff = jnp.zeros_like(x1)
    gu_next = gate_up(0)
    for c in range(n_chunks):
        (g, u), gu_next = gu_next, (gate_up(c + 1) if c + 1 < n_chunks else None)
        act = (g * jax.nn.sigmoid(g) * u).astype(BF16)
        ff = ff + _dot(act, wd_ref[c * FF_CHUNK:(c + 1) * FF_CHUNK, :])
    y_ref[...] = x1 + g2 * ff


def _post(x, na, nb, mod, p, rows_per_mod, tile):
    n = x.shape[0]
    shared = rows_per_mod > 1
    if shared:
        mod_spec = pl.BlockSpec((1, 1, 6 * D_MODEL), lambda i: (i * tile // rows_per_mod, 0, 0))
    else:
        mod_spec = pl.BlockSpec((tile, 6 * D_MODEL), lambda i: (i, 0))

    def rows(w):
        return pl.BlockSpec((tile, w), lambda i: (i, 0))

    def weight(shape):
        return pl.BlockSpec(shape, lambda i: (0, 0), pipeline_mode=pl.Buffered(1))

    return pl.pallas_call(
        functools.partial(_post_kernel, shared_mod=shared),
        grid=(n // tile,),
        in_specs=[rows(D_MODEL), rows(LRU_W), rows(ATT_W), mod_spec,
                  weight((LRU_W, D_MODEL)), weight((ATT_W, D_MODEL)), weight((1, D_MODEL)),
                  weight((D_MODEL, D_FF)), weight((D_MODEL, D_FF)), weight((D_FF, D_MODEL))],
        out_specs=rows(D_MODEL),
        out_shape=jax.ShapeDtypeStruct((n, D_MODEL), F32),
        compiler_params=pltpu.CompilerParams(dimension_semantics=("arbitrary",), vmem_limit_bytes=VMEM_LIMIT),
        name="post_ffn",
    )(x, na, nb, mod, p["wo_a"], p["wo_b"], p["ln2"], p["wg"], p["wu"], p["wd"])


def _sample_stage_kernel(x_ref, mod_ref, ln1_ref, wmain_ref, widx_ref, convw_ref, convb_ref,
                         wa_ref, ba_ref, wx_ref, bx_ref, lam_ref, qg_ref, kg_ref, gnl_ref,
                         onesq_ref, onesk_ref, rope_ref, sconv_ref, sh_ref,
                         na_ref, q_ref, k_ref, v_ref, ik_ref, iq_ref, iw_ref, h_ref, conv_ref):
    mod = mod_ref[...]
    z, zi = _project_in(x_ref[...], mod[:, 0:D_MODEL], mod[:, D_MODEL:2 * D_MODEL], ln1_ref[...],
                        wmain_ref[...], _split3(widx_ref[...]))
    xr = z[:, COL_XR:COL_GR]
    convw = convw_ref[...]
    sconv = sconv_ref[...]
    xc = convb_ref[...]
    for i in range(CONV_W - 1):
        xc = xc + sconv[:, i * LRU_W:(i + 1) * LRU_W] * convw[i:i + 1, :]
    xc = xc + xr * convw[CONV_W - 1:CONV_W, :]
    conv_ref[:, 0:(CONV_W - 2) * LRU_W] = sconv[:, LRU_W:(CONV_W - 1) * LRU_W]
    conv_ref[:, (CONV_W - 2) * LRU_W:(CONV_W - 1) * LRU_W] = xr

    a, b = _lru_gates(xc, wa_ref[...], ba_ref[...], wx_ref[...], bx_ref[...], lam_ref[...])
    hr = a * sh_ref[...] + b
    h_ref[...] = hr
    na_ref[...] = _rms(_gelu_tanh(z[:, COL_GR:COL_Q]) * hr, gnl_ref[...]).astype(BF16)

    q, k, v = _qkv_inputs(z, rope_ref[...], qg_ref[...], kg_ref[...], onesq_ref[...], onesk_ref[...])
    iq, tail = _index_inputs(zi[:, 0:IDX_W], zi[:, IDX_W:IDXP_W], rope_ref[...])
    q_ref[...] = q
    k_ref[...] = k
    v_ref[...] = v
    ik_ref[...] = tail[:, 0:IDX_DIM]
    iq_ref[...] = iq
    iw_ref[...] = zi[:, IDX_W:IDXP_W]


def _sample_stage(x, mod_s, p, rope_row, sconv, sh):
    n = x.shape[0]

    def out(w, dt=F32):
        return jax.ShapeDtypeStruct((n, w), dt)

    return pl.pallas_call(
        _sample_stage_kernel,
        out_shape=[out(LRU_W, BF16), out(ATT_W), out(KV_W), out(KV_W), out(IDX_DIM), out(IDX_W),
                   out(TAIL_W), out(LRU_W), out((CONV_W - 1) * LRU_W)],
        compiler_params=pltpu.CompilerParams(vmem_limit_bytes=VMEM_LIMIT),
        name="sample_stage",
    )(x, mod_s, p["ln1"], p["w_main"], p["w_idx"], p["conv_w"], p["conv_b"], p["wa"], p["ba"],
      p["wx"], p["bx"], p["lam"], p["qg"], p["kg"], p["gnl"], p["ones_q"], p["ones_k"], rope_row, sconv, sh)


def _page_copy(cache_ref, buf_ref, sem, page, slot, j, pg):
    return pltpu.make_async_copy(cache_ref.at[page], buf_ref.at[slot, j, pg], sem)


def _paged_prefetch(pt_ref, caches, bufs, sems, group, n_pages):
    s = pl.program_id(0)
    slot = s % 2

    def start(grp, half):
        for cache_ref, buf_ref, sem in zip(caches, bufs, sems):
            for j in range(group):
                for pg in range(n_pages):
                    page = pt_ref[grp * group + j, pg]
                    _page_copy(cache_ref, buf_ref, sem.at[half], page, half, j, pg).start(priority=pg % 2)

    @pl.when(s == 0)
    def _():
        start(0, 0)

    @pl.when(s + 1 < pl.num_programs(0))
    def _():
        start(s + 1, 1 - slot)

    for cache_ref, buf_ref, sem in zip(caches, bufs, sems):
        for j in range(group):
            for pg in range(n_pages):
                _page_copy(cache_ref, buf_ref, sem.at[slot], 0, slot, j, pg).wait()
    return slot


def _sample_score_kernel(pt_ref, iq_ref, iw_ref, ik_hbm, o_ref, ik_buf, sem, *, n_pages):
    group = iq_ref.shape[0]
    slot = _paged_prefetch(pt_ref, (ik_hbm,), (ik_buf,), (sem,), group, n_pages)
    iqcat = [_cat6_lhs(iq_ref[j], 1) for j in range(group)]
    for pg in range(n_pages):
        for j in range(group):
            s = _dot(iqcat[j], _cat6_rhs(ik_buf[slot, j, pg], 0))
            o_ref[j, :, pg * PAGE_SIZE:(pg + 1) * PAGE_SIZE] = jnp.sum(iw_ref[j] * jnp.maximum(s, 0.0), axis=0,
                                                                        keepdims=True)


def _sample_scores(page_table, iq3, iw3, cache_ik):
    n, n_pages = page_table.shape
    g = SCORE_GROUP
    grid_spec = pltpu.PrefetchScalarGridSpec(
        num_scalar_prefetch=1,
        grid=(n // g,),
        in_specs=[pl.BlockSpec((g, IDX_HEADS, IDX_DIM), lambda s, pt: (s, 0, 0)),
                  pl.BlockSpec((g, IDX_HEADS, 1), lambda s, pt: (s, 0, 0)),
                  pl.BlockSpec(memory_space=pl.ANY)],
        out_specs=pl.BlockSpec((g, 1, n_pages * PAGE_SIZE), lambda s, pt: (s, 0, 0)),
        scratch_shapes=[pltpu.VMEM((2, g, n_pages, IDX_DIM, PAGE_SIZE), F32), pltpu.SemaphoreType.DMA((2,))],
    )
    return pl.pallas_call(
        functools.partial(_sample_score_kernel, n_pages=n_pages),
        grid_spec=grid_spec,
        out_shape=jax.ShapeDtypeStruct((n, 1, n_pages * PAGE_SIZE), F32),
        compiler_params=pltpu.CompilerParams(dimension_semantics=("arbitrary",), vmem_limit_bytes=VMEM_LIMIT),
        name="sample_scores",
    )(page_table, iq3, iw3, cache_ik)


def _sample_select_kernel(score_ref, iq_ref, ik_ref, iw_ref, tri_ref, bias_ref, bias_new_ref,
                          key_ref, hi_ref, lo_ref):
    iq = iq_ref[...]
    ik = ik_ref[...]
    iw = iw_ref[...]
    new = jnp.zeros((iq.shape[0], 1), F32)
    for h in range(IDX_HEADS):
        s = jnp.sum(iq[:, h * IDX_DIM:(h + 1) * IDX_DIM] * ik, axis=1, keepdims=True)
        new = new + iw[:, IW_LANE + h:IW_LANE + h + 1] * jnp.maximum(s, 0.0)
    key_ref[...] = _sortable_key(score_ref[...])
    b_new = _selection_bias(key_ref, hi_ref, lo_ref, bias_ref, tri_ref[...], float(TOPK_MAX),
                            extra=_sortable_key(new))
    bias_new_ref[...] = jnp.broadcast_to(b_new, bias_new_ref.shape)


def _sample_select(score, iq, ik, iw, tri):
    n, n_keys = score.shape
    return pl.pallas_call(
        _sample_select_kernel,
        out_shape=[jax.ShapeDtypeStruct((n, n_keys), F32), jax.ShapeDtypeStruct((n, LANES), F32)],
        scratch_shapes=[pltpu.VMEM((n, n_keys), I32), pltpu.VMEM((n, n_keys), HALF_T), pltpu.VMEM((n, n_keys), HALF_T)],
        compiler_params=pltpu.CompilerParams(vmem_limit_bytes=VMEM_LIMIT),
        name="sample_select",
    )(score, iq, ik, iw, tri)


def _sample_attn_kernel(pt_ref, q_ref, kn_ref, vn_ref, bias_ref, bnew_ref, gn_ref, k_hbm, v_hbm, o_ref,
                        k_buf, v_buf, k_sem, v_sem, *, n_pages):
    group = q_ref.shape[0]
    seqs = range(group)
    slot = _paged_prefetch(pt_ref, (k_hbm, v_hbm), (k_buf, v_buf), (k_sem, v_sem), group, n_pages)
    row = lax.broadcasted_iota(I32, (N_HEADS, HEAD_DIM), 0)
    first = row < KV_GROUP
    q = [q_ref[j] for j in seqs]
    q2 = [jnp.concatenate([jnp.where(first, q[j], 0.0), jnp.where(first, 0.0, q[j])], axis=1).astype(BF16)
          for j in seqs]
    logits = [[None] * n_pages for _ in seqs]
    for pg in range(n_pages):
        for j in seqs:
            kp = k_buf[slot, j, pg].astype(BF16).reshape(KV_W, PAGE_SIZE)
            logits[j][pg] = _dot(q2[j], kp) + bias_ref[j][:, pg * PAGE_SIZE:(pg + 1) * PAGE_SIZE]
    m, p_new, den, acc, v_sel = [], [], [], [], []
    for j in seqs:
        kn = kn_ref[j]
        vn = vn_ref[j]
        k_sel = jnp.where(first, kn[:, 0:HEAD_DIM], kn[:, HEAD_DIM:KV_W])
        v_sel.append(jnp.where(first, vn[:, 0:HEAD_DIM], vn[:, HEAD_DIM:KV_W]))
        l_new = jnp.sum(q[j] * k_sel, axis=1, keepdims=True) + bnew_ref[j][:, 0:1]
        mj = l_new
        for l in logits[j]:
            mj = jnp.maximum(mj, jnp.max(l, axis=1, keepdims=True))
        m.append(mj)
        p_new.append(jnp.exp2(l_new - mj))
        den.append(p_new[j])
        acc.append(jnp.zeros((N_HEADS, KV_W), F32))
    for pg in range(n_pages):
        for j in seqs:
            p = jnp.exp2(logits[j][pg] - m[j])
            den[j] = den[j] + jnp.sum(p, axis=1, keepdims=True)
            vp = v_buf[slot, j, pg].astype(BF16).reshape(KV_W, PAGE_SIZE)
            acc[j] = acc[j] + _dot_nt(p.astype(BF16), vp)
    for j in seqs:
        out = (p_new[j] * v_sel[j] + jnp.where(first, acc[j][:, 0:HEAD_DIM], acc[j][:, HEAD_DIM:KV_W])) / den[j]
        ms = jnp.sum(jnp.sum(out * out, axis=1, keepdims=True), axis=0, keepdims=True) * (1.0 / ATT_W)
        o_ref[j] = (out * lax.rsqrt(ms + EPS) * gn_ref[...]).astype(BF16)


def _sample_attention(page_table, q3, k_new, v_new, bias, bias_new, gn8, cache_k, cache_v):
    n, n_pages = page_table.shape
    n_keys = n_pages * PAGE_SIZE

    g = DECODE_GROUP

    def per_seq(shape):
        return pl.BlockSpec((g,) + shape, lambda s, pt: (s, 0, 0))

    page_buf = pltpu.VMEM((2, g, n_pages, N_KV_HEADS, HEAD_DIM, PAGE_SIZE), F32)
    grid_spec = pltpu.PrefetchScalarGridSpec(
        num_scalar_prefetch=1,
        grid=(n // g,),
        in_specs=[per_seq((N_HEADS, HEAD_DIM)), per_seq((1, KV_W)), per_seq((1, KV_W)),
                  per_seq((1, n_keys)), per_seq((1, LANES)),
                  pl.BlockSpec((N_HEADS, HEAD_DIM), lambda s, pt: (0, 0)),
                  pl.BlockSpec(memory_space=pl.ANY), pl.BlockSpec(memory_space=pl.ANY)],
        out_specs=per_seq((N_HEADS, HEAD_DIM)),
        scratch_shapes=[page_buf, page_buf, pltpu.SemaphoreType.DMA((2,)), pltpu.SemaphoreType.DMA((2,))],
    )
    return pl.pallas_call(
        functools.partial(_sample_attn_kernel, n_pages=n_pages),
        grid_spec=grid_spec,
        out_shape=jax.ShapeDtypeStruct((n, N_HEADS, HEAD_DIM), BF16),
        compiler_params=pltpu.CompilerParams(dimension_semantics=("arbitrary",), vmem_limit_bytes=VMEM_LIMIT),
        name="sample_attention",
    )(page_table, q3, k_new, v_new, bias, bias_new, gn8, cache_k, cache_v)


def _rope_table_np(positions):
    half = ROT_DIM // 2
    freq = ROPE_THETA ** (-(np.arange(half, dtype=np.float64) / half))
    ang = np.asarray(positions, np.float64)[:, None] * freq[None, :]
    cos, sin = np.cos(ang), np.sin(ang)
    n = len(positions)
    tab = np.zeros((n, 3, LANES), np.float64)
    tab[:, 0, :] = 1.0
    for base in range(0, LANES, HEAD_DIM):
        tab[:, 0, base:base + half] = cos
        tab[:, 0, base + half:base + ROT_DIM] = cos
        tab[:, 1, base:base + half] = -sin
        tab[:, 2, base + half:base + ROT_DIM] = sin
    return tab.reshape(n, 3 * LANES).astype(np.float32)


def _block_diag(w):
    n, a, b = w.shape
    return jnp.einsum("nij,nm->nimj", w, jnp.eye(n, dtype=w.dtype)).reshape(n * a, n * b)


def _layer_params(l, ln1_g, w_in, conv_w, conv_b, lru_wa, lru_ba, lru_wx, lru_bx, lru_lambda, q_norm_g,
                  k_norm_g, gn_lru_g, gn_att_g, w_out, ln2_g, w_gate, w_up, w_down):
    w_in_t = jnp.transpose(w_in[l])
    w_idx = jnp.pad(w_in_t[COL_IQ:], ((0, IDXP_W - (w_in.shape[2] - COL_IQ)), (0, 0)))
    head_id = np.arange(ATT_W) // HEAD_DIM
    ones_q = (head_id[:, None] == head_id[None, :]).astype(np.float32)
    return {
        "ln1": ln1_g[l][None, :],
        "w_main": w_in_t[:COL_IQ].astype(BF16),
        "w_idx": w_idx,
        "conv_w": conv_w[l],
        "conv_b": conv_b[l][None, :],
        "wa": _block_diag(lru_wa[l]).astype(BF16),
        "ba": lru_ba[l][None, :],
        "wx": _block_diag(lru_wx[l]).astype(BF16),
        "bx": lru_bx[l][None, :],
        "lam": lru_lambda[l][None, :],
        "qg": jnp.tile(q_norm_g[l], N_HEADS)[None, :],
        "kg": jnp.tile(k_norm_g[l], N_KV_HEADS)[None, :],
        "gnl": gn_lru_g[l][None, :],
        "gna": gn_att_g[l][None, :],
        "gna8": gn_att_g[l].reshape(N_HEADS, HEAD_DIM),
        "ones_q": jnp.asarray(ones_q, BF16),
        "ones_k": jnp.asarray(ones_q[:KV_W, :KV_W], BF16),
        "wo_a": w_out[l][:LRU_W].astype(BF16),
        "wo_b": w_out[l][LRU_W:].astype(BF16),
        "ln2": ln2_g[l][None, :],
        "wg": w_gate[l].astype(BF16),
        "wu": w_up[l].astype(BF16),
        "wd": w_down[l].astype(BF16),
    }


def kernel(x_prompt, x_sample, cache_k, cache_v, cache_ik, state_h, state_conv, page_table, c_prompt, c_sample, ada_w, ada_b, ln1_g, w_in, conv_w, conv_b, lru_wa, lru_ba, lru_wx, lru_bx, lru_lambda, q_norm_g, k_norm_g, gn_lru_g, gn_att_g, w_out, ln2_g, w_gate, w_up, w_down):
    bsz, t, _ = x_prompt.shape
    dbsz, dt, _ = x_sample.shape
    depth = ada_w.shape[0]
    n_pages = page_table.shape[1]
    past_len = n_pages * PAGE_SIZE
    assert dt == 1 and t % PROMPT_ROW_TILE == 0 and t % POST_ROW_TILE == 0 and t % Q_BLOCK == 0
    assert t // 4 >= TOPK_MAX and (past_len + dt) // 4 >= TOPK_MAX
    assert dbsz % DECODE_GROUP == 0 and dbsz % SCORE_GROUP == 0 and cache_k.shape[2] == PAGE_SIZE

    rope_p = jnp.asarray(_rope_table_np(np.arange(t)))
    rope_s = jnp.asarray(_rope_table_np(past_len + np.arange(dt)))
    tri = jnp.asarray(np.triu(np.ones((TRI_W, TRI_W), np.float32)), BF16)

    yp = x_prompt.reshape(bsz * t, D_MODEL)
    ys = x_sample.reshape(dbsz, D_MODEL)
    c_all = jnp.concatenate([c_prompt, c_sample], axis=0)
    outs_p, outs_s = [], []
    for l in range(depth):
        p = _layer_params(l, ln1_g, w_in, conv_w, conv_b, lru_wa, lru_ba, lru_wx, lru_bx, lru_lambda,
                          q_norm_g, k_norm_g, gn_lru_g, gn_att_g, w_out, ln2_g, w_gate, w_up, w_down)
        mod = _ada(c_all, ada_w[l], ada_b[l][None, :])
        mod_p = mod[:bsz].reshape(bsz, 1, 6 * D_MODEL)
        mod_s = mod[bsz:]

        (na, qh, kt, vt, ikt, ktb, vb, ikcat, iqcat, iw, h_last, conv_new) = _prompt_stage(
            yp.reshape(bsz, t, D_MODEL), mod_p, p, rope_p)
        nb = _prompt_attention(qh, iqcat, iw, ktb, vb, ikcat, p["gna"], tri)
        yp = _post(yp, na.reshape(bsz * t, LRU_W), nb.reshape(bsz * t, ATT_W), mod_p, p, t, POST_ROW_TILE)
        outs_p.append((jnp.transpose(kt, (0, 3, 1, 2)), jnp.transpose(vt, (0, 3, 1, 2)),
                       jnp.transpose(ikt, (0, 2, 1)), h_last.reshape(bsz, LRU_W), conv_new))

        (na_s, q_s, k_s, v_s, ik_s, iq_s, iw_s, h_s, conv_s) = _sample_stage(
            ys, mod_s, p, rope_s, state_conv[l].reshape(dbsz, (CONV_W - 1) * LRU_W), state_h[l])
        score = _sample_scores(page_table, iq_s.reshape(dbsz, IDX_HEADS, IDX_DIM),
                               iw_s[:, IW_LANE:IW_LANE + IDX_HEADS].reshape(dbsz, IDX_HEADS, 1),
                               jnp.transpose(cache_ik[l], (0, 2, 1)))
        bias, bias_new = _sample_select(score.reshape(dbsz, past_len), iq_s, ik_s, iw_s, tri)
        nb_s = _sample_attention(page_table, q_s.reshape(dbsz, N_HEADS, HEAD_DIM),
                                 k_s.reshape(dbsz, 1, KV_W), v_s.reshape(dbsz, 1, KV_W),
                                 bias.reshape(dbsz, 1, past_len), bias_new.reshape(dbsz, 1, LANES), p["gna8"],
                                 jnp.transpose(cache_k[l], (0, 2, 3, 1)), jnp.transpose(cache_v[l], (0, 2, 3, 1)))
        ys = _post(ys, na_s, nb_s.reshape(dbsz, ATT_W), mod_s, p, 1, dbsz)
        outs_s.append((k_s.reshape(dbsz, dt, N_KV_HEADS, HEAD_DIM), v_s.reshape(dbsz, dt, N_KV_HEADS, HEAD_DIM),
                       ik_s.reshape(dbsz, dt, IDX_DIM), h_s,
                       conv_s.reshape(dbsz, CONV_W - 1, LRU_W)))

    def stack(outs, i):
        return jnp.stack([o[i] for o in outs])

    return (yp.reshape(bsz, t, D_MODEL), ys.reshape(dbsz, dt, D_MODEL),
            stack(outs_p, 0), stack(outs_p, 1), stack(outs_p, 2), stack(outs_p, 3), stack(outs_p, 4),
            stack(outs_s, 0), stack(outs_s, 1), stack(outs_s, 2), stack(outs_s, 3), stack(outs_s, 4))
```

```python
import functools

import numpy as np
import jax
import jax.numpy as jnp
from jax import lax
from jax.experimental import pallas as pl
from jax.experimental.pallas import tpu as pltpu

F32 = jnp.float32
BF16 = jnp.bfloat16
I32 = jnp.int32
HALF_T = jnp.int16

D_MODEL = 1024
LRU_W = 512
LRU_BLOCKS = 8
LRU_BW = LRU_W // LRU_BLOCKS
CONV_W = 4
LRU_C = 8.0
N_HEADS = 8
HEAD_DIM = 64
ATT_W = N_HEADS * HEAD_DIM
N_KV_HEADS = 2
KV_GROUP = N_HEADS // N_KV_HEADS
KV_W = N_KV_HEADS * HEAD_DIM
ROT_DIM = HEAD_DIM // 4
ROPE_THETA = 500000.0
IDX_HEADS = 4
IDX_DIM = 64
IDX_W = IDX_HEADS * IDX_DIM
TOPK_MAX = 256
PAGE_SIZE = 128
D_FF = 2816
EPS = 1e-6

COL_XR = 0
COL_GR = COL_XR + LRU_W
COL_Q = COL_GR + LRU_W
COL_K = COL_Q + ATT_W
COL_V = COL_K + KV_W
COL_IQ = COL_V + KV_W
MAIN_W = COL_IQ
TAIL_W = 128
IDXP_W = IDX_W + TAIL_W
IW_LANE = IDX_DIM
IDX_CAT = 6 * IDX_DIM

LANES = 128
SUBLANES = 8
TRI_W = 256
GATE_TILE = 256
INT_MIN = -2147483648
HALF_MIN = -32768
LOG2E = 1.4426950408889634
NEG_BIG = -1e30
VMEM_LIMIT = 48 * 1024 * 1024

PROMPT_ROW_TILE = 512
Q_BLOCK = 256
POST_ROW_TILE = 512
FF_CHUNK = D_FF
ADA_COL_TILE = 512
DECODE_GROUP = 8
SCORE_GROUP = 8
SEARCH_KEYS_PER_STEP = 2048


def _dot(a, b):
    return jnp.dot(a, b, preferred_element_type=F32)


def _dot_nt(a, b):
    return lax.dot_general(a, b, (((1,), (1,)), ((), ())), preferred_element_type=F32)


def _split_bf16(x):
    hi = x.astype(BF16)
    lo = (x - hi.astype(F32)).astype(BF16)
    return hi, lo


def _split3(x):
    p1 = x.astype(BF16)
    r1 = x - p1.astype(F32)
    p2 = r1.astype(BF16)
    p3 = (r1 - p2.astype(F32)).astype(BF16)
    return p1, p2, p3


def _dot_f32(a, b, dot=_dot):
    a1, a2, a3 = a
    b1, b2, b3 = b
    return ((dot(a1, b3) + dot(a2, b2) + dot(a3, b1)) + (dot(a1, b2) + dot(a2, b1))) + dot(a1, b1)


def _cat6_lhs(x, axis):
    r1 = x - x.astype(BF16).astype(F32)
    r2 = r1 - r1.astype(BF16).astype(F32)
    return jnp.concatenate([x, x, r1, x, r1, r2], axis=axis).astype(BF16)


def _cat6_rhs(x, axis):
    p1, p2, p3 = _split3(x)
    return jnp.concatenate([p1, p2, p1, p3, p2, p1], axis=axis)


def _rms(x, g):
    return x * lax.rsqrt(jnp.mean(x * x, axis=-1, keepdims=True) + EPS) * g


def _head_rms(x, ones_bd, g):
    hi, lo = _split_bf16(x * x)
    ss = _dot(hi, ones_bd) + _dot(lo, ones_bd)
    return x * lax.rsqrt(ss * (1.0 / HEAD_DIM) + EPS) * g


def _rope(x, cos, sin_next, sin_prev):
    w = x.shape[-1]
    half = ROT_DIM // 2
    return x * cos + pltpu.roll(x, w - half, 1) * sin_next + pltpu.roll(x, half, 1) * sin_prev


def _rope_tables(tab, reps):
    cos = tab[:, 0:LANES]
    sa = tab[:, LANES:2 * LANES]
    sb = tab[:, 2 * LANES:3 * LANES]
    if reps > 1:
        cos = jnp.concatenate([cos] * reps, axis=1)
        sa = jnp.concatenate([sa] * reps, axis=1)
        sb = jnp.concatenate([sb] * reps, axis=1)
    return cos, sa, sb


def _rope_tail(zt, tab):
    cos, sa, sb = _rope_tables(tab, 1)
    lane = lax.broadcasted_iota(I32, zt.shape, 1)
    first = lane < IDX_DIM
    return _rope(zt, jnp.where(first, cos, 1.0), jnp.where(first, sa, 0.0), jnp.where(first, sb, 0.0))


def _gelu_tanh(x):
    c = float(np.sqrt(2.0 / np.pi))
    return x * (0.5 * (1.0 + jnp.tanh(c * (x + 0.044715 * (x * x * x)))))


def _softplus(x):
    return jnp.maximum(x, 0.0) + jnp.log1p(jnp.exp(-jnp.abs(x)))


def _project_in(x, sh1, sc1, ln1, w_main_t, w_idx_t_parts):
    h = _rms(x, ln1) * (1.0 + sc1) + sh1
    hp = _split3(h)
    return _dot_nt(hp[0], w_main_t), _dot_f32(hp, w_idx_t_parts, _dot_nt)


def _lru_gates(xc, wa, ba, wx, bx, lam):
    xcb = xc.astype(BF16)

    def block_diag_dot(w):
        return jnp.concatenate([_dot(xcb[:, j:j + GATE_TILE], w[j:j + GATE_TILE, j:j + GATE_TILE])
                                for j in range(0, LRU_W, GATE_TILE)], axis=1)

    r = jax.nn.sigmoid(block_diag_dot(wa) + ba)
    gi = jax.nn.sigmoid(block_diag_dot(wx) + bx)
    log_a = (-LRU_C * _softplus(-lam)) * r
    a = jnp.exp(log_a)
    y = jnp.tanh(-log_a) * (1.0 + a * a)
    inp = jnp.where(y > 0.0, y * lax.rsqrt(y), 0.0) * gi * xc
    return a, inp


def _qkv_inputs(z, tab, qg, kg, ones_q, ones_k):
    cq, saq, sbq = _rope_tables(tab, ATT_W // LANES)
    ck, sak, sbk = _rope_tables(tab, KV_W // LANES)
    q = _rope(_head_rms(z[:, COL_Q:COL_K], ones_q, qg), cq, saq, sbq) * (HEAD_DIM ** -0.5 * LOG2E)
    k = _rope(_head_rms(z[:, COL_K:COL_V], ones_k, kg), ck, sak, sbk)
    return q, k, z[:, COL_V:COL_IQ]


def _index_inputs(zi_iq, zi_tail, tab):
    ci, sai, sbi = _rope_tables(tab, IDX_W // LANES)
    return _rope(zi_iq, ci, sai, sbi), _rope_tail(zi_tail, tab)


def _ada_kernel(c_ref, w_ref, b_ref, o_ref):
    c = c_ref[...]
    s = c * jax.nn.sigmoid(c)
    feeds_index = pl.program_id(0) * ADA_COL_TILE < 2 * D_MODEL

    @pl.when(feeds_index)
    def _():
        o_ref[...] = _dot_f32(_split3(s), _split3(w_ref[...])) + b_ref[...]

    @pl.when(jnp.logical_not(feeds_index))
    def _():
        (s1, s2, _), (w1, w2, _) = _split3(s), _split3(w_ref[...])
        o_ref[...] = ((_dot(s1, w2) + _dot(s2, w1)) + _dot(s1, w1)) + b_ref[...]


def _ada(c_all, ada_w, ada_b):
    rows = c_all.shape[0]
    n = ada_w.shape[1]
    return pl.pallas_call(
        _ada_kernel,
        grid=(n // ADA_COL_TILE,),
        in_specs=[
            pl.BlockSpec((rows, D_MODEL), lambda j: (0, 0)),
            pl.BlockSpec((D_MODEL, ADA_COL_TILE), lambda j: (0, j)),
            pl.BlockSpec((1, ADA_COL_TILE), lambda j: (0, j)),
        ],
        out_specs=pl.BlockSpec((rows, ADA_COL_TILE), lambda j: (0, j)),
        out_shape=jax.ShapeDtypeStruct((rows, n), F32),
        compiler_params=pltpu.CompilerParams(dimension_semantics=("arbitrary",), vmem_limit_bytes=VMEM_LIMIT),
        name="ada_modulation",
    )(c_all, ada_w, ada_b)


def _prompt_stage_kernel(x_ref, mod_ref, ln1_ref, wmain_ref, widx_ref, convw_ref, convb_ref,
                         wa_ref, ba_ref, wx_ref, bx_ref, lam_ref, qg_ref, kg_ref, gnl_ref,
                         onesq_ref, onesk_ref, rope_ref,
                         na_ref, q_ref, kt_ref, vt_ref, ikt_ref, ktb_ref, vb_ref,
                         ikcat_ref, iqcat_ref, iw_ref, h_ref, conv_ref,
                         xr_buf, hcarry, wiq_parts, wtail_cat):
    tt = x_ref.shape[1]
    t = pl.program_id(1)

    @pl.when((t == 0) & (pl.program_id(0) == 0))
    def _():
        for j, part in enumerate(_split3(widx_ref[...])):
            wiq_parts[j] = part[0:IDX_W]
            wtail_cat[j * TAIL_W:(j + 1) * TAIL_W] = part[IDX_W:IDXP_W]

    @pl.when(t == 0)
    def _():
        xr_buf[0:SUBLANES, :] = jnp.zeros((SUBLANES, LRU_W), F32)
        hcarry[...] = jnp.zeros_like(hcarry)

    mod = mod_ref[0]
    h1, h2, h3 = _split3(_rms(x_ref[0], ln1_ref[...]) * (1.0 + mod[:, D_MODEL:2 * D_MODEL]) + mod[:, 0:D_MODEL])
    z = _dot_nt(h1, wmain_ref[...])
    iq_small = _dot_nt(h1, wiq_parts[2]) + _dot_nt(h2, wiq_parts[1])
    t1 = _dot_nt(h1, wtail_cat[...])
    t2 = _dot_nt(h2, wtail_cat[0:2 * TAIL_W])
    t3 = _dot_nt(h3, wtail_cat[0:TAIL_W])

    xr = z[:, COL_XR:COL_GR]
    xr_buf[SUBLANES:SUBLANES + tt, :] = xr
    convw = convw_ref[...]
    xc = convb_ref[...]
    for i in range(CONV_W):
        xc = xc + xr_buf[pl.ds(SUBLANES - (CONV_W - 1) + i, tt), :] * convw[i:i + 1, :]
    conv_ref[0] = xr_buf[pl.ds(SUBLANES + tt - (CONV_W - 1), CONV_W - 1), :]
    xr_buf[0:SUBLANES, :] = xr_buf[tt:tt + SUBLANES, :]

    iq_small = iq_small + _dot_nt(h3, wiq_parts[0])
    iq_mid = _dot_nt(h1, wiq_parts[1])
    a, b = _lru_gates(xc, wa_ref[...], ba_ref[...], wx_ref[...], bx_ref[...], lam_ref[...])
    iq_mid = iq_mid + _dot_nt(h2, wiq_parts[0])

    q, k, v = _qkv_inputs(z, rope_ref[...], qg_ref[...], kg_ref[...], onesq_ref[...], onesk_ref[...])
    for h in range(N_HEADS):
        q_ref[0, h] = q[:, h * HEAD_DIM:(h + 1) * HEAD_DIM].astype(BF16)
    kt = jnp.transpose(k)
    vt = jnp.transpose(v)
    kt_ref[0] = kt.reshape(N_KV_HEADS, HEAD_DIM, tt)
    vt_ref[0] = vt.reshape(N_KV_HEADS, HEAD_DIM, tt)
    ktb_ref[0] = kt.astype(BF16).reshape(N_KV_HEADS, HEAD_DIM, tt)
    lane = lax.broadcasted_iota(I32, v.shape, 1)
    pad = jnp.where(lane == HEAD_DIM, 1.0, 0.0)
    for n in range(N_KV_HEADS):
        vn = v if n == 0 else pltpu.roll(v, KV_W - n * HEAD_DIM, 1)
        vb_ref[0, n] = jnp.where(lane < HEAD_DIM, vn, pad).astype(BF16)

    iq_big = _dot_nt(h1, wiq_parts[0])
    row = lax.broadcasted_iota(I32, a.shape, 0) % SUBLANES
    d = 1
    while d < SUBLANES:
        keep = row >= d
        a_s = jnp.where(keep, pltpu.roll(a, d, 0), 1.0)
        b_s = jnp.where(keep, pltpu.roll(b, d, 0), 0.0)
        b = a * b_s + b
        a = a * a_s
        d *= 2
    h_in = hcarry[...]
    groups = []
    for g in range(tt // SUBLANES):
        rows = slice(g * SUBLANES, (g + 1) * SUBLANES)
        hg = a[rows] * h_in + b[rows]
        h_in = hg[SUBLANES - 1:SUBLANES, :]
        groups.append(hg)
    hr = jnp.concatenate(groups, axis=0)
    hcarry[...] = h_in
    h_ref[0] = h_in

    na_ref[0] = _rms(_gelu_tanh(z[:, COL_GR:COL_Q]) * hr, gnl_ref[...]).astype(BF16)

    zi_iq = (iq_small + iq_mid) + iq_big
    zi_tail = (((t1[:, 2 * TAIL_W:3 * TAIL_W] + t2[:, TAIL_W:2 * TAIL_W]) + t3)
               + (t1[:, TAIL_W:2 * TAIL_W] + t2[:, 0:TAIL_W])) + t1[:, 0:TAIL_W]
    iq, tail = _index_inputs(zi_iq, zi_tail, rope_ref[...])
    ikt = jnp.transpose(tail)[0:IDX_DIM, :]
    ikt_ref[0] = ikt
    ikcat_ref[0] = _cat6_rhs(ikt, 0)
    for h in range(IDX_HEADS):
        iqcat_ref[0, h] = _cat6_lhs(iq[:, h * IDX_DIM:(h + 1) * IDX_DIM], 1)
    iw_ref[0] = zi_tail


def _const_spec(shape):
    nd = len(shape)
    return pl.BlockSpec(shape, lambda *_: (0,) * nd)


def _prompt_stage(x, mod_p, p, rope_tab):
    bsz, t, _ = x.shape
    tt = PROMPT_ROW_TILE

    def rows(w):
        return pl.BlockSpec((1, tt, w), lambda b, i: (b, i, 0))

    in_specs = [
        rows(D_MODEL),
        pl.BlockSpec((1, 1, 6 * D_MODEL), lambda b, i: (b, 0, 0)),
        _const_spec((1, D_MODEL)),
        _const_spec((MAIN_W, D_MODEL)),
        _const_spec((IDXP_W, D_MODEL)),
        _const_spec((CONV_W, LRU_W)),
        _const_spec((1, LRU_W)),
        _const_spec((LRU_W, LRU_W)),
        _const_spec((1, LRU_W)),
        _const_spec((LRU_W, LRU_W)),
        _const_spec((1, LRU_W)),
        _const_spec((1, LRU_W)),
        _const_spec((1, ATT_W)),
        _const_spec((1, KV_W)),
        _const_spec((1, LRU_W)),
        _const_spec((ATT_W, ATT_W)),
        _const_spec((KV_W, KV_W)),
        pl.BlockSpec((tt, 3 * LANES), lambda b, i: (i, 0)),
    ]

    def heads(n, w):
        return pl.BlockSpec((1, n, tt, w), lambda b, i: (b, 0, i, 0))

    def heads_t(n):
        return pl.BlockSpec((1, n, HEAD_DIM, tt), lambda b, i: (b, 0, 0, i))

    def feat_t(w):
        return pl.BlockSpec((1, w, tt), lambda b, i: (b, 0, i))

    out_specs = [
        rows(LRU_W), heads(N_HEADS, HEAD_DIM), heads_t(N_KV_HEADS), heads_t(N_KV_HEADS), feat_t(IDX_DIM),
        heads_t(N_KV_HEADS), heads(N_KV_HEADS, KV_W), feat_t(IDX_CAT), heads(IDX_HEADS, IDX_CAT), rows(TAIL_W),
        pl.BlockSpec((1, 1, LRU_W), lambda b, i: (b, 0, 0)),
        pl.BlockSpec((1, CONV_W - 1, LRU_W), lambda b, i: (b, 0, 0)),
    ]
    out_shape = [
        jax.ShapeDtypeStruct((bsz, t, LRU_W), BF16),
        jax.ShapeDtypeStruct((bsz, N_HEADS, t, HEAD_DIM), BF16),
        jax.ShapeDtypeStruct((bsz, N_KV_HEADS, HEAD_DIM, t), F32),
        jax.ShapeDtypeStruct((bsz, N_KV_HEADS, HEAD_DIM, t), F32),
        jax.ShapeDtypeStruct((bsz, IDX_DIM, t), F32),
        jax.ShapeDtypeStruct((bsz, N_KV_HEADS, HEAD_DIM, t), BF16),
        jax.ShapeDtypeStruct((bsz, N_KV_HEADS, t, KV_W), BF16),
        jax.ShapeDtypeStruct((bsz, IDX_CAT, t), BF16),
        jax.ShapeDtypeStruct((bsz, IDX_HEADS, t, IDX_CAT), BF16),
        jax.ShapeDtypeStruct((bsz, t, TAIL_W), F32),
        jax.ShapeDtypeStruct((bsz, 1, LRU_W), F32),
        jax.ShapeDtypeStruct((bsz, CONV_W - 1, LRU_W), F32),
    ]
    return pl.pallas_call(
        _prompt_stage_kernel,
        grid=(bsz, t // tt),
        in_specs=in_specs,
        out_specs=out_specs,
        out_shape=out_shape,
        scratch_shapes=[pltpu.VMEM((tt + SUBLANES, LRU_W), F32), pltpu.VMEM((1, LRU_W), F32),
                        pltpu.VMEM((3, IDX_W, D_MODEL), BF16), pltpu.VMEM((3 * TAIL_W, D_MODEL), BF16)],
        compiler_params=pltpu.CompilerParams(dimension_semantics=("arbitrary", "arbitrary"),
                                             vmem_limit_bytes=VMEM_LIMIT),
        name="prompt_stage",
    )(x, mod_p, p["ln1"], p["w_main"], p["w_idx"], p["conv_w"], p["conv_b"], p["wa"], p["ba"],
      p["wx"], p["bx"], p["lam"], p["qg"], p["kg"], p["gnl"], p["ones_q"], p["ones_k"], rope_tab)


def _sortable_key(score, fold_zero=True):
    if fold_zero:
        score = jnp.where(score == 0.0, 0.0, score)
    bits = lax.bitcast_convert_type(score, I32)
    return jnp.where(bits < 0, bits ^ jnp.int32(0x7FFFFFFF), bits)


def _count_ge16(ref, c, strict=False):
    c16 = c.astype(HALF_T)
    hit = ref[...] > c16 if strict else ref[...] >= c16
    ones = jnp.where(hit, jnp.ones((), HALF_T), jnp.zeros((), HALF_T))
    acc = ones[:, 0:LANES]
    for j in range(1, ref.shape[1] // LANES):
        acc = acc + ones[:, j * LANES:(j + 1) * LANES]
    return jnp.sum(acc.astype(F32), axis=1, keepdims=True)


def _bitwise_max16(count_ge, k, n_all):
    def body(i, carry):
        t, ct = carry
        c = t + lax.shift_left(jnp.int32(1), jnp.int32(15) - i)
        cn = count_ge(c)
        ok = cn >= k
        return jnp.where(ok, c, t), jnp.where(ok, cn, ct)

    init = (jnp.full(k.shape, HALF_MIN, I32), jnp.full(k.shape, float(n_all), F32))
    return lax.fori_loop(0, 16, body, init, unroll=True)


def _kth_largest(key_ref, hi_ref, lo_ref, k, extra=None):
    key = key_ref[...]
    hi_ref[...] = lax.shift_right_arithmetic(key, 16).astype(HALF_T)
    lo_ref[...] = ((key & 0xFFFF) + HALF_MIN).astype(HALF_T)
    kf = jnp.full((key_ref.shape[0], 1), k, F32)
    n_all = key_ref.shape[1] + (0 if extra is None else 1)
    if extra is not None:
        hi_x = lax.shift_right_arithmetic(extra, 16)
        lo_x = (extra & 0xFFFF) + HALF_MIN

    def count_hi(c):
        cnt = _count_ge16(hi_ref, c)
        return cnt if extra is None else cnt + jnp.where(hi_x >= c, 1.0, 0.0)

    t_hi, n_bucket_up = _bitwise_max16(count_hi, kf, n_all)
    above = _count_ge16(hi_ref, t_hi, strict=True)
    if extra is not None:
        above = above + jnp.where(hi_x > t_hi, 1.0, 0.0)
    k_lo = kf - above
    lo_ref[...] = jnp.where(hi_ref[...] == t_hi.astype(HALF_T), lo_ref[...], jnp.full((), HALF_MIN, HALF_T))
    if extra is not None:
        lo_x = jnp.where(hi_x == t_hi, lo_x, HALF_MIN)

    def count_lo(c):
        cnt = _count_ge16(lo_ref, c)
        return cnt if extra is None else cnt + jnp.where(lo_x >= c, 1.0, 0.0)

    t_lo, n_lo = _bitwise_max16(count_lo, k_lo, n_all)
    n_ge = jnp.where(t_lo > HALF_MIN, above + n_lo, n_bucket_up)
    return t_hi * 65536 + (t_lo - HALF_MIN), n_ge


def _selection_bias(key_ref, hi_ref, lo_ref, bias_ref, tri, k, extra=None):
    thr, n_ge = _kth_largest(key_ref, hi_ref, lo_ref, k, extra)
    has_ties = jnp.max(jnp.where(n_ge > k, 1.0, 0.0)) > 0.0

    @pl.when(jnp.logical_not(has_ties))
    def _():
        bias_ref[...] = jnp.where(key_ref[...] >= thr, 0.0, NEG_BIG)

    @pl.when(has_ties)
    def _():
        n_gt = jnp.sum(jnp.where(key_ref[...] > thr, 1.0, 0.0), axis=1, keepdims=True)
        if extra is not None:
            n_gt = n_gt + jnp.where(extra > thr, 1.0, 0.0)
        need = k - n_gt
        offset = jnp.zeros_like(need)
        for c in range(key_ref.shape[1] // TRI_W):
            kc = key_ref[:, c * TRI_W:(c + 1) * TRI_W]
            eq = kc == thr
            e = jnp.where(eq, 1.0, 0.0)
            incl = _dot(e.astype(BF16), tri)
            rank = incl - e + offset
            tie = jnp.where(rank < need, 0.0, NEG_BIG)
            bias_ref[:, c * TRI_W:(c + 1) * TRI_W] = jnp.where(kc > thr, 0.0, jnp.where(eq, tie, NEG_BIG))
            offset = offset + incl[:, TRI_W - 1:TRI_W]

    if extra is None:
        return None
    n_gt = jnp.sum(jnp.where(key_ref[...] > thr, 1.0, 0.0), axis=1, keepdims=True) + jnp.where(extra > thr, 1.0, 0.0)
    n_eq_main = n_ge - n_gt - jnp.where(extra == thr, 1.0, 0.0)
    tie = jnp.where(n_eq_main < k - n_gt, 0.0, NEG_BIG)
    return jnp.where(extra > thr, 0.0, jnp.where(extra == thr, tie, NEG_BIG))


def _prompt_attn_kernel(q_ref, iq_ref, iw_ref, kt_ref, v_ref, ikt_ref, gn_ref, tri_ref, o_ref,
                        key_ref, hi_ref, lo_ref, bias_ref):
    nb, qb = o_ref.shape[0], o_ref.shape[1]
    n_keys = bias_ref.shape[1]
    past = n_keys - qb
    causal = lax.broadcasted_iota(I32, (qb, qb), 1) <= lax.broadcasted_iota(I32, (qb, qb), 0)
    if n_keys > TOPK_MAX:
        for b in range(nb):
            rows = slice(b * qb, (b + 1) * qb)
            iw = iw_ref[b]
            ikt = ikt_ref[b]
            score = jnp.zeros((qb, n_keys), F32)
            s_next = _dot(iq_ref[b, 0], ikt)
            for h in range(IDX_HEADS):
                s, s_next = s_next, (_dot(iq_ref[b, h + 1], ikt) if h + 1 < IDX_HEADS else None)
                score = score + iw[:, IW_LANE + h:IW_LANE + h + 1] * jnp.maximum(s, 0.0)
            key = _sortable_key(score, fold_zero=False)
            key_ref[rows, 0:past] = key[:, 0:past]
            key_ref[rows, past:n_keys] = jnp.where(causal, key[:, past:n_keys], INT_MIN)
        _selection_bias(key_ref, hi_ref, lo_ref, bias_ref, tri_ref[...], float(TOPK_MAX))
        for b in range(nb):
            rows = slice(b * qb, (b + 1) * qb)
            bias_ref[rows, past:n_keys] = jnp.where(causal, bias_ref[rows, past:n_keys], NEG_BIG)
    else:
        for b in range(nb):
            bias_ref[b * qb:(b + 1) * qb, :] = jnp.where(causal, 0.0, NEG_BIG)

    def qk(i):
        b, h = divmod(i, N_HEADS)
        return _dot(q_ref[b, h], kt_ref[b, h // KV_GROUP]) + bias_ref[b * qb:(b + 1) * qb, :]

    logits_next = qk(0)
    for b in range(nb):
        outs = []
        for h in range(N_HEADS):
            i = b * N_HEADS + h
            logits, logits_next = logits_next, (qk(i + 1) if i + 1 < nb * N_HEADS else None)
            m = jnp.max(logits, axis=1, keepdims=True)
            p = jnp.exp2(logits - m).astype(BF16)
            pv = _dot(p, v_ref[b, h // KV_GROUP])
            outs.append(pv[:, 0:HEAD_DIM] / pv[:, HEAD_DIM:HEAD_DIM + 1])
        o_ref[b] = _rms(jnp.concatenate(outs, axis=1), gn_ref[...]).astype(BF16)


def _prompt_attention_block(qh, iqcat, iw, ktb, vb, ikcat, gn_att, tri, q_block):
    bsz, _, t, _ = qh.shape
    qb = Q_BLOCK
    n_keys = (q_block + 1) * qb
    nb = max(1, min(bsz, SEARCH_KEYS_PER_STEP // n_keys))
    while bsz % nb:
        nb -= 1
    rows = nb * qb

    def qheads(n, w):
        return pl.BlockSpec((nb, n, qb, w), lambda b: (b, 0, q_block, 0))

    return pl.pallas_call(
        _prompt_attn_kernel,
        grid=(bsz // nb,),
        in_specs=[qheads(N_HEADS, HEAD_DIM), qheads(IDX_HEADS, IDX_CAT),
                  pl.BlockSpec((nb, qb, TAIL_W), lambda b: (b, q_block, 0)),
                  pl.BlockSpec((nb, N_KV_HEADS, HEAD_DIM, n_keys), lambda b: (b, 0, 0, 0)),
                  pl.BlockSpec((nb, N_KV_HEADS, n_keys, KV_W), lambda b: (b, 0, 0, 0)),
                  pl.BlockSpec((nb, IDX_CAT, n_keys), lambda b: (b, 0, 0)),
                  _const_spec((1, ATT_W)), _const_spec((TRI_W, TRI_W))],
        out_specs=pl.BlockSpec((nb, qb, ATT_W), lambda b: (b, 0, 0)),
        out_shape=jax.ShapeDtypeStruct((bsz, qb, ATT_W), BF16),
        scratch_shapes=[pltpu.VMEM((rows, n_keys), I32), pltpu.VMEM((rows, n_keys), HALF_T),
                        pltpu.VMEM((rows, n_keys), HALF_T), pltpu.VMEM((rows, n_keys), F32)],
        compiler_params=pltpu.CompilerParams(dimension_semantics=("arbitrary",), vmem_limit_bytes=VMEM_LIMIT),
        name=f"prompt_attention_{q_block}",
    )(qh, iqcat, iw, ktb, vb, ikcat, gn_att, tri)


def _prompt_attention(qh, iqcat, iw, ktb, vb, ikcat, gn_att, tri):
    t = qh.shape[2]
    blocks = [_prompt_attention_block(qh, iqcat, iw, ktb, vb, ikcat, gn_att, tri, i) for i in range(t // Q_BLOCK)]
    return jnp.concatenate(blocks, axis=1)


def _post_kernel(x_ref, na_ref, nb_ref, mod_ref, woa_ref, wob_ref, ln2_ref, wg_ref, wu_ref, wd_ref,
                 y_ref, *, shared_mod):
    mod = mod_ref[0] if shared_mod else mod_ref[...]
    g1 = mod[:, 2 * D_MODEL:3 * D_MODEL]
    sh2 = mod[:, 3 * D_MODEL:4 * D_MODEL]
    sc2 = mod[:, 4 * D_MODEL:5 * D_MODEL]
    g2 = mod[:, 5 * D_MODEL:6 * D_MODEL]
    x = x_ref[...]
    mix = _dot(na_ref[...], woa_ref[...]) + _dot(nb_ref[...], wob_ref[...])
    x1 = x + g1 * mix
    h2 = (_rms(x1, ln2_ref[...]) * (1.0 + sc2) + sh2).astype(BF16)
    def gate_up(c):
        return _dot(h2, wg_ref[:, c * FF_CHUNK:(c + 1) * FF_CHUNK]), _dot(h2, wu_ref[:, c * FF_CHUNK:(c + 1) * FF_CHUNK])

    n_chunks = D_FF // FF_CHUNK
    ff = jnp.zeros_like(x1)
    gu_next = gate_up(0)
    for c in range(n_chunks):
        (g, u), gu_next = gu_next, (gate_up(c + 1) if c + 1 < n_chunks else None)
        act = (g * jax.nn.sigmoid(g) * u).astype(BF16)
        ff = ff + _dot(act, wd_ref[c * FF_CHUNK:(c + 1) * FF_CHUNK, :])
    y_ref[...] = x1 + g2 * ff


def _post(x, na, nb, mod, p, rows_per_mod, tile):
    n = x.shape[0]
    shared = rows_per_mod > 1
    if shared:
        mod_spec = pl.BlockSpec((1, 1, 6 * D_MODEL), lambda i: (i * tile // rows_per_mod, 0, 0))
    else:
        mod_spec = pl.BlockSpec((tile, 6 * D_MODEL), lambda i: (i, 0))

    def rows(w):
        return pl.BlockSpec((tile, w), lambda i: (i, 0))

    def weight(shape):
        return pl.BlockSpec(shape, lambda i: (0, 0), pipeline_mode=pl.Buffered(1))

    return pl.pallas_call(
        functools.partial(_post_kernel, shared_mod=shared),
        grid=(n // tile,),
        in_specs=[rows(D_MODEL), rows(LRU_W), rows(ATT_W), mod_spec,
                  weight((LRU_W, D_MODEL)), weight((ATT_W, D_MODEL)), weight((1, D_MODEL)),
                  weight((D_MODEL, D_FF)), weight((D_MODEL, D_FF)), weight((D_FF, D_MODEL))],
        out_specs=rows(D_MODEL),
        out_shape=jax.ShapeDtypeStruct((n, D_MODEL), F32),
        compiler_params=pltpu.CompilerParams(dimension_semantics=("arbitrary",), vmem_limit_bytes=VMEM_LIMIT),
        name="post_ffn",
    )(x, na, nb, mod, p["wo_a"], p["wo_b"], p["ln2"], p["wg"], p["wu"], p["wd"])


def _sample_stage_kernel(x_ref, mod_ref, ln1_ref, wmain_ref, widx_ref, convw_ref, convb_ref,
                         wa_ref, ba_ref, wx_ref, bx_ref, lam_ref, qg_ref, kg_ref, gnl_ref,
                         onesq_ref, onesk_ref, rope_ref, sconv_ref, sh_ref,
                         na_ref, q_ref, k_ref, v_ref, ik_ref, iq_ref, iw_ref, h_ref, conv_ref):
    mod = mod_ref[...]
    z, zi = _project_in(x_ref[...], mod[:, 0:D_MODEL], mod[:, D_MODEL:2 * D_MODEL], ln1_ref[...],
                        wmain_ref[...], _split3(widx_ref[...]))
    xr = z[:, COL_XR:COL_GR]
    convw = convw_ref[...]
    sconv = sconv_ref[...]
    xc = convb_ref[...]
    for i in range(CONV_W - 1):
        xc = xc + sconv[:, i * LRU_W:(i + 1) * LRU_W] * convw[i:i + 1, :]
    xc = xc + xr * convw[CONV_W - 1:CONV_W, :]
    conv_ref[:, 0:(CONV_W - 2) * LRU_W] = sconv[:, LRU_W:(CONV_W - 1) * LRU_W]
    conv_ref[:, (CONV_W - 2) * LRU_W:(CONV_W - 1) * LRU_W] = xr

    a, b = _lru_gates(xc, wa_ref[...], ba_ref[...], wx_ref[...], bx_ref[...], lam_ref[...])
    hr = a * sh_ref[...] + b
    h_ref[...] = hr
    na_ref[...] = _rms(_gelu_tanh(z[:, COL_GR:COL_Q]) * hr, gnl_ref[...]).astype(BF16)

    q, k, v = _qkv_inputs(z, rope_ref[...], qg_ref[...], kg_ref[...], onesq_ref[...], onesk_ref[...])
    iq, tail = _index_inputs(zi[:, 0:IDX_W], zi[:, IDX_W:IDXP_W], rope_ref[...])
    q_ref[...] = q
    k_ref[...] = k
    v_ref[...] = v
    ik_ref[...] = tail[:, 0:IDX_DIM]
    iq_ref[...] = iq
    iw_ref[...] = zi[:, IDX_W:IDXP_W]


def _sample_stage(x, mod_s, p, rope_row, sconv, sh):
    n = x.shape[0]

    def out(w, dt=F32):
        return jax.ShapeDtypeStruct((n, w), dt)

    return pl.pallas_call(
        _sample_stage_kernel,
        out_shape=[out(LRU_W, BF16), out(ATT_W), out(KV_W), out(KV_W), out(IDX_DIM), out(IDX_W),
                   out(TAIL_W), out(LRU_W), out((CONV_W - 1) * LRU_W)],
        compiler_params=pltpu.CompilerParams(vmem_limit_bytes=VMEM_LIMIT),
        name="sample_stage",
    )(x, mod_s, p["ln1"], p["w_main"], p["w_idx"], p["conv_w"], p["conv_b"], p["wa"], p["ba"],
      p["wx"], p["bx"], p["lam"], p["qg"], p["kg"], p["gnl"], p["ones_q"], p["ones_k"], rope_row, sconv, sh)


def _page_copy(cache_ref, buf_ref, sem, page, slot, j, pg):
    return pltpu.make_async_copy(cache_ref.at[page], buf_ref.at[slot, j, pg], sem)


def _paged_prefetch(pt_ref, caches, bufs, sems, group, n_pages):
    s = pl.program_id(0)
    slot = s % 2

    def start(grp, half):
        for cache_ref, buf_ref, sem in zip(caches, bufs, sems):
            for j in range(group):
                for pg in range(n_pages):
                    page = pt_ref[grp * group + j, pg]
                    _page_copy(cache_ref, buf_ref, sem.at[half], page, half, j, pg).start()

    @pl.when(s == 0)
    def _():
        start(0, 0)

    @pl.when(s + 1 < pl.num_programs(0))
    def _():
        start(s + 1, 1 - slot)

    for cache_ref, buf_ref, sem in zip(caches, bufs, sems):
        for j in range(group):
            for pg in range(n_pages):
                _page_copy(cache_ref, buf_ref, sem.at[slot], 0, slot, j, pg).wait()
    return slot


def _sample_score_kernel(pt_ref, iq_ref, iw_ref, ik_hbm, o_ref, ik_buf, sem, *, n_pages):
    group = iq_ref.shape[0]
    slot = _paged_prefetch(pt_ref, (ik_hbm,), (ik_buf,), (sem,), group, n_pages)
    iqcat = [_cat6_lhs(iq_ref[j], 1) for j in range(group)]
    for pg in range(n_pages):
        for j in range(group):
            s = _dot(iqcat[j], _cat6_rhs(ik_buf[slot, j, pg], 0))
            o_ref[j, :, pg * PAGE_SIZE:(pg + 1) * PAGE_SIZE] = jnp.sum(iw_ref[j] * jnp.maximum(s, 0.0), axis=0,
                                                                        keepdims=True)


def _sample_scores(page_table, iq3, iw3, cache_ik):
    n, n_pages = page_table.shape
    g = SCORE_GROUP
    grid_spec = pltpu.PrefetchScalarGridSpec(
        num_scalar_prefetch=1,
        grid=(n // g,),
        in_specs=[pl.BlockSpec((g, IDX_HEADS, IDX_DIM), lambda s, pt: (s, 0, 0)),
                  pl.BlockSpec((g, IDX_HEADS, 1), lambda s, pt: (s, 0, 0)),
                  pl.BlockSpec(memory_space=pl.ANY)],
        out_specs=pl.BlockSpec((g, 1, n_pages * PAGE_SIZE), lambda s, pt: (s, 0, 0)),
        scratch_shapes=[pltpu.VMEM((2, g, n_pages, IDX_DIM, PAGE_SIZE), F32), pltpu.SemaphoreType.DMA((2,))],
    )
    return pl.pallas_call(
        functools.partial(_sample_score_kernel, n_pages=n_pages),
        grid_spec=grid_spec,
        out_shape=jax.ShapeDtypeStruct((n, 1, n_pages * PAGE_SIZE), F32),
        compiler_params=pltpu.CompilerParams(dimension_semantics=("arbitrary",), vmem_limit_bytes=VMEM_LIMIT),
        name="sample_scores",
    )(page_table, iq3, iw3, cache_ik)


def _sample_select_kernel(score_ref, iq_ref, ik_ref, iw_ref, tri_ref, bias_ref, bias_new_ref,
                          key_ref, hi_ref, lo_ref):
    iq = iq_ref[...]
    ik = ik_ref[...]
    iw = iw_ref[...]
    new = jnp.zeros((iq.shape[0], 1), F32)
    for h in range(IDX_HEADS):
        s = jnp.sum(iq[:, h * IDX_DIM:(h + 1) * IDX_DIM] * ik, axis=1, keepdims=True)
        new = new + iw[:, IW_LANE + h:IW_LANE + h + 1] * jnp.maximum(s, 0.0)
    key_ref[...] = _sortable_key(score_ref[...])
    b_new = _selection_bias(key_ref, hi_ref, lo_ref, bias_ref, tri_ref[...], float(TOPK_MAX),
                            extra=_sortable_key(new))
    bias_new_ref[...] = jnp.broadcast_to(b_new, bias_new_ref.shape)


def _sample_select(score, iq, ik, iw, tri):
    n, n_keys = score.shape
    return pl.pallas_call(
        _sample_select_kernel,
        out_shape=[jax.ShapeDtypeStruct((n, n_keys), F32), jax.ShapeDtypeStruct((n, LANES), F32)],
        scratch_shapes=[pltpu.VMEM((n, n_keys), I32), pltpu.VMEM((n, n_keys), HALF_T), pltpu.VMEM((n, n_keys), HALF_T)],
        compiler_params=pltpu.CompilerParams(vmem_limit_bytes=VMEM_LIMIT),
        name="sample_select",
    )(score, iq, ik, iw, tri)


def _sample_attn_kernel(pt_ref, q_ref, kn_ref, vn_ref, bias_ref, bnew_ref, gn_ref, k_hbm, v_hbm, o_ref,
                        k_buf, v_buf, k_sem, v_sem, *, n_pages):
    group = q_ref.shape[0]
    seqs = range(group)
    slot = _paged_prefetch(pt_ref, (k_hbm, v_hbm), (k_buf, v_buf), (k_sem, v_sem), group, n_pages)
    row = lax.broadcasted_iota(I32, (N_HEADS, HEAD_DIM), 0)
    first = row < KV_GROUP
    q = [q_ref[j] for j in seqs]
    q2 = [jnp.concatenate([jnp.where(first, q[j], 0.0), jnp.where(first, 0.0, q[j])], axis=1).astype(BF16)
          for j in seqs]
    logits = [[None] * n_pages for _ in seqs]
    for pg in range(n_pages):
        for j in seqs:
            kp = k_buf[slot, j, pg].astype(BF16).reshape(KV_W, PAGE_SIZE)
            logits[j][pg] = _dot(q2[j], kp) + bias_ref[j][:, pg * PAGE_SIZE:(pg + 1) * PAGE_SIZE]
    m, p_new, den, acc, v_sel = [], [], [], [], []
    for j in seqs:
        kn = kn_ref[j]
        vn = vn_ref[j]
        k_sel = jnp.where(first, kn[:, 0:HEAD_DIM], kn[:, HEAD_DIM:KV_W])
        v_sel.append(jnp.where(first, vn[:, 0:HEAD_DIM], vn[:, HEAD_DIM:KV_W]))
        l_new = jnp.sum(q[j] * k_sel, axis=1, keepdims=True) + bnew_ref[j][:, 0:1]
        mj = l_new
        for l in logits[j]:
            mj = jnp.maximum(mj, jnp.max(l, axis=1, keepdims=True))
        m.append(mj)
        p_new.append(jnp.exp2(l_new - mj))
        den.append(p_new[j])
        acc.append(jnp.zeros((N_HEADS, KV_W), F32))
    for pg in range(n_pages):
        for j in seqs:
            p = jnp.exp2(logits[j][pg] - m[j])
            den[j] = den[j] + jnp.sum(p, axis=1, keepdims=True)
            vp = v_buf[slot, j, pg].astype(BF16).reshape(KV_W, PAGE_SIZE)
            acc[j] = acc[j] + _dot_nt(p.astype(BF16), vp)
    for j in seqs:
        out = (p_new[j] * v_sel[j] + jnp.where(first, acc[j][:, 0:HEAD_DIM], acc[j][:, HEAD_DIM:KV_W])) / den[j]
        ms = jnp.sum(jnp.sum(out * out, axis=1, keepdims=True), axis=0, keepdims=True) * (1.0 / ATT_W)
        o_ref[j] = (out * lax.rsqrt(ms + EPS) * gn_ref[...]).astype(BF16)


def _sample_attention(page_table, q3, k_new, v_new, bias, bias_new, gn8, cache_k, cache_v):
    n, n_pages = page_table.shape
    n_keys = n_pages * PAGE_SIZE

    g = DECODE_GROUP

    def per_seq(shape):
        return pl.BlockSpec((g,) + shape, lambda s, pt: (s, 0, 0))

    page_buf = pltpu.VMEM((2, g, n_pages, N_KV_HEADS, HEAD_DIM, PAGE_SIZE), F32)
    grid_spec = pltpu.PrefetchScalarGridSpec(
        num_scalar_prefetch=1,
        grid=(n // g,),
        in_specs=[per_seq((N_HEADS, HEAD_DIM)), per_seq((1, KV_W)), per_seq((1, KV_W)),
                  per_seq((1, n_keys)), per_seq((1, LANES)),
                  pl.BlockSpec((N_HEADS, HEAD_DIM), lambda s, pt: (0, 0)),
                  pl.BlockSpec(memory_space=pl.ANY), pl.BlockSpec(memory_space=pl.ANY)],
        out_specs=per_seq((N_HEADS, HEAD_DIM)),
        scratch_shapes=[page_buf, page_buf, pltpu.SemaphoreType.DMA((2,)), pltpu.SemaphoreType.DMA((2,))],
    )
    return pl.pallas_call(
        functools.partial(_sample_attn_kernel, n_pages=n_pages),
        grid_spec=grid_spec,
        out_shape=jax.ShapeDtypeStruct((n, N_HEADS, HEAD_DIM), BF16),
        compiler_params=pltpu.CompilerParams(dimension_semantics=("arbitrary",), vmem_limit_bytes=VMEM_LIMIT),
        name="sample_attention",
    )(page_table, q3, k_new, v_new, bias, bias_new, gn8, cache_k, cache_v)


def _rope_table_np(positions):
    half = ROT_DIM // 2
    freq = ROPE_THETA ** (-(np.arange(half, dtype=np.float64) / half))
    ang = np.asarray(positions, np.float64)[:, None] * freq[None, :]
    cos, sin = np.cos(ang), np.sin(ang)
    n = len(positions)
    tab = np.zeros((n, 3, LANES), np.float64)
    tab[:, 0, :] = 1.0
    for base in range(0, LANES, HEAD_DIM):
        tab[:, 0, base:base + half] = cos
        tab[:, 0, base + half:base + ROT_DIM] = cos
        tab[:, 1, base:base + half] = -sin
        tab[:, 2, base + half:base + ROT_DIM] = sin
    return tab.reshape(n, 3 * LANES).astype(np.float32)


def _block_diag(w):
    n, a, b = w.shape
    return jnp.einsum("nij,nm->nimj", w, jnp.eye(n, dtype=w.dtype)).reshape(n * a, n * b)


def _layer_params(l, ln1_g, w_in, conv_w, conv_b, lru_wa, lru_ba, lru_wx, lru_bx, lru_lambda, q_norm_g,
                  k_norm_g, gn_lru_g, gn_att_g, w_out, ln2_g, w_gate, w_up, w_down):
    w_in_t = jnp.transpose(w_in[l])
    w_idx = jnp.pad(w_in_t[COL_IQ:], ((0, IDXP_W - (w_in.shape[2] - COL_IQ)), (0, 0)))
    head_id = np.arange(ATT_W) // HEAD_DIM
    ones_q = (head_id[:, None] == head_id[None, :]).astype(np.float32)
    return {
        "ln1": ln1_g[l][None, :],
        "w_main": w_in_t[:COL_IQ].astype(BF16),
        "w_idx": w_idx,
        "conv_w": conv_w[l],
        "conv_b": conv_b[l][None, :],
        "wa": _block_diag(lru_wa[l]).astype(BF16),
        "ba": lru_ba[l][None, :],
        "wx": _block_diag(lru_wx[l]).astype(BF16),
        "bx": lru_bx[l][None, :],
        "lam": lru_lambda[l][None, :],
        "qg": jnp.tile(q_norm_g[l], N_HEADS)[None, :],
        "kg": jnp.tile(k_norm_g[l], N_KV_HEADS)[None, :],
        "gnl": gn_lru_g[l][None, :],
        "gna": gn_att_g[l][None, :],
        "gna8": gn_att_g[l].reshape(N_HEADS, HEAD_DIM),
        "ones_q": jnp.asarray(ones_q, BF16),
        "ones_k": jnp.asarray(ones_q[:KV_W, :KV_W], BF16),
        "wo_a": w_out[l][:LRU_W].astype(BF16),
        "wo_b": w_out[l][LRU_W:].astype(BF16),
        "ln2": ln2_g[l][None, :],
        "wg": w_gate[l].astype(BF16),
        "wu": w_up[l].astype(BF16),
        "wd": w_down[l].astype(BF16),
    }


def kernel(x_prompt, x_sample, cache_k, cache_v, cache_ik, state_h, state_conv, page_table, c_prompt, c_sample, ada_w, ada_b, ln1_g, w_in, conv_w, conv_b, lru_wa, lru_ba, lru_wx, lru_bx, lru_lambda, q_norm_g, k_norm_g, gn_lru_g, gn_att_g, w_out, ln2_g, w_gate, w_up, w_down):
    bsz, t, _ = x_prompt.shape
    dbsz, dt, _ = x_sample.shape
    depth = ada_w.shape[0]
    n_pages = page_table.shape[1]
    past_len = n_pages * PAGE_SIZE
    assert dt == 1 and t % PROMPT_ROW_TILE == 0 and t % POST_ROW_TILE == 0 and t % Q_BLOCK == 0
    assert t // 4 >= TOPK_MAX and (past_len + dt) // 4 >= TOPK_MAX
    assert dbsz % DECODE_GROUP == 0 and dbsz % SCORE_GROUP == 0 and cache_k.shape[2] == PAGE_SIZE

    rope_p = jnp.asarray(_rope_table_np(np.arange(t)))
    rope_s = jnp.asarray(_rope_table_np(past_len + np.arange(dt)))
    tri = jnp.asarray(np.triu(np.ones((TRI_W, TRI_W), np.float32)), BF16)

    yp = x_prompt.reshape(bsz * t, D_MODEL)
    ys = x_sample.reshape(dbsz, D_MODEL)
    c_all = jnp.concatenate([c_prompt, c_sample], axis=0)
    outs_p, outs_s = [], []
    for l in range(depth):
        p = _layer_params(l, ln1_g, w_in, conv_w, conv_b, lru_wa, lru_ba, lru_wx, lru_bx, lru_lambda,
                          q_norm_g, k_norm_g, gn_lru_g, gn_att_g, w_out, ln2_g, w_gate, w_up, w_down)
        mod = _ada(c_all, ada_w[l], ada_b[l][None, :])
        mod_p = mod[:bsz].reshape(bsz, 1, 6 * D_MODEL)
        mod_s = mod[bsz:]

        (na, qh, kt, vt, ikt, ktb, vb, ikcat, iqcat, iw, h_last, conv_new) = _prompt_stage(
            yp.reshape(bsz, t, D_MODEL), mod_p, p, rope_p)
        nb = _prompt_attention(qh, iqcat, iw, ktb, vb, ikcat, p["gna"], tri)
        yp = _post(yp, na.reshape(bsz * t, LRU_W), nb.reshape(bsz * t, ATT_W), mod_p, p, t, POST_ROW_TILE)
        outs_p.append((jnp.transpose(kt, (0, 3, 1, 2)), jnp.transpose(vt, (0, 3, 1, 2)),
                       jnp.transpose(ikt, (0, 2, 1)), h_last.reshape(bsz, LRU_W), conv_new))

        (na_s, q_s, k_s, v_s, ik_s, iq_s, iw_s, h_s, conv_s) = _sample_stage(
            ys, mod_s, p, rope_s, state_conv[l].reshape(dbsz, (CONV_W - 1) * LRU_W), state_h[l])
        score = _sample_scores(page_table, iq_s.reshape(dbsz, IDX_HEADS, IDX_DIM),
                               iw_s[:, IW_LANE:IW_LANE + IDX_HEADS].reshape(dbsz, IDX_HEADS, 1),
                               jnp.transpose(cache_ik[l], (0, 2, 1)))
        bias, bias_new = _sample_select(score.reshape(dbsz, past_len), iq_s, ik_s, iw_s, tri)
        nb_s = _sample_attention(page_table, q_s.reshape(dbsz, N_HEADS, HEAD_DIM),
                                 k_s.reshape(dbsz, 1, KV_W), v_s.reshape(dbsz, 1, KV_W),
                                 bias.reshape(dbsz, 1, past_len), bias_new.reshape(dbsz, 1, LANES), p["gna8"],
                                 jnp.transpose(cache_k[l], (0, 2, 3, 1)), jnp.transpose(cache_v[l], (0, 2, 3, 1)))
        ys = _post(ys, na_s, nb_s.reshape(dbsz, ATT_W), mod_s, p, 1, dbsz)
        outs_s.append((k_s.reshape(dbsz, dt, N_KV_HEADS, HEAD_DIM), v_s.reshape(dbsz, dt, N_KV_HEADS, HEAD_DIM),
                       ik_s.reshape(dbsz, dt, IDX_DIM), h_s,
                       conv_s.reshape(dbsz, CONV_W - 1, LRU_W)))

    def stack(outs, i):
        return jnp.stack([o[i] for o in outs])

    return (yp.reshape(bsz, t, D_MODEL), ys.reshape(dbsz, dt, D_MODEL),
            stack(outs_p, 0), stack(outs_p, 1), stack(outs_p, 2), stack(outs_p, 3), stack(outs_p, 4),
            stack(outs_s, 0), stack(outs_s, 1), stack(outs_s, 2), stack(outs_s, 3), stack(outs_s, 4))
```
